```python
import math, functools
import jax, jax.numpy as jnp
from jax import lax
import numpy as np

D_MODEL = 1024
BATCH = 32
SEQ = 256
DEPTH = 2
DEC_BATCH = 2
DEC_SEQ = 2048
PAST_LEN = 512

GRID_W = 64
POOL_WINDOWS = (2, 4, 8, 16)
POOL_GROUPS = 4
POOL_GC = 128
POOL_WIDTH = POOL_GROUPS * POOL_GC
NA_HEADS = 8
NA_HEAD_DIM = 64
NA_WIDTH = NA_HEADS * NA_HEAD_DIM
NA_WIN_ROWS = 8
NA_WIN_COLS = 16
MLA_HEADS = 8
MLA_NOPE_DIM = 64
MLA_ROPE_DIM = 32
MLA_V_DIM = 64
MLA_QK_DIM = MLA_NOPE_DIM + MLA_ROPE_DIM
MLA_Q_LORA = 384
MLA_KV_LORA = 256
MLA_WIDTH = MLA_HEADS * MLA_V_DIM
N_BRANCHES = 3
BRANCH_WIDTH = 512
IN_SPLIT_SIZES = (POOL_WIDTH, NA_WIDTH, NA_WIDTH, NA_WIDTH, MLA_Q_LORA, MLA_KV_LORA, MLA_ROPE_DIM, N_BRANCHES * D_MODEL)
IN_COLS = POOL_WIDTH + 3 * NA_WIDTH + MLA_Q_LORA + MLA_KV_LORA + MLA_ROPE_DIM + N_BRANCHES * D_MODEL
N_GROUPS = 4
EXPERTS_PER_GROUP = 4
N_EXPERTS = N_GROUPS * EXPERTS_PER_GROUP
TOP_K = 2
EXPERT_FF = 256
ROPE_BASE = 10000.0
EPS = 1e-6
Q_BLOCK = 128
NEG_INF = -1e30

kernel_name = 'hybrid_diffusion_prefix_step'


def rmsnorm(x, g):
    xf = x.astype(jnp.float32)
    y = xf * lax.rsqrt(jnp.mean(xf * xf, axis=-1, keepdims=True) + EPS)
    return y.astype(x.dtype) * g


def modulation(cvec, w_mod, b_mod):
    m = (jax.nn.silu(cvec) @ w_mod + b_mod)[..., None, :]
    return jnp.split(m, 6, axis=-1)


def ada_norm(x, g, shift, scale):
    return rmsnorm(x, g) * (1.0 + scale) + shift


def heads(t, n, d):
    return t.reshape(t.shape[:-1] + (n, d))


def split_inputs(h, w_in):
    z = h @ w_in
    idx = np.cumsum(IN_SPLIT_SIZES)[:-1].tolist()
    return jnp.split(z, idx, axis=-1)


def axial_rope_tables(T, dtype):
    t = jnp.arange(T)
    n_freq = MLA_ROPE_DIM // 4
    inv = ROPE_BASE ** (-jnp.arange(n_freq, dtype=jnp.float32) / n_freq)
    ang_r = (t // GRID_W).astype(jnp.float32)[:, None] * inv
    ang_c = (t % GRID_W).astype(jnp.float32)[:, None] * inv
    ang = jnp.stack([ang_r, ang_c], axis=1)
    return jnp.cos(ang).astype(dtype), jnp.sin(ang).astype(dtype)


def apply_axial_rope(x, cos, sin):
    B, T, H, _ = x.shape
    xa = x.reshape(B, T, H, 2, 2, MLA_ROPE_DIM // 4)
    x1, x2 = xa[..., 0, :], xa[..., 1, :]
    c = cos[None, :, None]
    s = sin[None, :, None]
    out = jnp.stack([x1 * c - x2 * s, x1 * s + x2 * c], axis=-2)
    return out.reshape(B, T, H, MLA_ROPE_DIM)


def blocked_attention(q, k, v, scale):
    B, T, H, dk = q.shape
    nb = T // Q_BLOCK
    qb = jnp.moveaxis(q.reshape(B, nb, Q_BLOCK, H, dk), 1, 0)

    def one_block(qi):
        s = jnp.einsum('bqhd,bshd->bhqs', qi, k, preferred_element_type=jnp.float32) * scale
        p = jax.nn.softmax(s, axis=-1).astype(v.dtype)
        return jnp.einsum('bhqs,bshd->bqhd', p, v)

    out = lax.map(one_block, qb)
    return jnp.moveaxis(out, 0, 1).reshape(B, T, H, v.shape[-1])


def pool_mixer(u, w_pool, pool_scale):
    B, T, _ = u.shape
    uf = u.astype(jnp.float32)
    cs = jnp.concatenate([jnp.zeros((B, 1, POOL_WIDTH), jnp.float32), jnp.cumsum(uf, axis=1)], axis=1)
    t = jnp.arange(T)
    outs = []
    for g, w in enumerate(POOL_WINDOWS):
        lo = jnp.clip(t - w // 2, 0, T)
        hi = jnp.clip(t + (w - w // 2), 0, T)
        csg = cs[..., g * POOL_GC:(g + 1) * POOL_GC]
        cnt = (hi - lo).astype(jnp.float32)[None, :, None]
        outs.append((csg[:, hi] - csg[:, lo]) / cnt)
    pooled = jnp.stack(outs, axis=2)
    d = (pooled - uf.reshape(B, T, POOL_GROUPS, POOL_GC)).astype(u.dtype)
    y = jnp.einsum('btgc,gcd->btgd', d, w_pool).reshape(B, T, POOL_WIDTH)
    return y * pool_scale


def neighbourhood_attention(q, k, v, k_ctx, v_ctx, rpb):
    B, T, H, d = q.shape
    R = T // GRID_W
    WR = min(NA_WIN_ROWS, R)
    K = WR * GRID_W
    scale = d ** -0.5
    rows = jnp.arange(R)
    r0 = jnp.clip(rows - NA_WIN_ROWS // 2, 0, R - WR)
    krow = r0[:, None] + jnp.arange(WR)[None, :]
    kg = k.reshape(B, R, GRID_W, H, d)[:, krow].reshape(B, R, K, H, d)
    vg = v.reshape(B, R, GRID_W, H, d)[:, krow].reshape(B, R, K, H, d)
    qg = q.reshape(B, R, GRID_W, H, d)
    s_win = jnp.einsum('brqhd,brkhd->brhqk', qg, kg, preferred_element_type=jnp.float32) * scale
    cols = jnp.arange(GRID_W)
    c0 = jnp.clip(cols - NA_WIN_COLS // 2, 0, GRID_W - NA_WIN_COLS)
    kcol = jnp.arange(K) % GRID_W
    kloc = jnp.arange(K) // GRID_W
    in_win = (kcol[None, :] >= c0[:, None]) & (kcol[None, :] < c0[:, None] + NA_WIN_COLS)
    dr_idx = (krow - rows[:, None])[:, kloc] + NA_WIN_ROWS - 1
    dc_idx = jnp.clip(kcol[None, :] - cols[:, None], -(NA_WIN_COLS - 1), NA_WIN_COLS - 1) + NA_WIN_COLS - 1
    bias = jnp.moveaxis(rpb[:, dr_idx[:, None, :], dc_idx[None, :, :]], 0, 1)
    s_win = jnp.where(in_win[None, None, None], s_win + bias[None].astype(jnp.float32), NEG_INF)
    s_ctx = jnp.einsum('brqhd,bshd->brhqs', qg, k_ctx, preferred_element_type=jnp.float32) * scale
    p = jax.nn.softmax(jnp.concatenate([s_win, s_ctx], axis=-1), axis=-1).astype(v.dtype)
    out = (jnp.einsum('brhqk,brkhd->brqhd', p[..., :K], vg)
           + jnp.einsum('brhqs,bshd->brqhd', p[..., K:], v_ctx))
    return out.reshape(B, T, H, d)


def mla_queries(cq, q_norm, w_uq):
    return heads(rmsnorm(cq, q_norm) @ w_uq, MLA_HEADS, MLA_QK_DIM)


def mla_keys_values(ckv, kpe, w_ukv):
    kv = heads(ckv @ w_ukv, MLA_HEADS, MLA_NOPE_DIM + MLA_V_DIM)
    k_nope, v = kv[..., :MLA_NOPE_DIM], kv[..., MLA_NOPE_DIM:]
    kpe_b = jnp.broadcast_to(kpe[:, :, None, :], k_nope.shape[:-1] + (MLA_ROPE_DIM,))
    return jnp.concatenate([k_nope, kpe_b], axis=-1), v


def merge_branches(y_pool, y_na, y_mla, g_logits, w_branch, w_out):
    B, T, _ = y_pool.shape
    ys = jnp.stack([y_pool, y_na.reshape(B, T, NA_WIDTH), y_mla.reshape(B, T, MLA_WIDTH)], axis=2)
    proj = jnp.einsum('btnc,ncd->btnd', ys, w_branch)
    gates = jax.nn.sigmoid(g_logits.reshape(B, T, N_BRANCHES, D_MODEL))
    return jnp.sum(gates * proj, axis=2) @ w_out


def context_mixer(h, w_in, w_pool, pool_scale, q_norm, w_uq, kv_norm, w_ukv, w_branch, w_out):
    u, nq, nk, nv, cq, ckv, kpe, g_logits = split_inputs(h, w_in)
    y_pool = pool_mixer(u, w_pool, pool_scale)
    nq = heads(nq, NA_HEADS, NA_HEAD_DIM)
    nk = heads(nk, NA_HEADS, NA_HEAD_DIM)
    nv = heads(nv, NA_HEADS, NA_HEAD_DIM)
    y_na = blocked_attention(nq, nk, nv, NA_HEAD_DIM ** -0.5)
    ckv = rmsnorm(ckv, kv_norm)
    q = mla_queries(cq, q_norm, w_uq)
    k, v = mla_keys_values(ckv, kpe, w_ukv)
    y_mla = blocked_attention(q, k, v, MLA_QK_DIM ** -0.5)
    y = merge_branches(y_pool, y_na, y_mla, g_logits, w_branch, w_out)
    return y, nk, nv, ckv, kpe


def latent_mixer(h, ctx_na_k, ctx_na_v, ctx_ckv, ctx_kpe, w_in, w_pool, pool_scale, na_rpb,
                 q_norm, w_uq, kv_norm, w_ukv, w_branch, w_out):
    T = h.shape[1]
    u, nq, nk, nv, cq, ckv, kpe, g_logits = split_inputs(h, w_in)
    y_pool = pool_mixer(u, w_pool, pool_scale)
    y_na = neighbourhood_attention(heads(nq, NA_HEADS, NA_HEAD_DIM), heads(nk, NA_HEADS, NA_HEAD_DIM),
                                   heads(nv, NA_HEADS, NA_HEAD_DIM), ctx_na_k, ctx_na_v, na_rpb)
    cos, sin = axial_rope_tables(T, h.dtype)
    q = mla_queries(cq, q_norm, w_uq)
    q = jnp.concatenate([q[..., :MLA_NOPE_DIM], apply_axial_rope(q[..., MLA_NOPE_DIM:], cos, sin)], axis=-1)
    kpe = apply_axial_rope(kpe[:, :, None, :], cos, sin)[:, :, 0]
    k_lat, v_lat = mla_keys_values(rmsnorm(ckv, kv_norm), kpe, w_ukv)
    k_ctx, v_ctx = mla_keys_values(ctx_ckv, ctx_kpe, w_ukv)
    k = jnp.concatenate([k_lat, k_ctx], axis=1)
    v = jnp.concatenate([v_lat, v_ctx], axis=1)
    y_mla = blocked_attention(q, k, v, MLA_QK_DIM ** -0.5)
    return merge_branches(y_pool, y_na, y_mla, g_logits, w_branch, w_out)


def hier_moe(h, w_group_router, w_expert_router, w1, w3, w2):
    B, T, D = h.shape
    x = h.reshape(-1, D)
    N = x.shape[0]
    gp = jax.nn.softmax((x @ w_group_router).astype(jnp.float32), axis=-1)
    g_w, g_idx = lax.top_k(gp, 1)
    el = (x @ w_expert_router).astype(jnp.float32).reshape(N, N_GROUPS, EXPERTS_PER_GROUP)
    el_g = jnp.take_along_axis(el, g_idx[:, :, None], axis=1)[:, 0]
    top_w, top_i = lax.top_k(jax.nn.softmax(el_g, axis=-1), TOP_K)
    top_w = top_w / jnp.sum(top_w, axis=-1, keepdims=True)
    expert_id = g_idx * EXPERTS_PER_GROUP + top_i
    gates = jnp.einsum('nk,nke->ne', g_w * top_w, jax.nn.one_hot(expert_id, N_EXPERTS, dtype=jnp.float32))
    hid = jax.nn.silu(jnp.einsum('nd,edf->nef', x, w1)) * jnp.einsum('nd,edf->nef', x, w3)
    out = jnp.einsum('nef,efd->nd', hid * gates[:, :, None].astype(hid.dtype), w2)
    return out.reshape(B, T, D)


def setup_inputs(seed: int = 0) -> dict:
    key = jax.random.key(seed)
    ks = jax.random.split(key, 32)

    def nrm(k, shape, scale):
        return jax.random.normal(k, shape, jnp.float32) * scale

    def gain(k, shape):
        return 1.0 + 0.05 * jax.random.normal(k, shape, jnp.float32)

    return {
        'x_prompt': nrm(ks[0], (BATCH, SEQ, D_MODEL), 1.0),
        'x_sample': nrm(ks[1], (DEC_BATCH, DEC_SEQ, D_MODEL), 1.0),
        'cache_na_k': nrm(ks[2], (DEC_BATCH, DEPTH, PAST_LEN, NA_HEADS, NA_HEAD_DIM), 1.0),
        'cache_na_v': nrm(ks[3], (DEC_BATCH, DEPTH, PAST_LEN, NA_HEADS, NA_HEAD_DIM), 1.0),
        'cache_mla_ckv': nrm(ks[4], (DEC_BATCH, DEPTH, PAST_LEN, MLA_KV_LORA), 1.0),
        'cache_mla_kpe': nrm(ks[5], (DEC_BATCH, DEPTH, PAST_LEN, MLA_ROPE_DIM), 1.0),
        'c': nrm(ks[6], (DEC_BATCH, D_MODEL), 1.0),
        'c_ctx': nrm(ks[7], (D_MODEL,), 1.0),
        'w_mod': nrm(ks[8], (DEPTH, D_MODEL, 6 * D_MODEL), 0.5 * D_MODEL ** -0.5),
        'b_mod': nrm(ks[9], (DEPTH, 6 * D_MODEL), 0.02),
        'g_pre_mix': gain(ks[10], (DEPTH, D_MODEL)),
        'g_post_mix': gain(ks[11], (DEPTH, D_MODEL)),
        'g_pre_ffn': gain(ks[12], (DEPTH, D_MODEL)),
        'g_post_ffn': gain(ks[13], (DEPTH, D_MODEL)),
        'w_in': nrm(ks[14], (DEPTH, D_MODEL, IN_COLS), D_MODEL ** -0.5),
        'w_pool': nrm(ks[15], (DEPTH, POOL_GROUPS, POOL_GC, POOL_GC), POOL_GC ** -0.5),
        'pool_scale': gain(ks[16], (DEPTH, POOL_WIDTH)),
        'na_rpb': nrm(ks[17], (DEPTH, NA_HEADS, 2 * NA_WIN_ROWS - 1, 2 * NA_WIN_COLS - 1), 0.1),
        'mla_q_norm': gain(ks[18], (DEPTH, MLA_Q_LORA)),
        'mla_w_uq': nrm(ks[19], (DEPTH, MLA_Q_LORA, MLA_HEADS * MLA_QK_DIM), MLA_Q_LORA ** -0.5),
        'mla_kv_norm': gain(ks[20], (DEPTH, MLA_KV_LORA)),
        'mla_w_ukv': nrm(ks[21], (DEPTH, MLA_KV_LORA, MLA_HEADS * (MLA_NOPE_DIM + MLA_V_DIM)), MLA_KV_LORA ** -0.5),
        'w_branch': nrm(ks[22], (DEPTH, N_BRANCHES, BRANCH_WIDTH, D_MODEL), BRANCH_WIDTH ** -0.5),
        'w_out': nrm(ks[23], (DEPTH, D_MODEL, D_MODEL), D_MODEL ** -0.5),
        'w_group_router': nrm(ks[24], (DEPTH, D_MODEL, N_GROUPS), D_MODEL ** -0.5),
        'w_expert_router': nrm(ks[25], (DEPTH, D_MODEL, N_EXPERTS), D_MODEL ** -0.5),
        'w_exp_gate': nrm(ks[26], (DEPTH, N_EXPERTS, D_MODEL, EXPERT_FF), D_MODEL ** -0.5),
        'w_exp_up': nrm(ks[27], (DEPTH, N_EXPERTS, D_MODEL, EXPERT_FF), D_MODEL ** -0.5),
        'w_exp_down': nrm(ks[28], (DEPTH, N_EXPERTS, EXPERT_FF, D_MODEL), EXPERT_FF ** -0.5),
    }


def reference(x_prompt, x_sample, cache_na_k, cache_na_v, cache_mla_ckv, cache_mla_kpe, c, c_ctx,
              w_mod, b_mod, g_pre_mix, g_post_mix, g_pre_ffn, g_post_ffn, w_in, w_pool, pool_scale, na_rpb,
              mla_q_norm, mla_w_uq, mla_kv_norm, mla_w_ukv, w_branch, w_out, w_group_router, w_expert_router,
              w_exp_gate, w_exp_up, w_exp_down):
    xp = x_prompt
    na_k_l, na_v_l, ckv_l, kpe_l = [], [], [], []
    for l in range(DEPTH):
        sh1, sc1, gt1, sh2, sc2, gt2 = modulation(c_ctx, w_mod[l], b_mod[l])
        h = ada_norm(xp, g_pre_mix[l], sh1, sc1)
        y, nk, nv, ckv, kpe = context_mixer(h, w_in[l], w_pool[l], pool_scale[l], mla_q_norm[l], mla_w_uq[l],
                                            mla_kv_norm[l], mla_w_ukv[l], w_branch[l], w_out[l])
        na_k_l.append(nk)
        na_v_l.append(nv)
        ckv_l.append(ckv)
        kpe_l.append(kpe)
        xp = xp + gt1 * rmsnorm(y, g_post_mix[l])
        h = ada_norm(xp, g_pre_ffn[l], sh2, sc2)
        y = hier_moe(h, w_group_router[l], w_expert_router[l], w_exp_gate[l], w_exp_up[l], w_exp_down[l])
        xp = xp + gt2 * rmsnorm(y, g_post_ffn[l])

    xs = x_sample
    for l in range(DEPTH):
        sh1, sc1, gt1, sh2, sc2, gt2 = modulation(c, w_mod[l], b_mod[l])
        h = ada_norm(xs, g_pre_mix[l], sh1, sc1)
        y = latent_mixer(h, cache_na_k[:, l], cache_na_v[:, l], cache_mla_ckv[:, l], cache_mla_kpe[:, l],
                         w_in[l], w_pool[l], pool_scale[l], na_rpb[l], mla_q_norm[l], mla_w_uq[l],
                         mla_kv_norm[l], mla_w_ukv[l], w_branch[l], w_out[l])
        xs = xs + gt1 * rmsnorm(y, g_post_mix[l])
        h = ada_norm(xs, g_pre_ffn[l], sh2, sc2)
        y = hier_moe(h, w_group_router[l], w_expert_router[l], w_exp_gate[l], w_exp_up[l], w_exp_down[l])
        xs = xs + gt2 * rmsnorm(y, g_post_ffn[l])

    return (xp, xs, jnp.stack(na_k_l, axis=1), jnp.stack(na_v_l, axis=1), jnp.stack(ckv_l, axis=1), jnp.stack(kpe_l, axis=1))
```

```python
import functools

import numpy as np
import jax
import jax.numpy as jnp
from jax import lax
from jax.experimental import pallas as pl
from jax.experimental.pallas import tpu as pltpu

F32 = jnp.float32
BF16 = jnp.bfloat16

D_MODEL = 1024
DEPTH = 2
GRID_W = 64
POOL_WINDOWS = (2, 4, 8, 16)
POOL_GC = 128
POOL_WIDTH = 512
NA_HEADS = 8
NA_HEAD_DIM = 64
NA_WIDTH = 512
NA_WIN_ROWS = 8
NA_WIN_COLS = 16
MLA_HEADS = 8
MLA_NOPE_DIM = 64
MLA_ROPE_DIM = 32
MLA_V_DIM = 64
MLA_QK_DIM = 96
MLA_Q_LORA = 384
MLA_KV_LORA = 256
N_GROUPS = 4
EXPERTS_PER_GROUP = 4
N_EXPERTS = 16
EXPERT_FF = 256
ROPE_BASE = 10000.0
EPS = 1e-6
NEG_INF = -1e30

LANES = 128
SLOT = LANES
VMEM_LIMIT = 56 * 1024 * 1024

C_U, C_NQ, C_NK, C_NV, C_CQ, C_CKV, C_G, C_KPE, C_END = 0, 512, 1024, 1536, 2048, 2432, 2688, 5760, 5888
KPE_LANE = MLA_NOPE_DIM

NA_BLOCK_ROWS = 4
NA_SLAB_ROWS = 12


def _cparams(sem):
    return pltpu.CompilerParams(dimension_semantics=sem, vmem_limit_bytes=VMEM_LIMIT)


def _rms(x, g):
    return x * lax.rsqrt(jnp.mean(x * x, axis=-1, keepdims=True) + EPS) * g


def _dot(a, b):
    return jnp.dot(a, b, preferred_element_type=F32)


def _dot_nt(a, b):
    return lax.dot_general(a, b, (((1,), (1,)), ((), ())), preferred_element_type=F32)


def _mod_kernel(c_ref, w_ref, b_ref, o_ref):
    c = c_ref[...]
    s = (c * jax.nn.sigmoid(c)).astype(BF16)
    o_ref[...] = _dot(s, w_ref[...].astype(BF16)) + b_ref[...]


def _modulation(cvec, w_mod, b_mod):
    tn = 1536
    n = w_mod.shape[-1]
    return pl.pallas_call(
        _mod_kernel,
        grid=(DEPTH, n // tn),
        in_specs=[
            pl.BlockSpec((8, D_MODEL), lambda l, j: (0, 0)),
            pl.BlockSpec((None, D_MODEL, tn), lambda l, j: (l, 0, j)),
            pl.BlockSpec((None, 1, tn), lambda l, j: (l, 0, j)),
        ],
        out_specs=pl.BlockSpec((None, 8, tn), lambda l, j: (l, 0, j)),
        out_shape=jax.ShapeDtypeStruct((DEPTH, 8, n), F32),
        compiler_params=_cparams(("arbitrary", "arbitrary")),
        name="modulation",
    )(cvec, w_mod, b_mod.reshape(DEPTH, 1, n))


def _rope_slot(x, tab):
    c, s_up, s_dn = tab[:, 0:SLOT], tab[:, SLOT:2 * SLOT], tab[:, 2 * SLOT:3 * SLOT]
    half = MLA_ROPE_DIM // 4
    x_up = pltpu.roll(x, SLOT - half, axis=1)
    x_dn = pltpu.roll(x, half, axis=1)
    return x * c + x_up * s_up + x_dn * s_dn


def _inproj_kernel(*refs, rope, ctx_out):
    if rope:
        (x_ref, sh_ref, sc_ref, g_ref, win_ref, qn_ref, wuq_ref, kvn_ref, wukv_ref, tab_ref), outs = refs[:10], refs[10:]
    else:
        (x_ref, sh_ref, sc_ref, g_ref, win_ref, qn_ref, wuq_ref, kvn_ref, wukv_ref), outs = refs[:9], refs[9:]
        tab_ref = None
    if ctx_out:
        u_ref, nq_ref, nk_ref, nv_ref, qm_ref, km_ref, vm_ref, gate_ref, ckv_ref, kpe_ref = outs
    else:
        u_ref, nq_ref, nk_ref, nv_ref, qm_ref, km_ref, vm_ref, gate_ref = outs

    x = x_ref[...]
    h = (_rms(x, g_ref[...]) * (1.0 + sc_ref[...]) + sh_ref[...]).astype(BF16)

    def seg(a, b):
        return _dot(h, win_ref[:, a:b])

    u_ref[...] = seg(C_U, C_NQ).astype(u_ref.dtype)
    nq_ref[...] = (seg(C_NQ, C_NK) * (NA_HEAD_DIM ** -0.5)).astype(BF16)
    nk_ref[...] = seg(C_NK, C_NV).astype(nk_ref.dtype)
    nv_ref[...] = seg(C_NV, C_CQ).astype(nv_ref.dtype)
    gate_ref[...] = jax.nn.sigmoid(seg(C_G, C_KPE)).astype(BF16)

    cq = _rms(seg(C_CQ, C_CKV), qn_ref[...]).astype(BF16)
    q = _dot(cq, wuq_ref[...])
    tab = tab_ref[...] if rope else None
    for hd in range(MLA_HEADS):
        qs = q[:, hd * SLOT:(hd + 1) * SLOT]
        if rope:
            qs = _rope_slot(qs, tab)
        qm_ref[:, hd * SLOT:(hd + 1) * SLOT] = (qs * (MLA_QK_DIM ** -0.5)).astype(BF16)

    ckv = _rms(seg(C_CKV, C_G), kvn_ref[...])
    kpe_slot = seg(C_KPE, C_END)
    if ctx_out:
        ckv_ref[...] = ckv
        kpe_ref[...] = kpe_slot[:, KPE_LANE:KPE_LANE + MLA_ROPE_DIM]
    if rope:
        kpe_slot = _rope_slot(kpe_slot, tab)
    kv = _dot(ckv.astype(BF16), wukv_ref[...])
    for hd in range(MLA_HEADS):
        km_ref[:, hd * SLOT:(hd + 1) * SLOT] = (kv[:, hd * SLOT:(hd + 1) * SLOT] + kpe_slot).astype(BF16)
    vm_ref[...] = kv[:, MLA_HEADS * SLOT:].astype(BF16)


def _inproj(x, mod5, l, w, *, seq, rope_tab, ctx_out, tm):
    n = x.shape[0]
    steps_per_seq = seq // tm if not ctx_out else 0

    def mrow(i):
        return 0 if ctx_out else 1 + i // steps_per_seq

    const = lambda i: (l, 0, 0)
    in_specs = [
        pl.BlockSpec((tm, D_MODEL), lambda i: (i, 0)),
        pl.BlockSpec((None, None, None, 1, D_MODEL), lambda i: (l, mrow(i), 0, 0, 0)),
        pl.BlockSpec((None, None, None, 1, D_MODEL), lambda i: (l, mrow(i), 1, 0, 0)),
        pl.BlockSpec((None, 1, D_MODEL), const),
        pl.BlockSpec((None, D_MODEL, C_END), const),
        pl.BlockSpec((None, 1, MLA_Q_LORA), const),
        pl.BlockSpec((None, MLA_Q_LORA, MLA_HEADS * SLOT), const),
        pl.BlockSpec((None, 1, MLA_KV_LORA), const),
        pl.BlockSpec((None, MLA_KV_LORA, MLA_HEADS * SLOT + 512), const),
    ]
    args = [x, mod5, mod5, w["g_pre_mix"], w["w_in"], w["q_norm"], w["w_uq"], w["kv_norm"], w["w_ukv"]]
    rope = rope_tab is not None
    if rope:
        in_specs.append(pl.BlockSpec((tm, 3 * SLOT), lambda i: (i % steps_per_seq, 0)))
        args.append(rope_tab)
    kdt = F32 if ctx_out else BF16
    widths = [(512, BF16), (512, BF16), (512, kdt), (512, kdt), (MLA_HEADS * SLOT, BF16),
              (MLA_HEADS * SLOT, BF16), (512, BF16), (3 * D_MODEL, BF16)]
    if ctx_out:
        widths += [(MLA_KV_LORA, F32), (MLA_ROPE_DIM, F32)]
    out_specs = [pl.BlockSpec((tm, wd), lambda i: (i, 0)) for wd, _ in widths]
    out_shape = [jax.ShapeDtypeStruct((n, wd), dt) for wd, dt in widths]
    return pl.pallas_call(
        functools.partial(_inproj_kernel, rope=rope, ctx_out=ctx_out),
        grid=(n // tm,),
        in_specs=in_specs,
        out_specs=out_specs,
        out_shape=out_shape,
        compiler_params=_cparams(("arbitrary",)),
        name="inproj_ctx" if ctx_out else "inproj_lat",
    )(*args)


def _attend(q_h, segs):
    scores = []
    for k, _, bias in segs:
        s = _dot_nt(q_h, k)
        if bias is not None:
            s = s + bias
        scores.append(s)
    m = scores[0].max(axis=-1, keepdims=True)
    for s in scores[1:]:
        m = jnp.maximum(m, s.max(axis=-1, keepdims=True))
    den = None
    acc = None
    for s, (_, v, _) in zip(scores, segs):
        p = jnp.exp(s - m)
        ps = p.sum(axis=-1, keepdims=True)
        pv = _dot(p.astype(BF16), v)
        den = ps if den is None else den + ps
        acc = pv if acc is None else acc + pv
    return acc / den


def _lane_lo():
    return lax.broadcasted_iota(jnp.int32, (1, SLOT), 1) < NA_HEAD_DIM


def _na_heads(q_ref, kv_segs, bias_ref, out_ref):
    lo = _lane_lo()
    for p in range(NA_HEADS // 2):
        q_pair = q_ref[:, p * SLOT:(p + 1) * SLOT]
        kvs = [(kg(p), vg(p)) for kg, vg in kv_segs]
        outs = []
        for e in range(2):
            q_h = jnp.where(lo if e == 0 else jnp.logical_not(lo), q_pair, jnp.zeros_like(q_pair))
            segs = []
            for si, (k, v) in enumerate(kvs):
                bias = None
                if si == 0 and bias_ref is not None:
                    bias = bias_ref[2 * p + e].astype(F32)
                segs.append((k, v, bias))
            outs.append(_attend(q_h, segs))
        out_ref[:, p * SLOT:(p + 1) * SLOT] = jnp.where(lo, outs[0], outs[1]).astype(out_ref.dtype)


def _mla_heads(q_ref, kv_segs, out_ref):
    lo = _lane_lo()
    for p in range(MLA_HEADS // 2):
        vs = [vg(p) for _, vg in kv_segs]
        outs = []
        for e in range(2):
            hd = 2 * p + e
            q_h = q_ref[:, hd * SLOT:(hd + 1) * SLOT]
            segs = [(kg(hd), v, None) for (kg, _), v in zip(kv_segs, vs)]
            outs.append(_attend(q_h, segs))
        out_ref[:, p * SLOT:(p + 1) * SLOT] = jnp.where(lo, outs[0], outs[1]).astype(out_ref.dtype)


def _slot_get(ref, rows=None):
    def get(i):
        if rows is None:
            return ref[:, i * SLOT:(i + 1) * SLOT].astype(BF16)
        return ref[rows, i * SLOT:(i + 1) * SLOT].astype(BF16)
    return get


def _attn_ctx_kernel(nq_ref, nk_ref, nv_ref, qm_ref, km_ref, vm_ref, yna_ref, ymla_ref):
    _na_heads(nq_ref, [(_slot_get(nk_ref), _slot_get(nv_ref))], None, yna_ref)
    _mla_heads(qm_ref, [(_slot_get(km_ref), _slot_get(vm_ref))], ymla_ref)


def _attn_ctx(nq, nk, nv, qm, km, vm, *, seq):
    n = nq.shape[0]
    spec = lambda wd: pl.BlockSpec((seq, wd), lambda b: (b, 0))
    return pl.pallas_call(
        _attn_ctx_kernel,
        grid=(n // seq,),
        in_specs=[spec(512), spec(512), spec(512), spec(1024), spec(1024), spec(512)],
        out_specs=[spec(512), spec(512)],
        out_shape=[jax.ShapeDtypeStruct((n, 512), BF16)] * 2,
        compiler_params=_cparams(("arbitrary",)),
        name="attn_ctx",
    )(nq, nk, nv, qm, km, vm)


def _attn_na_kernel(nq_ref, nk_ref, nv_ref, ck_ref, cv_ref, bias_ref, out_ref, *, n_blocks):
    i = pl.program_id(1)
    ks = jnp.clip(NA_BLOCK_ROWS * i - NA_WIN_ROWS // 2, 0, n_blocks * NA_BLOCK_ROWS - NA_SLAB_ROWS)
    rows = pl.ds(pl.multiple_of(ks * GRID_W, GRID_W), NA_SLAB_ROWS * GRID_W)
    _na_heads(nq_ref,
              [(_slot_get(nk_ref, rows), _slot_get(nv_ref, rows)), (_slot_get(ck_ref), _slot_get(cv_ref))],
              bias_ref, out_ref)


def _attn_na(nq, nk, nv, cache_k, cache_v, bias_tab, l, *, batch, seq):
    tq = NA_BLOCK_ROWS * GRID_W
    nb = seq // tq
    past = cache_k.shape[2]

    def cls(i):
        return jnp.where(i == 0, 0, jnp.where(i == nb - 1, 2, 1))

    return pl.pallas_call(
        functools.partial(_attn_na_kernel, n_blocks=nb),
        grid=(batch, nb),
        in_specs=[
            pl.BlockSpec((tq, 512), lambda b, i: (b * nb + i, 0)),
            pl.BlockSpec((seq, 512), lambda b, i: (b, 0)),
            pl.BlockSpec((seq, 512), lambda b, i: (b, 0)),
            pl.BlockSpec((None, None, past, 512), lambda b, i: (b, l, 0, 0)),
            pl.BlockSpec((None, None, past, 512), lambda b, i: (b, l, 0, 0)),
            pl.BlockSpec((None, NA_HEADS, tq, NA_SLAB_ROWS * GRID_W), lambda b, i: (cls(i), 0, 0, 0)),
        ],
        out_specs=pl.BlockSpec((tq, 512), lambda b, i: (b * nb + i, 0)),
        out_shape=jax.ShapeDtypeStruct((batch * seq, 512), BF16),
        compiler_params=_cparams(("arbitrary", "arbitrary")),
        name="attn_na_lat",
    )(nq, nk, nv, cache_k, cache_v, bias_tab)


def _kvup_kernel(ckv_ref, kpe_ref, wukv_ref, place_ref, km_ref, vm_ref):
    kv = _dot(ckv_ref[...].astype(BF16), wukv_ref[...])
    kslots = kv[:, :MLA_HEADS * SLOT] + _dot(kpe_ref[...].astype(BF16), place_ref[...])
    km_ref[...] = kslots.astype(BF16)
    vm_ref[...] = kv[:, MLA_HEADS * SLOT:].astype(BF16)


def _kvup(cache_ckv, cache_kpe, w_ukv, place, l):
    batch, _, past, _ = cache_ckv.shape
    return pl.pallas_call(
        _kvup_kernel,
        grid=(batch,),
        in_specs=[
            pl.BlockSpec((None, None, past, MLA_KV_LORA), lambda b: (b, l, 0, 0)),
            pl.BlockSpec((None, None, past, MLA_ROPE_DIM), lambda b: (b, l, 0, 0)),
            pl.BlockSpec((None, MLA_KV_LORA, MLA_HEADS * SLOT + 512), lambda b: (l, 0, 0)),
            pl.BlockSpec((MLA_ROPE_DIM, MLA_HEADS * SLOT), lambda b: (0, 0)),
        ],
        out_specs=[pl.BlockSpec((None, past, MLA_HEADS * SLOT), lambda b: (b, 0, 0)),
                   pl.BlockSpec((None, past, 512), lambda b: (b, 0, 0))],
        out_shape=[jax.ShapeDtypeStruct((batch, past, MLA_HEADS * SLOT), BF16),
                   jax.ShapeDtypeStruct((batch, past, 512), BF16)],
        compiler_params=_cparams(("arbitrary",)),
        name="kv_up_cache",
    )(cache_ckv, cache_kpe, w_ukv, place)


def _attn_mla_kernel(qm_ref, km_ref, vm_ref, ckm_ref, cvm_ref, out_ref):
    _mla_heads(qm_ref, [(_slot_get(km_ref), _slot_get(vm_ref)), (_slot_get(ckm_ref), _slot_get(cvm_ref))], out_ref)


def _attn_mla(qm, km, vm, ckm, cvm, *, batch, seq, tq):
    nb = seq // tq
    past = ckm.shape[1]
    return pl.pallas_call(
        _attn_mla_kernel,
        grid=(batch, nb),
        in_specs=[
            pl.BlockSpec((tq, MLA_HEADS * SLOT), lambda b, i: (b * nb + i, 0)),
            pl.BlockSpec((seq, MLA_HEADS * SLOT), lambda b, i: (b, 0)),
            pl.BlockSpec((seq, 512), lambda b, i: (b, 0)),
            pl.BlockSpec((None, past, MLA_HEADS * SLOT), lambda b, i: (b, 0, 0)),
            pl.BlockSpec((None, past, 512), lambda b, i: (b, 0, 0)),
        ],
        out_specs=pl.BlockSpec((tq, 512), lambda b, i: (b * nb + i, 0)),
        out_shape=jax.ShapeDtypeStruct((batch * seq, 512), BF16),
        compiler_params=_cparams(("arbitrary", "arbitrary")),
        name="attn_mla_lat",
    )(qm, km, vm, ckm, cvm)


POOL_HALO = 16


def _route(logits):
    lane = lax.broadcasted_iota(jnp.int32, logits.shape, 1).astype(F32)
    big = jnp.float32(1 << 20)
    is_g = lane < N_GROUPS
    gl = jnp.where(is_g, logits, NEG_INF)
    gmax = gl.max(axis=-1, keepdims=True)
    ge = jnp.where(is_g, jnp.exp(gl - gmax), 0.0)
    gp = ge / ge.sum(axis=-1, keepdims=True)
    g_w = jnp.where(is_g, gp, -1.0).max(axis=-1, keepdims=True)
    g_idx = jnp.where(is_g & (gp == g_w), lane, big).min(axis=-1, keepdims=True)
    e_lane = lane - N_GROUPS
    in_grp = (e_lane >= g_idx * EXPERTS_PER_GROUP) & (e_lane < (g_idx + 1) * EXPERTS_PER_GROUP)
    el = jnp.where(in_grp, logits, NEG_INF)
    emax = el.max(axis=-1, keepdims=True)
    ee = jnp.where(in_grp, jnp.exp(el - emax), 0.0)
    ep = ee / ee.sum(axis=-1, keepdims=True)
    p1 = jnp.where(in_grp, ep, -1.0).max(axis=-1, keepdims=True)
    i1 = jnp.where(in_grp & (ep == p1), lane, big).min(axis=-1, keepdims=True)
    rest = in_grp & (lane != i1)
    p2 = jnp.where(rest, ep, -1.0).max(axis=-1, keepdims=True)
    i2 = jnp.where(rest & (ep == p2), lane, big).min(axis=-1, keepdims=True)
    tot = p1 + p2
    gates = jnp.where(lane == i1, g_w * (p1 / tot), 0.0) + jnp.where(lane == i2, g_w * (p2 / tot), 0.0)
    return pltpu.roll(gates, LANES - N_GROUPS, axis=1)


def _merge_kernel(x_ref, u_ref, yna_ref, ymla_ref, gate_ref, gt1_ref, sh2_ref, sc2_ref, gpost_ref, gpre_ref,
                  wpool_ref, pscale_ref, wbr_ref, wout_ref, wr2_ref, wr1_ref,
                  xo_ref, h2_ref, rg_ref, *, seq, tm, steps_per_seq):
    nwin = min(tm + 2 * POOL_HALO, seq)
    if steps_per_seq == 1:
        r0 = 0
        s0 = 0
        uwin = u_ref[...]
        utile = uwin
    else:
        r0 = (pl.program_id(0) % steps_per_seq) * tm
        s0 = pl.multiple_of(jnp.clip(r0 - POOL_HALO, 0, seq - nwin), POOL_HALO)
        uwin = u_ref[pl.ds(s0, nwin), :]
        utile = u_ref[pl.ds(pl.multiple_of(r0, POOL_HALO), tm), :]
    t = r0 + lax.broadcasted_iota(jnp.int32, (tm, nwin), 0)
    s = s0 + lax.broadcasted_iota(jnp.int32, (tm, nwin), 1)
    t1 = r0 + lax.broadcasted_iota(jnp.int32, (tm, 1), 0)
    ds = []
    for g, w in enumerate(POOL_WINDOWS):
        lo = jnp.maximum(t - w // 2, 0)
        hi = jnp.minimum(t + (w - w // 2), seq)
        band = ((s >= lo) & (s < hi)).astype(BF16)
        cnt = (jnp.minimum(t1 + (w - w // 2), seq) - jnp.maximum(t1 - w // 2, 0)).astype(F32)
        sl = slice(g * POOL_GC, (g + 1) * POOL_GC)
        pooled = _dot(band, uwin[:, sl]) / cnt
        ds.append(pooled - utile[:, sl].astype(F32))
    d = jnp.concatenate(ds, axis=-1).astype(BF16)
    y_pool = (_dot(d, wpool_ref[...]) * pscale_ref[...]).astype(BF16)

    g = gate_ref[...]
    merged = (g[:, 0:D_MODEL].astype(F32) * _dot(y_pool, wbr_ref[0])
              + g[:, D_MODEL:2 * D_MODEL].astype(F32) * _dot(yna_ref[...], wbr_ref[1])
              + g[:, 2 * D_MODEL:].astype(F32) * _dot(ymla_ref[...], wbr_ref[2]))
    y = _dot(merged.astype(BF16), wout_ref[...])
    xn = x_ref[...] + gt1_ref[...] * _rms(y, gpost_ref[...])
    xo_ref[...] = xn

    h2 = _rms(xn, gpre_ref[...]) * (1.0 + sc2_ref[...]) + sh2_ref[...]
    h_hi = h2.astype(BF16)
    h2_ref[...] = h_hi
    h_lo = (h2 - h_hi.astype(F32)).astype(BF16)
    two = _dot(h_hi, wr2_ref[...])
    logits = two[:, :LANES] + two[:, LANES:] + _dot(h_lo, wr1_ref[...])
    rg_ref[...] = _route(logits)


def _merge(x, u, yna, ymla, gate, mod5, l, w, *, seq, ctx, tm):
    n = x.shape[0]
    steps_per_seq = seq // tm
    nseq = n // seq

    def mrow(i):
        return 0 if ctx else 1 + i // steps_per_seq

    const = lambda i: (l, 0, 0)
    modspec = lambda k: pl.BlockSpec((None, None, None, 1, D_MODEL), lambda i: (l, mrow(i), k, 0, 0))
    row = lambda wd: pl.BlockSpec((tm, wd), lambda i: (i, 0))
    return pl.pallas_call(
        functools.partial(_merge_kernel, seq=seq, tm=tm, steps_per_seq=steps_per_seq),
        grid=(n // tm,),
        in_specs=[
            row(D_MODEL),
            pl.BlockSpec((None, seq, 512), lambda i: (i // steps_per_seq, 0, 0)),
            row(512), row(512), row(3 * D_MODEL),
            modspec(2), modspec(3), modspec(4),
            pl.BlockSpec((None, 1, D_MODEL), const),
            pl.BlockSpec((None, 1, D_MODEL), const),
            pl.BlockSpec((None, 512, 512), const),
            pl.BlockSpec((None, 1, 512), const),
            pl.BlockSpec((None, 3, 512, D_MODEL), lambda i: (l, 0, 0, 0)),
            pl.BlockSpec((None, D_MODEL, D_MODEL), const),
            pl.BlockSpec((None, D_MODEL, 2 * LANES), const),
            pl.BlockSpec((None, D_MODEL, LANES), const),
        ],
        out_specs=[row(D_MODEL), row(D_MODEL), row(LANES)],
        out_shape=[jax.ShapeDtypeStruct((n, D_MODEL), F32), jax.ShapeDtypeStruct((n, D_MODEL), BF16),
                   jax.ShapeDtypeStruct((n, LANES), F32)],
        compiler_params=_cparams(("arbitrary",)),
        name="merge_ctx" if ctx else "merge_lat",
    )(x, u.reshape(nseq, seq, 512), yna, ymla, gate, mod5, mod5, mod5, w["g_post_mix"], w["g_pre_ffn"],
      w["w_pool_bd"], w["pool_scale"], w["w_branch"], w["w_out"], w["w_r2"], w["w_r1"])


MOE_CHUNK = 4


def _moe_kernel(h_ref, rg_ref, x_ref, gt2_ref, gpost_ref, w1_ref, w3_ref, w2_ref, o_ref, acc_ref):
    c = pl.program_id(1)
    h = h_ref[...]
    rg = rg_ref[...]
    lane = lax.broadcasted_iota(jnp.int32, rg.shape, 1)
    part = None
    for j in range(MOE_CHUNK):
        gate = jnp.where(lane == c * MOE_CHUNK + j, rg, 0.0).sum(axis=-1, keepdims=True)
        a = _dot(h, w1_ref[j])
        b = _dot(h, w3_ref[j])
        hid = (a * jax.nn.sigmoid(a)) * b * gate
        y = _dot(hid.astype(BF16), w2_ref[j])
        part = y if part is None else part + y

    @pl.when(c == 0)
    def _():
        acc_ref[...] = part

    @pl.when(c > 0)
    def _():
        acc_ref[...] += part

    @pl.when(c == pl.num_programs(1) - 1)
    def _():
        o_ref[...] = x_ref[...] + gt2_ref[...] * _rms(acc_ref[...], gpost_ref[...])


def _moe(h2, rg, x, mod5, l, w, *, seq, ctx, tm):
    n = x.shape[0]
    steps_per_seq = seq // tm if not ctx else 0

    def mrow(i):
        return 0 if ctx else 1 + i // steps_per_seq

    nchunk = N_EXPERTS // MOE_CHUNK
    row = lambda wd: pl.BlockSpec((tm, wd), lambda i, c: (i, 0))
    wspec = lambda a, b: pl.BlockSpec((None, MOE_CHUNK, a, b), lambda i, c: (l, c, 0, 0))
    return pl.pallas_call(
        _moe_kernel,
        grid=(n // tm, nchunk),
        in_specs=[
            row(D_MODEL), row(LANES), row(D_MODEL),
            pl.BlockSpec((None, None, None, 1, D_MODEL), lambda i, c: (l, mrow(i), 5, 0, 0)),
            pl.BlockSpec((None, 1, D_MODEL), lambda i, c: (l, 0, 0)),
            wspec(D_MODEL, EXPERT_FF), wspec(D_MODEL, EXPERT_FF), wspec(EXPERT_FF, D_MODEL),
        ],
        out_specs=row(D_MODEL),
        out_shape=jax.ShapeDtypeStruct((n, D_MODEL), F32),
        scratch_shapes=[pltpu.VMEM((tm, D_MODEL), F32)],
        compiler_params=_cparams(("arbitrary", "arbitrary")),
        name="moe_ctx" if ctx else "moe_lat",
    )(h2, rg, x, mod5, w["g_post_ffn"], w["w1"], w["w3"], w["w2"])


def _rope_table(seq):
    t = np.arange(seq)
    n_freq = MLA_ROPE_DIM // 4
    inv = jnp.asarray(ROPE_BASE, F32) ** (-jnp.arange(n_freq, dtype=F32) / n_freq)
    ang_r = jnp.asarray(t // GRID_W, F32)[:, None] * inv
    ang_c = jnp.asarray(t % GRID_W, F32)[:, None] * inv
    cos = jnp.cos(jnp.stack([ang_r, ang_c], axis=1))
    sin = jnp.sin(jnp.stack([ang_r, ang_c], axis=1))
    zeros = jnp.zeros_like(sin)
    c32 = jnp.stack([cos, cos], axis=2).reshape(seq, MLA_ROPE_DIM)
    up32 = jnp.stack([-sin, zeros], axis=2).reshape(seq, MLA_ROPE_DIM)
    dn32 = jnp.stack([zeros, sin], axis=2).reshape(seq, MLA_ROPE_DIM)

    def slot(v, fill):
        return jnp.concatenate([jnp.full((seq, KPE_LANE), fill, F32), v,
                                jnp.full((seq, SLOT - KPE_LANE - MLA_ROPE_DIM), fill, F32)], axis=1)

    return jnp.concatenate([slot(c32, 1.0), slot(up32, 0.0), slot(dn32, 0.0)], axis=1)


def _na_bias_table(rpb, n_rows):
    i = np.arange(NA_BLOCK_ROWS)[:, None]
    j = np.arange(NA_SLAB_ROWS)[None, :]
    last_ks = n_rows - NA_SLAB_ROWS
    last_q0 = n_rows - NA_BLOCK_ROWS
    specs = [
        (j < NA_WIN_ROWS, j - i),
        ((j >= i) & (j < i + NA_WIN_ROWS), j - NA_WIN_ROWS // 2 - i),
        (last_ks + j >= n_rows - NA_WIN_ROWS, last_ks + j - last_q0 - i),
    ]
    c = np.arange(GRID_W)[:, None]
    kc = np.arange(GRID_W)[None, :]
    c0 = np.clip(c - NA_WIN_COLS // 2, 0, GRID_W - NA_WIN_COLS)
    in_col = (kc >= c0) & (kc < c0 + NA_WIN_COLS)
    dc = np.clip(kc - c, -(NA_WIN_COLS - 1), NA_WIN_COLS - 1) + NA_WIN_COLS - 1
    tabs = []
    for in_row, dr in specs:
        dr_idx = np.clip(dr + NA_WIN_ROWS - 1, 0, 2 * NA_WIN_ROWS - 2)
        b = rpb[:, :, dr_idx[:, None, :, None], dc[None, :, None, :]]
        ok = in_row[:, None, :, None] & in_col[None, :, None, :]
        b = jnp.where(ok[None, None], b, NEG_INF)
        tabs.append(b.reshape(DEPTH, NA_HEADS, NA_BLOCK_ROWS * GRID_W, NA_SLAB_ROWS * GRID_W))
    return jnp.stack(tabs, axis=1).astype(BF16)


def _prep(p):
    L = DEPTH
    w_in = p["w_in"]
    kpe0 = C_G
    kpe1 = C_G + MLA_ROPE_DIM
    w_in_p = jnp.concatenate(
        [w_in[..., :kpe0], w_in[..., kpe1:], jnp.zeros((L, D_MODEL, KPE_LANE), F32), w_in[..., kpe0:kpe1],
         jnp.zeros((L, D_MODEL, SLOT - KPE_LANE - MLA_ROPE_DIM), F32)], axis=-1).astype(BF16)
    w_uq = jnp.pad(p["mla_w_uq"].reshape(L, MLA_Q_LORA, MLA_HEADS, MLA_QK_DIM),
                   ((0, 0), (0, 0), (0, 0), (0, SLOT - MLA_QK_DIM))).reshape(L, MLA_Q_LORA, MLA_HEADS * SLOT)
    ukv = p["mla_w_ukv"].reshape(L, MLA_KV_LORA, MLA_HEADS, MLA_NOPE_DIM + MLA_V_DIM)
    uk = jnp.pad(ukv[..., :MLA_NOPE_DIM], ((0, 0), (0, 0), (0, 0), (0, SLOT - MLA_NOPE_DIM)))
    w_ukv = jnp.concatenate([uk.reshape(L, MLA_KV_LORA, MLA_HEADS * SLOT),
                             ukv[..., MLA_NOPE_DIM:].reshape(L, MLA_KV_LORA, MLA_HEADS * MLA_V_DIM)], axis=-1)
    w_pool_bd = jnp.einsum("lgcd,gh->lgchd", p["w_pool"], jnp.eye(4, dtype=F32)).reshape(L, POOL_WIDTH, POOL_WIDTH)
    w_r = jnp.concatenate([p["w_group_router"], p["w_expert_router"]], axis=-1)
    r_hi = w_r.astype(BF16)
    r_lo = (w_r - r_hi.astype(F32)).astype(BF16)
    padl = lambda a: jnp.pad(a, ((0, 0), (0, 0), (0, LANES - a.shape[-1])))
    vec = lambda a: a.reshape(L, 1, a.shape[-1])
    return {
        "w_in": w_in_p,
        "w_uq": w_uq.astype(BF16),
        "w_ukv": w_ukv.astype(BF16),
        "w_pool_bd": w_pool_bd.astype(BF16),
        "w_branch": p["w_branch"].astype(BF16),
        "w_out": p["w_out"].astype(BF16),
        "w_r2": jnp.concatenate([padl(r_hi), padl(r_lo)], axis=-1),
        "w_r1": padl(r_hi),
        "w1": p["w_exp_gate"].astype(BF16),
        "w3": p["w_exp_up"].astype(BF16),
        "w2": p["w_exp_down"].astype(BF16),
        "g_pre_mix": vec(p["g_pre_mix"]), "g_post_mix": vec(p["g_post_mix"]),
        "g_pre_ffn": vec(p["g_pre_ffn"]), "g_post_ffn": vec(p["g_post_ffn"]),
        "pool_scale": vec(p["pool_scale"]), "q_norm": vec(p["mla_q_norm"]), "kv_norm": vec(p["mla_kv_norm"]),
    }


def kernel(x_prompt, x_sample, cache_na_k, cache_na_v, cache_mla_ckv, cache_mla_kpe, c, c_ctx, w_mod, b_mod,
           g_pre_mix, g_post_mix, g_pre_ffn, g_post_ffn, w_in, w_pool, pool_scale, na_rpb, mla_q_norm, mla_w_uq,
           mla_kv_norm, mla_w_ukv, w_branch, w_out, w_group_router, w_expert_router, w_exp_gate, w_exp_up,
           w_exp_down):
    batch, seq, _ = x_prompt.shape
    dbatch, dseq, _ = x_sample.shape
    past = cache_na_k.shape[2]
    assert dbatch + 1 <= 8 and dseq % (NA_BLOCK_ROWS * GRID_W) == 0

    w = _prep(dict(w_in=w_in, w_pool=w_pool, pool_scale=pool_scale, mla_q_norm=mla_q_norm, mla_w_uq=mla_w_uq,
                   mla_kv_norm=mla_kv_norm, mla_w_ukv=mla_w_ukv, w_branch=w_branch, w_out=w_out,
                   w_group_router=w_group_router, w_expert_router=w_expert_router, w_exp_gate=w_exp_gate,
                   w_exp_up=w_exp_up, w_exp_down=w_exp_down, g_pre_mix=g_pre_mix, g_post_mix=g_post_mix,
                   g_pre_ffn=g_pre_ffn, g_post_ffn=g_post_ffn))
    rope_tab = _rope_table(dseq)
    bias_tab = _na_bias_table(na_rpb, dseq // GRID_W)
    place = jnp.tile(jnp.pad(jnp.eye(MLA_ROPE_DIM, dtype=F32), ((0, 0), (KPE_LANE, SLOT - KPE_LANE - MLA_ROPE_DIM))),
                     (1, MLA_HEADS)).astype(BF16)

    cvec = jnp.concatenate([c_ctx[None], c, jnp.zeros((8 - 1 - dbatch, D_MODEL), F32)], axis=0)
    mod5 = _modulation(cvec, w_mod, b_mod).reshape(DEPTH, 8, 6, 1, D_MODEL)

    ck = cache_na_k.reshape(dbatch, DEPTH, past, NA_WIDTH)
    cv = cache_na_v.reshape(dbatch, DEPTH, past, NA_WIDTH)

    xp = x_prompt.reshape(batch * seq, D_MODEL)
    nk_l, nv_l, ckv_l, kpe_l = [], [], [], []
    for l in range(DEPTH):
        u, nq, nk, nv, qm, km, vm, gate, ckv, kpe = _inproj(xp, mod5, l, w, seq=seq, rope_tab=None, ctx_out=True, tm=256)
        nk_l.append(nk)
        nv_l.append(nv)
        ckv_l.append(ckv)
        kpe_l.append(kpe)
        yna, ymla = _attn_ctx(nq, nk, nv, qm, km, vm, seq=seq)
        xp, h2, rg = _merge(xp, u, yna, ymla, gate, mod5, l, w, seq=seq, ctx=True, tm=seq)
        xp = _moe(h2, rg, xp, mod5, l, w, seq=seq, ctx=True, tm=1024)

    xs = x_sample.reshape(dbatch * dseq, D_MODEL)
    for l in range(DEPTH):
        u, nq, nk, nv, qm, km, vm, gate = _inproj(xs, mod5, l, w, seq=dseq, rope_tab=rope_tab, ctx_out=False, tm=256)
        yna = _attn_na(nq, nk, nv, ck, cv, bias_tab[l], l, batch=dbatch, seq=dseq)
        ckm, cvm = _kvup(cache_mla_ckv, cache_mla_kpe, w["w_ukv"], place, l)
        ymla = _attn_mla(qm, km, vm, ckm, cvm, batch=dbatch, seq=dseq, tq=256)
        xs, h2, rg = _merge(xs, u, yna, ymla, gate, mod5, l, w, seq=dseq, ctx=False, tm=256)
        xs = _moe(h2, rg, xs, mod5, l, w, seq=dseq, ctx=False, tm=1024)

    stack = lambda parts, tail: jnp.stack([a.reshape((batch, seq) + tail) for a in parts], axis=1)
    return (xp.reshape(batch, seq, D_MODEL), xs.reshape(dbatch, dseq, D_MODEL),
            stack(nk_l, (NA_HEADS, NA_HEAD_DIM)), stack(nv_l, (NA_HEADS, NA_HEAD_DIM)),
            stack(ckv_l, (MLA_KV_LORA,)), stack(kpe_l, (MLA_ROPE_DIM,)))
```

```python
import functools

import numpy as np
import jax
import jax.numpy as jnp
from jax import lax
from jax.experimental import pallas as pl
from jax.experimental.pallas import tpu as pltpu

F32 = jnp.float32
BF16 = jnp.bfloat16

D_MODEL = 1024
DEPTH = 2
GRID_W = 64
POOL_WINDOWS = (2, 4, 8, 16)
POOL_GC = 128
POOL_WIDTH = 512
NA_HEADS = 8
NA_HEAD_DIM = 64
NA_WIDTH = 512
NA_WIN_ROWS = 8
NA_WIN_COLS = 16
MLA_HEADS = 8
MLA_NOPE_DIM = 64
MLA_ROPE_DIM = 32
MLA_V_DIM = 64
MLA_QK_DIM = 96
MLA_Q_LORA = 384
MLA_KV_LORA = 256
N_GROUPS = 4
EXPERTS_PER_GROUP = 4
N_EXPERTS = 16
EXPERT_FF = 256
ROPE_BASE = 10000.0
EPS = 1e-6
NEG_INF = -1e30

LANES = 128
SLOT = LANES
VMEM_LIMIT = 56 * 1024 * 1024

C_U, C_NQ, C_NK, C_NV, C_CQ, C_CKV, C_G, C_KPE, C_END = 0, 512, 1024, 1536, 2048, 2432, 2688, 5760, 5888
KPE_LANE = MLA_NOPE_DIM

NA_BLOCK_ROWS = 4
NA_SLAB_ROWS = 12


def _cparams(sem):
    return pltpu.CompilerParams(dimension_semantics=sem, vmem_limit_bytes=VMEM_LIMIT)


def _rms(x, g):
    return x * lax.rsqrt(jnp.mean(x * x, axis=-1, keepdims=True) + EPS) * g


def _dot(a, b):
    return jnp.dot(a, b, preferred_element_type=F32)


def _dot_nt(a, b):
    return lax.dot_general(a, b, (((1,), (1,)), ((), ())), preferred_element_type=F32)


def _mod_kernel(c_ref, w_ref, b_ref, o_ref):
    c = c_ref[...]
    s = (c * jax.nn.sigmoid(c)).astype(BF16)
    o_ref[...] = _dot(s, w_ref[...].astype(BF16)) + b_ref[...]


def _modulation(cvec, w_mod, b_mod):
    tn = 1536
    n = w_mod.shape[-1]
    return pl.pallas_call(
        _mod_kernel,
        grid=(DEPTH, n // tn),
        in_specs=[
            pl.BlockSpec((8, D_MODEL), lambda l, j: (0, 0)),
            pl.BlockSpec((None, D_MODEL, tn), lambda l, j: (l, 0, j)),
            pl.BlockSpec((None, 1, tn), lambda l, j: (l, 0, j)),
        ],
        out_specs=pl.BlockSpec((None, 8, tn), lambda l, j: (l, 0, j)),
        out_shape=jax.ShapeDtypeStruct((DEPTH, 8, n), F32),
        compiler_params=_cparams(("arbitrary", "arbitrary")),
        name="modulation",
    )(cvec, w_mod, b_mod.reshape(DEPTH, 1, n))


def _rope_slot(x, tab):
    c, s_up, s_dn = tab[:, 0:SLOT], tab[:, SLOT:2 * SLOT], tab[:, 2 * SLOT:3 * SLOT]
    half = MLA_ROPE_DIM // 4
    x_up = pltpu.roll(x, SLOT - half, axis=1)
    x_dn = pltpu.roll(x, half, axis=1)
    return x * c + x_up * s_up + x_dn * s_dn


def _inproj_kernel(*refs, rope, ctx_out):
    if rope:
        (x_ref, sh_ref, sc_ref, g_ref, win_ref, qn_ref, wuq_ref, kvn_ref, wukv_ref, tab_ref), outs = refs[:10], refs[10:]
    else:
        (x_ref, sh_ref, sc_ref, g_ref, win_ref, qn_ref, wuq_ref, kvn_ref, wukv_ref), outs = refs[:9], refs[9:]
        tab_ref = None
    if ctx_out:
        u_ref, nq_ref, nk_ref, nv_ref, qm_ref, km_ref, vm_ref, gate_ref, ckv_ref, kpe_ref = outs
    else:
        u_ref, nq_ref, nk_ref, nv_ref, qm_ref, km_ref, vm_ref, gate_ref = outs

    x = x_ref[...]
    h = (_rms(x, g_ref[...]) * (1.0 + sc_ref[...]) + sh_ref[...]).astype(BF16)

    def seg(a, b):
        return _dot(h, win_ref[:, a:b])

    u_ref[...] = seg(C_U, C_NQ).astype(u_ref.dtype)
    nq_ref[...] = (seg(C_NQ, C_NK) * (NA_HEAD_DIM ** -0.5)).astype(BF16)
    nk_ref[...] = seg(C_NK, C_NV).astype(nk_ref.dtype)
    nv_ref[...] = seg(C_NV, C_CQ).astype(nv_ref.dtype)
    gate_ref[...] = jax.nn.sigmoid(seg(C_G, C_KPE)).astype(BF16)

    cq = _rms(seg(C_CQ, C_CKV), qn_ref[...]).astype(BF16)
    q = _dot(cq, wuq_ref[...])
    tab = tab_ref[...] if rope else None
    for hd in range(MLA_HEADS):
        qs = q[:, hd * SLOT:(hd + 1) * SLOT]
        if rope:
            qs = _rope_slot(qs, tab)
        qm_ref[:, hd * SLOT:(hd + 1) * SLOT] = (qs * (MLA_QK_DIM ** -0.5)).astype(BF16)

    ckv = _rms(seg(C_CKV, C_G), kvn_ref[...])
    kpe_slot = seg(C_KPE, C_END)
    if ctx_out:
        ckv_ref[...] = ckv
        kpe_ref[...] = kpe_slot[:, KPE_LANE:KPE_LANE + MLA_ROPE_DIM]
    if rope:
        kpe_slot = _rope_slot(kpe_slot, tab)
    kv = _dot(ckv.astype(BF16), wukv_ref[...])
    for hd in range(MLA_HEADS):
        km_ref[:, hd * SLOT:(hd + 1) * SLOT] = (kv[:, hd * SLOT:(hd + 1) * SLOT] + kpe_slot).astype(BF16)
    vm_ref[...] = kv[:, MLA_HEADS * SLOT:].astype(BF16)


def _inproj(x, mod5, l, w, *, seq, rope_tab, ctx_out, tm):
    n = x.shape[0]
    steps_per_seq = seq // tm if not ctx_out else 0

    def mrow(i):
        return 0 if ctx_out else 1 + i // steps_per_seq

    const = lambda i: (l, 0, 0)
    in_specs = [
        pl.BlockSpec((tm, D_MODEL), lambda i: (i, 0)),
        pl.BlockSpec((None, None, None, 1, D_MODEL), lambda i: (l, mrow(i), 0, 0, 0)),
        pl.BlockSpec((None, None, None, 1, D_MODEL), lambda i: (l, mrow(i), 1, 0, 0)),
        pl.BlockSpec((None, 1, D_MODEL), const),
        pl.BlockSpec((None, D_MODEL, C_END), const),
        pl.BlockSpec((None, 1, MLA_Q_LORA), const),
        pl.BlockSpec((None, MLA_Q_LORA, MLA_HEADS * SLOT), const),
        pl.BlockSpec((None, 1, MLA_KV_LORA), const),
        pl.BlockSpec((None, MLA_KV_LORA, MLA_HEADS * SLOT + 512), const),
    ]
    args = [x, mod5, mod5, w["g_pre_mix"], w["w_in"], w["q_norm"], w["w_uq"], w["kv_norm"], w["w_ukv"]]
    rope = rope_tab is not None
    if rope:
        in_specs.append(pl.BlockSpec((tm, 3 * SLOT), lambda i: (i % steps_per_seq, 0)))
        args.append(rope_tab)
    kdt = F32 if ctx_out else BF16
    widths = [(512, BF16), (512, BF16), (512, kdt), (512, kdt), (MLA_HEADS * SLOT, BF16),
              (MLA_HEADS * SLOT, BF16), (512, BF16), (3 * D_MODEL, BF16)]
    if ctx_out:
        widths += [(MLA_KV_LORA, F32), (MLA_ROPE_DIM, F32)]
    out_specs = [pl.BlockSpec((tm, wd), lambda i: (i, 0)) for wd, _ in widths]
    out_shape = [jax.ShapeDtypeStruct((n, wd), dt) for wd, dt in widths]
    return pl.pallas_call(
        functools.partial(_inproj_kernel, rope=rope, ctx_out=ctx_out),
        grid=(n // tm,),
        in_specs=in_specs,
        out_specs=out_specs,
        out_shape=out_shape,
        compiler_params=_cparams(("arbitrary",)),
        name="inproj_ctx" if ctx_out else "inproj_lat",
    )(*args)


def _attend(q_h, segs):
    scores = []
    for k, _, bias in segs:
        s = _dot_nt(q_h, k)
        if bias is not None:
            s = s + bias
        scores.append(s)
    m = scores[0].max(axis=-1, keepdims=True)
    for s in scores[1:]:
        m = jnp.maximum(m, s.max(axis=-1, keepdims=True))
    den = None
    acc = None
    for s, (_, v, _) in zip(scores, segs):
        p = jnp.exp(s - m)
        ps = p.sum(axis=-1, keepdims=True)
        pv = _dot(p.astype(BF16), v)
        den = ps if den is None else den + ps
        acc = pv if acc is None else acc + pv
    return acc / den


def _lane_lo():
    return lax.broadcasted_iota(jnp.int32, (1, SLOT), 1) < NA_HEAD_DIM


def _na_heads(q_ref, kv_segs, bias_ref, out_ref):
    lo = _lane_lo()
    for p in range(NA_HEADS // 2):
        q_pair = q_ref[:, p * SLOT:(p + 1) * SLOT]
        kvs = [(kg(p), vg(p)) for kg, vg in kv_segs]
        outs = []
        for e in range(2):
            q_h = jnp.where(lo if e == 0 else jnp.logical_not(lo), q_pair, jnp.zeros_like(q_pair))
            segs = []
            for si, (k, v) in enumerate(kvs):
                bias = None
                if si == 0 and bias_ref is not None:
                    bias = bias_ref[2 * p + e].astype(F32)
                segs.append((k, v, bias))
            outs.append(_attend(q_h, segs))
        out_ref[:, p * SLOT:(p + 1) * SLOT] = jnp.where(lo, outs[0], outs[1]).astype(out_ref.dtype)


def _mla_heads(q_ref, kv_segs, out_ref):
    lo = _lane_lo()
    for p in range(MLA_HEADS // 2):
        vs = [vg(p) for _, vg in kv_segs]
        outs = []
        for e in range(2):
            hd = 2 * p + e
            q_h = q_ref[:, hd * SLOT:(hd + 1) * SLOT]
            segs = [(kg(hd), v, None) for (kg, _), v in zip(kv_segs, vs)]
            outs.append(_attend(q_h, segs))
        out_ref[:, p * SLOT:(p + 1) * SLOT] = jnp.where(lo, outs[0], outs[1]).astype(out_ref.dtype)


def _slot_get(ref, rows=None):
    def get(i):
        if rows is None:
            return ref[:, i * SLOT:(i + 1) * SLOT].astype(BF16)
        return ref[rows, i * SLOT:(i + 1) * SLOT].astype(BF16)
    return get


def _attn_ctx_kernel(nq_ref, nk_ref, nv_ref, qm_ref, km_ref, vm_ref, yna_ref, ymla_ref):
    _na_heads(nq_ref, [(_slot_get(nk_ref), _slot_get(nv_ref))], None, yna_ref)
    _mla_heads(qm_ref, [(_slot_get(km_ref), _slot_get(vm_ref))], ymla_ref)


def _attn_ctx(nq, nk, nv, qm, km, vm, *, seq):
    n = nq.shape[0]
    spec = lambda wd: pl.BlockSpec((seq, wd), lambda b: (b, 0))
    return pl.pallas_call(
        _attn_ctx_kernel,
        grid=(n // seq,),
        in_specs=[spec(512), spec(512), spec(512), spec(1024), spec(1024), spec(512)],
        out_specs=[spec(512), spec(512)],
        out_shape=[jax.ShapeDtypeStruct((n, 512), BF16)] * 2,
        compiler_params=_cparams(("arbitrary",)),
        name="attn_ctx",
    )(nq, nk, nv, qm, km, vm)


def _attn_na_kernel(nq_ref, nk_ref, nv_ref, ck_ref, cv_ref, bias_ref, out_ref, *, n_blocks):
    i = pl.program_id(1)
    ks = jnp.clip(NA_BLOCK_ROWS * i - NA_WIN_ROWS // 2, 0, n_blocks * NA_BLOCK_ROWS - NA_SLAB_ROWS)
    rows = pl.ds(pl.multiple_of(ks * GRID_W, GRID_W), NA_SLAB_ROWS * GRID_W)
    _na_heads(nq_ref,
              [(_slot_get(nk_ref, rows), _slot_get(nv_ref, rows)), (_slot_get(ck_ref), _slot_get(cv_ref))],
              bias_ref, out_ref)


def _attn_na(nq, nk, nv, cache_k, cache_v, bias_tab, l, *, batch, seq):
    tq = NA_BLOCK_ROWS * GRID_W
    nb = seq // tq
    past = cache_k.shape[2]

    def cls(i):
        return jnp.where(i == 0, 0, jnp.where(i == nb - 1, 2, 1))

    return pl.pallas_call(
        functools.partial(_attn_na_kernel, n_blocks=nb),
        grid=(batch, nb),
        in_specs=[
            pl.BlockSpec((tq, 512), lambda b, i: (b * nb + i, 0)),
            pl.BlockSpec((seq, 512), lambda b, i: (b, 0)),
            pl.BlockSpec((seq, 512), lambda b, i: (b, 0)),
            pl.BlockSpec((None, None, past, 512), lambda b, i: (b, l, 0, 0)),
            pl.BlockSpec((None, None, past, 512), lambda b, i: (b, l, 0, 0)),
            pl.BlockSpec((None, None, NA_HEADS, tq, NA_SLAB_ROWS * GRID_W), lambda b, i: (l, cls(i), 0, 0, 0)),
        ],
        out_specs=pl.BlockSpec((tq, 512), lambda b, i: (b * nb + i, 0)),
        out_shape=jax.ShapeDtypeStruct((batch * seq, 512), BF16),
        compiler_params=_cparams(("arbitrary", "arbitrary")),
        name="attn_na_lat",
    )(nq, nk, nv, cache_k, cache_v, bias_tab)


def _kvup_kernel(ckv_ref, kpe_ref, wukv_ref, place_ref, km_ref, vm_ref):
    kv = _dot(ckv_ref[...].astype(BF16), wukv_ref[...])
    kslots = kv[:, :MLA_HEADS * SLOT] + _dot(kpe_ref[...].astype(BF16), place_ref[...])
    km_ref[...] = kslots.astype(BF16)
    vm_ref[...] = kv[:, MLA_HEADS * SLOT:].astype(BF16)


def _kvup(cache_ckv, cache_kpe, w_ukv, place, l):
    batch, _, past, _ = cache_ckv.shape
    return pl.pallas_call(
        _kvup_kernel,
        grid=(batch,),
        in_specs=[
            pl.BlockSpec((None, None, past, MLA_KV_LORA), lambda b: (b, l, 0, 0)),
            pl.BlockSpec((None, None, past, MLA_ROPE_DIM), lambda b: (b, l, 0, 0)),
            pl.BlockSpec((None, MLA_KV_LORA, MLA_HEADS * SLOT + 512), lambda b: (l, 0, 0)),
            pl.BlockSpec((MLA_ROPE_DIM, MLA_HEADS * SLOT), lambda b: (0, 0)),
        ],
        out_specs=[pl.BlockSpec((None, past, MLA_HEADS * SLOT), lambda b: (b, 0, 0)),
                   pl.BlockSpec((None, past, 512), lambda b: (b, 0, 0))],
        out_shape=[jax.ShapeDtypeStruct((batch, past, MLA_HEADS * SLOT), BF16),
                   jax.ShapeDtypeStruct((batch, past, 512), BF16)],
        compiler_params=_cparams(("arbitrary",)),
        name="kv_up_cache",
    )(cache_ckv, cache_kpe, w_ukv, place)


def _attn_mla_kernel(qm_ref, km_ref, vm_ref, ckm_ref, cvm_ref, out_ref):
    _mla_heads(qm_ref, [(_slot_get(km_ref), _slot_get(vm_ref)), (_slot_get(ckm_ref), _slot_get(cvm_ref))], out_ref)


def _attn_mla(qm, km, vm, ckm, cvm, *, batch, seq, tq):
    nb = seq // tq
    past = ckm.shape[1]
    return pl.pallas_call(
        _attn_mla_kernel,
        grid=(batch, nb),
        in_specs=[
            pl.BlockSpec((tq, MLA_HEADS * SLOT), lambda b, i: (b * nb + i, 0)),
            pl.BlockSpec((seq, MLA_HEADS * SLOT), lambda b, i: (b, 0)),
            pl.BlockSpec((seq, 512), lambda b, i: (b, 0)),
            pl.BlockSpec((None, past, MLA_HEADS * SLOT), lambda b, i: (b, 0, 0)),
            pl.BlockSpec((None, past, 512), lambda b, i: (b, 0, 0)),
        ],
        out_specs=pl.BlockSpec((tq, 512), lambda b, i: (b * nb + i, 0)),
        out_shape=jax.ShapeDtypeStruct((batch * seq, 512), BF16),
        compiler_params=_cparams(("arbitrary", "arbitrary")),
        name="attn_mla_lat",
    )(qm, km, vm, ckm, cvm)


POOL_HALO = 16


def _route(logits):
    lane = lax.broadcasted_iota(jnp.int32, logits.shape, 1).astype(F32)
    big = jnp.float32(1 << 20)
    is_g = lane < N_GROUPS
    gl = jnp.where(is_g, logits, NEG_INF)
    gmax = gl.max(axis=-1, keepdims=True)
    ge = jnp.where(is_g, jnp.exp(gl - gmax), 0.0)
    gp = ge / ge.sum(axis=-1, keepdims=True)
    g_w = jnp.where(is_g, gp, -1.0).max(axis=-1, keepdims=True)
    g_idx = jnp.where(is_g & (gp == g_w), lane, big).min(axis=-1, keepdims=True)
    e_lane = lane - N_GROUPS
    in_grp = (e_lane >= g_idx * EXPERTS_PER_GROUP) & (e_lane < (g_idx + 1) * EXPERTS_PER_GROUP)
    el = jnp.where(in_grp, logits, NEG_INF)
    emax = el.max(axis=-1, keepdims=True)
    ee = jnp.where(in_grp, jnp.exp(el - emax), 0.0)
    ep = ee / ee.sum(axis=-1, keepdims=True)
    p1 = jnp.where(in_grp, ep, -1.0).max(axis=-1, keepdims=True)
    i1 = jnp.where(in_grp & (ep == p1), lane, big).min(axis=-1, keepdims=True)
    rest = in_grp & (lane != i1)
    p2 = jnp.where(rest, ep, -1.0).max(axis=-1, keepdims=True)
    i2 = jnp.where(rest & (ep == p2), lane, big).min(axis=-1, keepdims=True)
    tot = p1 + p2
    gates = jnp.where(lane == i1, g_w * (p1 / tot), 0.0) + jnp.where(lane == i2, g_w * (p2 / tot), 0.0)
    return pltpu.roll(gates, LANES - N_GROUPS, axis=1)


def _merge_kernel(x_ref, u_ref, yna_ref, ymla_ref, gate_ref, gt1_ref, sh2_ref, sc2_ref, gpost_ref, gpre_ref,
                  wpool_ref, pscale_ref, wbr_ref, wout_ref, wr2_ref, wr1_ref,
                  xo_ref, h2_ref, rg_ref, *, seq, tm, steps_per_seq):
    nwin = min(tm + 2 * POOL_HALO, seq)
    if steps_per_seq == 1:
        r0 = 0
        s0 = 0
        uwin = u_ref[...]
        utile = uwin
    else:
        r0 = (pl.program_id(0) % steps_per_seq) * tm
        s0 = pl.multiple_of(jnp.clip(r0 - POOL_HALO, 0, seq - nwin), POOL_HALO)
        uwin = u_ref[pl.ds(s0, nwin), :]
        utile = u_ref[pl.ds(pl.multiple_of(r0, POOL_HALO), tm), :]
    t = r0 + lax.broadcasted_iota(jnp.int32, (tm, nwin), 0)
    s = s0 + lax.broadcasted_iota(jnp.int32, (tm, nwin), 1)
    t1 = r0 + lax.broadcasted_iota(jnp.int32, (tm, 1), 0)
    ds = []
    for g, w in enumerate(POOL_WINDOWS):
        lo = jnp.maximum(t - w // 2, 0)
        hi = jnp.minimum(t + (w - w // 2), seq)
        band = ((s >= lo) & (s < hi)).astype(BF16)
        cnt = (jnp.minimum(t1 + (w - w // 2), seq) - jnp.maximum(t1 - w // 2, 0)).astype(F32)
        sl = slice(g * POOL_GC, (g + 1) * POOL_GC)
        pooled = _dot(band, uwin[:, sl]) / cnt
        ds.append(pooled - utile[:, sl].astype(F32))
    d = jnp.concatenate(ds, axis=-1).astype(BF16)
    y_pool = (_dot(d, wpool_ref[...]) * pscale_ref[...]).astype(BF16)

    g = gate_ref[...]
    merged = (g[:, 0:D_MODEL].astype(F32) * _dot(y_pool, wbr_ref[0])
              + g[:, D_MODEL:2 * D_MODEL].astype(F32) * _dot(yna_ref[...], wbr_ref[1])
              + g[:, 2 * D_MODEL:].astype(F32) * _dot(ymla_ref[...], wbr_ref[2]))
    y = _dot(merged.astype(BF16), wout_ref[...])
    xn = x_ref[...] + gt1_ref[...] * _rms(y, gpost_ref[...])
    xo_ref[...] = xn

    h2 = _rms(xn, gpre_ref[...]) * (1.0 + sc2_ref[...]) + sh2_ref[...]
    h_hi = h2.astype(BF16)
    h2_ref[...] = h_hi
    h_lo = (h2 - h_hi.astype(F32)).astype(BF16)
    two = _dot(h_hi, wr2_ref[...])
    logits = two[:, :LANES] + two[:, LANES:] + _dot(h_lo, wr1_ref[...])
    rg_ref[...] = _route(logits)


def _merge(x, u, yna, ymla, gate, mod5, l, w, *, seq, ctx, tm):
    n = x.shape[0]
    steps_per_seq = seq // tm
    nseq = n // seq

    def mrow(i):
        return 0 if ctx else 1 + i // steps_per_seq

    const = lambda i: (l, 0, 0)
    modspec = lambda k: pl.BlockSpec((None, None, None, 1, D_MODEL), lambda i: (l, mrow(i), k, 0, 0))
    row = lambda wd: pl.BlockSpec((tm, wd), lambda i: (i, 0))
    return pl.pallas_call(
        functools.partial(_merge_kernel, seq=seq, tm=tm, steps_per_seq=steps_per_seq),
        grid=(n // tm,),
        in_specs=[
            row(D_MODEL),
            pl.BlockSpec((None, seq, 512), lambda i: (i // steps_per_seq, 0, 0)),
            row(512), row(512), row(3 * D_MODEL),
            modspec(2), modspec(3), modspec(4),
            pl.BlockSpec((None, 1, D_MODEL), const),
            pl.BlockSpec((None, 1, D_MODEL), const),
            pl.BlockSpec((None, 512, 512), const),
            pl.BlockSpec((None, 1, 512), const),
            pl.BlockSpec((None, 3, 512, D_MODEL), lambda i: (l, 0, 0, 0)),
            pl.BlockSpec((None, D_MODEL, D_MODEL), const),
            pl.BlockSpec((None, D_MODEL, 2 * LANES), const),
            pl.BlockSpec((None, D_MODEL, LANES), const),
        ],
        out_specs=[row(D_MODEL), row(D_MODEL), row(LANES)],
        out_shape=[jax.ShapeDtypeStruct((n, D_MODEL), F32), jax.ShapeDtypeStruct((n, D_MODEL), BF16),
                   jax.ShapeDtypeStruct((n, LANES), F32)],
        compiler_params=_cparams(("arbitrary",)),
        name="merge_ctx" if ctx else "merge_lat",
    )(x, u.reshape(nseq, seq, 512), yna, ymla, gate, mod5, mod5, mod5, w["g_post_mix"], w["g_pre_ffn"],
      w["w_pool_bd"], w["pool_scale"], w["w_branch"], w["w_out"], w["w_r2"], w["w_r1"])


MOE_CHUNK = 4


def _moe_kernel(h_ref, rg_ref, x_ref, gt2_ref, gpost_ref, w1_ref, w3_ref, w2_ref, o_ref, acc_ref):
    c = pl.program_id(1)
    h = h_ref[...]
    rg = rg_ref[...]
    lane = lax.broadcasted_iota(jnp.int32, rg.shape, 1)
    part = None
    for j in range(MOE_CHUNK):
        gate = jnp.where(lane == c * MOE_CHUNK + j, rg, 0.0).sum(axis=-1, keepdims=True)
        a = _dot(h, w1_ref[j])
        b = _dot(h, w3_ref[j])
        hid = (a * jax.nn.sigmoid(a)) * b * gate
        y = _dot(hid.astype(BF16), w2_ref[j])
        part = y if part is None else part + y

    @pl.when(c == 0)
    def _():
        acc_ref[...] = part

    @pl.when(c > 0)
    def _():
        acc_ref[...] += part

    @pl.when(c == pl.num_programs(1) - 1)
    def _():
        o_ref[...] = x_ref[...] + gt2_ref[...] * _rms(acc_ref[...], gpost_ref[...])


def _moe(h2, rg, x, mod5, l, w, *, seq, ctx, tm):
    n = x.shape[0]
    steps_per_seq = seq // tm if not ctx else 0

    def mrow(i):
        return 0 if ctx else 1 + i // steps_per_seq

    nchunk = N_EXPERTS // MOE_CHUNK
    row = lambda wd: pl.BlockSpec((tm, wd), lambda i, c: (i, 0))
    wspec = lambda a, b: pl.BlockSpec((None, MOE_CHUNK, a, b), lambda i, c: (l, c, 0, 0))
    return pl.pallas_call(
        _moe_kernel,
        grid=(n // tm, nchunk),
        in_specs=[
            row(D_MODEL), row(LANES), row(D_MODEL),
            pl.BlockSpec((None, None, None, 1, D_MODEL), lambda i, c: (l, mrow(i), 5, 0, 0)),
            pl.BlockSpec((None, 1, D_MODEL), lambda i, c: (l, 0, 0)),
            wspec(D_MODEL, EXPERT_FF), wspec(D_MODEL, EXPERT_FF), wspec(EXPERT_FF, D_MODEL),
        ],
        out_specs=row(D_MODEL),
        out_shape=jax.ShapeDtypeStruct((n, D_MODEL), F32),
        scratch_shapes=[pltpu.VMEM((tm, D_MODEL), F32)],
        compiler_params=_cparams(("arbitrary", "arbitrary")),
        name="moe_ctx" if ctx else "moe_lat",
    )(h2, rg, x, mod5, w["g_post_ffn"], w["w1"], w["w3"], w["w2"])


def _rope_table(seq):
    t = np.arange(seq)
    n_freq = MLA_ROPE_DIM // 4
    inv = jnp.asarray(ROPE_BASE, F32) ** (-jnp.arange(n_freq, dtype=F32) / n_freq)
    ang_r = jnp.asarray(t // GRID_W, F32)[:, None] * inv
    ang_c = jnp.asarray(t % GRID_W, F32)[:, None] * inv
    cos = jnp.cos(jnp.stack([ang_r, ang_c], axis=1))
    sin = jnp.sin(jnp.stack([ang_r, ang_c], axis=1))
    zeros = jnp.zeros_like(sin)
    c32 = jnp.stack([cos, cos], axis=2).reshape(seq, MLA_ROPE_DIM)
    up32 = jnp.stack([-sin, zeros], axis=2).reshape(seq, MLA_ROPE_DIM)
    dn32 = jnp.stack([zeros, sin], axis=2).reshape(seq, MLA_ROPE_DIM)

    def slot(v, fill):
        return jnp.concatenate([jnp.full((seq, KPE_LANE), fill, F32), v,
                                jnp.full((seq, SLOT - KPE_LANE - MLA_ROPE_DIM), fill, F32)], axis=1)

    return jnp.concatenate([slot(c32, 1.0), slot(up32, 0.0), slot(dn32, 0.0)], axis=1)


def _na_bias_table(rpb, n_rows):
    i = np.arange(NA_BLOCK_ROWS)[:, None]
    j = np.arange(NA_SLAB_ROWS)[None, :]
    last_ks = n_rows - NA_SLAB_ROWS
    last_q0 = n_rows - NA_BLOCK_ROWS
    specs = [
        (j < NA_WIN_ROWS, j - i),
        ((j >= i) & (j < i + NA_WIN_ROWS), j - NA_WIN_ROWS // 2 - i),
        (last_ks + j >= n_rows - NA_WIN_ROWS, last_ks + j - last_q0 - i),
    ]
    c = np.arange(GRID_W)[:, None]
    kc = np.arange(GRID_W)[None, :]
    c0 = np.clip(c - NA_WIN_COLS // 2, 0, GRID_W - NA_WIN_COLS)
    in_col = (kc >= c0) & (kc < c0 + NA_WIN_COLS)
    dc = np.clip(kc - c, -(NA_WIN_COLS - 1), NA_WIN_COLS - 1) + NA_WIN_COLS - 1
    n_dr, n_dc = 2 * NA_WIN_ROWS - 1, 2 * NA_WIN_COLS - 1
    sel_r = np.zeros((3, NA_BLOCK_ROWS, NA_SLAB_ROWS, n_dr), np.float32)
    ok_r = np.zeros((3, NA_BLOCK_ROWS, NA_SLAB_ROWS), bool)
    for k, (in_row, dr) in enumerate(specs):
        dr_idx = np.clip(dr + NA_WIN_ROWS - 1, 0, n_dr - 1)
        sel_r[k] = np.eye(n_dr, dtype=np.float32)[dr_idx]
        ok_r[k] = np.broadcast_to(in_row, (NA_BLOCK_ROWS, NA_SLAB_ROWS))
    sel_c = np.eye(n_dc, dtype=np.float32)[dc]
    b = jnp.einsum("lhab,kija,cmb->lkhicjm", rpb, sel_r, sel_c, precision=lax.Precision.HIGHEST)
    ok = ok_r[:, :, None, :, None] & in_col[None, None, :, None, :]
    b = jnp.where(ok[None, :, None], b, NEG_INF).astype(BF16)
    return b.reshape(DEPTH, 3, NA_HEADS, NA_BLOCK_ROWS * GRID_W, NA_SLAB_ROWS * GRID_W)


def _prep(p):
    L = DEPTH
    w_in = p["w_in"]
    kpe0 = C_G
    kpe1 = C_G + MLA_ROPE_DIM
    w_in_p = jnp.concatenate(
        [w_in[..., :kpe0], w_in[..., kpe1:], jnp.zeros((L, D_MODEL, KPE_LANE), F32), w_in[..., kpe0:kpe1],
         jnp.zeros((L, D_MODEL, SLOT - KPE_LANE - MLA_ROPE_DIM), F32)], axis=-1).astype(BF16)
    w_uq = jnp.pad(p["mla_w_uq"].reshape(L, MLA_Q_LORA, MLA_HEADS, MLA_QK_DIM),
                   ((0, 0), (0, 0), (0, 0), (0, SLOT - MLA_QK_DIM))).reshape(L, MLA_Q_LORA, MLA_HEADS * SLOT)
    ukv = p["mla_w_ukv"].reshape(L, MLA_KV_LORA, MLA_HEADS, MLA_NOPE_DIM + MLA_V_DIM)
    uk = jnp.pad(ukv[..., :MLA_NOPE_DIM], ((0, 0), (0, 0), (0, 0), (0, SLOT - MLA_NOPE_DIM)))
    w_ukv = jnp.concatenate([uk.reshape(L, MLA_KV_LORA, MLA_HEADS * SLOT),
                             ukv[..., MLA_NOPE_DIM:].reshape(L, MLA_KV_LORA, MLA_HEADS * MLA_V_DIM)], axis=-1)
    w_pool_bd = jnp.einsum("lgcd,gh->lgchd", p["w_pool"], jnp.eye(4, dtype=F32)).reshape(L, POOL_WIDTH, POOL_WIDTH)
    w_r = jnp.concatenate([p["w_group_router"], p["w_expert_router"]], axis=-1)
    r_hi = w_r.astype(BF16)
    r_lo = (w_r - r_hi.astype(F32)).astype(BF16)
    padl = lambda a: jnp.pad(a, ((0, 0), (0, 0), (0, LANES - a.shape[-1])))
    vec = lambda a: a.reshape(L, 1, a.shape[-1])
    return {
        "w_in": w_in_p,
        "w_uq": w_uq.astype(BF16),
        "w_ukv": w_ukv.astype(BF16),
        "w_pool_bd": w_pool_bd.astype(BF16),
        "w_branch": p["w_branch"].astype(BF16),
        "w_out": p["w_out"].astype(BF16),
        "w_r2": jnp.concatenate([padl(r_hi), padl(r_lo)], axis=-1),
        "w_r1": padl(r_hi),
        "w1": p["w_exp_gate"].astype(BF16),
        "w3": p["w_exp_up"].astype(BF16),
        "w2": p["w_exp_down"].astype(BF16),
        "g_pre_mix": vec(p["g_pre_mix"]), "g_post_mix": vec(p["g_post_mix"]),
        "g_pre_ffn": vec(p["g_pre_ffn"]), "g_post_ffn": vec(p["g_post_ffn"]),
        "pool_scale": vec(p["pool_scale"]), "q_norm": vec(p["mla_q_norm"]), "kv_norm": vec(p["mla_kv_norm"]),
    }


def kernel(x_prompt, x_sample, cache_na_k, cache_na_v, cache_mla_ckv, cache_mla_kpe, c, c_ctx, w_mod, b_mod,
           g_pre_mix, g_post_mix, g_pre_ffn, g_post_ffn, w_in, w_pool, pool_scale, na_rpb, mla_q_norm, mla_w_uq,
           mla_kv_norm, mla_w_ukv, w_branch, w_out, w_group_router, w_expert_router, w_exp_gate, w_exp_up,
           w_exp_down):
    batch, seq, _ = x_prompt.shape
    dbatch, dseq, _ = x_sample.shape
    past = cache_na_k.shape[2]
    assert dbatch + 1 <= 8 and dseq % (NA_BLOCK_ROWS * GRID_W) == 0

    w = _prep(dict(w_in=w_in, w_pool=w_pool, pool_scale=pool_scale, mla_q_norm=mla_q_norm, mla_w_uq=mla_w_uq,
                   mla_kv_norm=mla_kv_norm, mla_w_ukv=mla_w_ukv, w_branch=w_branch, w_out=w_out,
                   w_group_router=w_group_router, w_expert_router=w_expert_router, w_exp_gate=w_exp_gate,
                   w_exp_up=w_exp_up, w_exp_down=w_exp_down, g_pre_mix=g_pre_mix, g_post_mix=g_post_mix,
                   g_pre_ffn=g_pre_ffn, g_post_ffn=g_post_ffn))
    rope_tab = _rope_table(dseq)
    bias_tab = _na_bias_table(na_rpb, dseq // GRID_W)
    place = jnp.tile(jnp.pad(jnp.eye(MLA_ROPE_DIM, dtype=F32), ((0, 0), (KPE_LANE, SLOT - KPE_LANE - MLA_ROPE_DIM))),
                     (1, MLA_HEADS)).astype(BF16)

    cvec = jnp.concatenate([c_ctx[None], c, jnp.zeros((8 - 1 - dbatch, D_MODEL), F32)], axis=0)
    mod5 = _modulation(cvec, w_mod, b_mod).reshape(DEPTH, 8, 6, 1, D_MODEL)

    ck = cache_na_k.reshape(dbatch, DEPTH, past, NA_WIDTH)
    cv = cache_na_v.reshape(dbatch, DEPTH, past, NA_WIDTH)

    xp = x_prompt.reshape(batch * seq, D_MODEL)
    nk_l, nv_l, ckv_l, kpe_l = [], [], [], []
    for l in range(DEPTH):
        u, nq, nk, nv, qm, km, vm, gate, ckv, kpe = _inproj(xp, mod5, l, w, seq=seq, rope_tab=None, ctx_out=True, tm=256)
        nk_l.append(nk)
        nv_l.append(nv)
        ckv_l.append(ckv)
        kpe_l.append(kpe)
        yna, ymla = _attn_ctx(nq, nk, nv, qm, km, vm, seq=seq)
        xp, h2, rg = _merge(xp, u, yna, ymla, gate, mod5, l, w, seq=seq, ctx=True, tm=seq)
        xp = _moe(h2, rg, xp, mod5, l, w, seq=seq, ctx=True, tm=1024)

    xs = x_sample.reshape(dbatch * dseq, D_MODEL)
    for l in range(DEPTH):
        u, nq, nk, nv, qm, km, vm, gate = _inproj(xs, mod5, l, w, seq=dseq, rope_tab=rope_tab, ctx_out=False, tm=256)
        yna = _attn_na(nq, nk, nv, ck, cv, bias_tab, l, batch=dbatch, seq=dseq)
        ckm, cvm = _kvup(cache_mla_ckv, cache_mla_kpe, w["w_ukv"], place, l)
        ymla = _attn_mla(qm, km, vm, ckm, cvm, batch=dbatch, seq=dseq, tq=256)
        xs, h2, rg = _merge(xs, u, yna, ymla, gate, mod5, l, w, seq=dseq, ctx=False, tm=256)
        xs = _moe(h2, rg, xs, mod5, l, w, seq=dseq, ctx=False, tm=1024)

    stack = lambda parts, tail: jnp.stack([a.reshape((batch, seq) + tail) for a in parts], axis=1)
    return (xp.reshape(batch, seq, D_MODEL), xs.reshape(dbatch, dseq, D_MODEL),
            stack(nk_l, (NA_HEADS, NA_HEAD_DIM)), stack(nv_l, (NA_HEADS, NA_HEAD_DIM)),
            stack(ckv_l, (MLA_KV_LORA,)), stack(kpe_l, (MLA_ROPE_DIM,)))
```

```python
import functools

import numpy as np
import jax
import jax.numpy as jnp
from jax import lax
from jax.experimental import pallas as pl
from jax.experimental.pallas import tpu as pltpu

F32 = jnp.float32
BF16 = jnp.bfloat16

D_MODEL = 1024
DEPTH = 2
GRID_W = 64
POOL_WINDOWS = (2, 4, 8, 16)
POOL_GC = 128
POOL_WIDTH = 512
NA_HEADS = 8
NA_HEAD_DIM = 64
NA_WIDTH = 512
NA_WIN_ROWS = 8
NA_WIN_COLS = 16
MLA_HEADS = 8
MLA_NOPE_DIM = 64
MLA_ROPE_DIM = 32
MLA_V_DIM = 64
MLA_QK_DIM = 96
MLA_Q_LORA = 384
MLA_KV_LORA = 256
N_GROUPS = 4
EXPERTS_PER_GROUP = 4
N_EXPERTS = 16
EXPERT_FF = 256
ROPE_BASE = 10000.0
EPS = 1e-6
NEG_INF = -1e30

LANES = 128
SLOT = LANES
VMEM_LIMIT = 56 * 1024 * 1024

C_U, C_NQ, C_NK, C_NV, C_CQ, C_CKV, C_G, C_KPE, C_END = 0, 512, 1024, 1536, 2048, 2432, 2688, 5760, 5888
KPE_LANE = MLA_NOPE_DIM

NA_BLOCK_ROWS = 4
NA_SLAB_ROWS = 12


def _cparams(sem):
    return pltpu.CompilerParams(dimension_semantics=sem, vmem_limit_bytes=VMEM_LIMIT)


def _rms(x, g):
    return x * lax.rsqrt(jnp.mean(x * x, axis=-1, keepdims=True) + EPS) * g


def _dot(a, b):
    return jnp.dot(a, b, preferred_element_type=F32)


def _dot_nt(a, b):
    return lax.dot_general(a, b, (((1,), (1,)), ((), ())), preferred_element_type=F32)


def _mod_kernel(c_ref, w_ref, b_ref, o_ref):
    c = c_ref[...]
    s = (c * jax.nn.sigmoid(c)).astype(BF16)
    o_ref[...] = _dot(s, w_ref[...].astype(BF16)) + b_ref[...]


def _modulation(cvec, w_mod, b_mod):
    tn = 1536
    n = w_mod.shape[-1]
    return pl.pallas_call(
        _mod_kernel,
        grid=(DEPTH, n // tn),
        in_specs=[
            pl.BlockSpec((8, D_MODEL), lambda l, j: (0, 0)),
            pl.BlockSpec((None, D_MODEL, tn), lambda l, j: (l, 0, j)),
            pl.BlockSpec((None, 1, tn), lambda l, j: (l, 0, j)),
        ],
        out_specs=pl.BlockSpec((None, 8, tn), lambda l, j: (l, 0, j)),
        out_shape=jax.ShapeDtypeStruct((DEPTH, 8, n), F32),
        compiler_params=_cparams(("arbitrary", "arbitrary")),
        name="modulation",
    )(cvec, w_mod, b_mod.reshape(DEPTH, 1, n))


def _rope_slot(x, tab):
    c, s_up, s_dn = tab[:, 0:SLOT], tab[:, SLOT:2 * SLOT], tab[:, 2 * SLOT:3 * SLOT]
    half = MLA_ROPE_DIM // 4
    x_up = pltpu.roll(x, SLOT - half, axis=1)
    x_dn = pltpu.roll(x, half, axis=1)
    return x * c + x_up * s_up + x_dn * s_dn


def _inproj_kernel(*refs, rope, ctx_out):
    if rope:
        (x_ref, sh_ref, sc_ref, g_ref, win_ref, qn_ref, wuq_ref, kvn_ref, wukv_ref, tab_ref), outs = refs[:10], refs[10:]
    else:
        (x_ref, sh_ref, sc_ref, g_ref, win_ref, qn_ref, wuq_ref, kvn_ref, wukv_ref), outs = refs[:9], refs[9:]
        tab_ref = None
    if ctx_out:
        u_ref, nq_ref, nk_ref, nv_ref, qm_ref, km_ref, vm_ref, gate_ref, ckv_ref, kpe_ref = outs
    else:
        u_ref, nq_ref, nk_ref, nv_ref, qm_ref, km_ref, vm_ref, gate_ref = outs

    x = x_ref[...]
    h = (_rms(x, g_ref[...]) * (1.0 + sc_ref[...]) + sh_ref[...]).astype(BF16)

    def seg(a, b):
        return _dot(h, win_ref[:, a:b])

    u_ref[...] = seg(C_U, C_NQ).astype(u_ref.dtype)
    nq_ref[...] = (seg(C_NQ, C_NK) * (NA_HEAD_DIM ** -0.5)).astype(BF16)
    nk_ref[...] = seg(C_NK, C_NV).astype(nk_ref.dtype)
    nv_ref[...] = seg(C_NV, C_CQ).astype(nv_ref.dtype)
    gate_ref[...] = jax.nn.sigmoid(seg(C_G, C_KPE)).astype(BF16)

    cq = _rms(seg(C_CQ, C_CKV), qn_ref[...]).astype(BF16)
    q = _dot(cq, wuq_ref[...])
    tab = tab_ref[...] if rope else None
    for hd in range(MLA_HEADS):
        qs = q[:, hd * SLOT:(hd + 1) * SLOT]
        if rope:
            qs = _rope_slot(qs, tab)
        qm_ref[:, hd * SLOT:(hd + 1) * SLOT] = (qs * (MLA_QK_DIM ** -0.5)).astype(BF16)

    ckv = _rms(seg(C_CKV, C_G), kvn_ref[...])
    kpe_slot = seg(C_KPE, C_END)
    if ctx_out:
        ckv_ref[...] = ckv
        kpe_ref[...] = kpe_slot[:, KPE_LANE:KPE_LANE + MLA_ROPE_DIM]
    if rope:
        kpe_slot = _rope_slot(kpe_slot, tab)
    kv = _dot(ckv.astype(BF16), wukv_ref[...])
    for hd in range(MLA_HEADS):
        km_ref[:, hd * SLOT:(hd + 1) * SLOT] = (kv[:, hd * SLOT:(hd + 1) * SLOT] + kpe_slot).astype(BF16)
    vm_ref[...] = kv[:, MLA_HEADS * SLOT:].astype(BF16)


def _inproj(x, mod5, l, w, *, seq, rope_tab, ctx_out, tm):
    n = x.shape[0]
    steps_per_seq = seq // tm if not ctx_out else 0

    def mrow(i):
        return 0 if ctx_out else 1 + i // steps_per_seq

    const = lambda i: (l, 0, 0)
    in_specs = [
        pl.BlockSpec((tm, D_MODEL), lambda i: (i, 0)),
        pl.BlockSpec((None, None, None, 1, D_MODEL), lambda i: (l, mrow(i), 0, 0, 0)),
        pl.BlockSpec((None, None, None, 1, D_MODEL), lambda i: (l, mrow(i), 1, 0, 0)),
        pl.BlockSpec((None, 1, D_MODEL), const),
        pl.BlockSpec((None, D_MODEL, C_END), const),
        pl.BlockSpec((None, 1, MLA_Q_LORA), const),
        pl.BlockSpec((None, MLA_Q_LORA, MLA_HEADS * SLOT), const),
        pl.BlockSpec((None, 1, MLA_KV_LORA), const),
        pl.BlockSpec((None, MLA_KV_LORA, MLA_HEADS * SLOT + 512), const),
    ]
    args = [x, mod5, mod5, w["g_pre_mix"], w["w_in"], w["q_norm"], w["w_uq"], w["kv_norm"], w["w_ukv"]]
    rope = rope_tab is not None
    if rope:
        in_specs.append(pl.BlockSpec((tm, 3 * SLOT), lambda i: (i % steps_per_seq, 0)))
        args.append(rope_tab)
    kdt = F32 if ctx_out else BF16
    widths = [(512, BF16), (512, BF16), (512, kdt), (512, kdt), (MLA_HEADS * SLOT, BF16),
              (MLA_HEADS * SLOT, BF16), (512, BF16), (3 * D_MODEL, BF16)]
    if ctx_out:
        widths += [(MLA_KV_LORA, F32), (MLA_ROPE_DIM, F32)]
    out_specs = [pl.BlockSpec((tm, wd), lambda i: (i, 0)) for wd, _ in widths]
    out_shape = [jax.ShapeDtypeStruct((n, wd), dt) for wd, dt in widths]
    return pl.pallas_call(
        functools.partial(_inproj_kernel, rope=rope, ctx_out=ctx_out),
        grid=(n // tm,),
        in_specs=in_specs,
        out_specs=out_specs,
        out_shape=out_shape,
        compiler_params=_cparams(("arbitrary",)),
        name="inproj_ctx" if ctx_out else "inproj_lat",
    )(*args)


def _attend(q_h, segs):
    scores = []
    for k, _, bias in segs:
        s = _dot_nt(q_h, k)
        if bias is not None:
            s = s + bias
        scores.append(s)
    m = scores[0].max(axis=-1, keepdims=True)
    for s in scores[1:]:
        m = jnp.maximum(m, s.max(axis=-1, keepdims=True))
    den = None
    acc = None
    for s, (_, v, _) in zip(scores, segs):
        p = jnp.exp(s - m)
        ps = p.sum(axis=-1, keepdims=True)
        pv = _dot(p.astype(BF16), v)
        den = ps if den is None else den + ps
        acc = pv if acc is None else acc + pv
    return acc / den


def _lane_lo():
    return lax.broadcasted_iota(jnp.int32, (1, SLOT), 1) < NA_HEAD_DIM


def _na_heads(q_ref, kv_segs, bias_ref, out_ref):
    lo = _lane_lo()
    for p in range(NA_HEADS // 2):
        q_pair = q_ref[:, p * SLOT:(p + 1) * SLOT]
        kvs = [(kg(p), vg(p)) for kg, vg in kv_segs]
        outs = []
        for e in range(2):
            q_h = jnp.where(lo if e == 0 else jnp.logical_not(lo), q_pair, jnp.zeros_like(q_pair))
            segs = []
            for si, (k, v) in enumerate(kvs):
                bias = None
                if si == 0 and bias_ref is not None:
                    bias = bias_ref[2 * p + e].astype(F32)
                segs.append((k, v, bias))
            outs.append(_attend(q_h, segs))
        out_ref[:, p * SLOT:(p + 1) * SLOT] = jnp.where(lo, outs[0], outs[1]).astype(out_ref.dtype)


def _mla_heads(q_ref, kv_segs, out_ref):
    lo = _lane_lo()
    for p in range(MLA_HEADS // 2):
        vs = [vg(p) for _, vg in kv_segs]
        outs = []
        for e in range(2):
            hd = 2 * p + e
            q_h = q_ref[:, hd * SLOT:(hd + 1) * SLOT]
            segs = [(kg(hd), v, None) for (kg, _), v in zip(kv_segs, vs)]
            outs.append(_attend(q_h, segs))
        out_ref[:, p * SLOT:(p + 1) * SLOT] = jnp.where(lo, outs[0], outs[1]).astype(out_ref.dtype)


def _slot_get(ref, rows=None):
    def get(i):
        if rows is None:
            return ref[:, i * SLOT:(i + 1) * SLOT].astype(BF16)
        return ref[rows, i * SLOT:(i + 1) * SLOT].astype(BF16)
    return get


def _attn_ctx_kernel(nq_ref, nk_ref, nv_ref, qm_ref, km_ref, vm_ref, yna_ref, ymla_ref):
    _na_heads(nq_ref, [(_slot_get(nk_ref), _slot_get(nv_ref))], None, yna_ref)
    _mla_heads(qm_ref, [(_slot_get(km_ref), _slot_get(vm_ref))], ymla_ref)


def _attn_ctx(nq, nk, nv, qm, km, vm, *, seq):
    n = nq.shape[0]
    spec = lambda wd: pl.BlockSpec((seq, wd), lambda b: (b, 0))
    return pl.pallas_call(
        _attn_ctx_kernel,
        grid=(n // seq,),
        in_specs=[spec(512), spec(512), spec(512), spec(1024), spec(1024), spec(512)],
        out_specs=[spec(512), spec(512)],
        out_shape=[jax.ShapeDtypeStruct((n, 512), BF16)] * 2,
        compiler_params=_cparams(("arbitrary",)),
        name="attn_ctx",
    )(nq, nk, nv, qm, km, vm)


def _attn_na_kernel(nq_ref, nk_ref, nv_ref, ck_ref, cv_ref, bias_ref, out_ref, *, n_blocks):
    i = pl.program_id(1)
    ks = jnp.clip(NA_BLOCK_ROWS * i - NA_WIN_ROWS // 2, 0, n_blocks * NA_BLOCK_ROWS - NA_SLAB_ROWS)
    rows = pl.ds(pl.multiple_of(ks * GRID_W, GRID_W), NA_SLAB_ROWS * GRID_W)
    _na_heads(nq_ref,
              [(_slot_get(nk_ref, rows), _slot_get(nv_ref, rows)), (_slot_get(ck_ref), _slot_get(cv_ref))],
              bias_ref, out_ref)


def _attn_na(nq, nk, nv, cache_k, cache_v, bias_tab, l, *, batch, seq):
    tq = NA_BLOCK_ROWS * GRID_W
    nb = seq // tq
    past = cache_k.shape[2]

    def cls(i):
        return jnp.where(i == 0, 0, jnp.where(i == nb - 1, 2, 1))

    return pl.pallas_call(
        functools.partial(_attn_na_kernel, n_blocks=nb),
        grid=(batch, nb),
        in_specs=[
            pl.BlockSpec((tq, 512), lambda b, i: (b * nb + i, 0)),
            pl.BlockSpec((seq, 512), lambda b, i: (b, 0)),
            pl.BlockSpec((seq, 512), lambda b, i: (b, 0)),
            pl.BlockSpec((None, None, past, 512), lambda b, i: (b, l, 0, 0)),
            pl.BlockSpec((None, None, past, 512), lambda b, i: (b, l, 0, 0)),
            pl.BlockSpec((None, None, NA_HEADS, tq, NA_SLAB_ROWS * GRID_W), lambda b, i: (l, cls(i), 0, 0, 0)),
        ],
        out_specs=pl.BlockSpec((tq, 512), lambda b, i: (b * nb + i, 0)),
        out_shape=jax.ShapeDtypeStruct((batch * seq, 512), BF16),
        compiler_params=_cparams(("arbitrary", "arbitrary")),
        name="attn_na_lat",
    )(nq, nk, nv, cache_k, cache_v, bias_tab)


def _kvup_kernel(ckv_ref, kpe_ref, wukv_ref, place_ref, km_ref, vm_ref):
    kv = _dot(ckv_ref[...].astype(BF16), wukv_ref[...])
    kslots = kv[:, :MLA_HEADS * SLOT] + _dot(kpe_ref[...].astype(BF16), place_ref[...])
    km_ref[...] = kslots.astype(BF16)
    vm_ref[...] = kv[:, MLA_HEADS * SLOT:].astype(BF16)


def _kvup(cache_ckv, cache_kpe, w_ukv, place, l):
    batch, _, past, _ = cache_ckv.shape
    return pl.pallas_call(
        _kvup_kernel,
        grid=(batch,),
        in_specs=[
            pl.BlockSpec((None, None, past, MLA_KV_LORA), lambda b: (b, l, 0, 0)),
            pl.BlockSpec((None, None, past, MLA_ROPE_DIM), lambda b: (b, l, 0, 0)),
            pl.BlockSpec((None, MLA_KV_LORA, MLA_HEADS * SLOT + 512), lambda b: (l, 0, 0)),
            pl.BlockSpec((MLA_ROPE_DIM, MLA_HEADS * SLOT), lambda b: (0, 0)),
        ],
        out_specs=[pl.BlockSpec((None, past, MLA_HEADS * SLOT), lambda b: (b, 0, 0)),
                   pl.BlockSpec((None, past, 512), lambda b: (b, 0, 0))],
        out_shape=[jax.ShapeDtypeStruct((batch, past, MLA_HEADS * SLOT), BF16),
                   jax.ShapeDtypeStruct((batch, past, 512), BF16)],
        compiler_params=_cparams(("arbitrary",)),
        name="kv_up_cache",
    )(cache_ckv, cache_kpe, w_ukv, place)


def _attn_mla_kernel(qm_ref, km_ref, vm_ref, ckm_ref, cvm_ref, out_ref):
    _mla_heads(qm_ref, [(_slot_get(km_ref), _slot_get(vm_ref)), (_slot_get(ckm_ref), _slot_get(cvm_ref))], out_ref)


def _attn_mla(qm, km, vm, ckm, cvm, *, batch, seq, tq):
    nb = seq // tq
    past = ckm.shape[1]
    return pl.pallas_call(
        _attn_mla_kernel,
        grid=(batch, nb),
        in_specs=[
            pl.BlockSpec((tq, MLA_HEADS * SLOT), lambda b, i: (b * nb + i, 0)),
            pl.BlockSpec((seq, MLA_HEADS * SLOT), lambda b, i: (b, 0)),
            pl.BlockSpec((seq, 512), lambda b, i: (b, 0)),
            pl.BlockSpec((None, past, MLA_HEADS * SLOT), lambda b, i: (b, 0, 0)),
            pl.BlockSpec((None, past, 512), lambda b, i: (b, 0, 0)),
        ],
        out_specs=pl.BlockSpec((tq, 512), lambda b, i: (b * nb + i, 0)),
        out_shape=jax.ShapeDtypeStruct((batch * seq, 512), BF16),
        compiler_params=_cparams(("arbitrary", "arbitrary")),
        name="attn_mla_lat",
    )(qm, km, vm, ckm, cvm)


POOL_HALO = 16


def _route(logits):
    lane = lax.broadcasted_iota(jnp.int32, logits.shape, 1).astype(F32)
    big = jnp.float32(1 << 20)
    is_g = lane < N_GROUPS
    gl = jnp.where(is_g, logits, NEG_INF)
    gmax = gl.max(axis=-1, keepdims=True)
    ge = jnp.where(is_g, jnp.exp(gl - gmax), 0.0)
    gp = ge / ge.sum(axis=-1, keepdims=True)
    g_w = jnp.where(is_g, gp, -1.0).max(axis=-1, keepdims=True)
    g_idx = jnp.where(is_g & (gp == g_w), lane, big).min(axis=-1, keepdims=True)
    e_lane = lane - N_GROUPS
    in_grp = (e_lane >= g_idx * EXPERTS_PER_GROUP) & (e_lane < (g_idx + 1) * EXPERTS_PER_GROUP)
    el = jnp.where(in_grp, logits, NEG_INF)
    emax = el.max(axis=-1, keepdims=True)
    ee = jnp.where(in_grp, jnp.exp(el - emax), 0.0)
    ep = ee / ee.sum(axis=-1, keepdims=True)
    p1 = jnp.where(in_grp, ep, -1.0).max(axis=-1, keepdims=True)
    i1 = jnp.where(in_grp & (ep == p1), lane, big).min(axis=-1, keepdims=True)
    rest = in_grp & (lane != i1)
    p2 = jnp.where(rest, ep, -1.0).max(axis=-1, keepdims=True)
    i2 = jnp.where(rest & (ep == p2), lane, big).min(axis=-1, keepdims=True)
    tot = p1 + p2
    gates = jnp.where(lane == i1, g_w * (p1 / tot), 0.0) + jnp.where(lane == i2, g_w * (p2 / tot), 0.0)
    return pltpu.roll(gates, LANES - N_GROUPS, axis=1)


def _merge_kernel(x_ref, u_ref, yna_ref, ymla_ref, gate_ref, gt1_ref, sh2_ref, sc2_ref, gpost_ref, gpre_ref,
                  wpool_ref, pscale_ref, wbr_ref, wout_ref, wr2_ref, wr1_ref,
                  xo_ref, h2_ref, rg_ref, *, seq, tm, steps_per_seq):
    nwin = min(tm + 2 * POOL_HALO, seq)
    if steps_per_seq == 1:
        r0 = 0
        s0 = 0
        uwin = u_ref[...]
        utile = uwin
    else:
        r0 = (pl.program_id(0) % steps_per_seq) * tm
        s0 = pl.multiple_of(jnp.clip(r0 - POOL_HALO, 0, seq - nwin), POOL_HALO)
        uwin = u_ref[pl.ds(s0, nwin), :]
        utile = u_ref[pl.ds(pl.multiple_of(r0, POOL_HALO), tm), :]
    t = r0 + lax.broadcasted_iota(jnp.int32, (tm, nwin), 0)
    s = s0 + lax.broadcasted_iota(jnp.int32, (tm, nwin), 1)
    t1 = r0 + lax.broadcasted_iota(jnp.int32, (tm, 1), 0)
    ds = []
    for g, w in enumerate(POOL_WINDOWS):
        lo = jnp.maximum(t - w // 2, 0)
        hi = jnp.minimum(t + (w - w // 2), seq)
        band = ((s >= lo) & (s < hi)).astype(BF16)
        cnt = (jnp.minimum(t1 + (w - w // 2), seq) - jnp.maximum(t1 - w // 2, 0)).astype(F32)
        sl = slice(g * POOL_GC, (g + 1) * POOL_GC)
        pooled = _dot(band, uwin[:, sl]) / cnt
        ds.append(pooled - utile[:, sl].astype(F32))
    d = jnp.concatenate(ds, axis=-1).astype(BF16)
    y_pool = (_dot(d, wpool_ref[...]) * pscale_ref[...]).astype(BF16)

    g = gate_ref[...]
    merged = (g[:, 0:D_MODEL].astype(F32) * _dot(y_pool, wbr_ref[0])
              + g[:, D_MODEL:2 * D_MODEL].astype(F32) * _dot(yna_ref[...], wbr_ref[1])
              + g[:, 2 * D_MODEL:].astype(F32) * _dot(ymla_ref[...], wbr_ref[2]))
    y = _dot(merged.astype(BF16), wout_ref[...])
    xn = x_ref[...] + gt1_ref[...] * _rms(y, gpost_ref[...])
    xo_ref[...] = xn

    h2 = _rms(xn, gpre_ref[...]) * (1.0 + sc2_ref[...]) + sh2_ref[...]
    h_hi = h2.astype(BF16)
    h2_ref[...] = h_hi
    h_lo = (h2 - h_hi.astype(F32)).astype(BF16)
    two = _dot(h_hi, wr2_ref[...])
    logits = two[:, :LANES] + two[:, LANES:] + _dot(h_lo, wr1_ref[...])
    rg_ref[...] = _route(logits)


def _merge(x, u, yna, ymla, gate, mod5, l, w, *, seq, ctx, tm):
    n = x.shape[0]
    steps_per_seq = seq // tm
    nseq = n // seq

    def mrow(i):
        return 0 if ctx else 1 + i // steps_per_seq

    const = lambda i: (l, 0, 0)
    modspec = lambda k: pl.BlockSpec((None, None, None, 1, D_MODEL), lambda i: (l, mrow(i), k, 0, 0))
    row = lambda wd: pl.BlockSpec((tm, wd), lambda i: (i, 0))
    return pl.pallas_call(
        functools.partial(_merge_kernel, seq=seq, tm=tm, steps_per_seq=steps_per_seq),
        grid=(n // tm,),
        in_specs=[
            row(D_MODEL),
            pl.BlockSpec((None, seq, 512), lambda i: (i // steps_per_seq, 0, 0)),
            row(512), row(512), row(3 * D_MODEL),
            modspec(2), modspec(3), modspec(4),
            pl.BlockSpec((None, 1, D_MODEL), const),
            pl.BlockSpec((None, 1, D_MODEL), const),
            pl.BlockSpec((None, 512, 512), const),
            pl.BlockSpec((None, 1, 512), const),
            pl.BlockSpec((None, 3, 512, D_MODEL), lambda i: (l, 0, 0, 0)),
            pl.BlockSpec((None, D_MODEL, D_MODEL), const),
            pl.BlockSpec((None, D_MODEL, 2 * LANES), const),
            pl.BlockSpec((None, D_MODEL, LANES), const),
        ],
        out_specs=[row(D_MODEL), row(D_MODEL), row(LANES)],
        out_shape=[jax.ShapeDtypeStruct((n, D_MODEL), F32), jax.ShapeDtypeStruct((n, D_MODEL), BF16),
                   jax.ShapeDtypeStruct((n, LANES), F32)],
        compiler_params=_cparams(("arbitrary",)),
        name="merge_ctx" if ctx else "merge_lat",
    )(x, u.reshape(nseq, seq, 512), yna, ymla, gate, mod5, mod5, mod5, w["g_post_mix"], w["g_pre_ffn"],
      w["w_pool_bd"], w["pool_scale"], w["w_branch"], w["w_out"], w["w_r2"], w["w_r1"])


MOE_CHUNK = 4


def _moe_kernel(h_ref, rg_ref, x_ref, gt2_ref, gpost_ref, w1_ref, w3_ref, w2_ref, o_ref, acc_ref):
    c = pl.program_id(1)
    h = h_ref[...]
    rg = rg_ref[...]
    lane = lax.broadcasted_iota(jnp.int32, rg.shape, 1)
    part = None
    for j in range(MOE_CHUNK):
        gate = jnp.where(lane == c * MOE_CHUNK + j, rg, 0.0).sum(axis=-1, keepdims=True)
        a = _dot(h, w1_ref[j])
        b = _dot(h, w3_ref[j])
        hid = (a * jax.nn.sigmoid(a)) * b * gate
        y = _dot(hid.astype(BF16), w2_ref[j])
        part = y if part is None else part + y

    @pl.when(c == 0)
    def _():
        acc_ref[...] = part

    @pl.when(c > 0)
    def _():
        acc_ref[...] += part

    @pl.when(c == pl.num_programs(1) - 1)
    def _():
        o_ref[...] = x_ref[...] + gt2_ref[...] * _rms(acc_ref[...], gpost_ref[...])


def _moe(h2, rg, x, mod5, l, w, *, seq, ctx, tm):
    n = x.shape[0]
    steps_per_seq = seq // tm if not ctx else 0

    def mrow(i):
        return 0 if ctx else 1 + i // steps_per_seq

    nchunk = N_EXPERTS // MOE_CHUNK
    row = lambda wd: pl.BlockSpec((tm, wd), lambda i, c: (i, 0))
    wspec = lambda a, b: pl.BlockSpec((None, MOE_CHUNK, a, b), lambda i, c: (l, c, 0, 0))
    return pl.pallas_call(
        _moe_kernel,
        grid=(n // tm, nchunk),
        in_specs=[
            row(D_MODEL), row(LANES), row(D_MODEL),
            pl.BlockSpec((None, None, None, 1, D_MODEL), lambda i, c: (l, mrow(i), 5, 0, 0)),
            pl.BlockSpec((None, 1, D_MODEL), lambda i, c: (l, 0, 0)),
            wspec(D_MODEL, EXPERT_FF), wspec(D_MODEL, EXPERT_FF), wspec(EXPERT_FF, D_MODEL),
        ],
        out_specs=row(D_MODEL),
        out_shape=jax.ShapeDtypeStruct((n, D_MODEL), F32),
        scratch_shapes=[pltpu.VMEM((tm, D_MODEL), F32)],
        compiler_params=_cparams(("arbitrary", "arbitrary")),
        name="moe_ctx" if ctx else "moe_lat",
    )(h2, rg, x, mod5, w["g_post_ffn"], w["w1"], w["w3"], w["w2"])


def _rope_table(seq):
    t = np.arange(seq)
    n_freq = MLA_ROPE_DIM // 4
    inv = jnp.asarray(ROPE_BASE, F32) ** (-jnp.arange(n_freq, dtype=F32) / n_freq)
    ang_r = jnp.asarray(t // GRID_W, F32)[:, None] * inv
    ang_c = jnp.asarray(t % GRID_W, F32)[:, None] * inv
    cos = jnp.cos(jnp.stack([ang_r, ang_c], axis=1))
    sin = jnp.sin(jnp.stack([ang_r, ang_c], axis=1))
    zeros = jnp.zeros_like(sin)
    c32 = jnp.stack([cos, cos], axis=2).reshape(seq, MLA_ROPE_DIM)
    up32 = jnp.stack([-sin, zeros], axis=2).reshape(seq, MLA_ROPE_DIM)
    dn32 = jnp.stack([zeros, sin], axis=2).reshape(seq, MLA_ROPE_DIM)

    def slot(v, fill):
        return jnp.concatenate([jnp.full((seq, KPE_LANE), fill, F32), v,
                                jnp.full((seq, SLOT - KPE_LANE - MLA_ROPE_DIM), fill, F32)], axis=1)

    return jnp.concatenate([slot(c32, 1.0), slot(up32, 0.0), slot(dn32, 0.0)], axis=1)


def _na_row_classes(n_rows):
    i = np.arange(NA_BLOCK_ROWS)[:, None]
    j = np.arange(NA_SLAB_ROWS)[None, :]
    last_ks = n_rows - NA_SLAB_ROWS
    last_q0 = n_rows - NA_BLOCK_ROWS
    specs = [
        (j < NA_WIN_ROWS, j - i),
        ((j >= i) & (j < i + NA_WIN_ROWS), j - NA_WIN_ROWS // 2 - i),
        (last_ks + j >= n_rows - NA_WIN_ROWS, last_ks + j - last_q0 - i),
    ]
    ok = np.stack([np.broadcast_to(o, (NA_BLOCK_ROWS, NA_SLAB_ROWS)) for o, _ in specs])
    dr = np.stack([np.clip(d + NA_WIN_ROWS - 1, 0, 2 * NA_WIN_ROWS - 2) for _, d in specs])
    return ok, dr


def _bias_kernel(rpb_ref, out_ref, *, row_ok, row_dr):
    reach = NA_WIN_COLS - 1
    rows = rpb_ref[...]
    v_lo = pltpu.roll(rows, SLOT - reach, axis=1)
    v_hi = pltpu.roll(rows, GRID_W - reach, axis=1)
    lane1 = lax.broadcasted_iota(jnp.int32, (1, SLOT), 1)
    mid = (lane1 >= GRID_W - reach) & (lane1 <= GRID_W + reach)
    c = lax.broadcasted_iota(jnp.int32, (GRID_W, SLOT), 0)
    lane = lax.broadcasted_iota(jnp.int32, (GRID_W, SLOT), 1)
    kc = lane & (GRID_W - 1)
    c0 = jnp.clip(c - NA_WIN_COLS // 2, 0, GRID_W - NA_WIN_COLS)
    in_col = (kc >= c0) & (kc < c0 + NA_WIN_COLS)
    first = lane < GRID_W
    for k in range(3):
        for i in range(NA_BLOCK_ROWS):
            for jp in range(NA_SLAB_ROWS // 2):
                ok_a, ok_b = bool(row_ok[k, i, 2 * jp]), bool(row_ok[k, i, 2 * jp + 1])
                a, b = int(row_dr[k, i, 2 * jp]), int(row_dr[k, i, 2 * jp + 1])
                rs = slice(i * GRID_W, (i + 1) * GRID_W)
                cs = slice(jp * SLOT, (jp + 1) * SLOT)
                if not (ok_a or ok_b):
                    out_ref[k, rs, cs] = jnp.full((GRID_W, SLOT), NEG_INF, out_ref.dtype)
                    continue
                v = jnp.where(mid, v_hi[b:b + 1], v_lo[a:a + 1])
                tile = pltpu.roll(jnp.broadcast_to(v, (GRID_W, SLOT)), 0, axis=1, stride=1, stride_axis=0)
                ok = in_col
                if not ok_b:
                    ok = ok & first
                if not ok_a:
                    ok = ok & jnp.logical_not(first)
                out_ref[k, rs, cs] = jnp.where(ok, tile, NEG_INF).astype(out_ref.dtype)


def _na_bias_table(rpb, n_rows):
    n_dr, n_dc = rpb.shape[-2:]
    row_ok, row_dr = _na_row_classes(n_rows)
    rpb_p = jnp.pad(rpb, ((0, 0), (0, 0), (0, 0), (0, SLOT - n_dc)))
    tq, tk = NA_BLOCK_ROWS * GRID_W, NA_SLAB_ROWS * GRID_W
    return pl.pallas_call(
        functools.partial(_bias_kernel, row_ok=row_ok, row_dr=row_dr),
        grid=(DEPTH, NA_HEADS),
        in_specs=[pl.BlockSpec((None, None, n_dr, SLOT), lambda l, h: (l, h, 0, 0))],
        out_specs=pl.BlockSpec((None, 3, None, tq, tk), lambda l, h: (l, 0, h, 0, 0)),
        out_shape=jax.ShapeDtypeStruct((DEPTH, 3, NA_HEADS, tq, tk), BF16),
        compiler_params=_cparams(("arbitrary", "arbitrary")),
        name="na_bias_table",
    )(rpb_p)


def _prep(p):
    L = DEPTH
    w_in = p["w_in"]
    kpe0 = C_G
    kpe1 = C_G + MLA_ROPE_DIM
    w_in_p = jnp.concatenate(
        [w_in[..., :kpe0], w_in[..., kpe1:], jnp.zeros((L, D_MODEL, KPE_LANE), F32), w_in[..., kpe0:kpe1],
         jnp.zeros((L, D_MODEL, SLOT - KPE_LANE - MLA_ROPE_DIM), F32)], axis=-1).astype(BF16)
    w_uq = jnp.pad(p["mla_w_uq"].reshape(L, MLA_Q_LORA, MLA_HEADS, MLA_QK_DIM),
                   ((0, 0), (0, 0), (0, 0), (0, SLOT - MLA_QK_DIM))).reshape(L, MLA_Q_LORA, MLA_HEADS * SLOT)
    ukv = p["mla_w_ukv"].reshape(L, MLA_KV_LORA, MLA_HEADS, MLA_NOPE_DIM + MLA_V_DIM)
    uk = jnp.pad(ukv[..., :MLA_NOPE_DIM], ((0, 0), (0, 0), (0, 0), (0, SLOT - MLA_NOPE_DIM)))
    w_ukv = jnp.concatenate([uk.reshape(L, MLA_KV_LORA, MLA_HEADS * SLOT),
                             ukv[..., MLA_NOPE_DIM:].reshape(L, MLA_KV_LORA, MLA_HEADS * MLA_V_DIM)], axis=-1)
    w_pool_bd = jnp.einsum("lgcd,gh->lgchd", p["w_pool"], jnp.eye(4, dtype=F32)).reshape(L, POOL_WIDTH, POOL_WIDTH)
    w_r = jnp.concatenate([p["w_group_router"], p["w_expert_router"]], axis=-1)
    r_hi = w_r.astype(BF16)
    r_lo = (w_r - r_hi.astype(F32)).astype(BF16)
    padl = lambda a: jnp.pad(a, ((0, 0), (0, 0), (0, LANES - a.shape[-1])))
    vec = lambda a: a.reshape(L, 1, a.shape[-1])
    return {
        "w_in": w_in_p,
        "w_uq": w_uq.astype(BF16),
        "w_ukv": w_ukv.astype(BF16),
        "w_pool_bd": w_pool_bd.astype(BF16),
        "w_branch": p["w_branch"].astype(BF16),
        "w_out": p["w_out"].astype(BF16),
        "w_r2": jnp.concatenate([padl(r_hi), padl(r_lo)], axis=-1),
        "w_r1": padl(r_hi),
        "w1": p["w_exp_gate"].astype(BF16),
        "w3": p["w_exp_up"].astype(BF16),
        "w2": p["w_exp_down"].astype(BF16),
        "g_pre_mix": vec(p["g_pre_mix"]), "g_post_mix": vec(p["g_post_mix"]),
        "g_pre_ffn": vec(p["g_pre_ffn"]), "g_post_ffn": vec(p["g_post_ffn"]),
        "pool_scale": vec(p["pool_scale"]), "q_norm": vec(p["mla_q_norm"]), "kv_norm": vec(p["mla_kv_norm"]),
    }


def kernel(x_prompt, x_sample, cache_na_k, cache_na_v, cache_mla_ckv, cache_mla_kpe, c, c_ctx, w_mod, b_mod,
           g_pre_mix, g_post_mix, g_pre_ffn, g_post_ffn, w_in, w_pool, pool_scale, na_rpb, mla_q_norm, mla_w_uq,
           mla_kv_norm, mla_w_ukv, w_branch, w_out, w_group_router, w_expert_router, w_exp_gate, w_exp_up,
           w_exp_down):
    batch, seq, _ = x_prompt.shape
    dbatch, dseq, _ = x_sample.shape
    past = cache_na_k.shape[2]
    assert dbatch + 1 <= 8 and dseq % (NA_BLOCK_ROWS * GRID_W) == 0

    w = _prep(dict(w_in=w_in, w_pool=w_pool, pool_scale=pool_scale, mla_q_norm=mla_q_norm, mla_w_uq=mla_w_uq,
                   mla_kv_norm=mla_kv_norm, mla_w_ukv=mla_w_ukv, w_branch=w_branch, w_out=w_out,
                   w_group_router=w_group_router, w_expert_router=w_expert_router, w_exp_gate=w_exp_gate,
                   w_exp_up=w_exp_up, w_exp_down=w_exp_down, g_pre_mix=g_pre_mix, g_post_mix=g_post_mix,
                   g_pre_ffn=g_pre_ffn, g_post_ffn=g_post_ffn))
    rope_tab = _rope_table(dseq)
    bias_tab = _na_bias_table(na_rpb, dseq // GRID_W)
    place = jnp.tile(jnp.pad(jnp.eye(MLA_ROPE_DIM, dtype=F32), ((0, 0), (KPE_LANE, SLOT - KPE_LANE - MLA_ROPE_DIM))),
                     (1, MLA_HEADS)).astype(BF16)

    cvec = jnp.concatenate([c_ctx[None], c, jnp.zeros((8 - 1 - dbatch, D_MODEL), F32)], axis=0)
    mod5 = _modulation(cvec, w_mod, b_mod).reshape(DEPTH, 8, 6, 1, D_MODEL)

    ck = cache_na_k.reshape(dbatch, DEPTH, past, NA_WIDTH)
    cv = cache_na_v.reshape(dbatch, DEPTH, past, NA_WIDTH)

    xp = x_prompt.reshape(batch * seq, D_MODEL)
    nk_l, nv_l, ckv_l, kpe_l = [], [], [], []
    for l in range(DEPTH):
        u, nq, nk, nv, qm, km, vm, gate, ckv, kpe = _inproj(xp, mod5, l, w, seq=seq, rope_tab=None, ctx_out=True, tm=256)
        nk_l.append(nk)
        nv_l.append(nv)
        ckv_l.append(ckv)
        kpe_l.append(kpe)
        yna, ymla = _attn_ctx(nq, nk, nv, qm, km, vm, seq=seq)
        xp, h2, rg = _merge(xp, u, yna, ymla, gate, mod5, l, w, seq=seq, ctx=True, tm=seq)
        xp = _moe(h2, rg, xp, mod5, l, w, seq=seq, ctx=True, tm=1024)

    xs = x_sample.reshape(dbatch * dseq, D_MODEL)
    for l in range(DEPTH):
        u, nq, nk, nv, qm, km, vm, gate = _inproj(xs, mod5, l, w, seq=dseq, rope_tab=rope_tab, ctx_out=False, tm=256)
        yna = _attn_na(nq, nk, nv, ck, cv, bias_tab, l, batch=dbatch, seq=dseq)
        ckm, cvm = _kvup(cache_mla_ckv, cache_mla_kpe, w["w_ukv"], place, l)
        ymla = _attn_mla(qm, km, vm, ckm, cvm, batch=dbatch, seq=dseq, tq=256)
        xs, h2, rg = _merge(xs, u, yna, ymla, gate, mod5, l, w, seq=dseq, ctx=False, tm=256)
        xs = _moe(h2, rg, xs, mod5, l, w, seq=dseq, ctx=False, tm=1024)

    stack = lambda parts, tail: jnp.stack([a.reshape((batch, seq) + tail) for a in parts], axis=1)
    return (xp.reshape(batch, seq, D_MODEL), xs.reshape(dbatch, dseq, D_MODEL),
            stack(nk_l, (NA_HEADS, NA_HEAD_DIM)), stack(nv_l, (NA_HEADS, NA_HEAD_DIM)),
            stack(ckv_l, (MLA_KV_LORA,)), stack(kpe_l, (MLA_ROPE_DIM,)))
```

```python
import functools

import numpy as np
import jax
import jax.numpy as jnp
from jax import lax
from jax.experimental import pallas as pl
from jax.experimental.pallas import tpu as pltpu

F32 = jnp.float32
BF16 = jnp.bfloat16

D_MODEL = 1024
DEPTH = 2
GRID_W = 64
POOL_WINDOWS = (2, 4, 8, 16)
POOL_GC = 128
POOL_WIDTH = 512
NA_HEADS = 8
NA_HEAD_DIM = 64
NA_WIDTH = 512
NA_WIN_ROWS = 8
NA_WIN_COLS = 16
MLA_HEADS = 8
MLA_NOPE_DIM = 64
MLA_ROPE_DIM = 32
MLA_V_DIM = 64
MLA_QK_DIM = 96
MLA_Q_LORA = 384
MLA_KV_LORA = 256
N_GROUPS = 4
EXPERTS_PER_GROUP = 4
N_EXPERTS = 16
EXPERT_FF = 256
ROPE_BASE = 10000.0
EPS = 1e-6
NEG_INF = -1e30

LANES = 128
SLOT = LANES
VMEM_LIMIT = 56 * 1024 * 1024

C_U, C_NQ, C_NK, C_NV, C_CQ, C_CKV, C_G, C_KPE, C_END = 0, 512, 1024, 1536, 2048, 2432, 2688, 5760, 5888
KPE_LANE = MLA_NOPE_DIM

LOG2E = 1.4426950408889634
VT_ROWS = 80
VT_ALL = NA_HEADS * VT_ROWS

NA_BLOCK_ROWS = 4
NA_SLAB_ROWS = 12


def _cparams(sem):
    return pltpu.CompilerParams(dimension_semantics=sem, vmem_limit_bytes=VMEM_LIMIT)


def _rms(x, g):
    return x * lax.rsqrt(jnp.mean(x * x, axis=-1, keepdims=True) + EPS) * g


def _dot(a, b):
    return jnp.dot(a, b, preferred_element_type=F32)


def _dot_nt(a, b):
    return lax.dot_general(a, b, (((1,), (1,)), ((), ())), preferred_element_type=F32)


def _mod_kernel(c_ref, w_ref, b_ref, o_ref):
    c = c_ref[...]
    s = (c * jax.nn.sigmoid(c)).astype(BF16)
    o_ref[...] = _dot(s, w_ref[...].astype(BF16)) + b_ref[...]


def _modulation(cvec, w_mod, b_mod):
    tn = 1536
    n = w_mod.shape[-1]
    return pl.pallas_call(
        _mod_kernel,
        grid=(DEPTH, n // tn),
        in_specs=[
            pl.BlockSpec((8, D_MODEL), lambda l, j: (0, 0)),
            pl.BlockSpec((None, D_MODEL, tn), lambda l, j: (l, 0, j)),
            pl.BlockSpec((None, 1, tn), lambda l, j: (l, 0, j)),
        ],
        out_specs=pl.BlockSpec((None, 8, tn), lambda l, j: (l, 0, j)),
        out_shape=jax.ShapeDtypeStruct((DEPTH, 8, n), F32),
        compiler_params=_cparams(("arbitrary", "arbitrary")),
        name="modulation",
    )(cvec, w_mod, b_mod.reshape(DEPTH, 1, n))


def _rope_slot(x, tab):
    c, s_up, s_dn = tab[:, 0:SLOT], tab[:, SLOT:2 * SLOT], tab[:, 2 * SLOT:3 * SLOT]
    half = MLA_ROPE_DIM // 4
    x_up = pltpu.roll(x, SLOT - half, axis=1)
    x_dn = pltpu.roll(x, half, axis=1)
    return x * c + x_up * s_up + x_dn * s_dn


def _store_vt(vt_ref, v):
    vt = v.T
    tm = v.shape[0]
    for hd in range(NA_HEADS):
        vt_ref[hd * VT_ROWS:hd * VT_ROWS + NA_HEAD_DIM, :] = vt[hd * NA_HEAD_DIM:(hd + 1) * NA_HEAD_DIM, :].astype(BF16)
        vt_ref[hd * VT_ROWS + NA_HEAD_DIM:(hd + 1) * VT_ROWS, :] = jnp.ones((VT_ROWS - NA_HEAD_DIM, tm), BF16)


def _inproj_kernel(*refs, rope, ctx_out, stacked):
    n_in = 9 + (1 if rope else 0) + (4 if stacked else 0)
    ins, outs = refs[:n_in], refs[n_in:]
    x_ref, sh_ref, sc_ref, g_ref, win_ref, qn_ref, wuq_ref, kvn_ref, wukv_ref = ins[:9]
    tab_ref = ins[9] if rope else None
    prev = ins[n_in - 4:] if stacked else None
    if ctx_out:
        u_ref, nq_ref, nvt_ref, qm_ref, km_ref, vmt_ref, gate_ref, nk_ref, nv_ref, ckv_ref, kpe_ref = outs
    else:
        u_ref, nq_ref, nvt_ref, qm_ref, km_ref, vmt_ref, gate_ref, nk_ref = outs

    def put(ref, k, val):
        if stacked:
            ref[0] = prev[k][...]
            ref[1] = val
        else:
            ref[...] = val

    x = x_ref[...]
    h = (_rms(x, g_ref[...]) * (1.0 + sc_ref[...]) + sh_ref[...]).astype(BF16)

    def seg(a, b):
        return _dot(h, win_ref[:, a:b])

    u_ref[...] = seg(C_U, C_NQ).astype(u_ref.dtype)
    nq_ref[...] = (seg(C_NQ, C_NK) * (NA_HEAD_DIM ** -0.5 * LOG2E)).astype(BF16)
    nk = seg(C_NK, C_NV)
    nv = seg(C_NV, C_CQ)
    if ctx_out:
        put(nk_ref, 0, nk)
        put(nv_ref, 1, nv)
    else:
        nk_ref[...] = nk.astype(BF16)
    _store_vt(nvt_ref, nv)
    gate_ref[...] = jax.nn.sigmoid(seg(C_G, C_KPE)).astype(BF16)

    cq = _rms(seg(C_CQ, C_CKV), qn_ref[...]).astype(BF16)
    q = _dot(cq, wuq_ref[...])
    tab = tab_ref[...] if rope else None
    for hd in range(MLA_HEADS):
        qs = q[:, hd * SLOT:(hd + 1) * SLOT]
        if rope:
            qs = _rope_slot(qs, tab)
        qm_ref[:, hd * SLOT:(hd + 1) * SLOT] = (qs * (MLA_QK_DIM ** -0.5 * LOG2E)).astype(BF16)

    ckv = _rms(seg(C_CKV, C_G), kvn_ref[...])
    kpe_slot = seg(C_KPE, C_END)
    if ctx_out:
        put(ckv_ref, 2, ckv)
        put(kpe_ref, 3, kpe_slot[:, KPE_LANE:KPE_LANE + MLA_ROPE_DIM])
    if rope:
        kpe_slot = _rope_slot(kpe_slot, tab)
    kv = _dot(ckv.astype(BF16), wukv_ref[...])
    for hd in range(MLA_HEADS):
        km_ref[:, hd * SLOT:(hd + 1) * SLOT] = (kv[:, hd * SLOT:(hd + 1) * SLOT] + kpe_slot).astype(BF16)
    _store_vt(vmt_ref, kv[:, MLA_HEADS * SLOT:])


def _inproj(x, mod5, l, w, *, seq, rope_tab, ctx_out, tm, prev=None):
    n = x.shape[0]
    steps_per_seq = seq // tm if not ctx_out else 0
    stacked = prev is not None
    assert not stacked or (ctx_out and tm == seq and l == 1)

    def mrow(i):
        return 0 if ctx_out else 1 + i // steps_per_seq

    const = lambda i: (l, 0, 0)
    in_specs = [
        pl.BlockSpec((tm, D_MODEL), lambda i: (i, 0)),
        pl.BlockSpec((None, None, None, 1, D_MODEL), lambda i: (l, mrow(i), 0, 0, 0)),
        pl.BlockSpec((None, None, None, 1, D_MODEL), lambda i: (l, mrow(i), 1, 0, 0)),
        pl.BlockSpec((None, 1, D_MODEL), const),
        pl.BlockSpec((None, D_MODEL, C_END), const),
        pl.BlockSpec((None, 1, MLA_Q_LORA), const),
        pl.BlockSpec((None, MLA_Q_LORA, MLA_HEADS * SLOT), const),
        pl.BlockSpec((None, 1, MLA_KV_LORA), const),
        pl.BlockSpec((None, MLA_KV_LORA, MLA_HEADS * SLOT + 512), const),
    ]
    args = [x, mod5, mod5, w["g_pre_mix"], w["w_in"], w["q_norm"], w["w_uq"], w["kv_norm"], w["w_ukv"]]
    rope = rope_tab is not None
    if rope:
        in_specs.append(pl.BlockSpec((tm, 3 * SLOT), lambda i: (i % steps_per_seq, 0)))
        args.append(rope_tab)
    row = lambda wd: pl.BlockSpec((tm, wd), lambda i: (i, 0))
    col = pl.BlockSpec((VT_ALL, tm), lambda i: (0, i))
    rows_of = lambda wd, dt: jax.ShapeDtypeStruct((n, wd), dt)
    cols = jax.ShapeDtypeStruct((VT_ALL, n), BF16)
    out_specs = [row(512), row(512), col, row(MLA_HEADS * SLOT), row(MLA_HEADS * SLOT), col, row(3 * D_MODEL)]
    out_shape = [rows_of(512, BF16), rows_of(512, BF16), cols, rows_of(MLA_HEADS * SLOT, BF16),
                 rows_of(MLA_HEADS * SLOT, BF16), cols, rows_of(3 * D_MODEL, BF16)]
    cache_w = [512, 512, MLA_KV_LORA, MLA_ROPE_DIM] if ctx_out else []
    if not ctx_out:
        out_specs.append(row(512))
        out_shape.append(rows_of(512, BF16))
    elif stacked:
        in_specs += [row(wd) for wd in cache_w]
        args += list(prev)
        out_specs += [pl.BlockSpec((None, DEPTH, seq, wd), lambda i: (i, 0, 0, 0)) for wd in cache_w]
        out_shape += [jax.ShapeDtypeStruct((n // seq, DEPTH, seq, wd), F32) for wd in cache_w]
    else:
        out_specs += [row(wd) for wd in cache_w]
        out_shape += [rows_of(wd, F32) for wd in cache_w]
    return pl.pallas_call(
        functools.partial(_inproj_kernel, rope=rope, ctx_out=ctx_out, stacked=stacked),
        grid=(n // tm,),
        in_specs=in_specs,
        out_specs=out_specs,
        out_shape=out_shape,
        compiler_params=_cparams(("arbitrary",)),
        name="inproj_ctx" if ctx_out else "inproj_lat",
    )(*args)


def _attend_t(q_h, segs):
    scores = []
    for k, _, bias in segs:
        s = _dot_nt(k, q_h)
        if bias is not None:
            s = s + bias
        scores.append(s)
    m = scores[0].max(axis=0, keepdims=True)
    for s in scores[1:]:
        m = jnp.maximum(m, s.max(axis=0, keepdims=True))
    acc = None
    for s, (_, vt, _) in zip(scores, segs):
        o = _dot(vt, jnp.exp2(s - m).astype(BF16))
        acc = o if acc is None else acc + o
    return acc[:NA_HEAD_DIM] / acc[NA_HEAD_DIM:NA_HEAD_DIM + 1]


def _attn_heads(out_ref, q_of, segs_of):
    for p in range(NA_HEADS // 2):
        pair_t = jnp.concatenate([_attend_t(q_of(2 * p + e), segs_of(2 * p + e)) for e in range(2)], axis=0)
        out_ref[:, p * SLOT:(p + 1) * SLOT] = pair_t.T.astype(out_ref.dtype)


def _pair_q(q_ref):
    lo = lax.broadcasted_iota(jnp.int32, (1, SLOT), 1) < NA_HEAD_DIM

    def q_of(h):
        q_pair = q_ref[:, (h // 2) * SLOT:(h // 2 + 1) * SLOT]
        keep = lo if h % 2 == 0 else jnp.logical_not(lo)
        return jnp.where(keep, q_pair, jnp.zeros_like(q_pair))
    return q_of


def _slot(ref, i, rows=slice(None)):
    return ref[rows, i * SLOT:(i + 1) * SLOT].astype(BF16)


def _vt(ref, h, cols=slice(None)):
    return ref[h * VT_ROWS:(h + 1) * VT_ROWS, cols]


def _attn_ctx_kernel(nq_ref, nk_ref, nvt_ref, qm_ref, km_ref, vmt_ref, yna_ref, ymla_ref):
    _attn_heads(yna_ref, _pair_q(nq_ref), lambda h: [(_slot(nk_ref, h // 2), _vt(nvt_ref, h), None)])
    _attn_heads(ymla_ref, lambda h: _slot(qm_ref, h), lambda h: [(_slot(km_ref, h), _vt(vmt_ref, h), None)])


def _attn_ctx(nq, nk, nvt, qm, km, vmt, *, seq, nk_layer=None):
    n = nq.shape[0]
    spec = lambda wd: pl.BlockSpec((seq, wd), lambda b: (b, 0))
    tspec = pl.BlockSpec((VT_ALL, seq), lambda b: (0, b))
    nk_spec = spec(512) if nk_layer is None else pl.BlockSpec((None, None, seq, 512), lambda b: (b, nk_layer, 0, 0))
    return pl.pallas_call(
        _attn_ctx_kernel,
        grid=(n // seq,),
        in_specs=[spec(512), nk_spec, tspec, spec(1024), spec(1024), tspec],
        out_specs=[spec(512), spec(512)],
        out_shape=[jax.ShapeDtypeStruct((n, 512), BF16)] * 2,
        compiler_params=_cparams(("arbitrary",)),
        name="attn_ctx",
    )(nq, nk, nvt, qm, km, vmt)


def _attn_na_kernel(nq_ref, nk_ref, nvt_ref, ck_ref, cvt_ref, bias_ref, out_ref, *, n_blocks):
    i = pl.program_id(1)
    ks = jnp.clip(NA_BLOCK_ROWS * i - NA_WIN_ROWS // 2, 0, n_blocks * NA_BLOCK_ROWS - NA_SLAB_ROWS)
    slab = pl.ds(pl.multiple_of(ks * GRID_W, NA_BLOCK_ROWS * GRID_W), NA_SLAB_ROWS * GRID_W)

    def segs_of(h):
        return [(_slot(nk_ref, h // 2, slab), _vt(nvt_ref, h, slab), bias_ref[h].astype(F32)),
                (_slot(ck_ref, h // 2), _vt(cvt_ref, h), None)]

    _attn_heads(out_ref, _pair_q(nq_ref), segs_of)


def _attn_na(nq, nk, nvt, cache_k, cvt, bias_tab, l, *, batch, seq):
    tq = NA_BLOCK_ROWS * GRID_W
    nb = seq // tq
    past = cache_k.shape[2]
    assert (NA_WIN_ROWS // 2) % NA_BLOCK_ROWS == 0 and (seq // GRID_W - NA_SLAB_ROWS) % NA_BLOCK_ROWS == 0

    def cls(i):
        return jnp.where(i == 0, 0, jnp.where(i == nb - 1, 2, 1))

    return pl.pallas_call(
        functools.partial(_attn_na_kernel, n_blocks=nb),
        grid=(batch, nb),
        in_specs=[
            pl.BlockSpec((tq, 512), lambda b, i: (b * nb + i, 0)),
            pl.BlockSpec((seq, 512), lambda b, i: (b, 0)),
            pl.BlockSpec((VT_ALL, seq), lambda b, i: (0, b)),
            pl.BlockSpec((None, None, past, 512), lambda b, i: (b, l, 0, 0)),
            pl.BlockSpec((None, VT_ALL, past), lambda b, i: (b, 0, 0)),
            pl.BlockSpec((None, None, NA_HEADS, NA_SLAB_ROWS * GRID_W, tq), lambda b, i: (l, cls(i), 0, 0, 0)),
        ],
        out_specs=pl.BlockSpec((tq, 512), lambda b, i: (b * nb + i, 0)),
        out_shape=jax.ShapeDtypeStruct((batch * seq, 512), BF16),
        compiler_params=_cparams(("arbitrary", "arbitrary")),
        name="attn_na_lat",
    )(nq, nk, nvt, cache_k, cvt, bias_tab)


def _cache_prep_kernel(ckv_ref, kpe_ref, nv_ref, wukv_ref, place_ref, km_ref, vmt_ref, nvt_ref):
    kv = _dot(ckv_ref[...].astype(BF16), wukv_ref[...])
    kslots = kv[:, :MLA_HEADS * SLOT] + _dot(kpe_ref[...].astype(BF16), place_ref[...])
    km_ref[...] = kslots.astype(BF16)
    _store_vt(vmt_ref, kv[:, MLA_HEADS * SLOT:])
    _store_vt(nvt_ref, nv_ref[...])


def _cache_prep(cache_ckv, cache_kpe, cache_nv, w_ukv, place, l):
    batch, _, past, _ = cache_ckv.shape
    tout = pl.BlockSpec((None, VT_ALL, past), lambda b: (b, 0, 0))
    tshape = jax.ShapeDtypeStruct((batch, VT_ALL, past), BF16)
    return pl.pallas_call(
        _cache_prep_kernel,
        grid=(batch,),
        in_specs=[
            pl.BlockSpec((None, None, past, MLA_KV_LORA), lambda b: (b, l, 0, 0)),
            pl.BlockSpec((None, None, past, MLA_ROPE_DIM), lambda b: (b, l, 0, 0)),
            pl.BlockSpec((None, None, past, 512), lambda b: (b, l, 0, 0)),
            pl.BlockSpec((None, MLA_KV_LORA, MLA_HEADS * SLOT + 512), lambda b: (l, 0, 0)),
            pl.BlockSpec((MLA_ROPE_DIM, MLA_HEADS * SLOT), lambda b: (0, 0)),
        ],
        out_specs=[pl.BlockSpec((None, past, MLA_HEADS * SLOT), lambda b: (b, 0, 0)), tout, tout],
        out_shape=[jax.ShapeDtypeStruct((batch, past, MLA_HEADS * SLOT), BF16), tshape, tshape],
        compiler_params=_cparams(("arbitrary",)),
        name="cache_prep",
    )(cache_ckv, cache_kpe, cache_nv, w_ukv, place)


def _attn_mla_kernel(qm_ref, km_ref, vmt_ref, ckm_ref, cvmt_ref, out_ref):
    _attn_heads(out_ref, lambda h: _slot(qm_ref, h),
                lambda h: [(_slot(km_ref, h), _vt(vmt_ref, h), None), (_slot(ckm_ref, h), _vt(cvmt_ref, h), None)])


def _attn_mla(qm, km, vmt, ckm, cvmt, *, batch, seq, tq):
    nb = seq // tq
    past = ckm.shape[1]
    return pl.pallas_call(
        _attn_mla_kernel,
        grid=(batch, nb),
        in_specs=[
            pl.BlockSpec((tq, MLA_HEADS * SLOT), lambda b, i: (b * nb + i, 0)),
            pl.BlockSpec((seq, MLA_HEADS * SLOT), lambda b, i: (b, 0)),
            pl.BlockSpec((VT_ALL, seq), lambda b, i: (0, b)),
            pl.BlockSpec((None, past, MLA_HEADS * SLOT), lambda b, i: (b, 0, 0)),
            pl.BlockSpec((None, VT_ALL, past), lambda b, i: (b, 0, 0)),
        ],
        out_specs=pl.BlockSpec((tq, 512), lambda b, i: (b * nb + i, 0)),
        out_shape=jax.ShapeDtypeStruct((batch * seq, 512), BF16),
        compiler_params=_cparams(("arbitrary", "arbitrary")),
        name="attn_mla_lat",
    )(qm, km, vmt, ckm, cvmt)


POOL_HALO = 16


def _route(logits):
    lane = lax.broadcasted_iota(jnp.int32, logits.shape, 1).astype(F32)
    big = jnp.float32(1 << 20)
    is_g = lane < N_GROUPS
    gl = jnp.where(is_g, logits, NEG_INF)
    gmax = gl.max(axis=-1, keepdims=True)
    ge = jnp.where(is_g, jnp.exp(gl - gmax), 0.0)
    gp = ge / ge.sum(axis=-1, keepdims=True)
    g_w = jnp.where(is_g, gp, -1.0).max(axis=-1, keepdims=True)
    g_idx = jnp.where(is_g & (gp == g_w), lane, big).min(axis=-1, keepdims=True)
    e_lane = lane - N_GROUPS
    in_grp = (e_lane >= g_idx * EXPERTS_PER_GROUP) & (e_lane < (g_idx + 1) * EXPERTS_PER_GROUP)
    el = jnp.where(in_grp, logits, NEG_INF)
    emax = el.max(axis=-1, keepdims=True)
    ee = jnp.where(in_grp, jnp.exp(el - emax), 0.0)
    ep = ee / ee.sum(axis=-1, keepdims=True)
    p1 = jnp.where(in_grp, ep, -1.0).max(axis=-1, keepdims=True)
    i1 = jnp.where(in_grp & (ep == p1), lane, big).min(axis=-1, keepdims=True)
    rest = in_grp & (lane != i1)
    p2 = jnp.where(rest, ep, -1.0).max(axis=-1, keepdims=True)
    i2 = jnp.where(rest & (ep == p2), lane, big).min(axis=-1, keepdims=True)
    tot = p1 + p2
    gates = jnp.where(lane == i1, g_w * (p1 / tot), 0.0) + jnp.where(lane == i2, g_w * (p2 / tot), 0.0)
    return pltpu.roll(gates, LANES - N_GROUPS, axis=1)


def _merge_kernel(x_ref, u_ref, yna_ref, ymla_ref, gate_ref, gt1_ref, sh2_ref, sc2_ref, gpost_ref, gpre_ref,
                  wpool_ref, pscale_ref, wbr_ref, wout_ref, wr2_ref, wr1_ref,
                  xo_ref, h2_ref, rg_ref, *, seq, tm, steps_per_seq):
    nwin = min(tm + 2 * POOL_HALO, seq)
    if steps_per_seq == 1:
        r0 = 0
        s0 = 0
        uwin = u_ref[...]
        utile = uwin
    else:
        r0 = (pl.program_id(0) % steps_per_seq) * tm
        s0 = pl.multiple_of(jnp.clip(r0 - POOL_HALO, 0, seq - nwin), POOL_HALO)
        uwin = u_ref[pl.ds(s0, nwin), :]
        utile = u_ref[pl.ds(pl.multiple_of(r0, POOL_HALO), tm), :]
    t = r0 + lax.broadcasted_iota(jnp.int32, (tm, nwin), 0)
    s = s0 + lax.broadcasted_iota(jnp.int32, (tm, nwin), 1)
    t1 = r0 + lax.broadcasted_iota(jnp.int32, (tm, 1), 0)
    ds = []
    for g, w in enumerate(POOL_WINDOWS):
        lo = jnp.maximum(t - w // 2, 0)
        hi = jnp.minimum(t + (w - w // 2), seq)
        band = ((s >= lo) & (s < hi)).astype(BF16)
        cnt = (jnp.minimum(t1 + (w - w // 2), seq) - jnp.maximum(t1 - w // 2, 0)).astype(F32)
        sl = slice(g * POOL_GC, (g + 1) * POOL_GC)
        pooled = _dot(band, uwin[:, sl]) / cnt
        ds.append(pooled - utile[:, sl].astype(F32))
    d = jnp.concatenate(ds, axis=-1).astype(BF16)
    y_pool = (_dot(d, wpool_ref[...]) * pscale_ref[...]).astype(BF16)

    g = gate_ref[...]
    merged = (g[:, 0:D_MODEL].astype(F32) * _dot(y_pool, wbr_ref[0])
              + g[:, D_MODEL:2 * D_MODEL].astype(F32) * _dot(yna_ref[...], wbr_ref[1])
              + g[:, 2 * D_MODEL:].astype(F32) * _dot(ymla_ref[...], wbr_ref[2]))
    y = _dot(merged.astype(BF16), wout_ref[...])
    xn = x_ref[...] + gt1_ref[...] * _rms(y, gpost_ref[...])
    xo_ref[...] = xn

    h2 = _rms(xn, gpre_ref[...]) * (1.0 + sc2_ref[...]) + sh2_ref[...]
    h_hi = h2.astype(BF16)
    h2_ref[...] = h_hi
    h_lo = (h2 - h_hi.astype(F32)).astype(BF16)
    two = _dot(h_hi, wr2_ref[...])
    logits = two[:, :LANES] + two[:, LANES:] + _dot(h_lo, wr1_ref[...])
    rg_ref[...] = _route(logits)


def _merge(x, u, yna, ymla, gate, mod5, l, w, *, seq, ctx, tm):
    n = x.shape[0]
    steps_per_seq = seq // tm
    nseq = n // seq

    def mrow(i):
        return 0 if ctx else 1 + i // steps_per_seq

    const = lambda i: (l, 0, 0)
    modspec = lambda k: pl.BlockSpec((None, None, None, 1, D_MODEL), lambda i: (l, mrow(i), k, 0, 0))
    row = lambda wd: pl.BlockSpec((tm, wd), lambda i: (i, 0))
    return pl.pallas_call(
        functools.partial(_merge_kernel, seq=seq, tm=tm, steps_per_seq=steps_per_seq),
        grid=(n // tm,),
        in_specs=[
            row(D_MODEL),
            pl.BlockSpec((None, seq, 512), lambda i: (i // steps_per_seq, 0, 0)),
            row(512), row(512), row(3 * D_MODEL),
            modspec(2), modspec(3), modspec(4),
            pl.BlockSpec((None, 1, D_MODEL), const),
            pl.BlockSpec((None, 1, D_MODEL), const),
            pl.BlockSpec((None, 512, 512), const),
            pl.BlockSpec((None, 1, 512), const),
            pl.BlockSpec((None, 3, 512, D_MODEL), lambda i: (l, 0, 0, 0)),
            pl.BlockSpec((None, D_MODEL, D_MODEL), const),
            pl.BlockSpec((None, D_MODEL, 2 * LANES), const),
            pl.BlockSpec((None, D_MODEL, LANES), const),
        ],
        out_specs=[row(D_MODEL), row(D_MODEL), row(LANES)],
        out_shape=[jax.ShapeDtypeStruct((n, D_MODEL), F32), jax.ShapeDtypeStruct((n, D_MODEL), BF16),
                   jax.ShapeDtypeStruct((n, LANES), F32)],
        compiler_params=_cparams(("arbitrary",)),
        name="merge_ctx" if ctx else "merge_lat",
    )(x, u.reshape(nseq, seq, 512), yna, ymla, gate, mod5, mod5, mod5, w["g_post_mix"], w["g_pre_ffn"],
      w["w_pool_bd"], w["pool_scale"], w["w_branch"], w["w_out"], w["w_r2"], w["w_r1"])


MOE_CHUNK = 4


def _moe_kernel(h_ref, rg_ref, x_ref, gt2_ref, gpost_ref, w1_ref, w3_ref, w2_ref, o_ref, acc_ref):
    c = pl.program_id(1)
    h = h_ref[...]
    rg = rg_ref[...]
    lane = lax.broadcasted_iota(jnp.int32, rg.shape, 1)
    part = None
    for j in range(MOE_CHUNK):
        gate = jnp.where(lane == c * MOE_CHUNK + j, rg, 0.0).sum(axis=-1, keepdims=True)
        a = _dot(h, w1_ref[j])
        b = _dot(h, w3_ref[j])
        hid = (a * jax.nn.sigmoid(a)) * b * gate
        y = _dot(hid.astype(BF16), w2_ref[j])
        part = y if part is None else part + y

    @pl.when(c == 0)
    def _():
        acc_ref[...] = part

    @pl.when(c > 0)
    def _():
        acc_ref[...] += part

    @pl.when(c == pl.num_programs(1) - 1)
    def _():
        o_ref[...] = x_ref[...] + gt2_ref[...] * _rms(acc_ref[...], gpost_ref[...])


def _moe(h2, rg, x, mod5, l, w, *, seq, ctx, tm):
    n = x.shape[0]
    steps_per_seq = seq // tm if not ctx else 0

    def mrow(i):
        return 0 if ctx else 1 + i // steps_per_seq

    nchunk = N_EXPERTS // MOE_CHUNK
    row = lambda wd: pl.BlockSpec((tm, wd), lambda i, c: (i, 0))
    wspec = lambda a, b: pl.BlockSpec((None, MOE_CHUNK, a, b), lambda i, c: (l, c, 0, 0))
    return pl.pallas_call(
        _moe_kernel,
        grid=(n // tm, nchunk),
        in_specs=[
            row(D_MODEL), row(LANES), row(D_MODEL),
            pl.BlockSpec((None, None, None, 1, D_MODEL), lambda i, c: (l, mrow(i), 5, 0, 0)),
            pl.BlockSpec((None, 1, D_MODEL), lambda i, c: (l, 0, 0)),
            wspec(D_MODEL, EXPERT_FF), wspec(D_MODEL, EXPERT_FF), wspec(EXPERT_FF, D_MODEL),
        ],
        out_specs=row(D_MODEL),
        out_shape=jax.ShapeDtypeStruct((n, D_MODEL), F32),
        scratch_shapes=[pltpu.VMEM((tm, D_MODEL), F32)],
        compiler_params=_cparams(("arbitrary", "arbitrary")),
        name="moe_ctx" if ctx else "moe_lat",
    )(h2, rg, x, mod5, w["g_post_ffn"], w["w1"], w["w3"], w["w2"])


def _rope_table(seq):
    t = np.arange(seq)
    n_freq = MLA_ROPE_DIM // 4
    inv = jnp.asarray(ROPE_BASE, F32) ** (-jnp.arange(n_freq, dtype=F32) / n_freq)
    ang_r = jnp.asarray(t // GRID_W, F32)[:, None] * inv
    ang_c = jnp.asarray(t % GRID_W, F32)[:, None] * inv
    cos = jnp.cos(jnp.stack([ang_r, ang_c], axis=1))
    sin = jnp.sin(jnp.stack([ang_r, ang_c], axis=1))
    zeros = jnp.zeros_like(sin)
    c32 = jnp.stack([cos, cos], axis=2).reshape(seq, MLA_ROPE_DIM)
    up32 = jnp.stack([-sin, zeros], axis=2).reshape(seq, MLA_ROPE_DIM)
    dn32 = jnp.stack([zeros, sin], axis=2).reshape(seq, MLA_ROPE_DIM)

    def slot(v, fill):
        return jnp.concatenate([jnp.full((seq, KPE_LANE), fill, F32), v,
                                jnp.full((seq, SLOT - KPE_LANE - MLA_ROPE_DIM), fill, F32)], axis=1)

    return jnp.concatenate([slot(c32, 1.0), slot(up32, 0.0), slot(dn32, 0.0)], axis=1)


def _na_row_classes(n_rows):
    i = np.arange(NA_BLOCK_ROWS)[:, None]
    j = np.arange(NA_SLAB_ROWS)[None, :]
    last_ks = n_rows - NA_SLAB_ROWS
    last_q0 = n_rows - NA_BLOCK_ROWS
    specs = [
        (j < NA_WIN_ROWS, j - i),
        ((j >= i) & (j < i + NA_WIN_ROWS), j - NA_WIN_ROWS // 2 - i),
        (last_ks + j >= n_rows - NA_WIN_ROWS, last_ks + j - last_q0 - i),
    ]
    ok = np.stack([np.broadcast_to(o, (NA_BLOCK_ROWS, NA_SLAB_ROWS)) for o, _ in specs])
    dr = np.stack([np.clip(d + NA_WIN_ROWS - 1, 0, 2 * NA_WIN_ROWS - 2) for _, d in specs])
    return ok, dr


def _bias_kernel(rpb_ref, out_ref, *, row_ok, row_dr):
    reach = NA_WIN_COLS - 1
    rows = rpb_ref[...] * LOG2E
    v_lo = pltpu.roll(rows, SLOT - reach, axis=1)
    v_hi = pltpu.roll(rows, GRID_W - reach, axis=1)
    lane1 = lax.broadcasted_iota(jnp.int32, (1, SLOT), 1)
    mid = (lane1 >= GRID_W - reach) & (lane1 <= GRID_W + reach)
    kc = lax.broadcasted_iota(jnp.int32, (GRID_W, SLOT), 0)
    lane = lax.broadcasted_iota(jnp.int32, (GRID_W, SLOT), 1)
    c = lane & (GRID_W - 1)
    c0 = jnp.clip(c - NA_WIN_COLS // 2, 0, GRID_W - NA_WIN_COLS)
    in_col = (kc >= c0) & (kc < c0 + NA_WIN_COLS)
    first = lane < GRID_W
    for k in range(3):
        for j in range(NA_SLAB_ROWS):
            for ip in range(NA_BLOCK_ROWS // 2):
                ok_a, ok_b = bool(row_ok[k, 2 * ip, j]), bool(row_ok[k, 2 * ip + 1, j])
                a, b = int(row_dr[k, 2 * ip, j]), int(row_dr[k, 2 * ip + 1, j])
                rs = slice(j * GRID_W, (j + 1) * GRID_W)
                cs = slice(ip * SLOT, (ip + 1) * SLOT)
                if not (ok_a or ok_b):
                    out_ref[k, rs, cs] = jnp.full((GRID_W, SLOT), NEG_INF, out_ref.dtype)
                    continue
                v = jnp.where(mid, v_hi[b:b + 1], v_lo[a:a + 1])
                tile = pltpu.roll(jnp.broadcast_to(v, (GRID_W, SLOT)), 0, axis=1, stride=1, stride_axis=0)
                ok = in_col
                if not ok_b:
                    ok = ok & first
                if not ok_a:
                    ok = ok & jnp.logical_not(first)
                out_ref[k, rs, cs] = jnp.where(ok, tile, NEG_INF).astype(out_ref.dtype)


def _na_bias_table(rpb, n_rows):
    n_dr, n_dc = rpb.shape[-2:]
    row_ok, row_dr = _na_row_classes(n_rows)
    rpb_p = jnp.pad(jnp.flip(rpb, axis=-1), ((0, 0), (0, 0), (0, 0), (0, SLOT - n_dc)))
    tq, tk = NA_BLOCK_ROWS * GRID_W, NA_SLAB_ROWS * GRID_W
    return pl.pallas_call(
        functools.partial(_bias_kernel, row_ok=row_ok, row_dr=row_dr),
        grid=(DEPTH, NA_HEADS),
        in_specs=[pl.BlockSpec((None, None, n_dr, SLOT), lambda l, h: (l, h, 0, 0))],
        out_specs=pl.BlockSpec((None, 3, None, tk, tq), lambda l, h: (l, 0, h, 0, 0)),
        out_shape=jax.ShapeDtypeStruct((DEPTH, 3, NA_HEADS, tk, tq), BF16),
        compiler_params=_cparams(("arbitrary", "arbitrary")),
        name="na_bias_table",
    )(rpb_p)


def _prep(p):
    L = DEPTH
    w_in = p["w_in"]
    kpe0 = C_G
    kpe1 = C_G + MLA_ROPE_DIM
    w_in_p = jnp.concatenate(
        [w_in[..., :kpe0], w_in[..., kpe1:], jnp.zeros((L, D_MODEL, KPE_LANE), F32), w_in[..., kpe0:kpe1],
         jnp.zeros((L, D_MODEL, SLOT - KPE_LANE - MLA_ROPE_DIM), F32)], axis=-1).astype(BF16)
    w_uq = jnp.pad(p["mla_w_uq"].reshape(L, MLA_Q_LORA, MLA_HEADS, MLA_QK_DIM),
                   ((0, 0), (0, 0), (0, 0), (0, SLOT - MLA_QK_DIM))).reshape(L, MLA_Q_LORA, MLA_HEADS * SLOT)
    ukv = p["mla_w_ukv"].reshape(L, MLA_KV_LORA, MLA_HEADS, MLA_NOPE_DIM + MLA_V_DIM)
    uk = jnp.pad(ukv[..., :MLA_NOPE_DIM], ((0, 0), (0, 0), (0, 0), (0, SLOT - MLA_NOPE_DIM)))
    w_ukv = jnp.concatenate([uk.reshape(L, MLA_KV_LORA, MLA_HEADS * SLOT),
                             ukv[..., MLA_NOPE_DIM:].reshape(L, MLA_KV_LORA, MLA_HEADS * MLA_V_DIM)], axis=-1)
    w_pool_bd = jnp.einsum("lgcd,gh->lgchd", p["w_pool"], jnp.eye(4, dtype=F32)).reshape(L, POOL_WIDTH, POOL_WIDTH)
    w_r = jnp.concatenate([p["w_group_router"], p["w_expert_router"]], axis=-1)
    r_hi = w_r.astype(BF16)
    r_lo = (w_r - r_hi.astype(F32)).astype(BF16)
    padl = lambda a: jnp.pad(a, ((0, 0), (0, 0), (0, LANES - a.shape[-1])))
    vec = lambda a: a.reshape(L, 1, a.shape[-1])
    return {
        "w_in": w_in_p,
        "w_uq": w_uq.astype(BF16),
        "w_ukv": w_ukv.astype(BF16),
        "w_pool_bd": w_pool_bd.astype(BF16),
        "w_branch": p["w_branch"].astype(BF16),
        "w_out": p["w_out"].astype(BF16),
        "w_r2": jnp.concatenate([padl(r_hi), padl(r_lo)], axis=-1),
        "w_r1": padl(r_hi),
        "w1": p["w_exp_gate"].astype(BF16),
        "w3": p["w_exp_up"].astype(BF16),
        "w2": p["w_exp_down"].astype(BF16),
        "g_pre_mix": vec(p["g_pre_mix"]), "g_post_mix": vec(p["g_post_mix"]),
        "g_pre_ffn": vec(p["g_pre_ffn"]), "g_post_ffn": vec(p["g_post_ffn"]),
        "pool_scale": vec(p["pool_scale"]), "q_norm": vec(p["mla_q_norm"]), "kv_norm": vec(p["mla_kv_norm"]),
    }


def kernel(x_prompt, x_sample, cache_na_k, cache_na_v, cache_mla_ckv, cache_mla_kpe, c, c_ctx, w_mod, b_mod,
           g_pre_mix, g_post_mix, g_pre_ffn, g_post_ffn, w_in, w_pool, pool_scale, na_rpb, mla_q_norm, mla_w_uq,
           mla_kv_norm, mla_w_ukv, w_branch, w_out, w_group_router, w_expert_router, w_exp_gate, w_exp_up,
           w_exp_down):
    batch, seq, _ = x_prompt.shape
    dbatch, dseq, _ = x_sample.shape
    past = cache_na_k.shape[2]
    assert dbatch + 1 <= 8 and dseq % (NA_BLOCK_ROWS * GRID_W) == 0

    w = _prep(dict(w_in=w_in, w_pool=w_pool, pool_scale=pool_scale, mla_q_norm=mla_q_norm, mla_w_uq=mla_w_uq,
                   mla_kv_norm=mla_kv_norm, mla_w_ukv=mla_w_ukv, w_branch=w_branch, w_out=w_out,
                   w_group_router=w_group_router, w_expert_router=w_expert_router, w_exp_gate=w_exp_gate,
                   w_exp_up=w_exp_up, w_exp_down=w_exp_down, g_pre_mix=g_pre_mix, g_post_mix=g_post_mix,
                   g_pre_ffn=g_pre_ffn, g_post_ffn=g_post_ffn))
    rope_tab = _rope_table(dseq)
    bias_tab = _na_bias_table(na_rpb, dseq // GRID_W)
    place = jnp.tile(jnp.pad(jnp.eye(MLA_ROPE_DIM, dtype=F32), ((0, 0), (KPE_LANE, SLOT - KPE_LANE - MLA_ROPE_DIM))),
                     (1, MLA_HEADS)).astype(BF16)

    cvec = jnp.concatenate([c_ctx[None], c, jnp.zeros((8 - 1 - dbatch, D_MODEL), F32)], axis=0)
    mod5 = _modulation(cvec, w_mod, b_mod).reshape(DEPTH, 8, 6, 1, D_MODEL)

    ck = cache_na_k.reshape(dbatch, DEPTH, past, NA_WIDTH)
    cv = cache_na_v.reshape(dbatch, DEPTH, past, NA_WIDTH)

    xp = x_prompt.reshape(batch * seq, D_MODEL)
    assert DEPTH == 2
    caches = None
    for l in range(DEPTH):
        u, nq, nvt, qm, km, vmt, gate, *caches = _inproj(xp, mod5, l, w, seq=seq, rope_tab=None, ctx_out=True,
                                                         tm=seq, prev=caches)
        yna, ymla = _attn_ctx(nq, caches[0], nvt, qm, km, vmt, seq=seq, nk_layer=l if l > 0 else None)
        xp, h2, rg = _merge(xp, u, yna, ymla, gate, mod5, l, w, seq=seq, ctx=True, tm=seq)
        xp = _moe(h2, rg, xp, mod5, l, w, seq=seq, ctx=True, tm=1024)
    new_nk, new_nv, new_ckv, new_kpe = caches

    xs = x_sample.reshape(dbatch * dseq, D_MODEL)
    for l in range(DEPTH):
        u, nq, nvt, qm, km, vmt, gate, nk = _inproj(xs, mod5, l, w, seq=dseq, rope_tab=rope_tab, ctx_out=False, tm=256)
        ckm, cvmt, cnvt = _cache_prep(cache_mla_ckv, cache_mla_kpe, cv, w["w_ukv"], place, l)
        yna = _attn_na(nq, nk, nvt, ck, cnvt, bias_tab, l, batch=dbatch, seq=dseq)
        ymla = _attn_mla(qm, km, vmt, ckm, cvmt, batch=dbatch, seq=dseq, tq=256)
        xs, h2, rg = _merge(xs, u, yna, ymla, gate, mod5, l, w, seq=dseq, ctx=False, tm=256)
        xs = _moe(h2, rg, xs, mod5, l, w, seq=dseq, ctx=False, tm=1024)

    heads = (batch, DEPTH, seq, NA_HEADS, NA_HEAD_DIM)
    return (xp.reshape(batch, seq, D_MODEL), xs.reshape(dbatch, dseq, D_MODEL),
            new_nk.reshape(heads), new_nv.reshape(heads), new_ckv, new_kpe)
```

```python
import functools

import numpy as np
import jax
import jax.numpy as jnp
from jax import lax
from jax.experimental import pallas as pl
from jax.experimental.pallas import tpu as pltpu

F32 = jnp.float32
BF16 = jnp.bfloat16

D_MODEL = 1024
DEPTH = 2
GRID_W = 64
POOL_WINDOWS = (2, 4, 8, 16)
POOL_GC = 128
POOL_WIDTH = 512
NA_HEADS = 8
NA_HEAD_DIM = 64
NA_WIDTH = 512
NA_WIN_ROWS = 8
NA_WIN_COLS = 16
MLA_HEADS = 8
MLA_NOPE_DIM = 64
MLA_ROPE_DIM = 32
MLA_V_DIM = 64
MLA_QK_DIM = 96
MLA_Q_LORA = 384
MLA_KV_LORA = 256
N_GROUPS = 4
EXPERTS_PER_GROUP = 4
N_EXPERTS = 16
EXPERT_FF = 256
ROPE_BASE = 10000.0
EPS = 1e-6
NEG_INF = -1e30

LANES = 128
SLOT = LANES
VMEM_LIMIT = 56 * 1024 * 1024

C_U, C_NQ, C_NK, C_NV, C_CQ, C_CKV, C_G, C_KPE, C_END = 0, 512, 1024, 1536, 2048, 2432, 2688, 5760, 5888
KPE_LANE = MLA_NOPE_DIM

LOG2E = 1.4426950408889634
VO_W = 2 * SLOT
VO_ALL = (NA_HEADS // 2) * VO_W

NA_BLOCK_ROWS = 4
NA_SLAB_ROWS = 12


def _cparams(sem):
    return pltpu.CompilerParams(dimension_semantics=sem, vmem_limit_bytes=VMEM_LIMIT)


def _rms(x, g):
    return x * lax.rsqrt(jnp.mean(x * x, axis=-1, keepdims=True) + EPS) * g


def _dot(a, b):
    return jnp.dot(a, b, preferred_element_type=F32)


def _dot_nt(a, b):
    return lax.dot_general(a, b, (((1,), (1,)), ((), ())), preferred_element_type=F32)


def _mod_kernel(c_ref, w_ref, b_ref, o_ref):
    c = c_ref[...]
    s = (c * jax.nn.sigmoid(c)).astype(BF16)
    o_ref[...] = _dot(s, w_ref[...].astype(BF16)) + b_ref[...]


def _modulation(cvec, w_mod, b_mod):
    tn = 1536
    n = w_mod.shape[-1]
    return pl.pallas_call(
        _mod_kernel,
        grid=(DEPTH, n // tn),
        in_specs=[
            pl.BlockSpec((8, D_MODEL), lambda l, j: (0, 0)),
            pl.BlockSpec((None, D_MODEL, tn), lambda l, j: (l, 0, j)),
            pl.BlockSpec((None, 1, tn), lambda l, j: (l, 0, j)),
        ],
        out_specs=pl.BlockSpec((None, 8, tn), lambda l, j: (l, 0, j)),
        out_shape=jax.ShapeDtypeStruct((DEPTH, 8, n), F32),
        compiler_params=_cparams(("arbitrary", "arbitrary")),
        name="modulation",
    )(cvec, w_mod, b_mod.reshape(DEPTH, 1, n))


def _rope_slot(x, tab):
    c, s_up, s_dn = tab[:, 0:SLOT], tab[:, SLOT:2 * SLOT], tab[:, 2 * SLOT:3 * SLOT]
    half = MLA_ROPE_DIM // 4
    x_up = pltpu.roll(x, SLOT - half, axis=1)
    x_dn = pltpu.roll(x, half, axis=1)
    return x * c + x_up * s_up + x_dn * s_dn


def _store_vo(vo_ref, v):
    tm = v.shape[0]
    for p in range(NA_HEADS // 2):
        vo_ref[:, p * VO_W:p * VO_W + SLOT] = v[:, p * SLOT:(p + 1) * SLOT].astype(BF16)
        vo_ref[:, p * VO_W + SLOT:(p + 1) * VO_W] = jnp.ones((tm, SLOT), BF16)


def _inproj_kernel(*refs, rope, ctx_out, stacked):
    n_in = 9 + (1 if rope else 0) + (4 if stacked else 0)
    ins, outs = refs[:n_in], refs[n_in:]
    x_ref, sh_ref, sc_ref, g_ref, win_ref, qn_ref, wuq_ref, kvn_ref, wukv_ref = ins[:9]
    tab_ref = ins[9] if rope else None
    prev = ins[n_in - 4:] if stacked else None
    if ctx_out:
        u_ref, nq_ref, nvo_ref, qm_ref, km_ref, vmo_ref, gate_ref, nk_ref, nv_ref, ckv_ref, kpe_ref = outs
    else:
        u_ref, nq_ref, nvo_ref, qm_ref, km_ref, vmo_ref, gate_ref, nk_ref = outs

    def put(ref, k, val):
        if stacked:
            ref[0] = prev[k][...]
            ref[1] = val
        else:
            ref[...] = val

    x = x_ref[...]
    h = (_rms(x, g_ref[...]) * (1.0 + sc_ref[...]) + sh_ref[...]).astype(BF16)

    def seg(a, b):
        return _dot(h, win_ref[:, a:b])

    u_ref[...] = seg(C_U, C_NQ).astype(u_ref.dtype)
    nq_ref[...] = (seg(C_NQ, C_NK) * (NA_HEAD_DIM ** -0.5 * LOG2E)).astype(BF16)
    nk = seg(C_NK, C_NV)
    nv = seg(C_NV, C_CQ)
    if ctx_out:
        put(nk_ref, 0, nk)
        put(nv_ref, 1, nv)
    else:
        nk_ref[...] = nk.astype(BF16)
    _store_vo(nvo_ref, nv)
    gate_ref[...] = jax.nn.sigmoid(seg(C_G, C_KPE)).astype(BF16)

    cq = _rms(seg(C_CQ, C_CKV), qn_ref[...]).astype(BF16)
    q = _dot(cq, wuq_ref[...])
    tab = tab_ref[...] if rope else None
    for hd in range(MLA_HEADS):
        qs = q[:, hd * SLOT:(hd + 1) * SLOT]
        if rope:
            qs = _rope_slot(qs, tab)
        qm_ref[:, hd * SLOT:(hd + 1) * SLOT] = (qs * (MLA_QK_DIM ** -0.5 * LOG2E)).astype(BF16)

    ckv = _rms(seg(C_CKV, C_G), kvn_ref[...])
    kpe_slot = seg(C_KPE, C_END)
    if ctx_out:
        put(ckv_ref, 2, ckv)
        put(kpe_ref, 3, kpe_slot[:, KPE_LANE:KPE_LANE + MLA_ROPE_DIM])
    if rope:
        kpe_slot = _rope_slot(kpe_slot, tab)
    kv = _dot(ckv.astype(BF16), wukv_ref[...])
    for hd in range(MLA_HEADS):
        km_ref[:, hd * SLOT:(hd + 1) * SLOT] = (kv[:, hd * SLOT:(hd + 1) * SLOT] + kpe_slot).astype(BF16)
    _store_vo(vmo_ref, kv[:, MLA_HEADS * SLOT:])


def _inproj(x, mod5, l, w, *, seq, rope_tab, ctx_out, tm, prev=None):
    n = x.shape[0]
    steps_per_seq = seq // tm if not ctx_out else 0
    stacked = prev is not None
    assert not stacked or (ctx_out and tm == seq and l == 1)

    def mrow(i):
        return 0 if ctx_out else 1 + i // steps_per_seq

    const = lambda i: (l, 0, 0)
    in_specs = [
        pl.BlockSpec((tm, D_MODEL), lambda i: (i, 0)),
        pl.BlockSpec((None, None, None, 1, D_MODEL), lambda i: (l, mrow(i), 0, 0, 0)),
        pl.BlockSpec((None, None, None, 1, D_MODEL), lambda i: (l, mrow(i), 1, 0, 0)),
        pl.BlockSpec((None, 1, D_MODEL), const),
        pl.BlockSpec((None, D_MODEL, C_END), const),
        pl.BlockSpec((None, 1, MLA_Q_LORA), const),
        pl.BlockSpec((None, MLA_Q_LORA, MLA_HEADS * SLOT), const),
        pl.BlockSpec((None, 1, MLA_KV_LORA), const),
        pl.BlockSpec((None, MLA_KV_LORA, MLA_HEADS * SLOT + 512), const),
    ]
    args = [x, mod5, mod5, w["g_pre_mix"], w["w_in"], w["q_norm"], w["w_uq"], w["kv_norm"], w["w_ukv"]]
    rope = rope_tab is not None
    if rope:
        in_specs.append(pl.BlockSpec((tm, 3 * SLOT), lambda i: (i % steps_per_seq, 0)))
        args.append(rope_tab)
    row = lambda wd: pl.BlockSpec((tm, wd), lambda i: (i, 0))
    rows_of = lambda wd, dt: jax.ShapeDtypeStruct((n, wd), dt)
    widths = [512, 512, VO_ALL, MLA_HEADS * SLOT, MLA_HEADS * SLOT, VO_ALL, 3 * D_MODEL]
    out_specs = [row(wd) for wd in widths]
    out_shape = [rows_of(wd, BF16) for wd in widths]
    cache_w = [512, 512, MLA_KV_LORA, MLA_ROPE_DIM] if ctx_out else []
    if not ctx_out:
        out_specs.append(row(512))
        out_shape.append(rows_of(512, BF16))
    elif stacked:
        in_specs += [row(wd) for wd in cache_w]
        args += list(prev)
        out_specs += [pl.BlockSpec((None, DEPTH, seq, wd), lambda i: (i, 0, 0, 0)) for wd in cache_w]
        out_shape += [jax.ShapeDtypeStruct((n // seq, DEPTH, seq, wd), F32) for wd in cache_w]
    else:
        out_specs += [row(wd) for wd in cache_w]
        out_shape += [rows_of(wd, F32) for wd in cache_w]
    return pl.pallas_call(
        functools.partial(_inproj_kernel, rope=rope, ctx_out=ctx_out, stacked=stacked),
        grid=(n // tm,),
        in_specs=in_specs,
        out_specs=out_specs,
        out_shape=out_shape,
        compiler_params=_cparams(("arbitrary",)),
        name="inproj_ctx" if ctx_out else "inproj_lat",
    )(*args)


def _attend(q_h, segs):
    scores = []
    for k, _, bias in segs:
        s = _dot_nt(q_h, k)
        if bias is not None:
            s = s + bias
        scores.append(s)
    m = scores[0].max(axis=-1, keepdims=True)
    for s in scores[1:]:
        m = jnp.maximum(m, s.max(axis=-1, keepdims=True))
    acc = None
    for s, (_, vo, _) in zip(scores, segs):
        o = _dot(jnp.exp2(s - m).astype(BF16), vo)
        acc = o if acc is None else acc + o
    return acc[:, :SLOT] / acc[:, SLOT:]


def _attn_heads(out_ref, q_of, segs_of):
    lo = lax.broadcasted_iota(jnp.int32, (1, SLOT), 1) < NA_HEAD_DIM
    for p in range(NA_HEADS // 2):
        o0, o1 = [_attend(q_of(2 * p + e), segs_of(2 * p + e)) for e in range(2)]
        out_ref[:, p * SLOT:(p + 1) * SLOT] = jnp.where(lo, o0, o1).astype(out_ref.dtype)


def _pair_q(q_ref):
    lo = lax.broadcasted_iota(jnp.int32, (1, SLOT), 1) < NA_HEAD_DIM

    def q_of(h):
        q_pair = q_ref[:, (h // 2) * SLOT:(h // 2 + 1) * SLOT]
        keep = lo if h % 2 == 0 else jnp.logical_not(lo)
        return jnp.where(keep, q_pair, jnp.zeros_like(q_pair))
    return q_of


def _slot(ref, i, rows=slice(None)):
    return ref[rows, i * SLOT:(i + 1) * SLOT].astype(BF16)


def _vo(ref, h, rows=slice(None)):
    return ref[rows, (h // 2) * VO_W:(h // 2 + 1) * VO_W]


def _attn_ctx_kernel(nq_ref, nk_ref, nvo_ref, qm_ref, km_ref, vmo_ref, yna_ref, ymla_ref):
    _attn_heads(yna_ref, _pair_q(nq_ref), lambda h: [(_slot(nk_ref, h // 2), _vo(nvo_ref, h), None)])
    _attn_heads(ymla_ref, lambda h: _slot(qm_ref, h), lambda h: [(_slot(km_ref, h), _vo(vmo_ref, h), None)])


def _attn_ctx(nq, nk, nvo, qm, km, vmo, *, seq, nk_layer=None):
    n = nq.shape[0]
    spec = lambda wd: pl.BlockSpec((seq, wd), lambda b: (b, 0))
    nk_spec = spec(512) if nk_layer is None else pl.BlockSpec((None, None, seq, 512), lambda b: (b, nk_layer, 0, 0))
    return pl.pallas_call(
        _attn_ctx_kernel,
        grid=(n // seq,),
        in_specs=[spec(512), nk_spec, spec(VO_ALL), spec(1024), spec(1024), spec(VO_ALL)],
        out_specs=[spec(512), spec(512)],
        out_shape=[jax.ShapeDtypeStruct((n, 512), BF16)] * 2,
        compiler_params=_cparams(("arbitrary",)),
        name="attn_ctx",
    )(nq, nk, nvo, qm, km, vmo)


def _attn_na_kernel(nq_ref, nk_ref, nvo_ref, ck_ref, cvo_ref, bias_ref, out_ref, *, n_blocks):
    i = pl.program_id(1)
    ks = jnp.clip(NA_BLOCK_ROWS * i - NA_WIN_ROWS // 2, 0, n_blocks * NA_BLOCK_ROWS - NA_SLAB_ROWS)
    slab = pl.ds(pl.multiple_of(ks * GRID_W, GRID_W), NA_SLAB_ROWS * GRID_W)

    def segs_of(h):
        return [(_slot(nk_ref, h // 2, slab), _vo(nvo_ref, h, slab), bias_ref[h].astype(F32)),
                (_slot(ck_ref, h // 2), _vo(cvo_ref, h), None)]

    _attn_heads(out_ref, _pair_q(nq_ref), segs_of)


def _attn_na(nq, nk, nvo, cache_k, cvo, bias_tab, l, *, batch, seq):
    tq = NA_BLOCK_ROWS * GRID_W
    nb = seq // tq
    past = cache_k.shape[2]

    def cls(i):
        return jnp.where(i == 0, 0, jnp.where(i == nb - 1, 2, 1))

    return pl.pallas_call(
        functools.partial(_attn_na_kernel, n_blocks=nb),
        grid=(batch, nb),
        in_specs=[
            pl.BlockSpec((tq, 512), lambda b, i: (b * nb + i, 0)),
            pl.BlockSpec((seq, 512), lambda b, i: (b, 0)),
            pl.BlockSpec((seq, VO_ALL), lambda b, i: (b, 0)),
            pl.BlockSpec((None, None, past, 512), lambda b, i: (b, l, 0, 0)),
            pl.BlockSpec((None, past, VO_ALL), lambda b, i: (b, 0, 0)),
            pl.BlockSpec((None, None, NA_HEADS, tq, NA_SLAB_ROWS * GRID_W), lambda b, i: (l, cls(i), 0, 0, 0)),
        ],
        out_specs=pl.BlockSpec((tq, 512), lambda b, i: (b * nb + i, 0)),
        out_shape=jax.ShapeDtypeStruct((batch * seq, 512), BF16),
        compiler_params=_cparams(("arbitrary", "arbitrary")),
        name="attn_na_lat",
    )(nq, nk, nvo, cache_k, cvo, bias_tab)


def _cache_prep_kernel(ckv_ref, kpe_ref, nv_ref, wukv_ref, place_ref, km_ref, vmo_ref, nvo_ref):
    kv = _dot(ckv_ref[...].astype(BF16), wukv_ref[...])
    kslots = kv[:, :MLA_HEADS * SLOT] + _dot(kpe_ref[...].astype(BF16), place_ref[...])
    km_ref[...] = kslots.astype(BF16)
    _store_vo(vmo_ref, kv[:, MLA_HEADS * SLOT:])
    _store_vo(nvo_ref, nv_ref[...])


def _cache_prep(cache_ckv, cache_kpe, cache_nv, w_ukv, place, l):
    batch, _, past, _ = cache_ckv.shape
    tout = pl.BlockSpec((None, past, VO_ALL), lambda b: (b, 0, 0))
    tshape = jax.ShapeDtypeStruct((batch, past, VO_ALL), BF16)
    return pl.pallas_call(
        _cache_prep_kernel,
        grid=(batch,),
        in_specs=[
            pl.BlockSpec((None, None, past, MLA_KV_LORA), lambda b: (b, l, 0, 0)),
            pl.BlockSpec((None, None, past, MLA_ROPE_DIM), lambda b: (b, l, 0, 0)),
            pl.BlockSpec((None, None, past, 512), lambda b: (b, l, 0, 0)),
            pl.BlockSpec((None, MLA_KV_LORA, MLA_HEADS * SLOT + 512), lambda b: (l, 0, 0)),
            pl.BlockSpec((MLA_ROPE_DIM, MLA_HEADS * SLOT), lambda b: (0, 0)),
        ],
        out_specs=[pl.BlockSpec((None, past, MLA_HEADS * SLOT), lambda b: (b, 0, 0)), tout, tout],
        out_shape=[jax.ShapeDtypeStruct((batch, past, MLA_HEADS * SLOT), BF16), tshape, tshape],
        compiler_params=_cparams(("arbitrary",)),
        name="cache_prep",
    )(cache_ckv, cache_kpe, cache_nv, w_ukv, place)


def _attn_mla_kernel(qm_ref, km_ref, vmo_ref, ckm_ref, cvmo_ref, out_ref):
    _attn_heads(out_ref, lambda h: _slot(qm_ref, h),
                lambda h: [(_slot(km_ref, h), _vo(vmo_ref, h), None), (_slot(ckm_ref, h), _vo(cvmo_ref, h), None)])


def _attn_mla(qm, km, vmo, ckm, cvmo, *, batch, seq, tq):
    nb = seq // tq
    past = ckm.shape[1]
    return pl.pallas_call(
        _attn_mla_kernel,
        grid=(batch, nb),
        in_specs=[
            pl.BlockSpec((tq, MLA_HEADS * SLOT), lambda b, i: (b * nb + i, 0)),
            pl.BlockSpec((seq, MLA_HEADS * SLOT), lambda b, i: (b, 0)),
            pl.BlockSpec((seq, VO_ALL), lambda b, i: (b, 0)),
            pl.BlockSpec((None, past, MLA_HEADS * SLOT), lambda b, i: (b, 0, 0)),
            pl.BlockSpec((None, past, VO_ALL), lambda b, i: (b, 0, 0)),
        ],
        out_specs=pl.BlockSpec((tq, 512), lambda b, i: (b * nb + i, 0)),
        out_shape=jax.ShapeDtypeStruct((batch * seq, 512), BF16),
        compiler_params=_cparams(("arbitrary", "arbitrary")),
        name="attn_mla_lat",
    )(qm, km, vmo, ckm, cvmo)


POOL_HALO = 16


def _route(logits):
    lane = lax.broadcasted_iota(jnp.int32, logits.shape, 1).astype(F32)
    big = jnp.float32(1 << 20)
    is_g = lane < N_GROUPS
    gl = jnp.where(is_g, logits, NEG_INF)
    gmax = gl.max(axis=-1, keepdims=True)
    ge = jnp.where(is_g, jnp.exp(gl - gmax), 0.0)
    gp = ge / ge.sum(axis=-1, keepdims=True)
    g_w = jnp.where(is_g, gp, -1.0).max(axis=-1, keepdims=True)
    g_idx = jnp.where(is_g & (gp == g_w), lane, big).min(axis=-1, keepdims=True)
    e_lane = lane - N_GROUPS
    in_grp = (e_lane >= g_idx * EXPERTS_PER_GROUP) & (e_lane < (g_idx + 1) * EXPERTS_PER_GROUP)
    el = jnp.where(in_grp, logits, NEG_INF)
    emax = el.max(axis=-1, keepdims=True)
    ee = jnp.where(in_grp, jnp.exp(el - emax), 0.0)
    ep = ee / ee.sum(axis=-1, keepdims=True)
    p1 = jnp.where(in_grp, ep, -1.0).max(axis=-1, keepdims=True)
    i1 = jnp.where(in_grp & (ep == p1), lane, big).min(axis=-1, keepdims=True)
    rest = in_grp & (lane != i1)
    p2 = jnp.where(rest, ep, -1.0).max(axis=-1, keepdims=True)
    i2 = jnp.where(rest & (ep == p2), lane, big).min(axis=-1, keepdims=True)
    tot = p1 + p2
    gates = jnp.where(lane == i1, g_w * (p1 / tot), 0.0) + jnp.where(lane == i2, g_w * (p2 / tot), 0.0)
    return pltpu.roll(gates, LANES - N_GROUPS, axis=1)


def _merge_kernel(x_ref, u_ref, yna_ref, ymla_ref, gate_ref, gt1_ref, sh2_ref, sc2_ref, gpost_ref, gpre_ref,
                  wpool_ref, pscale_ref, wbr_ref, wout_ref, wr2_ref, wr1_ref,
                  xo_ref, h2_ref, rg_ref, *, seq, tm, steps_per_seq):
    nwin = min(tm + 2 * POOL_HALO, seq)
    if steps_per_seq == 1:
        r0 = 0
        s0 = 0
        uwin = u_ref[...]
        utile = uwin
    else:
        r0 = (pl.program_id(0) % steps_per_seq) * tm
        s0 = pl.multiple_of(jnp.clip(r0 - POOL_HALO, 0, seq - nwin), POOL_HALO)
        uwin = u_ref[pl.ds(s0, nwin), :]
        utile = u_ref[pl.ds(pl.multiple_of(r0, POOL_HALO), tm), :]
    t = r0 + lax.broadcasted_iota(jnp.int32, (tm, nwin), 0)
    s = s0 + lax.broadcasted_iota(jnp.int32, (tm, nwin), 1)
    t1 = r0 + lax.broadcasted_iota(jnp.int32, (tm, 1), 0)
    ds = []
    for g, w in enumerate(POOL_WINDOWS):
        lo = jnp.maximum(t - w // 2, 0)
        hi = jnp.minimum(t + (w - w // 2), seq)
        band = ((s >= lo) & (s < hi)).astype(BF16)
        cnt = (jnp.minimum(t1 + (w - w // 2), seq) - jnp.maximum(t1 - w // 2, 0)).astype(F32)
        sl = slice(g * POOL_GC, (g + 1) * POOL_GC)
        pooled = _dot(band, uwin[:, sl]) / cnt
        ds.append(pooled - utile[:, sl].astype(F32))
    d = jnp.concatenate(ds, axis=-1).astype(BF16)
    y_pool = (_dot(d, wpool_ref[...]) * pscale_ref[...]).astype(BF16)

    g = gate_ref[...]
    merged = (g[:, 0:D_MODEL].astype(F32) * _dot(y_pool, wbr_ref[0])
              + g[:, D_MODEL:2 * D_MODEL].astype(F32) * _dot(yna_ref[...], wbr_ref[1])
              + g[:, 2 * D_MODEL:].astype(F32) * _dot(ymla_ref[...], wbr_ref[2]))
    y = _dot(merged.astype(BF16), wout_ref[...])
    xn = x_ref[...] + gt1_ref[...] * _rms(y, gpost_ref[...])
    xo_ref[...] = xn

    h2 = _rms(xn, gpre_ref[...]) * (1.0 + sc2_ref[...]) + sh2_ref[...]
    h_hi = h2.astype(BF16)
    h2_ref[...] = h_hi
    h_lo = (h2 - h_hi.astype(F32)).astype(BF16)
    two = _dot(h_hi, wr2_ref[...])
    logits = two[:, :LANES] + two[:, LANES:] + _dot(h_lo, wr1_ref[...])
    rg_ref[...] = _route(logits)


def _merge(x, u, yna, ymla, gate, mod5, l, w, *, seq, ctx, tm):
    n = x.shape[0]
    steps_per_seq = seq // tm
    nseq = n // seq

    def mrow(i):
        return 0 if ctx else 1 + i // steps_per_seq

    const = lambda i: (l, 0, 0)
    modspec = lambda k: pl.BlockSpec((None, None, None, 1, D_MODEL), lambda i: (l, mrow(i), k, 0, 0))
    row = lambda wd: pl.BlockSpec((tm, wd), lambda i: (i, 0))
    return pl.pallas_call(
        functools.partial(_merge_kernel, seq=seq, tm=tm, steps_per_seq=steps_per_seq),
        grid=(n // tm,),
        in_specs=[
            row(D_MODEL),
            pl.BlockSpec((None, seq, 512), lambda i: (i // steps_per_seq, 0, 0)),
            row(512), row(512), row(3 * D_MODEL),
            modspec(2), modspec(3), modspec(4),
            pl.BlockSpec((None, 1, D_MODEL), const),
            pl.BlockSpec((None, 1, D_MODEL), const),
            pl.BlockSpec((None, 512, 512), const),
            pl.BlockSpec((None, 1, 512), const),
            pl.BlockSpec((None, 3, 512, D_MODEL), lambda i: (l, 0, 0, 0)),
            pl.BlockSpec((None, D_MODEL, D_MODEL), const),
            pl.BlockSpec((None, D_MODEL, 2 * LANES), const),
            pl.BlockSpec((None, D_MODEL, LANES), const),
        ],
        out_specs=[row(D_MODEL), row(D_MODEL), row(LANES)],
        out_shape=[jax.ShapeDtypeStruct((n, D_MODEL), F32), jax.ShapeDtypeStruct((n, D_MODEL), BF16),
                   jax.ShapeDtypeStruct((n, LANES), F32)],
        compiler_params=_cparams(("arbitrary",)),
        name="merge_ctx" if ctx else "merge_lat",
    )(x, u.reshape(nseq, seq, 512), yna, ymla, gate, mod5, mod5, mod5, w["g_post_mix"], w["g_pre_ffn"],
      w["w_pool_bd"], w["pool_scale"], w["w_branch"], w["w_out"], w["w_r2"], w["w_r1"])


MOE_CHUNK = 4


def _moe_kernel(h_ref, rg_ref, x_ref, gt2_ref, gpost_ref, w1_ref, w3_ref, w2_ref, o_ref, acc_ref):
    c = pl.program_id(1)
    h = h_ref[...]
    rg = rg_ref[...]
    lane = lax.broadcasted_iota(jnp.int32, rg.shape, 1)
    part = None
    for j in range(MOE_CHUNK):
        gate = jnp.where(lane == c * MOE_CHUNK + j, rg, 0.0).sum(axis=-1, keepdims=True)
        a = _dot(h, w1_ref[j])
        b = _dot(h, w3_ref[j])
        hid = (a * jax.nn.sigmoid(a)) * b * gate
        y = _dot(hid.astype(BF16), w2_ref[j])
        part = y if part is None else part + y

    @pl.when(c == 0)
    def _():
        acc_ref[...] = part

    @pl.when(c > 0)
    def _():
        acc_ref[...] += part

    @pl.when(c == pl.num_programs(1) - 1)
    def _():
        o_ref[...] = x_ref[...] + gt2_ref[...] * _rms(acc_ref[...], gpost_ref[...])


def _moe(h2, rg, x, mod5, l, w, *, seq, ctx, tm):
    n = x.shape[0]
    steps_per_seq = seq // tm if not ctx else 0

    def mrow(i):
        return 0 if ctx else 1 + i // steps_per_seq

    nchunk = N_EXPERTS // MOE_CHUNK
    row = lambda wd: pl.BlockSpec((tm, wd), lambda i, c: (i, 0))
    wspec = lambda a, b: pl.BlockSpec((None, MOE_CHUNK, a, b), lambda i, c: (l, c, 0, 0))
    return pl.pallas_call(
        _moe_kernel,
        grid=(n // tm, nchunk),
        in_specs=[
            row(D_MODEL), row(LANES), row(D_MODEL),
            pl.BlockSpec((None, None, None, 1, D_MODEL), lambda i, c: (l, mrow(i), 5, 0, 0)),
            pl.BlockSpec((None, 1, D_MODEL), lambda i, c: (l, 0, 0)),
            wspec(D_MODEL, EXPERT_FF), wspec(D_MODEL, EXPERT_FF), wspec(EXPERT_FF, D_MODEL),
        ],
        out_specs=row(D_MODEL),
        out_shape=jax.ShapeDtypeStruct((n, D_MODEL), F32),
        scratch_shapes=[pltpu.VMEM((tm, D_MODEL), F32)],
        compiler_params=_cparams(("arbitrary", "arbitrary")),
        name="moe_ctx" if ctx else "moe_lat",
    )(h2, rg, x, mod5, w["g_post_ffn"], w["w1"], w["w3"], w["w2"])


def _rope_table(seq):
    t = np.arange(seq)
    n_freq = MLA_ROPE_DIM // 4
    inv = jnp.asarray(ROPE_BASE, F32) ** (-jnp.arange(n_freq, dtype=F32) / n_freq)
    ang_r = jnp.asarray(t // GRID_W, F32)[:, None] * inv
    ang_c = jnp.asarray(t % GRID_W, F32)[:, None] * inv
    cos = jnp.cos(jnp.stack([ang_r, ang_c], axis=1))
    sin = jnp.sin(jnp.stack([ang_r, ang_c], axis=1))
    zeros = jnp.zeros_like(sin)
    c32 = jnp.stack([cos, cos], axis=2).reshape(seq, MLA_ROPE_DIM)
    up32 = jnp.stack([-sin, zeros], axis=2).reshape(seq, MLA_ROPE_DIM)
    dn32 = jnp.stack([zeros, sin], axis=2).reshape(seq, MLA_ROPE_DIM)

    def slot(v, fill):
        return jnp.concatenate([jnp.full((seq, KPE_LANE), fill, F32), v,
                                jnp.full((seq, SLOT - KPE_LANE - MLA_ROPE_DIM), fill, F32)], axis=1)

    return jnp.concatenate([slot(c32, 1.0), slot(up32, 0.0), slot(dn32, 0.0)], axis=1)


def _na_row_classes(n_rows):
    i = np.arange(NA_BLOCK_ROWS)[:, None]
    j = np.arange(NA_SLAB_ROWS)[None, :]
    last_ks = n_rows - NA_SLAB_ROWS
    last_q0 = n_rows - NA_BLOCK_ROWS
    specs = [
        (j < NA_WIN_ROWS, j - i),
        ((j >= i) & (j < i + NA_WIN_ROWS), j - NA_WIN_ROWS // 2 - i),
        (last_ks + j >= n_rows - NA_WIN_ROWS, last_ks + j - last_q0 - i),
    ]
    ok = np.stack([np.broadcast_to(o, (NA_BLOCK_ROWS, NA_SLAB_ROWS)) for o, _ in specs])
    dr = np.stack([np.clip(d + NA_WIN_ROWS - 1, 0, 2 * NA_WIN_ROWS - 2) for _, d in specs])
    return ok, dr


def _bias_kernel(rpb_ref, out_ref, *, row_ok, row_dr):
    reach = NA_WIN_COLS - 1
    rows = rpb_ref[...] * LOG2E
    v_lo = pltpu.roll(rows, SLOT - reach, axis=1)
    v_hi = pltpu.roll(rows, GRID_W - reach, axis=1)
    lane1 = lax.broadcasted_iota(jnp.int32, (1, SLOT), 1)
    mid = (lane1 >= GRID_W - reach) & (lane1 <= GRID_W + reach)
    c = lax.broadcasted_iota(jnp.int32, (GRID_W, SLOT), 0)
    lane = lax.broadcasted_iota(jnp.int32, (GRID_W, SLOT), 1)
    kc = lane & (GRID_W - 1)
    c0 = jnp.clip(c - NA_WIN_COLS // 2, 0, GRID_W - NA_WIN_COLS)
    in_col = (kc >= c0) & (kc < c0 + NA_WIN_COLS)
    first = lane < GRID_W
    for k in range(3):
        for i in range(NA_BLOCK_ROWS):
            for jp in range(NA_SLAB_ROWS // 2):
                ok_a, ok_b = bool(row_ok[k, i, 2 * jp]), bool(row_ok[k, i, 2 * jp + 1])
                a, b = int(row_dr[k, i, 2 * jp]), int(row_dr[k, i, 2 * jp + 1])
                rs = slice(i * GRID_W, (i + 1) * GRID_W)
                cs = slice(jp * SLOT, (jp + 1) * SLOT)
                if not (ok_a or ok_b):
                    out_ref[k, rs, cs] = jnp.full((GRID_W, SLOT), NEG_INF, out_ref.dtype)
                    continue
                v = jnp.where(mid, v_hi[b:b + 1], v_lo[a:a + 1])
                tile = pltpu.roll(jnp.broadcast_to(v, (GRID_W, SLOT)), 0, axis=1, stride=1, stride_axis=0)
                ok = in_col
                if not ok_b:
                    ok = ok & first
                if not ok_a:
                    ok = ok & jnp.logical_not(first)
                out_ref[k, rs, cs] = jnp.where(ok, tile, NEG_INF).astype(out_ref.dtype)


def _na_bias_table(rpb, n_rows):
    n_dr, n_dc = rpb.shape[-2:]
    row_ok, row_dr = _na_row_classes(n_rows)
    rpb_p = jnp.pad(rpb, ((0, 0), (0, 0), (0, 0), (0, SLOT - n_dc)))
    tq, tk = NA_BLOCK_ROWS * GRID_W, NA_SLAB_ROWS * GRID_W
    return pl.pallas_call(
        functools.partial(_bias_kernel, row_ok=row_ok, row_dr=row_dr),
        grid=(DEPTH, NA_HEADS),
        in_specs=[pl.BlockSpec((None, None, n_dr, SLOT), lambda l, h: (l, h, 0, 0))],
        out_specs=pl.BlockSpec((None, 3, None, tq, tk), lambda l, h: (l, 0, h, 0, 0)),
        out_shape=jax.ShapeDtypeStruct((DEPTH, 3, NA_HEADS, tq, tk), BF16),
        compiler_params=_cparams(("arbitrary", "arbitrary")),
        name="na_bias_table",
    )(rpb_p)


def _prep(p):
    L = DEPTH
    w_in = p["w_in"]
    kpe0 = C_G
    kpe1 = C_G + MLA_ROPE_DIM
    w_in_p = jnp.concatenate(
        [w_in[..., :kpe0], w_in[..., kpe1:], jnp.zeros((L, D_MODEL, KPE_LANE), F32), w_in[..., kpe0:kpe1],
         jnp.zeros((L, D_MODEL, SLOT - KPE_LANE - MLA_ROPE_DIM), F32)], axis=-1).astype(BF16)
    w_uq = jnp.pad(p["mla_w_uq"].reshape(L, MLA_Q_LORA, MLA_HEADS, MLA_QK_DIM),
                   ((0, 0), (0, 0), (0, 0), (0, SLOT - MLA_QK_DIM))).reshape(L, MLA_Q_LORA, MLA_HEADS * SLOT)
    ukv = p["mla_w_ukv"].reshape(L, MLA_KV_LORA, MLA_HEADS, MLA_NOPE_DIM + MLA_V_DIM)
    uk = jnp.pad(ukv[..., :MLA_NOPE_DIM], ((0, 0), (0, 0), (0, 0), (0, SLOT - MLA_NOPE_DIM)))
    w_ukv = jnp.concatenate([uk.reshape(L, MLA_KV_LORA, MLA_HEADS * SLOT),
                             ukv[..., MLA_NOPE_DIM:].reshape(L, MLA_KV_LORA, MLA_HEADS * MLA_V_DIM)], axis=-1)
    w_pool_bd = jnp.einsum("lgcd,gh->lgchd", p["w_pool"], jnp.eye(4, dtype=F32)).reshape(L, POOL_WIDTH, POOL_WIDTH)
    w_r = jnp.concatenate([p["w_group_router"], p["w_expert_router"]], axis=-1)
    r_hi = w_r.astype(BF16)
    r_lo = (w_r - r_hi.astype(F32)).astype(BF16)
    padl = lambda a: jnp.pad(a, ((0, 0), (0, 0), (0, LANES - a.shape[-1])))
    vec = lambda a: a.reshape(L, 1, a.shape[-1])
    return {
        "w_in": w_in_p,
        "w_uq": w_uq.astype(BF16),
        "w_ukv": w_ukv.astype(BF16),
        "w_pool_bd": w_pool_bd.astype(BF16),
        "w_branch": p["w_branch"].astype(BF16),
        "w_out": p["w_out"].astype(BF16),
        "w_r2": jnp.concatenate([padl(r_hi), padl(r_lo)], axis=-1),
        "w_r1": padl(r_hi),
        "w1": p["w_exp_gate"].astype(BF16),
        "w3": p["w_exp_up"].astype(BF16),
        "w2": p["w_exp_down"].astype(BF16),
        "g_pre_mix": vec(p["g_pre_mix"]), "g_post_mix": vec(p["g_post_mix"]),
        "g_pre_ffn": vec(p["g_pre_ffn"]), "g_post_ffn": vec(p["g_post_ffn"]),
        "pool_scale": vec(p["pool_scale"]), "q_norm": vec(p["mla_q_norm"]), "kv_norm": vec(p["mla_kv_norm"]),
    }


def kernel(x_prompt, x_sample, cache_na_k, cache_na_v, cache_mla_ckv, cache_mla_kpe, c, c_ctx, w_mod, b_mod,
           g_pre_mix, g_post_mix, g_pre_ffn, g_post_ffn, w_in, w_pool, pool_scale, na_rpb, mla_q_norm, mla_w_uq,
           mla_kv_norm, mla_w_ukv, w_branch, w_out, w_group_router, w_expert_router, w_exp_gate, w_exp_up,
           w_exp_down):
    batch, seq, _ = x_prompt.shape
    dbatch, dseq, _ = x_sample.shape
    past = cache_na_k.shape[2]
    assert dbatch + 1 <= 8 and dseq % (NA_BLOCK_ROWS * GRID_W) == 0

    w = _prep(dict(w_in=w_in, w_pool=w_pool, pool_scale=pool_scale, mla_q_norm=mla_q_norm, mla_w_uq=mla_w_uq,
                   mla_kv_norm=mla_kv_norm, mla_w_ukv=mla_w_ukv, w_branch=w_branch, w_out=w_out,
                   w_group_router=w_group_router, w_expert_router=w_expert_router, w_exp_gate=w_exp_gate,
                   w_exp_up=w_exp_up, w_exp_down=w_exp_down, g_pre_mix=g_pre_mix, g_post_mix=g_post_mix,
                   g_pre_ffn=g_pre_ffn, g_post_ffn=g_post_ffn))
    rope_tab = _rope_table(dseq)
    bias_tab = _na_bias_table(na_rpb, dseq // GRID_W)
    place = jnp.tile(jnp.pad(jnp.eye(MLA_ROPE_DIM, dtype=F32), ((0, 0), (KPE_LANE, SLOT - KPE_LANE - MLA_ROPE_DIM))),
                     (1, MLA_HEADS)).astype(BF16)

    cvec = jnp.concatenate([c_ctx[None], c, jnp.zeros((8 - 1 - dbatch, D_MODEL), F32)], axis=0)
    mod5 = _modulation(cvec, w_mod, b_mod).reshape(DEPTH, 8, 6, 1, D_MODEL)

    ck = cache_na_k.reshape(dbatch, DEPTH, past, NA_WIDTH)
    cv = cache_na_v.reshape(dbatch, DEPTH, past, NA_WIDTH)

    xp = x_prompt.reshape(batch * seq, D_MODEL)
    assert DEPTH == 2
    caches = None
    for l in range(DEPTH):
        u, nq, nvo, qm, km, vmo, gate, *caches = _inproj(xp, mod5, l, w, seq=seq, rope_tab=None, ctx_out=True,
                                                         tm=seq, prev=caches)
        yna, ymla = _attn_ctx(nq, caches[0], nvo, qm, km, vmo, seq=seq, nk_layer=l if l > 0 else None)
        xp, h2, rg = _merge(xp, u, yna, ymla, gate, mod5, l, w, seq=seq, ctx=True, tm=seq)
        xp = _moe(h2, rg, xp, mod5, l, w, seq=seq, ctx=True, tm=1024)
    new_nk, new_nv, new_ckv, new_kpe = caches

    xs = x_sample.reshape(dbatch * dseq, D_MODEL)
    for l in range(DEPTH):
        u, nq, nvo, qm, km, vmo, gate, nk = _inproj(xs, mod5, l, w, seq=dseq, rope_tab=rope_tab, ctx_out=False, tm=256)
        ckm, cvmo, cnvo = _cache_prep(cache_mla_ckv, cache_mla_kpe, cv, w["w_ukv"], place, l)
        yna = _attn_na(nq, nk, nvo, ck, cnvo, bias_tab, l, batch=dbatch, seq=dseq)
        ymla = _attn_mla(qm, km, vmo, ckm, cvmo, batch=dbatch, seq=dseq, tq=256)
        xs, h2, rg = _merge(xs, u, yna, ymla, gate, mod5, l, w, seq=dseq, ctx=False, tm=256)
        xs = _moe(h2, rg, xs, mod5, l, w, seq=dseq, ctx=False, tm=1024)

    heads = (batch, DEPTH, seq, NA_HEADS, NA_HEAD_DIM)
    return (xp.reshape(batch, seq, D_MODEL), xs.reshape(dbatch, dseq, D_MODEL),
            new_nk.reshape(heads), new_nv.reshape(heads), new_ckv, new_kpe)
```

```python
import functools

import numpy as np
import jax
import jax.numpy as jnp
from jax import lax
from jax.experimental import pallas as pl
from jax.experimental.pallas import tpu as pltpu

F32 = jnp.float32
BF16 = jnp.bfloat16

D_MODEL = 1024
DEPTH = 2
GRID_W = 64
POOL_WINDOWS = (2, 4, 8, 16)
POOL_GC = 128
POOL_WIDTH = 512
NA_HEADS = 8
NA_HEAD_DIM = 64
NA_WIDTH = 512
NA_WIN_ROWS = 8
NA_WIN_COLS = 16
MLA_HEADS = 8
MLA_NOPE_DIM = 64
MLA_ROPE_DIM = 32
MLA_V_DIM = 64
MLA_QK_DIM = 96
MLA_Q_LORA = 384
MLA_KV_LORA = 256
N_GROUPS = 4
EXPERTS_PER_GROUP = 4
N_EXPERTS = 16
EXPERT_FF = 256
ROPE_BASE = 10000.0
EPS = 1e-6
NEG_INF = -1e30

LANES = 128
SLOT = LANES
VMEM_LIMIT = 56 * 1024 * 1024

C_U, C_NQ, C_NK, C_NV, C_CQ, C_CKV, C_G, C_KPE, C_END = 0, 512, 1024, 1536, 2048, 2432, 2688, 5760, 5888
KPE_LANE = MLA_NOPE_DIM

LOG2E = 1.4426950408889634
VO_W = 2 * SLOT
VO_ALL = (NA_HEADS // 2) * VO_W

NA_BLOCK_ROWS = 4
NA_SLAB_ROWS = 12


def _cparams(sem):
    return pltpu.CompilerParams(dimension_semantics=sem, vmem_limit_bytes=VMEM_LIMIT)


def _rms(x, g):
    return x * lax.rsqrt(jnp.mean(x * x, axis=-1, keepdims=True) + EPS) * g


def _dot(a, b):
    return jnp.dot(a, b, preferred_element_type=F32)


def _dot_nt(a, b):
    return lax.dot_general(a, b, (((1,), (1,)), ((), ())), preferred_element_type=F32)


def _mod_kernel(c_ref, w_ref, b_ref, o_ref):
    c = c_ref[...]
    s = (c * jax.nn.sigmoid(c)).astype(BF16)
    o_ref[...] = _dot(s, w_ref[...].astype(BF16)) + b_ref[...]


def _modulation(cvec, w_mod, b_mod):
    tn = 1536
    n = w_mod.shape[-1]
    return pl.pallas_call(
        _mod_kernel,
        grid=(DEPTH, n // tn),
        in_specs=[
            pl.BlockSpec((8, D_MODEL), lambda l, j: (0, 0)),
            pl.BlockSpec((None, D_MODEL, tn), lambda l, j: (l, 0, j)),
            pl.BlockSpec((None, 1, tn), lambda l, j: (l, 0, j)),
        ],
        out_specs=pl.BlockSpec((None, 8, tn), lambda l, j: (l, 0, j)),
        out_shape=jax.ShapeDtypeStruct((DEPTH, 8, n), F32),
        compiler_params=_cparams(("arbitrary", "arbitrary")),
        name="modulation",
    )(cvec, w_mod, b_mod.reshape(DEPTH, 1, n))


def _rope_slot(x, tab):
    c, s_up, s_dn = tab[:, 0:SLOT], tab[:, SLOT:2 * SLOT], tab[:, 2 * SLOT:3 * SLOT]
    half = MLA_ROPE_DIM // 4
    x_up = pltpu.roll(x, SLOT - half, axis=1)
    x_dn = pltpu.roll(x, half, axis=1)
    return x * c + x_up * s_up + x_dn * s_dn


def _store_vo(vo_ref, v):
    tm = v.shape[0]
    for p in range(NA_HEADS // 2):
        vo_ref[:, p * VO_W:p * VO_W + SLOT] = v[:, p * SLOT:(p + 1) * SLOT].astype(BF16)
        vo_ref[:, p * VO_W + SLOT:(p + 1) * VO_W] = jnp.ones((tm, SLOT), BF16)


def _inproj_kernel(*refs, rope, ctx_out, stacked):
    n_in = 9 + (1 if rope else 0) + (4 if stacked else 0)
    ins, outs = refs[:n_in], refs[n_in:]
    x_ref, sh_ref, sc_ref, g_ref, win_ref, qn_ref, wuq_ref, kvn_ref, wukv_ref = ins[:9]
    tab_ref = ins[9] if rope else None
    prev = ins[n_in - 4:] if stacked else None
    if ctx_out:
        u_ref, nq_ref, nvo_ref, qm_ref, km_ref, vmo_ref, gate_ref, nk_ref, nv_ref, ckv_ref, kpe_ref = outs
    else:
        u_ref, nq_ref, nvo_ref, qm_ref, km_ref, vmo_ref, gate_ref, nk_ref = outs

    def put(ref, k, val):
        if stacked:
            ref[0] = prev[k][...]
            ref[1] = val
        else:
            ref[...] = val

    x = x_ref[...]
    h = (_rms(x, g_ref[...]) * (1.0 + sc_ref[...]) + sh_ref[...]).astype(BF16)

    def seg(a, b):
        return _dot(h, win_ref[:, a:b])

    u_ref[...] = seg(C_U, C_NQ).astype(u_ref.dtype)
    nq_ref[...] = (seg(C_NQ, C_NK) * (NA_HEAD_DIM ** -0.5 * LOG2E)).astype(BF16)
    nk = seg(C_NK, C_NV)
    nv = seg(C_NV, C_CQ)
    if ctx_out:
        put(nk_ref, 0, nk)
        put(nv_ref, 1, nv)
    else:
        nk_ref[...] = nk.astype(BF16)
    _store_vo(nvo_ref, nv)
    gate_ref[...] = jax.nn.sigmoid(seg(C_G, C_KPE)).astype(BF16)

    cq = _rms(seg(C_CQ, C_CKV), qn_ref[...]).astype(BF16)
    q = _dot(cq, wuq_ref[...])
    tab = tab_ref[...] if rope else None
    for hd in range(MLA_HEADS):
        qs = q[:, hd * SLOT:(hd + 1) * SLOT]
        if rope:
            qs = _rope_slot(qs, tab)
        qm_ref[:, hd * SLOT:(hd + 1) * SLOT] = (qs * (MLA_QK_DIM ** -0.5 * LOG2E)).astype(BF16)

    ckv = _rms(seg(C_CKV, C_G), kvn_ref[...])
    kpe_slot = seg(C_KPE, C_END)
    if ctx_out:
        put(ckv_ref, 2, ckv)
        put(kpe_ref, 3, kpe_slot[:, KPE_LANE:KPE_LANE + MLA_ROPE_DIM])
    if rope:
        kpe_slot = _rope_slot(kpe_slot, tab)
    kv = _dot(ckv.astype(BF16), wukv_ref[...])
    for hd in range(MLA_HEADS):
        km_ref[:, hd * SLOT:(hd + 1) * SLOT] = (kv[:, hd * SLOT:(hd + 1) * SLOT] + kpe_slot).astype(BF16)
    _store_vo(vmo_ref, kv[:, MLA_HEADS * SLOT:])


def _inproj(x, mod5, l, w, *, seq, rope_tab, ctx_out, tm, prev=None):
    n = x.shape[0]
    steps_per_seq = seq // tm if not ctx_out else 0
    stacked = prev is not None
    assert not stacked or (ctx_out and tm == seq and l == 1)

    def mrow(i):
        return 0 if ctx_out else 1 + i // steps_per_seq

    const = lambda i: (l, 0, 0)
    in_specs = [
        pl.BlockSpec((tm, D_MODEL), lambda i: (i, 0)),
        pl.BlockSpec((None, None, None, 1, D_MODEL), lambda i: (l, mrow(i), 0, 0, 0)),
        pl.BlockSpec((None, None, None, 1, D_MODEL), lambda i: (l, mrow(i), 1, 0, 0)),
        pl.BlockSpec((None, 1, D_MODEL), const),
        pl.BlockSpec((None, D_MODEL, C_END), const),
        pl.BlockSpec((None, 1, MLA_Q_LORA), const),
        pl.BlockSpec((None, MLA_Q_LORA, MLA_HEADS * SLOT), const),
        pl.BlockSpec((None, 1, MLA_KV_LORA), const),
        pl.BlockSpec((None, MLA_KV_LORA, MLA_HEADS * SLOT + 512), const),
    ]
    args = [x, mod5, mod5, w["g_pre_mix"], w["w_in"], w["q_norm"], w["w_uq"], w["kv_norm"], w["w_ukv"]]
    rope = rope_tab is not None
    if rope:
        in_specs.append(pl.BlockSpec((tm, 3 * SLOT), lambda i: (i % steps_per_seq, 0)))
        args.append(rope_tab)
    row = lambda wd: pl.BlockSpec((tm, wd), lambda i: (i, 0))
    rows_of = lambda wd, dt: jax.ShapeDtypeStruct((n, wd), dt)
    widths = [512, 512, VO_ALL, MLA_HEADS * SLOT, MLA_HEADS * SLOT, VO_ALL, 3 * D_MODEL]
    out_specs = [row(wd) for wd in widths]
    out_shape = [rows_of(wd, BF16) for wd in widths]
    cache_w = [512, 512, MLA_KV_LORA, MLA_ROPE_DIM] if ctx_out else []
    if not ctx_out:
        out_specs.append(row(512))
        out_shape.append(rows_of(512, BF16))
    elif stacked:
        in_specs += [row(wd) for wd in cache_w]
        args += list(prev)
        out_specs += [pl.BlockSpec((None, DEPTH, seq, wd), lambda i: (i, 0, 0, 0)) for wd in cache_w]
        out_shape += [jax.ShapeDtypeStruct((n // seq, DEPTH, seq, wd), F32) for wd in cache_w]
    else:
        out_specs += [row(wd) for wd in cache_w]
        out_shape += [rows_of(wd, F32) for wd in cache_w]
    return pl.pallas_call(
        functools.partial(_inproj_kernel, rope=rope, ctx_out=ctx_out, stacked=stacked),
        grid=(n // tm,),
        in_specs=in_specs,
        out_specs=out_specs,
        out_shape=out_shape,
        compiler_params=_cparams(("arbitrary",)),
        name="inproj_ctx" if ctx_out else "inproj_lat",
    )(*args)


def _attend(q_h, segs):
    scores = []
    for k, _, bias in segs:
        s = _dot_nt(q_h, k)
        if bias is not None:
            s = s + bias
        scores.append(s)
    m = scores[0].max(axis=-1, keepdims=True)
    for s in scores[1:]:
        m = jnp.maximum(m, s.max(axis=-1, keepdims=True))
    acc = None
    for s, (_, vo, _) in zip(scores, segs):
        o = _dot(jnp.exp2(s - m).astype(BF16), vo)
        acc = o if acc is None else acc + o
    return acc[:, :SLOT] / acc[:, SLOT:]


def _attn_heads(out_ref, q_of, segs_of):
    lo = lax.broadcasted_iota(jnp.int32, (1, SLOT), 1) < NA_HEAD_DIM
    for p in range(NA_HEADS // 2):
        o0, o1 = [_attend(q_of(2 * p + e), segs_of(2 * p + e)) for e in range(2)]
        out_ref[:, p * SLOT:(p + 1) * SLOT] = jnp.where(lo, o0, o1).astype(out_ref.dtype)


def _pair_q(q_ref):
    lo = lax.broadcasted_iota(jnp.int32, (1, SLOT), 1) < NA_HEAD_DIM

    def q_of(h):
        q_pair = q_ref[:, (h // 2) * SLOT:(h // 2 + 1) * SLOT]
        keep = lo if h % 2 == 0 else jnp.logical_not(lo)
        return jnp.where(keep, q_pair, jnp.zeros_like(q_pair))
    return q_of


def _slot(ref, i, rows=slice(None)):
    return ref[rows, i * SLOT:(i + 1) * SLOT].astype(BF16)


def _vo(ref, h, rows=slice(None)):
    return ref[rows, (h // 2) * VO_W:(h // 2 + 1) * VO_W]


def _attn_ctx_kernel(nq_ref, nk_ref, nvo_ref, qm_ref, km_ref, vmo_ref, yna_ref, ymla_ref):
    _attn_heads(yna_ref, _pair_q(nq_ref), lambda h: [(_slot(nk_ref, h // 2), _vo(nvo_ref, h), None)])
    _attn_heads(ymla_ref, lambda h: _slot(qm_ref, h), lambda h: [(_slot(km_ref, h), _vo(vmo_ref, h), None)])


def _attn_ctx(nq, nk, nvo, qm, km, vmo, *, seq, nk_layer=None):
    n = nq.shape[0]
    spec = lambda wd: pl.BlockSpec((seq, wd), lambda b: (b, 0))
    nk_spec = spec(512) if nk_layer is None else pl.BlockSpec((None, None, seq, 512), lambda b: (b, nk_layer, 0, 0))
    return pl.pallas_call(
        _attn_ctx_kernel,
        grid=(n // seq,),
        in_specs=[spec(512), nk_spec, spec(VO_ALL), spec(1024), spec(1024), spec(VO_ALL)],
        out_specs=[spec(512), spec(512)],
        out_shape=[jax.ShapeDtypeStruct((n, 512), BF16)] * 2,
        compiler_params=_cparams(("arbitrary",)),
        name="attn_ctx",
    )(nq, nk, nvo, qm, km, vmo)


def _attn_na_kernel(nq_ref, nk_ref, nvo_ref, ck_ref, cvo_ref, bias_ref, out_ref, *, n_blocks):
    i = pl.program_id(1)
    ks = jnp.clip(NA_BLOCK_ROWS * i - NA_WIN_ROWS // 2, 0, n_blocks * NA_BLOCK_ROWS - NA_SLAB_ROWS)
    slab = pl.ds(pl.multiple_of(ks * GRID_W, GRID_W), NA_SLAB_ROWS * GRID_W)

    def segs_of(h):
        return [(_slot(nk_ref, h // 2, slab), _vo(nvo_ref, h, slab), bias_ref[h].astype(F32)),
                (_slot(ck_ref, h // 2), _vo(cvo_ref, h), None)]

    _attn_heads(out_ref, _pair_q(nq_ref), segs_of)


def _attn_na(nq, nk, nvo, cache_k, cvo, bias_tab, l, *, batch, seq):
    tq = NA_BLOCK_ROWS * GRID_W
    nb = seq // tq
    past = cache_k.shape[2]

    def cls(i):
        return jnp.where(i == 0, 0, jnp.where(i == nb - 1, 2, 1))

    return pl.pallas_call(
        functools.partial(_attn_na_kernel, n_blocks=nb),
        grid=(batch, nb),
        in_specs=[
            pl.BlockSpec((tq, 512), lambda b, i: (b * nb + i, 0)),
            pl.BlockSpec((seq, 512), lambda b, i: (b, 0)),
            pl.BlockSpec((seq, VO_ALL), lambda b, i: (b, 0)),
            pl.BlockSpec((None, None, past, 512), lambda b, i: (b, l, 0, 0)),
            pl.BlockSpec((None, past, VO_ALL), lambda b, i: (b, 0, 0)),
            pl.BlockSpec((None, None, NA_HEADS, tq, NA_SLAB_ROWS * GRID_W), lambda b, i: (l, cls(i), 0, 0, 0)),
        ],
        out_specs=pl.BlockSpec((tq, 512), lambda b, i: (b * nb + i, 0)),
        out_shape=jax.ShapeDtypeStruct((batch * seq, 512), BF16),
        compiler_params=_cparams(("arbitrary", "arbitrary")),
        name="attn_na_lat",
    )(nq, nk, nvo, cache_k, cvo, bias_tab)


def _cache_prep_kernel(ckv_ref, kpe_ref, nv_ref, wukv_ref, place_ref, km_ref, vmo_ref, nvo_ref):
    kv = _dot(ckv_ref[...].astype(BF16), wukv_ref[...])
    kslots = kv[:, :MLA_HEADS * SLOT] + _dot(kpe_ref[...].astype(BF16), place_ref[...])
    km_ref[...] = kslots.astype(BF16)
    _store_vo(vmo_ref, kv[:, MLA_HEADS * SLOT:])
    _store_vo(nvo_ref, nv_ref[...])


def _cache_prep(cache_ckv, cache_kpe, cache_nv, w_ukv, place, l):
    batch, _, past, _ = cache_ckv.shape
    tout = pl.BlockSpec((None, past, VO_ALL), lambda b: (b, 0, 0))
    tshape = jax.ShapeDtypeStruct((batch, past, VO_ALL), BF16)
    return pl.pallas_call(
        _cache_prep_kernel,
        grid=(batch,),
        in_specs=[
            pl.BlockSpec((None, None, past, MLA_KV_LORA), lambda b: (b, l, 0, 0)),
            pl.BlockSpec((None, None, past, MLA_ROPE_DIM), lambda b: (b, l, 0, 0)),
            pl.BlockSpec((None, None, past, 512), lambda b: (b, l, 0, 0)),
            pl.BlockSpec((None, MLA_KV_LORA, MLA_HEADS * SLOT + 512), lambda b: (l, 0, 0)),
            pl.BlockSpec((MLA_ROPE_DIM, MLA_HEADS * SLOT), lambda b: (0, 0)),
        ],
        out_specs=[pl.BlockSpec((None, past, MLA_HEADS * SLOT), lambda b: (b, 0, 0)), tout, tout],
        out_shape=[jax.ShapeDtypeStruct((batch, past, MLA_HEADS * SLOT), BF16), tshape, tshape],
        compiler_params=_cparams(("arbitrary",)),
        name="cache_prep",
    )(cache_ckv, cache_kpe, cache_nv, w_ukv, place)


def _attn_mla_kernel(qm_ref, km_ref, vmo_ref, ckm_ref, cvmo_ref, out_ref):
    _attn_heads(out_ref, lambda h: _slot(qm_ref, h),
                lambda h: [(_slot(km_ref, h), _vo(vmo_ref, h), None), (_slot(ckm_ref, h), _vo(cvmo_ref, h), None)])


def _attn_mla(qm, km, vmo, ckm, cvmo, *, batch, seq, tq):
    nb = seq // tq
    past = ckm.shape[1]
    return pl.pallas_call(
        _attn_mla_kernel,
        grid=(batch, nb),
        in_specs=[
            pl.BlockSpec((tq, MLA_HEADS * SLOT), lambda b, i: (b * nb + i, 0)),
            pl.BlockSpec((seq, MLA_HEADS * SLOT), lambda b, i: (b, 0)),
            pl.BlockSpec((seq, VO_ALL), lambda b, i: (b, 0)),
            pl.BlockSpec((None, past, MLA_HEADS * SLOT), lambda b, i: (b, 0, 0)),
            pl.BlockSpec((None, past, VO_ALL), lambda b, i: (b, 0, 0)),
        ],
        out_specs=pl.BlockSpec((tq, 512), lambda b, i: (b * nb + i, 0)),
        out_shape=jax.ShapeDtypeStruct((batch * seq, 512), BF16),
        compiler_params=_cparams(("arbitrary", "arbitrary")),
        name="attn_mla_lat",
    )(qm, km, vmo, ckm, cvmo)


POOL_HALO = 16


GROUP_LANE = N_EXPERTS


def _route(logits):
    lane = lax.broadcasted_iota(jnp.int32, logits.shape, 1).astype(F32)
    big = jnp.float32(1 << 20)
    is_g = lane < N_GROUPS
    gl = jnp.where(is_g, logits, NEG_INF)
    gmax = gl.max(axis=-1, keepdims=True)
    ge = jnp.where(is_g, jnp.exp(gl - gmax), 0.0)
    gp = ge / ge.sum(axis=-1, keepdims=True)
    g_w = jnp.where(is_g, gp, -1.0).max(axis=-1, keepdims=True)
    g_idx = jnp.where(is_g & (gp == g_w), lane, big).min(axis=-1, keepdims=True)
    e_lane = lane - N_GROUPS
    in_grp = (e_lane >= g_idx * EXPERTS_PER_GROUP) & (e_lane < (g_idx + 1) * EXPERTS_PER_GROUP)
    el = jnp.where(in_grp, logits, NEG_INF)
    emax = el.max(axis=-1, keepdims=True)
    ee = jnp.where(in_grp, jnp.exp(el - emax), 0.0)
    ep = ee / ee.sum(axis=-1, keepdims=True)
    p1 = jnp.where(in_grp, ep, -1.0).max(axis=-1, keepdims=True)
    i1 = jnp.where(in_grp & (ep == p1), lane, big).min(axis=-1, keepdims=True)
    rest = in_grp & (lane != i1)
    p2 = jnp.where(rest, ep, -1.0).max(axis=-1, keepdims=True)
    i2 = jnp.where(rest & (ep == p2), lane, big).min(axis=-1, keepdims=True)
    tot = p1 + p2
    gates = jnp.where(lane == i1, g_w * (p1 / tot), 0.0) + jnp.where(lane == i2, g_w * (p2 / tot), 0.0)
    return jnp.where(lane == GROUP_LANE, g_idx, pltpu.roll(gates, LANES - N_GROUPS, axis=1))


def _merge_kernel(x_ref, u_ref, yna_ref, ymla_ref, gate_ref, gt1_ref, sh2_ref, sc2_ref, gpost_ref, gpre_ref,
                  wpool_ref, pscale_ref, wbr_ref, wout_ref, wr2_ref, wr1_ref,
                  xo_ref, h2_ref, rg_ref, *, seq, tm, steps_per_seq):
    nwin = min(tm + 2 * POOL_HALO, seq)
    if steps_per_seq == 1:
        r0 = 0
        s0 = 0
        uwin = u_ref[...]
        utile = uwin
    else:
        r0 = (pl.program_id(0) % steps_per_seq) * tm
        s0 = pl.multiple_of(jnp.clip(r0 - POOL_HALO, 0, seq - nwin), POOL_HALO)
        uwin = u_ref[pl.ds(s0, nwin), :]
        utile = u_ref[pl.ds(pl.multiple_of(r0, POOL_HALO), tm), :]
    t = r0 + lax.broadcasted_iota(jnp.int32, (tm, nwin), 0)
    s = s0 + lax.broadcasted_iota(jnp.int32, (tm, nwin), 1)
    t1 = r0 + lax.broadcasted_iota(jnp.int32, (tm, 1), 0)
    ds = []
    for g, w in enumerate(POOL_WINDOWS):
        lo = jnp.maximum(t - w // 2, 0)
        hi = jnp.minimum(t + (w - w // 2), seq)
        band = ((s >= lo) & (s < hi)).astype(BF16)
        cnt = (jnp.minimum(t1 + (w - w // 2), seq) - jnp.maximum(t1 - w // 2, 0)).astype(F32)
        sl = slice(g * POOL_GC, (g + 1) * POOL_GC)
        pooled = _dot(band, uwin[:, sl]) / cnt
        ds.append(pooled - utile[:, sl].astype(F32))
    d = jnp.concatenate(ds, axis=-1).astype(BF16)
    y_pool = (_dot(d, wpool_ref[...]) * pscale_ref[...]).astype(BF16)

    g = gate_ref[...]
    merged = (g[:, 0:D_MODEL].astype(F32) * _dot(y_pool, wbr_ref[0])
              + g[:, D_MODEL:2 * D_MODEL].astype(F32) * _dot(yna_ref[...], wbr_ref[1])
              + g[:, 2 * D_MODEL:].astype(F32) * _dot(ymla_ref[...], wbr_ref[2]))
    y = _dot(merged.astype(BF16), wout_ref[...])
    xn = x_ref[...] + gt1_ref[...] * _rms(y, gpost_ref[...])
    xo_ref[...] = xn

    h2 = _rms(xn, gpre_ref[...]) * (1.0 + sc2_ref[...]) + sh2_ref[...]
    h_hi = h2.astype(BF16)
    h2_ref[...] = h_hi
    h_lo = (h2 - h_hi.astype(F32)).astype(BF16)
    two = _dot(h_hi, wr2_ref[...])
    logits = two[:, :LANES] + two[:, LANES:] + _dot(h_lo, wr1_ref[...])
    rg_ref[...] = _route(logits)


def _merge(x, u, yna, ymla, gate, mod5, l, w, *, seq, ctx, tm):
    n = x.shape[0]
    steps_per_seq = seq // tm
    nseq = n // seq

    def mrow(i):
        return 0 if ctx else 1 + i // steps_per_seq

    const = lambda i: (l, 0, 0)
    modspec = lambda k: pl.BlockSpec((None, None, None, 1, D_MODEL), lambda i: (l, mrow(i), k, 0, 0))
    row = lambda wd: pl.BlockSpec((tm, wd), lambda i: (i, 0))
    return pl.pallas_call(
        functools.partial(_merge_kernel, seq=seq, tm=tm, steps_per_seq=steps_per_seq),
        grid=(n // tm,),
        in_specs=[
            row(D_MODEL),
            pl.BlockSpec((None, seq, 512), lambda i: (i // steps_per_seq, 0, 0)),
            row(512), row(512), row(3 * D_MODEL),
            modspec(2), modspec(3), modspec(4),
            pl.BlockSpec((None, 1, D_MODEL), const),
            pl.BlockSpec((None, 1, D_MODEL), const),
            pl.BlockSpec((None, 512, 512), const),
            pl.BlockSpec((None, 1, 512), const),
            pl.BlockSpec((None, 3, 512, D_MODEL), lambda i: (l, 0, 0, 0)),
            pl.BlockSpec((None, D_MODEL, D_MODEL), const),
            pl.BlockSpec((None, D_MODEL, 2 * LANES), const),
            pl.BlockSpec((None, D_MODEL, LANES), const),
        ],
        out_specs=[row(D_MODEL), row(D_MODEL), row(LANES)],
        out_shape=[jax.ShapeDtypeStruct((n, D_MODEL), F32), jax.ShapeDtypeStruct((n, D_MODEL), BF16),
                   jax.ShapeDtypeStruct((n, LANES), F32)],
        compiler_params=_cparams(("arbitrary",)),
        name="merge_ctx" if ctx else "merge_lat",
    )(x, u.reshape(nseq, seq, 512), yna, ymla, gate, mod5, mod5, mod5, w["g_post_mix"], w["g_pre_ffn"],
      w["w_pool_bd"], w["pool_scale"], w["w_branch"], w["w_out"], w["w_r2"], w["w_r1"])


MOE_TM = 512
MOE_ROWS = 160
MOE_ROWS_PAD = 256


def _moe_kernel(h_ref, rg_ref, x_ref, gt2_ref, gpost_ref, lo_ref, up_ref, w1_ref, w3_ref, w2_ref, o_ref, acc_ref):
    tm = h_ref.shape[0]
    h = h_ref[...]
    rg = rg_ref[...]
    lane = lax.broadcasted_iota(jnp.int32, rg.shape, 1)
    grp = jnp.where(lane == GROUP_LANE, rg, 0.0).sum(axis=-1, keepdims=True)
    member = ((lane.astype(F32) == grp) & (lane < N_GROUPS)).astype(F32)
    rank_col = _dot(lo_ref[...], member.astype(BF16))
    member_t = member.T[:8]
    rank_row = _dot(member_t.astype(BF16), up_ref[...])
    count = member.sum(axis=0, keepdims=True)
    rg_hi = rg.astype(BF16)
    rg_lo = (rg - rg_hi.astype(F32)).astype(BF16)
    row_id = lax.broadcasted_iota(jnp.int32, (MOE_ROWS, 1), 0).astype(F32)
    col_id = lax.broadcasted_iota(jnp.int32, (1, MOE_ROWS_PAD), 1).astype(F32)
    lane1 = lax.broadcasted_iota(jnp.int32, (1, LANES), 1)
    acc_ref[...] = jnp.zeros_like(acc_ref)

    for g in range(N_GROUPS):
        rr, mr = rank_row[g:g + 1, :], member_t[g:g + 1, :]
        rc, mc = rank_col[:, g:g + 1], member[:, g:g + 1]
        n_g = jnp.max(jnp.where(lane1 == g, count, 0.0)).astype(jnp.int32)

        def chunk(c, carry, g=g, rr=rr, mr=mr, rc=rc, mc=mc):
            base = (c * MOE_ROWS).astype(F32)
            take = ((rr - base == row_id) & (mr > 0.0)).astype(BF16)
            hs = _dot(take, h).astype(BF16)
            gs = _dot(take, rg_hi) + _dot(take, rg_lo)
            part = None
            for e in range(EXPERTS_PER_GROUP):
                k = g * EXPERTS_PER_GROUP + e
                a = _dot(hs, w1_ref[k])
                b = _dot(hs, w3_ref[k])
                hid = (a * jax.nn.sigmoid(a)) * b * gs[:, k:k + 1]
                y = _dot(hid.astype(BF16), w2_ref[k])
                part = y if part is None else part + y
            part = jnp.concatenate([part.astype(BF16), jnp.zeros((MOE_ROWS_PAD - MOE_ROWS, D_MODEL), BF16)], axis=0)
            rel = rc - base
            give = ((rel == col_id) & (rel < MOE_ROWS) & (mc > 0.0)).astype(BF16)
            acc_ref[...] += _dot(give, part)
            return carry

        chunk(jnp.int32(0), 0)
        lax.fori_loop(1, (n_g + MOE_ROWS - 1) // MOE_ROWS, chunk, 0)

    o_ref[...] = x_ref[...] + gt2_ref[...] * _rms(acc_ref[...], gpost_ref[...])


def _moe(h2, rg, x, mod5, l, w, *, seq, ctx):
    n = x.shape[0]
    tm = MOE_TM
    steps_per_seq = seq // tm if not ctx else 0

    def mrow(i):
        return 0 if ctx else 1 + i // steps_per_seq

    row = lambda wd: pl.BlockSpec((tm, wd), lambda i: (i, 0))
    once = pl.Buffered(1)
    tri = pl.BlockSpec((tm, tm), lambda i: (0, 0), pipeline_mode=once)
    wspec = lambda a, b: pl.BlockSpec((None, N_EXPERTS, a, b), lambda i: (l, 0, 0, 0), pipeline_mode=once)
    return pl.pallas_call(
        _moe_kernel,
        grid=(n // tm,),
        in_specs=[
            row(D_MODEL), row(LANES), row(D_MODEL),
            pl.BlockSpec((None, None, None, 1, D_MODEL), lambda i: (l, mrow(i), 5, 0, 0)),
            pl.BlockSpec((None, 1, D_MODEL), lambda i: (l, 0, 0)),
            tri, tri,
            wspec(D_MODEL, EXPERT_FF), wspec(D_MODEL, EXPERT_FF), wspec(EXPERT_FF, D_MODEL),
        ],
        out_specs=row(D_MODEL),
        out_shape=jax.ShapeDtypeStruct((n, D_MODEL), F32),
        scratch_shapes=[pltpu.VMEM((tm, D_MODEL), F32)],
        compiler_params=_cparams(("arbitrary",)),
        name="moe_ctx" if ctx else "moe_lat",
    )(h2, rg, x, mod5, w["g_post_ffn"], w["tri_lo"], w["tri_up"], w["w1"], w["w3"], w["w2"])


def _rope_table(seq):
    t = np.arange(seq)
    n_freq = MLA_ROPE_DIM // 4
    inv = jnp.asarray(ROPE_BASE, F32) ** (-jnp.arange(n_freq, dtype=F32) / n_freq)
    ang_r = jnp.asarray(t // GRID_W, F32)[:, None] * inv
    ang_c = jnp.asarray(t % GRID_W, F32)[:, None] * inv
    cos = jnp.cos(jnp.stack([ang_r, ang_c], axis=1))
    sin = jnp.sin(jnp.stack([ang_r, ang_c], axis=1))
    zeros = jnp.zeros_like(sin)
    c32 = jnp.stack([cos, cos], axis=2).reshape(seq, MLA_ROPE_DIM)
    up32 = jnp.stack([-sin, zeros], axis=2).reshape(seq, MLA_ROPE_DIM)
    dn32 = jnp.stack([zeros, sin], axis=2).reshape(seq, MLA_ROPE_DIM)

    def slot(v, fill):
        return jnp.concatenate([jnp.full((seq, KPE_LANE), fill, F32), v,
                                jnp.full((seq, SLOT - KPE_LANE - MLA_ROPE_DIM), fill, F32)], axis=1)

    return jnp.concatenate([slot(c32, 1.0), slot(up32, 0.0), slot(dn32, 0.0)], axis=1)


def _na_row_classes(n_rows):
    i = np.arange(NA_BLOCK_ROWS)[:, None]
    j = np.arange(NA_SLAB_ROWS)[None, :]
    last_ks = n_rows - NA_SLAB_ROWS
    last_q0 = n_rows - NA_BLOCK_ROWS
    specs = [
        (j < NA_WIN_ROWS, j - i),
        ((j >= i) & (j < i + NA_WIN_ROWS), j - NA_WIN_ROWS // 2 - i),
        (last_ks + j >= n_rows - NA_WIN_ROWS, last_ks + j - last_q0 - i),
    ]
    ok = np.stack([np.broadcast_to(o, (NA_BLOCK_ROWS, NA_SLAB_ROWS)) for o, _ in specs])
    dr = np.stack([np.clip(d + NA_WIN_ROWS - 1, 0, 2 * NA_WIN_ROWS - 2) for _, d in specs])
    return ok, dr


def _bias_kernel(rpb_ref, out_ref, *, row_ok, row_dr):
    reach = NA_WIN_COLS - 1
    rows = rpb_ref[...] * LOG2E
    v_lo = pltpu.roll(rows, SLOT - reach, axis=1)
    v_hi = pltpu.roll(rows, GRID_W - reach, axis=1)
    lane1 = lax.broadcasted_iota(jnp.int32, (1, SLOT), 1)
    mid = (lane1 >= GRID_W - reach) & (lane1 <= GRID_W + reach)
    c = lax.broadcasted_iota(jnp.int32, (GRID_W, SLOT), 0)
    lane = lax.broadcasted_iota(jnp.int32, (GRID_W, SLOT), 1)
    kc = lane & (GRID_W - 1)
    c0 = jnp.clip(c - NA_WIN_COLS // 2, 0, GRID_W - NA_WIN_COLS)
    in_col = (kc >= c0) & (kc < c0 + NA_WIN_COLS)
    first = lane < GRID_W
    for k in range(3):
        for i in range(NA_BLOCK_ROWS):
            for jp in range(NA_SLAB_ROWS // 2):
                ok_a, ok_b = bool(row_ok[k, i, 2 * jp]), bool(row_ok[k, i, 2 * jp + 1])
                a, b = int(row_dr[k, i, 2 * jp]), int(row_dr[k, i, 2 * jp + 1])
                rs = slice(i * GRID_W, (i + 1) * GRID_W)
                cs = slice(jp * SLOT, (jp + 1) * SLOT)
                if not (ok_a or ok_b):
                    out_ref[k, rs, cs] = jnp.full((GRID_W, SLOT), NEG_INF, out_ref.dtype)
                    continue
                v = jnp.where(mid, v_hi[b:b + 1], v_lo[a:a + 1])
                tile = pltpu.roll(jnp.broadcast_to(v, (GRID_W, SLOT)), 0, axis=1, stride=1, stride_axis=0)
                ok = in_col
                if not ok_b:
                    ok = ok & first
                if not ok_a:
                    ok = ok & jnp.logical_not(first)
                out_ref[k, rs, cs] = jnp.where(ok, tile, NEG_INF).astype(out_ref.dtype)


def _na_bias_table(rpb, n_rows):
    n_dr, n_dc = rpb.shape[-2:]
    row_ok, row_dr = _na_row_classes(n_rows)
    rpb_p = jnp.pad(rpb, ((0, 0), (0, 0), (0, 0), (0, SLOT - n_dc)))
    tq, tk = NA_BLOCK_ROWS * GRID_W, NA_SLAB_ROWS * GRID_W
    return pl.pallas_call(
        functools.partial(_bias_kernel, row_ok=row_ok, row_dr=row_dr),
        grid=(DEPTH, NA_HEADS),
        in_specs=[pl.BlockSpec((None, None, n_dr, SLOT), lambda l, h: (l, h, 0, 0))],
        out_specs=pl.BlockSpec((None, 3, None, tq, tk), lambda l, h: (l, 0, h, 0, 0)),
        out_shape=jax.ShapeDtypeStruct((DEPTH, 3, NA_HEADS, tq, tk), BF16),
        compiler_params=_cparams(("arbitrary", "arbitrary")),
        name="na_bias_table",
    )(rpb_p)


def _prep(p):
    L = DEPTH
    w_in = p["w_in"]
    kpe0 = C_G
    kpe1 = C_G + MLA_ROPE_DIM
    w_in_p = jnp.concatenate(
        [w_in[..., :kpe0], w_in[..., kpe1:], jnp.zeros((L, D_MODEL, KPE_LANE), F32), w_in[..., kpe0:kpe1],
         jnp.zeros((L, D_MODEL, SLOT - KPE_LANE - MLA_ROPE_DIM), F32)], axis=-1).astype(BF16)
    w_uq = jnp.pad(p["mla_w_uq"].reshape(L, MLA_Q_LORA, MLA_HEADS, MLA_QK_DIM),
                   ((0, 0), (0, 0), (0, 0), (0, SLOT - MLA_QK_DIM))).reshape(L, MLA_Q_LORA, MLA_HEADS * SLOT)
    ukv = p["mla_w_ukv"].reshape(L, MLA_KV_LORA, MLA_HEADS, MLA_NOPE_DIM + MLA_V_DIM)
    uk = jnp.pad(ukv[..., :MLA_NOPE_DIM], ((0, 0), (0, 0), (0, 0), (0, SLOT - MLA_NOPE_DIM)))
    w_ukv = jnp.concatenate([uk.reshape(L, MLA_KV_LORA, MLA_HEADS * SLOT),
                             ukv[..., MLA_NOPE_DIM:].reshape(L, MLA_KV_LORA, MLA_HEADS * MLA_V_DIM)], axis=-1)
    w_pool_bd = jnp.einsum("lgcd,gh->lgchd", p["w_pool"], jnp.eye(4, dtype=F32)).reshape(L, POOL_WIDTH, POOL_WIDTH)
    w_r = jnp.concatenate([p["w_group_router"], p["w_expert_router"]], axis=-1)
    r_hi = w_r.astype(BF16)
    r_lo = (w_r - r_hi.astype(F32)).astype(BF16)
    padl = lambda a: jnp.pad(a, ((0, 0), (0, 0), (0, LANES - a.shape[-1])))
    vec = lambda a: a.reshape(L, 1, a.shape[-1])
    tri = np.tril(np.ones((MOE_TM, MOE_TM), np.float32), -1)
    return {
        "tri_lo": jnp.asarray(tri, BF16), "tri_up": jnp.asarray(tri.T, BF16),
        "w_in": w_in_p,
        "w_uq": w_uq.astype(BF16),
        "w_ukv": w_ukv.astype(BF16),
        "w_pool_bd": w_pool_bd.astype(BF16),
        "w_branch": p["w_branch"].astype(BF16),
        "w_out": p["w_out"].astype(BF16),
        "w_r2": jnp.concatenate([padl(r_hi), padl(r_lo)], axis=-1),
        "w_r1": padl(r_hi),
        "w1": p["w_exp_gate"].astype(BF16),
        "w3": p["w_exp_up"].astype(BF16),
        "w2": p["w_exp_down"].astype(BF16),
        "g_pre_mix": vec(p["g_pre_mix"]), "g_post_mix": vec(p["g_post_mix"]),
        "g_pre_ffn": vec(p["g_pre_ffn"]), "g_post_ffn": vec(p["g_post_ffn"]),
        "pool_scale": vec(p["pool_scale"]), "q_norm": vec(p["mla_q_norm"]), "kv_norm": vec(p["mla_kv_norm"]),
    }


def kernel(x_prompt, x_sample, cache_na_k, cache_na_v, cache_mla_ckv, cache_mla_kpe, c, c_ctx, w_mod, b_mod,
           g_pre_mix, g_post_mix, g_pre_ffn, g_post_ffn, w_in, w_pool, pool_scale, na_rpb, mla_q_norm, mla_w_uq,
           mla_kv_norm, mla_w_ukv, w_branch, w_out, w_group_router, w_expert_router, w_exp_gate, w_exp_up,
           w_exp_down):
    batch, seq, _ = x_prompt.shape
    dbatch, dseq, _ = x_sample.shape
    past = cache_na_k.shape[2]
    assert dbatch + 1 <= 8 and dseq % (NA_BLOCK_ROWS * GRID_W) == 0

    w = _prep(dict(w_in=w_in, w_pool=w_pool, pool_scale=pool_scale, mla_q_norm=mla_q_norm, mla_w_uq=mla_w_uq,
                   mla_kv_norm=mla_kv_norm, mla_w_ukv=mla_w_ukv, w_branch=w_branch, w_out=w_out,
                   w_group_router=w_group_router, w_expert_router=w_expert_router, w_exp_gate=w_exp_gate,
                   w_exp_up=w_exp_up, w_exp_down=w_exp_down, g_pre_mix=g_pre_mix, g_post_mix=g_post_mix,
                   g_pre_ffn=g_pre_ffn, g_post_ffn=g_post_ffn))
    rope_tab = _rope_table(dseq)
    bias_tab = _na_bias_table(na_rpb, dseq // GRID_W)
    place = jnp.tile(jnp.pad(jnp.eye(MLA_ROPE_DIM, dtype=F32), ((0, 0), (KPE_LANE, SLOT - KPE_LANE - MLA_ROPE_DIM))),
                     (1, MLA_HEADS)).astype(BF16)

    cvec = jnp.concatenate([c_ctx[None], c, jnp.zeros((8 - 1 - dbatch, D_MODEL), F32)], axis=0)
    mod5 = _modulation(cvec, w_mod, b_mod).reshape(DEPTH, 8, 6, 1, D_MODEL)

    ck = cache_na_k.reshape(dbatch, DEPTH, past, NA_WIDTH)
    cv = cache_na_v.reshape(dbatch, DEPTH, past, NA_WIDTH)

    xp = x_prompt.reshape(batch * seq, D_MODEL)
    assert DEPTH == 2
    caches = None
    for l in range(DEPTH):
        u, nq, nvo, qm, km, vmo, gate, *caches = _inproj(xp, mod5, l, w, seq=seq, rope_tab=None, ctx_out=True,
                                                         tm=seq, prev=caches)
        yna, ymla = _attn_ctx(nq, caches[0], nvo, qm, km, vmo, seq=seq, nk_layer=l if l > 0 else None)
        xp, h2, rg = _merge(xp, u, yna, ymla, gate, mod5, l, w, seq=seq, ctx=True, tm=seq)
        xp = _moe(h2, rg, xp, mod5, l, w, seq=seq, ctx=True)
    new_nk, new_nv, new_ckv, new_kpe = caches

    xs = x_sample.reshape(dbatch * dseq, D_MODEL)
    for l in range(DEPTH):
        u, nq, nvo, qm, km, vmo, gate, nk = _inproj(xs, mod5, l, w, seq=dseq, rope_tab=rope_tab, ctx_out=False, tm=256)
        ckm, cvmo, cnvo = _cache_prep(cache_mla_ckv, cache_mla_kpe, cv, w["w_ukv"], place, l)
        yna = _attn_na(nq, nk, nvo, ck, cnvo, bias_tab, l, batch=dbatch, seq=dseq)
        ymla = _attn_mla(qm, km, vmo, ckm, cvmo, batch=dbatch, seq=dseq, tq=256)
        xs, h2, rg = _merge(xs, u, yna, ymla, gate, mod5, l, w, seq=dseq, ctx=False, tm=256)
        xs = _moe(h2, rg, xs, mod5, l, w, seq=dseq, ctx=False)

    heads = (batch, DEPTH, seq, NA_HEADS, NA_HEAD_DIM)
    return (xp.reshape(batch, seq, D_MODEL), xs.reshape(dbatch, dseq, D_MODEL),
            new_nk.reshape(heads), new_nv.reshape(heads), new_ckv, new_kpe)
```

```python
import functools

import numpy as np
import jax
import jax.numpy as jnp
from jax import lax
from jax.experimental import pallas as pl
from jax.experimental.pallas import tpu as pltpu

F32 = jnp.float32
BF16 = jnp.bfloat16

D_MODEL = 1024
DEPTH = 2
GRID_W = 64
POOL_WINDOWS = (2, 4, 8, 16)
POOL_GC = 128
POOL_WIDTH = 512
NA_HEADS = 8
NA_HEAD_DIM = 64
NA_WIDTH = 512
NA_WIN_ROWS = 8
NA_WIN_COLS = 16
MLA_HEADS = 8
MLA_NOPE_DIM = 64
MLA_ROPE_DIM = 32
MLA_V_DIM = 64
MLA_QK_DIM = 96
MLA_Q_LORA = 384
MLA_KV_LORA = 256
N_GROUPS = 4
EXPERTS_PER_GROUP = 4
N_EXPERTS = 16
EXPERT_FF = 256
ROPE_BASE = 10000.0
EPS = 1e-6
NEG_INF = -1e30

LANES = 128
SLOT = LANES
VMEM_LIMIT = 56 * 1024 * 1024

C_U, C_NQ, C_NK, C_NV, C_CQ, C_CKV, C_G, C_KPE, C_END = 0, 512, 1024, 1536, 2048, 2432, 2688, 5760, 5888
KPE_LANE = MLA_NOPE_DIM

LOG2E = 1.4426950408889634
VO_W = 2 * SLOT
VO_ALL = (NA_HEADS // 2) * VO_W

NA_BLOCK_ROWS = 4
NA_SLAB_ROWS = 12


def _cparams(sem):
    return pltpu.CompilerParams(dimension_semantics=sem, vmem_limit_bytes=VMEM_LIMIT)


def _rms(x, g):
    return x * lax.rsqrt(jnp.mean(x * x, axis=-1, keepdims=True) + EPS) * g


def _dot(a, b):
    return jnp.dot(a, b, preferred_element_type=F32)


def _dot_nt(a, b):
    return lax.dot_general(a, b, (((1,), (1,)), ((), ())), preferred_element_type=F32)


def _mod_kernel(c_ref, w_ref, b_ref, o_ref):
    c = c_ref[...]
    s = (c * jax.nn.sigmoid(c)).astype(BF16)
    o_ref[...] = _dot(s, w_ref[...].astype(BF16)) + b_ref[...]


def _modulation(cvec, w_mod, b_mod):
    tn = 1536
    n = w_mod.shape[-1]
    return pl.pallas_call(
        _mod_kernel,
        grid=(DEPTH, n // tn),
        in_specs=[
            pl.BlockSpec((8, D_MODEL), lambda l, j: (0, 0)),
            pl.BlockSpec((None, D_MODEL, tn), lambda l, j: (l, 0, j)),
            pl.BlockSpec((None, 1, tn), lambda l, j: (l, 0, j)),
        ],
        out_specs=pl.BlockSpec((None, 8, tn), lambda l, j: (l, 0, j)),
        out_shape=jax.ShapeDtypeStruct((DEPTH, 8, n), F32),
        compiler_params=_cparams(("arbitrary", "arbitrary")),
        name="modulation",
    )(cvec, w_mod, b_mod.reshape(DEPTH, 1, n))


def _rope_slot(x, tab):
    c, s_up, s_dn = tab[:, 0:SLOT], tab[:, SLOT:2 * SLOT], tab[:, 2 * SLOT:3 * SLOT]
    half = MLA_ROPE_DIM // 4
    x_up = pltpu.roll(x, SLOT - half, axis=1)
    x_dn = pltpu.roll(x, half, axis=1)
    return x * c + x_up * s_up + x_dn * s_dn


def _store_vo(vo_ref, v):
    tm = v.shape[0]
    for p in range(NA_HEADS // 2):
        vo_ref[:, p * VO_W:p * VO_W + SLOT] = v[:, p * SLOT:(p + 1) * SLOT].astype(BF16)
        vo_ref[:, p * VO_W + SLOT:(p + 1) * VO_W] = jnp.ones((tm, SLOT), BF16)


def _inproj_kernel(*refs, rope, ctx_out, stacked):
    n_in = 9 + (1 if rope else 0) + (4 if stacked else 0)
    ins, outs = refs[:n_in], refs[n_in:]
    x_ref, sh_ref, sc_ref, g_ref, win_ref, qn_ref, wuq_ref, kvn_ref, wukv_ref = ins[:9]
    tab_ref = ins[9] if rope else None
    prev = ins[n_in - 4:] if stacked else None
    if ctx_out:
        u_ref, nq_ref, nvo_ref, qm_ref, km_ref, vmo_ref, gate_ref, nk_ref, nv_ref, ckv_ref, kpe_ref = outs
    else:
        u_ref, nq_ref, nvo_ref, qm_ref, km_ref, vmo_ref, gate_ref, nk_ref = outs

    def put(ref, k, val):
        if stacked:
            ref[0] = prev[k][...]
            ref[1] = val
        else:
            ref[...] = val

    x = x_ref[...]
    h = (_rms(x, g_ref[...]) * (1.0 + sc_ref[...]) + sh_ref[...]).astype(BF16)

    def seg(a, b):
        return _dot(h, win_ref[:, a:b])

    u_ref[...] = seg(C_U, C_NQ).astype(u_ref.dtype)
    nq_ref[...] = (seg(C_NQ, C_NK) * (NA_HEAD_DIM ** -0.5 * LOG2E)).astype(BF16)
    nk = seg(C_NK, C_NV)
    nv = seg(C_NV, C_CQ)
    if ctx_out:
        put(nk_ref, 0, nk)
        put(nv_ref, 1, nv)
    else:
        nk_ref[...] = nk.astype(BF16)
    _store_vo(nvo_ref, nv)
    gate_ref[...] = jax.nn.sigmoid(seg(C_G, C_KPE)).astype(BF16)

    cq = _rms(seg(C_CQ, C_CKV), qn_ref[...]).astype(BF16)
    q = _dot(cq, wuq_ref[...])
    tab = tab_ref[...] if rope else None
    for hd in range(MLA_HEADS):
        qs = q[:, hd * SLOT:(hd + 1) * SLOT]
        if rope:
            qs = _rope_slot(qs, tab)
        qm_ref[:, hd * SLOT:(hd + 1) * SLOT] = (qs * (MLA_QK_DIM ** -0.5 * LOG2E)).astype(BF16)

    ckv = _rms(seg(C_CKV, C_G), kvn_ref[...])
    kpe_slot = seg(C_KPE, C_END)
    if ctx_out:
        put(ckv_ref, 2, ckv)
        put(kpe_ref, 3, kpe_slot[:, KPE_LANE:KPE_LANE + MLA_ROPE_DIM])
    if rope:
        kpe_slot = _rope_slot(kpe_slot, tab)
    kv = _dot(ckv.astype(BF16), wukv_ref[...])
    for hd in range(MLA_HEADS):
        km_ref[:, hd * SLOT:(hd + 1) * SLOT] = (kv[:, hd * SLOT:(hd + 1) * SLOT] + kpe_slot).astype(BF16)
    _store_vo(vmo_ref, kv[:, MLA_HEADS * SLOT:])


def _inproj(x, mod5, l, w, *, seq, rope_tab, ctx_out, tm, prev=None):
    n = x.shape[0]
    steps_per_seq = seq // tm if not ctx_out else 0
    stacked = prev is not None
    assert not stacked or (ctx_out and tm == seq and l == 1)

    def mrow(i):
        return 0 if ctx_out else 1 + i // steps_per_seq

    const = lambda i: (l, 0, 0)
    in_specs = [
        pl.BlockSpec((tm, D_MODEL), lambda i: (i, 0)),
        pl.BlockSpec((None, None, None, 1, D_MODEL), lambda i: (l, mrow(i), 0, 0, 0)),
        pl.BlockSpec((None, None, None, 1, D_MODEL), lambda i: (l, mrow(i), 1, 0, 0)),
        pl.BlockSpec((None, 1, D_MODEL), const),
        pl.BlockSpec((None, D_MODEL, C_END), const, pipeline_mode=pl.Buffered(1)),
        pl.BlockSpec((None, 1, MLA_Q_LORA), const),
        pl.BlockSpec((None, MLA_Q_LORA, MLA_HEADS * SLOT), const, pipeline_mode=pl.Buffered(1)),
        pl.BlockSpec((None, 1, MLA_KV_LORA), const),
        pl.BlockSpec((None, MLA_KV_LORA, MLA_HEADS * SLOT + 512), const, pipeline_mode=pl.Buffered(1)),
    ]
    args = [x, mod5, mod5, w["g_pre_mix"], w["w_in"], w["q_norm"], w["w_uq"], w["kv_norm"], w["w_ukv"]]
    rope = rope_tab is not None
    if rope:
        in_specs.append(pl.BlockSpec((tm, 3 * SLOT), lambda i: (i % steps_per_seq, 0)))
        args.append(rope_tab)
    row = lambda wd: pl.BlockSpec((tm, wd), lambda i: (i, 0))
    rows_of = lambda wd, dt: jax.ShapeDtypeStruct((n, wd), dt)
    widths = [512, 512, VO_ALL, MLA_HEADS * SLOT, MLA_HEADS * SLOT, VO_ALL, 3 * D_MODEL]
    out_specs = [row(wd) for wd in widths]
    out_shape = [rows_of(wd, BF16) for wd in widths]
    cache_w = [512, 512, MLA_KV_LORA, MLA_ROPE_DIM] if ctx_out else []
    if not ctx_out:
        out_specs.append(row(512))
        out_shape.append(rows_of(512, BF16))
    elif stacked:
        in_specs += [row(wd) for wd in cache_w]
        args += list(prev)
        out_specs += [pl.BlockSpec((None, DEPTH, seq, wd), lambda i: (i, 0, 0, 0)) for wd in cache_w]
        out_shape += [jax.ShapeDtypeStruct((n // seq, DEPTH, seq, wd), F32) for wd in cache_w]
    else:
        out_specs += [row(wd) for wd in cache_w]
        out_shape += [rows_of(wd, F32) for wd in cache_w]
    return pl.pallas_call(
        functools.partial(_inproj_kernel, rope=rope, ctx_out=ctx_out, stacked=stacked),
        grid=(n // tm,),
        in_specs=in_specs,
        out_specs=out_specs,
        out_shape=out_shape,
        compiler_params=_cparams(("arbitrary",)),
        name="inproj_ctx" if ctx_out else "inproj_lat",
    )(*args)


def _attend(q_h, segs):
    scores = []
    for k, _, bias in segs:
        s = _dot_nt(q_h, k)
        if bias is not None:
            s = s + bias
        scores.append(s)
    m = scores[0].max(axis=-1, keepdims=True)
    for s in scores[1:]:
        m = jnp.maximum(m, s.max(axis=-1, keepdims=True))
    acc = None
    for s, (_, vo, _) in zip(scores, segs):
        o = _dot(jnp.exp2(s - m).astype(BF16), vo)
        acc = o if acc is None else acc + o
    return acc[:, :SLOT] / acc[:, SLOT:]


def _attn_heads(out_ref, q_of, segs_of):
    lo = lax.broadcasted_iota(jnp.int32, (1, SLOT), 1) < NA_HEAD_DIM
    for p in range(NA_HEADS // 2):
        o0, o1 = [_attend(q_of(2 * p + e), segs_of(2 * p + e)) for e in range(2)]
        out_ref[:, p * SLOT:(p + 1) * SLOT] = jnp.where(lo, o0, o1).astype(out_ref.dtype)


def _pair_q(q_ref):
    lo = lax.broadcasted_iota(jnp.int32, (1, SLOT), 1) < NA_HEAD_DIM

    def q_of(h):
        q_pair = q_ref[:, (h // 2) * SLOT:(h // 2 + 1) * SLOT]
        keep = lo if h % 2 == 0 else jnp.logical_not(lo)
        return jnp.where(keep, q_pair, jnp.zeros_like(q_pair))
    return q_of


def _slot(ref, i, rows=slice(None)):
    return ref[rows, i * SLOT:(i + 1) * SLOT].astype(BF16)


def _vo(ref, h, rows=slice(None)):
    return ref[rows, (h // 2) * VO_W:(h // 2 + 1) * VO_W]


def _attn_ctx_kernel(nq_ref, nk_ref, nvo_ref, qm_ref, km_ref, vmo_ref, yna_ref, ymla_ref):
    _attn_heads(yna_ref, _pair_q(nq_ref), lambda h: [(_slot(nk_ref, h // 2), _vo(nvo_ref, h), None)])
    _attn_heads(ymla_ref, lambda h: _slot(qm_ref, h), lambda h: [(_slot(km_ref, h), _vo(vmo_ref, h), None)])


def _attn_ctx(nq, nk, nvo, qm, km, vmo, *, seq, nk_layer=None):
    n = nq.shape[0]
    spec = lambda wd: pl.BlockSpec((seq, wd), lambda b: (b, 0))
    nk_spec = spec(512) if nk_layer is None else pl.BlockSpec((None, None, seq, 512), lambda b: (b, nk_layer, 0, 0))
    return pl.pallas_call(
        _attn_ctx_kernel,
        grid=(n // seq,),
        in_specs=[spec(512), nk_spec, spec(VO_ALL), spec(1024), spec(1024), spec(VO_ALL)],
        out_specs=[spec(512), spec(512)],
        out_shape=[jax.ShapeDtypeStruct((n, 512), BF16)] * 2,
        compiler_params=_cparams(("arbitrary",)),
        name="attn_ctx",
    )(nq, nk, nvo, qm, km, vmo)


def _attn_na_kernel(nq_ref, nk_ref, nvo_ref, ck_ref, cvo_ref, bias_ref, out_ref, *, n_blocks):
    i = pl.program_id(1)
    ks = jnp.clip(NA_BLOCK_ROWS * i - NA_WIN_ROWS // 2, 0, n_blocks * NA_BLOCK_ROWS - NA_SLAB_ROWS)
    slab = pl.ds(pl.multiple_of(ks * GRID_W, GRID_W), NA_SLAB_ROWS * GRID_W)

    def segs_of(h):
        return [(_slot(nk_ref, h // 2, slab), _vo(nvo_ref, h, slab), bias_ref[h].astype(F32)),
                (_slot(ck_ref, h // 2), _vo(cvo_ref, h), None)]

    _attn_heads(out_ref, _pair_q(nq_ref), segs_of)


def _attn_na(nq, nk, nvo, cache_k, cvo, bias_tab, l, *, batch, seq):
    tq = NA_BLOCK_ROWS * GRID_W
    nb = seq // tq
    past = cache_k.shape[2]

    def cls(i):
        return jnp.where(i == 0, 0, jnp.where(i == nb - 1, 2, 1))

    return pl.pallas_call(
        functools.partial(_attn_na_kernel, n_blocks=nb),
        grid=(batch, nb),
        in_specs=[
            pl.BlockSpec((tq, 512), lambda b, i: (b * nb + i, 0)),
            pl.BlockSpec((seq, 512), lambda b, i: (b, 0)),
            pl.BlockSpec((seq, VO_ALL), lambda b, i: (b, 0)),
            pl.BlockSpec((None, None, past, 512), lambda b, i: (b, l, 0, 0)),
            pl.BlockSpec((None, past, VO_ALL), lambda b, i: (b, 0, 0)),
            pl.BlockSpec((None, None, NA_HEADS, tq, NA_SLAB_ROWS * GRID_W), lambda b, i: (l, cls(i), 0, 0, 0)),
        ],
        out_specs=pl.BlockSpec((tq, 512), lambda b, i: (b * nb + i, 0)),
        out_shape=jax.ShapeDtypeStruct((batch * seq, 512), BF16),
        compiler_params=_cparams(("arbitrary", "arbitrary")),
        name="attn_na_lat",
    )(nq, nk, nvo, cache_k, cvo, bias_tab)


def _cache_prep_kernel(ckv_ref, kpe_ref, nv_ref, wukv_ref, place_ref, km_ref, vmo_ref, nvo_ref):
    kv = _dot(ckv_ref[...].astype(BF16), wukv_ref[...])
    kslots = kv[:, :MLA_HEADS * SLOT] + _dot(kpe_ref[...].astype(BF16), place_ref[...])
    km_ref[...] = kslots.astype(BF16)
    _store_vo(vmo_ref, kv[:, MLA_HEADS * SLOT:])
    _store_vo(nvo_ref, nv_ref[...])


def _cache_prep(cache_ckv, cache_kpe, cache_nv, w_ukv, place, l):
    batch, _, past, _ = cache_ckv.shape
    tout = pl.BlockSpec((None, past, VO_ALL), lambda b: (b, 0, 0))
    tshape = jax.ShapeDtypeStruct((batch, past, VO_ALL), BF16)
    return pl.pallas_call(
        _cache_prep_kernel,
        grid=(batch,),
        in_specs=[
            pl.BlockSpec((None, None, past, MLA_KV_LORA), lambda b: (b, l, 0, 0)),
            pl.BlockSpec((None, None, past, MLA_ROPE_DIM), lambda b: (b, l, 0, 0)),
            pl.BlockSpec((None, None, past, 512), lambda b: (b, l, 0, 0)),
            pl.BlockSpec((None, MLA_KV_LORA, MLA_HEADS * SLOT + 512), lambda b: (l, 0, 0)),
            pl.BlockSpec((MLA_ROPE_DIM, MLA_HEADS * SLOT), lambda b: (0, 0)),
        ],
        out_specs=[pl.BlockSpec((None, past, MLA_HEADS * SLOT), lambda b: (b, 0, 0)), tout, tout],
        out_shape=[jax.ShapeDtypeStruct((batch, past, MLA_HEADS * SLOT), BF16), tshape, tshape],
        compiler_params=_cparams(("arbitrary",)),
        name="cache_prep",
    )(cache_ckv, cache_kpe, cache_nv, w_ukv, place)


def _attn_mla_kernel(qm_ref, km_ref, vmo_ref, ckm_ref, cvmo_ref, out_ref):
    _attn_heads(out_ref, lambda h: _slot(qm_ref, h),
                lambda h: [(_slot(km_ref, h), _vo(vmo_ref, h), None), (_slot(ckm_ref, h), _vo(cvmo_ref, h), None)])


def _attn_mla(qm, km, vmo, ckm, cvmo, *, batch, seq, tq):
    nb = seq // tq
    past = ckm.shape[1]
    return pl.pallas_call(
        _attn_mla_kernel,
        grid=(batch, nb),
        in_specs=[
            pl.BlockSpec((tq, MLA_HEADS * SLOT), lambda b, i: (b * nb + i, 0)),
            pl.BlockSpec((seq, MLA_HEADS * SLOT), lambda b, i: (b, 0)),
            pl.BlockSpec((seq, VO_ALL), lambda b, i: (b, 0)),
            pl.BlockSpec((None, past, MLA_HEADS * SLOT), lambda b, i: (b, 0, 0)),
            pl.BlockSpec((None, past, VO_ALL), lambda b, i: (b, 0, 0)),
        ],
        out_specs=pl.BlockSpec((tq, 512), lambda b, i: (b * nb + i, 0)),
        out_shape=jax.ShapeDtypeStruct((batch * seq, 512), BF16),
        compiler_params=_cparams(("arbitrary", "arbitrary")),
        name="attn_mla_lat",
    )(qm, km, vmo, ckm, cvmo)


POOL_HALO = 16


GROUP_LANE = N_EXPERTS


def _route(logits):
    lane = lax.broadcasted_iota(jnp.int32, logits.shape, 1).astype(F32)
    big = jnp.float32(1 << 20)
    is_g = lane < N_GROUPS
    gl = jnp.where(is_g, logits, NEG_INF)
    gmax = gl.max(axis=-1, keepdims=True)
    ge = jnp.where(is_g, jnp.exp(gl - gmax), 0.0)
    gp = ge / ge.sum(axis=-1, keepdims=True)
    g_w = jnp.where(is_g, gp, -1.0).max(axis=-1, keepdims=True)
    g_idx = jnp.where(is_g & (gp == g_w), lane, big).min(axis=-1, keepdims=True)
    e_lane = lane - N_GROUPS
    in_grp = (e_lane >= g_idx * EXPERTS_PER_GROUP) & (e_lane < (g_idx + 1) * EXPERTS_PER_GROUP)
    el = jnp.where(in_grp, logits, NEG_INF)
    emax = el.max(axis=-1, keepdims=True)
    ee = jnp.where(in_grp, jnp.exp(el - emax), 0.0)
    ep = ee / ee.sum(axis=-1, keepdims=True)
    p1 = jnp.where(in_grp, ep, -1.0).max(axis=-1, keepdims=True)
    i1 = jnp.where(in_grp & (ep == p1), lane, big).min(axis=-1, keepdims=True)
    rest = in_grp & (lane != i1)
    p2 = jnp.where(rest, ep, -1.0).max(axis=-1, keepdims=True)
    i2 = jnp.where(rest & (ep == p2), lane, big).min(axis=-1, keepdims=True)
    tot = p1 + p2
    gates = jnp.where(lane == i1, g_w * (p1 / tot), 0.0) + jnp.where(lane == i2, g_w * (p2 / tot), 0.0)
    return jnp.where(lane == GROUP_LANE, g_idx, pltpu.roll(gates, LANES - N_GROUPS, axis=1))


def _merge_kernel(x_ref, u_ref, yna_ref, ymla_ref, gate_ref, gt1_ref, sh2_ref, sc2_ref, gpost_ref, gpre_ref,
                  wpool_ref, pscale_ref, wbr_ref, wout_ref, wr2_ref, wr1_ref,
                  xo_ref, h2_ref, rg_ref, *, seq, tm, steps_per_seq):
    def pool_diff(uwin, utile, r0, s0):
        rows, nwin = utile.shape[0], uwin.shape[0]
        t = r0 + lax.broadcasted_iota(jnp.int32, (rows, nwin), 0)
        s = s0 + lax.broadcasted_iota(jnp.int32, (rows, nwin), 1)
        t1 = r0 + lax.broadcasted_iota(jnp.int32, (rows, 1), 0)
        ds = []
        for g, w in enumerate(POOL_WINDOWS):
            lo = jnp.maximum(t - w // 2, 0)
            hi = jnp.minimum(t + (w - w // 2), seq)
            band = ((s >= lo) & (s < hi)).astype(BF16)
            cnt = (jnp.minimum(t1 + (w - w // 2), seq) - jnp.maximum(t1 - w // 2, 0)).astype(F32)
            sl = slice(g * POOL_GC, (g + 1) * POOL_GC)
            ds.append(_dot(band, uwin[:, sl]) / cnt - utile[:, sl].astype(F32))
        return jnp.concatenate(ds, axis=-1).astype(BF16)

    if steps_per_seq == 0:
        d = jnp.concatenate([pool_diff(u_ref[si], u_ref[si], 0, 0) for si in range(tm // seq)], axis=0)
    else:
        nwin = min(tm + 2 * POOL_HALO, seq)
        r0 = (pl.program_id(0) % steps_per_seq) * tm
        s0 = pl.multiple_of(jnp.clip(r0 - POOL_HALO, 0, seq - nwin), POOL_HALO)
        d = pool_diff(u_ref[0, pl.ds(s0, nwin), :], u_ref[0, pl.ds(pl.multiple_of(r0, POOL_HALO), tm), :], r0, s0)
    y_pool = (_dot(d, wpool_ref[...]) * pscale_ref[...]).astype(BF16)

    g = gate_ref[...]
    merged = (g[:, 0:D_MODEL].astype(F32) * _dot(y_pool, wbr_ref[0])
              + g[:, D_MODEL:2 * D_MODEL].astype(F32) * _dot(yna_ref[...], wbr_ref[1])
              + g[:, 2 * D_MODEL:].astype(F32) * _dot(ymla_ref[...], wbr_ref[2]))
    y = _dot(merged.astype(BF16), wout_ref[...])
    xn = x_ref[...] + gt1_ref[...] * _rms(y, gpost_ref[...])
    xo_ref[...] = xn

    h2 = _rms(xn, gpre_ref[...]) * (1.0 + sc2_ref[...]) + sh2_ref[...]
    h_hi = h2.astype(BF16)
    h2_ref[...] = h_hi
    h_lo = (h2 - h_hi.astype(F32)).astype(BF16)
    two = _dot(h_hi, wr2_ref[...])
    logits = two[:, :LANES] + two[:, LANES:] + _dot(h_lo, wr1_ref[...])
    rg_ref[...] = _route(logits)


def _merge(x, u, yna, ymla, gate, mod5, l, w, *, seq, ctx, tm):
    n = x.shape[0]
    nseq = n // seq
    if tm >= seq:
        assert ctx and tm % seq == 0
        steps_per_seq = 0
        u_spec = pl.BlockSpec((tm // seq, seq, 512), lambda i: (i, 0, 0))
    else:
        steps_per_seq = seq // tm
        u_spec = pl.BlockSpec((1, seq, 512), lambda i: (i // steps_per_seq, 0, 0))

    def mrow(i):
        return 0 if ctx else 1 + i // steps_per_seq

    const = lambda i: (l, 0, 0)
    modspec = lambda k: pl.BlockSpec((None, None, None, 1, D_MODEL), lambda i: (l, mrow(i), k, 0, 0))
    row = lambda wd: pl.BlockSpec((tm, wd), lambda i: (i, 0))
    return pl.pallas_call(
        functools.partial(_merge_kernel, seq=seq, tm=tm, steps_per_seq=steps_per_seq),
        grid=(n // tm,),
        in_specs=[
            row(D_MODEL),
            u_spec,
            row(512), row(512), row(3 * D_MODEL),
            modspec(2), modspec(3), modspec(4),
            pl.BlockSpec((None, 1, D_MODEL), const),
            pl.BlockSpec((None, 1, D_MODEL), const),
            pl.BlockSpec((None, 512, 512), const),
            pl.BlockSpec((None, 1, 512), const),
            pl.BlockSpec((None, 3, 512, D_MODEL), lambda i: (l, 0, 0, 0)),
            pl.BlockSpec((None, D_MODEL, D_MODEL), const),
            pl.BlockSpec((None, D_MODEL, 2 * LANES), const),
            pl.BlockSpec((None, D_MODEL, LANES), const),
        ],
        out_specs=[row(D_MODEL), row(D_MODEL), row(LANES)],
        out_shape=[jax.ShapeDtypeStruct((n, D_MODEL), F32), jax.ShapeDtypeStruct((n, D_MODEL), BF16),
                   jax.ShapeDtypeStruct((n, LANES), F32)],
        compiler_params=_cparams(("arbitrary",)),
        name="merge_ctx" if ctx else "merge_lat",
    )(x, u.reshape(nseq, seq, 512), yna, ymla, gate, mod5, mod5, mod5, w["g_post_mix"], w["g_pre_ffn"],
      w["w_pool_bd"], w["pool_scale"], w["w_branch"], w["w_out"], w["w_r2"], w["w_r1"])


MOE_TM = 1024
MOE_ROWS = 288
MOE_ROWS_PAD = 384


def _moe_kernel(h_ref, rg_ref, x_ref, gt2_ref, gpost_ref, w1_ref, w3_ref, w2_ref, o_ref):
    tm = h_ref.shape[0]
    acc_ref = o_ref
    h = h_ref[...]
    rg = rg_ref[...]
    lane = lax.broadcasted_iota(jnp.int32, rg.shape, 1)
    grp = jnp.where(lane == GROUP_LANE, rg, 0.0).sum(axis=-1, keepdims=True)
    member = ((lane.astype(F32) == grp) & (lane < N_GROUPS)).astype(F32)
    t_row = lax.broadcasted_iota(jnp.int32, (tm, tm), 0)
    t_col = lax.broadcasted_iota(jnp.int32, (tm, tm), 1)
    rank_col = _dot((t_col < t_row).astype(BF16), member.astype(BF16))
    member_t = member.T[:8]
    rank_row = _dot(member_t.astype(BF16), (t_row < t_col).astype(BF16))
    count = member.sum(axis=0, keepdims=True)
    rg_hi = rg.astype(BF16)
    rg_lo = (rg - rg_hi.astype(F32)).astype(BF16)
    row_id = lax.broadcasted_iota(jnp.int32, (MOE_ROWS, 1), 0).astype(F32)
    col_id = lax.broadcasted_iota(jnp.int32, (1, MOE_ROWS_PAD), 1).astype(F32)
    lane1 = lax.broadcasted_iota(jnp.int32, (1, LANES), 1)
    acc_ref[...] = jnp.zeros_like(acc_ref)

    for g in range(N_GROUPS):
        rr, mr = rank_row[g:g + 1, :], member_t[g:g + 1, :]
        rc, mc = rank_col[:, g:g + 1], member[:, g:g + 1]
        n_g = jnp.max(jnp.where(lane1 == g, count, 0.0)).astype(jnp.int32)

        def chunk(c, carry, g=g, rr=rr, mr=mr, rc=rc, mc=mc):
            base = (c * MOE_ROWS).astype(F32)
            take = ((rr - base == row_id) & (mr > 0.0)).astype(BF16)
            hs = _dot(take, h).astype(BF16)
            gs = _dot(take, rg_hi) + _dot(take, rg_lo)
            part = None
            for e in range(EXPERTS_PER_GROUP):
                k = g * EXPERTS_PER_GROUP + e
                a = _dot(hs, w1_ref[k])
                b = _dot(hs, w3_ref[k])
                hid = (a * jax.nn.sigmoid(a)) * b * gs[:, k:k + 1]
                y = _dot(hid.astype(BF16), w2_ref[k])
                part = y if part is None else part + y
            part = jnp.concatenate([part.astype(BF16), jnp.zeros((MOE_ROWS_PAD - MOE_ROWS, D_MODEL), BF16)], axis=0)
            rel = rc - base
            give = ((rel == col_id) & (rel < MOE_ROWS) & (mc > 0.0)).astype(BF16)
            acc_ref[...] += _dot(give, part)
            return carry

        chunk(jnp.int32(0), 0)
        lax.fori_loop(1, (n_g + MOE_ROWS - 1) // MOE_ROWS, chunk, 0)

    o_ref[...] = x_ref[...] + gt2_ref[...] * _rms(acc_ref[...], gpost_ref[...])


def _moe(h2, rg, x, mod5, l, w, *, seq, ctx):
    n = x.shape[0]
    tm = MOE_TM
    steps_per_seq = seq // tm if not ctx else 0

    def mrow(i):
        return 0 if ctx else 1 + i // steps_per_seq

    row = lambda wd: pl.BlockSpec((tm, wd), lambda i: (i, 0))
    once = pl.Buffered(1)
    wspec = lambda a, b: pl.BlockSpec((None, N_EXPERTS, a, b), lambda i: (l, 0, 0, 0), pipeline_mode=once)
    return pl.pallas_call(
        _moe_kernel,
        grid=(n // tm,),
        in_specs=[
            row(D_MODEL), row(LANES), row(D_MODEL),
            pl.BlockSpec((None, None, None, 1, D_MODEL), lambda i: (l, mrow(i), 5, 0, 0)),
            pl.BlockSpec((None, 1, D_MODEL), lambda i: (l, 0, 0)),
            wspec(D_MODEL, EXPERT_FF), wspec(D_MODEL, EXPERT_FF), wspec(EXPERT_FF, D_MODEL),
        ],
        out_specs=row(D_MODEL),
        out_shape=jax.ShapeDtypeStruct((n, D_MODEL), F32),
        compiler_params=_cparams(("arbitrary",)),
        name="moe_ctx" if ctx else "moe_lat",
    )(h2, rg, x, mod5, w["g_post_ffn"], w["w1"], w["w3"], w["w2"])


def _rope_table(seq):
    t = np.arange(seq)
    n_freq = MLA_ROPE_DIM // 4
    inv = jnp.asarray(ROPE_BASE, F32) ** (-jnp.arange(n_freq, dtype=F32) / n_freq)
    ang_r = jnp.asarray(t // GRID_W, F32)[:, None] * inv
    ang_c = jnp.asarray(t % GRID_W, F32)[:, None] * inv
    cos = jnp.cos(jnp.stack([ang_r, ang_c], axis=1))
    sin = jnp.sin(jnp.stack([ang_r, ang_c], axis=1))
    zeros = jnp.zeros_like(sin)
    c32 = jnp.stack([cos, cos], axis=2).reshape(seq, MLA_ROPE_DIM)
    up32 = jnp.stack([-sin, zeros], axis=2).reshape(seq, MLA_ROPE_DIM)
    dn32 = jnp.stack([zeros, sin], axis=2).reshape(seq, MLA_ROPE_DIM)

    def slot(v, fill):
        return jnp.concatenate([jnp.full((seq, KPE_LANE), fill, F32), v,
                                jnp.full((seq, SLOT - KPE_LANE - MLA_ROPE_DIM), fill, F32)], axis=1)

    return jnp.concatenate([slot(c32, 1.0), slot(up32, 0.0), slot(dn32, 0.0)], axis=1)


def _na_row_classes(n_rows):
    i = np.arange(NA_BLOCK_ROWS)[:, None]
    j = np.arange(NA_SLAB_ROWS)[None, :]
    last_ks = n_rows - NA_SLAB_ROWS
    last_q0 = n_rows - NA_BLOCK_ROWS
    specs = [
        (j < NA_WIN_ROWS, j - i),
        ((j >= i) & (j < i + NA_WIN_ROWS), j - NA_WIN_ROWS // 2 - i),
        (last_ks + j >= n_rows - NA_WIN_ROWS, last_ks + j - last_q0 - i),
    ]
    ok = np.stack([np.broadcast_to(o, (NA_BLOCK_ROWS, NA_SLAB_ROWS)) for o, _ in specs])
    dr = np.stack([np.clip(d + NA_WIN_ROWS - 1, 0, 2 * NA_WIN_ROWS - 2) for _, d in specs])
    return ok, dr


def _bias_kernel(rpb_ref, out_ref, *, row_ok, row_dr):
    reach = NA_WIN_COLS - 1
    rows = rpb_ref[...] * LOG2E
    v_lo = pltpu.roll(rows, SLOT - reach, axis=1)
    v_hi = pltpu.roll(rows, GRID_W - reach, axis=1)
    lane1 = lax.broadcasted_iota(jnp.int32, (1, SLOT), 1)
    mid = (lane1 >= GRID_W - reach) & (lane1 <= GRID_W + reach)
    c = lax.broadcasted_iota(jnp.int32, (GRID_W, SLOT), 0)
    lane = lax.broadcasted_iota(jnp.int32, (GRID_W, SLOT), 1)
    kc = lane & (GRID_W - 1)
    c0 = jnp.clip(c - NA_WIN_COLS // 2, 0, GRID_W - NA_WIN_COLS)
    in_col = (kc >= c0) & (kc < c0 + NA_WIN_COLS)
    first = lane < GRID_W
    for k in range(3):
        for i in range(NA_BLOCK_ROWS):
            for jp in range(NA_SLAB_ROWS // 2):
                ok_a, ok_b = bool(row_ok[k, i, 2 * jp]), bool(row_ok[k, i, 2 * jp + 1])
                a, b = int(row_dr[k, i, 2 * jp]), int(row_dr[k, i, 2 * jp + 1])
                rs = slice(i * GRID_W, (i + 1) * GRID_W)
                cs = slice(jp * SLOT, (jp + 1) * SLOT)
                if not (ok_a or ok_b):
                    out_ref[k, rs, cs] = jnp.full((GRID_W, SLOT), NEG_INF, out_ref.dtype)
                    continue
                v = jnp.where(mid, v_hi[b:b + 1], v_lo[a:a + 1])
                tile = pltpu.roll(jnp.broadcast_to(v, (GRID_W, SLOT)), 0, axis=1, stride=1, stride_axis=0)
                ok = in_col
                if not ok_b:
                    ok = ok & first
                if not ok_a:
                    ok = ok & jnp.logical_not(first)
                out_ref[k, rs, cs] = jnp.where(ok, tile, NEG_INF).astype(out_ref.dtype)


def _na_bias_table(rpb, n_rows):
    n_dr, n_dc = rpb.shape[-2:]
    row_ok, row_dr = _na_row_classes(n_rows)
    rpb_p = jnp.pad(rpb, ((0, 0), (0, 0), (0, 0), (0, SLOT - n_dc)))
    tq, tk = NA_BLOCK_ROWS * GRID_W, NA_SLAB_ROWS * GRID_W
    return pl.pallas_call(
        functools.partial(_bias_kernel, row_ok=row_ok, row_dr=row_dr),
        grid=(DEPTH, NA_HEADS),
        in_specs=[pl.BlockSpec((None, None, n_dr, SLOT), lambda l, h: (l, h, 0, 0))],
        out_specs=pl.BlockSpec((None, 3, None, tq, tk), lambda l, h: (l, 0, h, 0, 0)),
        out_shape=jax.ShapeDtypeStruct((DEPTH, 3, NA_HEADS, tq, tk), BF16),
        compiler_params=_cparams(("arbitrary", "arbitrary")),
        name="na_bias_table",
    )(rpb_p)


def _prep(p):
    L = DEPTH
    w_in = p["w_in"]
    kpe0 = C_G
    kpe1 = C_G + MLA_ROPE_DIM
    w_in_p = jnp.concatenate(
        [w_in[..., :kpe0], w_in[..., kpe1:], jnp.zeros((L, D_MODEL, KPE_LANE), F32), w_in[..., kpe0:kpe1],
         jnp.zeros((L, D_MODEL, SLOT - KPE_LANE - MLA_ROPE_DIM), F32)], axis=-1).astype(BF16)
    w_uq = jnp.pad(p["mla_w_uq"].reshape(L, MLA_Q_LORA, MLA_HEADS, MLA_QK_DIM),
                   ((0, 0), (0, 0), (0, 0), (0, SLOT - MLA_QK_DIM))).reshape(L, MLA_Q_LORA, MLA_HEADS * SLOT)
    ukv = p["mla_w_ukv"].reshape(L, MLA_KV_LORA, MLA_HEADS, MLA_NOPE_DIM + MLA_V_DIM)
    uk = jnp.pad(ukv[..., :MLA_NOPE_DIM], ((0, 0), (0, 0), (0, 0), (0, SLOT - MLA_NOPE_DIM)))
    w_ukv = jnp.concatenate([uk.reshape(L, MLA_KV_LORA, MLA_HEADS * SLOT),
                             ukv[..., MLA_NOPE_DIM:].reshape(L, MLA_KV_LORA, MLA_HEADS * MLA_V_DIM)], axis=-1)
    w_pool_bd = jnp.einsum("lgcd,gh->lgchd", p["w_pool"], jnp.eye(4, dtype=F32)).reshape(L, POOL_WIDTH, POOL_WIDTH)
    w_r = jnp.concatenate([p["w_group_router"], p["w_expert_router"]], axis=-1)
    r_hi = w_r.astype(BF16)
    r_lo = (w_r - r_hi.astype(F32)).astype(BF16)
    padl = lambda a: jnp.pad(a, ((0, 0), (0, 0), (0, LANES - a.shape[-1])))
    vec = lambda a: a.reshape(L, 1, a.shape[-1])
    return {
        "w_in": w_in_p,
        "w_uq": w_uq.astype(BF16),
        "w_ukv": w_ukv.astype(BF16),
        "w_pool_bd": w_pool_bd.astype(BF16),
        "w_branch": p["w_branch"].astype(BF16),
        "w_out": p["w_out"].astype(BF16),
        "w_r2": jnp.concatenate([padl(r_hi), padl(r_lo)], axis=-1),
        "w_r1": padl(r_hi),
        "w1": p["w_exp_gate"].astype(BF16),
        "w3": p["w_exp_up"].astype(BF16),
        "w2": p["w_exp_down"].astype(BF16),
        "g_pre_mix": vec(p["g_pre_mix"]), "g_post_mix": vec(p["g_post_mix"]),
        "g_pre_ffn": vec(p["g_pre_ffn"]), "g_post_ffn": vec(p["g_post_ffn"]),
        "pool_scale": vec(p["pool_scale"]), "q_norm": vec(p["mla_q_norm"]), "kv_norm": vec(p["mla_kv_norm"]),
    }


def kernel(x_prompt, x_sample, cache_na_k, cache_na_v, cache_mla_ckv, cache_mla_kpe, c, c_ctx, w_mod, b_mod,
           g_pre_mix, g_post_mix, g_pre_ffn, g_post_ffn, w_in, w_pool, pool_scale, na_rpb, mla_q_norm, mla_w_uq,
           mla_kv_norm, mla_w_ukv, w_branch, w_out, w_group_router, w_expert_router, w_exp_gate, w_exp_up,
           w_exp_down):
    batch, seq, _ = x_prompt.shape
    dbatch, dseq, _ = x_sample.shape
    past = cache_na_k.shape[2]
    assert dbatch + 1 <= 8 and dseq % (NA_BLOCK_ROWS * GRID_W) == 0

    w = _prep(dict(w_in=w_in, w_pool=w_pool, pool_scale=pool_scale, mla_q_norm=mla_q_norm, mla_w_uq=mla_w_uq,
                   mla_kv_norm=mla_kv_norm, mla_w_ukv=mla_w_ukv, w_branch=w_branch, w_out=w_out,
                   w_group_router=w_group_router, w_expert_router=w_expert_router, w_exp_gate=w_exp_gate,
                   w_exp_up=w_exp_up, w_exp_down=w_exp_down, g_pre_mix=g_pre_mix, g_post_mix=g_post_mix,
                   g_pre_ffn=g_pre_ffn, g_post_ffn=g_post_ffn))
    rope_tab = _rope_table(dseq)
    bias_tab = _na_bias_table(na_rpb, dseq // GRID_W)
    place = jnp.tile(jnp.pad(jnp.eye(MLA_ROPE_DIM, dtype=F32), ((0, 0), (KPE_LANE, SLOT - KPE_LANE - MLA_ROPE_DIM))),
                     (1, MLA_HEADS)).astype(BF16)

    cvec = jnp.concatenate([c_ctx[None], c, jnp.zeros((8 - 1 - dbatch, D_MODEL), F32)], axis=0)
    mod5 = _modulation(cvec, w_mod, b_mod).reshape(DEPTH, 8, 6, 1, D_MODEL)

    ck = cache_na_k.reshape(dbatch, DEPTH, past, NA_WIDTH)
    cv = cache_na_v.reshape(dbatch, DEPTH, past, NA_WIDTH)

    xp = x_prompt.reshape(batch * seq, D_MODEL)
    assert DEPTH == 2
    caches = None
    for l in range(DEPTH):
        u, nq, nvo, qm, km, vmo, gate, *caches = _inproj(xp, mod5, l, w, seq=seq, rope_tab=None, ctx_out=True,
                                                         tm=seq, prev=caches)
        yna, ymla = _attn_ctx(nq, caches[0], nvo, qm, km, vmo, seq=seq, nk_layer=l if l > 0 else None)
        xp, h2, rg = _merge(xp, u, yna, ymla, gate, mod5, l, w, seq=seq, ctx=True, tm=2 * seq)
        xp = _moe(h2, rg, xp, mod5, l, w, seq=seq, ctx=True)
    new_nk, new_nv, new_ckv, new_kpe = caches

    xs = x_sample.reshape(dbatch * dseq, D_MODEL)
    for l in range(DEPTH):
        u, nq, nvo, qm, km, vmo, gate, nk = _inproj(xs, mod5, l, w, seq=dseq, rope_tab=rope_tab, ctx_out=False, tm=512)
        ckm, cvmo, cnvo = _cache_prep(cache_mla_ckv, cache_mla_kpe, cv, w["w_ukv"], place, l)
        yna = _attn_na(nq, nk, nvo, ck, cnvo, bias_tab, l, batch=dbatch, seq=dseq)
        ymla = _attn_mla(qm, km, vmo, ckm, cvmo, batch=dbatch, seq=dseq, tq=256)
        xs, h2, rg = _merge(xs, u, yna, ymla, gate, mod5, l, w, seq=dseq, ctx=False, tm=512)
        xs = _moe(h2, rg, xs, mod5, l, w, seq=dseq, ctx=False)

    heads = (batch, DEPTH, seq, NA_HEADS, NA_HEAD_DIM)
    return (xp.reshape(batch, seq, D_MODEL), xs.reshape(dbatch, dseq, D_MODEL),
            new_nk.reshape(heads), new_nv.reshape(heads), new_ckv, new_kpe)
```

```python
import functools

import numpy as np
import jax
import jax.numpy as jnp
from jax import lax
from jax.experimental import pallas as pl
from jax.experimental.pallas import tpu as pltpu

F32 = jnp.float32
BF16 = jnp.bfloat16

D_MODEL = 1024
DEPTH = 2
GRID_W = 64
POOL_WINDOWS = (2, 4, 8, 16)
POOL_GC = 128
POOL_WIDTH = 512
NA_HEADS = 8
NA_HEAD_DIM = 64
NA_WIDTH = 512
NA_WIN_ROWS = 8
NA_WIN_COLS = 16
MLA_HEADS = 8
MLA_NOPE_DIM = 64
MLA_ROPE_DIM = 32
MLA_V_DIM = 64
MLA_QK_DIM = 96
MLA_Q_LORA = 384
MLA_KV_LORA = 256
N_GROUPS = 4
EXPERTS_PER_GROUP = 4
N_EXPERTS = 16
EXPERT_FF = 256
ROPE_BASE = 10000.0
EPS = 1e-6
NEG_INF = -1e30

LANES = 128
SLOT = LANES
VMEM_LIMIT = 56 * 1024 * 1024

C_U, C_NQ, C_NK, C_NV, C_CQ, C_CKV, C_G, C_KPE, C_END = 0, 512, 1024, 1536, 2048, 2432, 2688, 5760, 5888
KPE_LANE = MLA_NOPE_DIM

LOG2E = 1.4426950408889634
VO_W = 2 * SLOT
VO_ALL = (NA_HEADS // 2) * VO_W

NA_BLOCK_ROWS = 4
NA_SLAB_ROWS = 12


def _cparams(sem):
    return pltpu.CompilerParams(dimension_semantics=sem, vmem_limit_bytes=VMEM_LIMIT)


def _rms(x, g):
    return x * lax.rsqrt(jnp.mean(x * x, axis=-1, keepdims=True) + EPS) * g


def _dot(a, b):
    return jnp.dot(a, b, preferred_element_type=F32)


def _dot_nt(a, b):
    return lax.dot_general(a, b, (((1,), (1,)), ((), ())), preferred_element_type=F32)


def _mod_kernel(c_ref, w_ref, b_ref, o_ref):
    c = c_ref[...]
    s = (c * jax.nn.sigmoid(c)).astype(BF16)
    o_ref[...] = _dot(s, w_ref[...].astype(BF16)) + b_ref[...]


def _modulation(cvec, w_mod, b_mod):
    tn = 1536
    n = w_mod.shape[-1]
    return pl.pallas_call(
        _mod_kernel,
        grid=(DEPTH, n // tn),
        in_specs=[
            pl.BlockSpec((8, D_MODEL), lambda l, j: (0, 0)),
            pl.BlockSpec((None, D_MODEL, tn), lambda l, j: (l, 0, j)),
            pl.BlockSpec((None, 1, tn), lambda l, j: (l, 0, j)),
        ],
        out_specs=pl.BlockSpec((None, 8, tn), lambda l, j: (l, 0, j)),
        out_shape=jax.ShapeDtypeStruct((DEPTH, 8, n), F32),
        compiler_params=_cparams(("arbitrary", "arbitrary")),
        name="modulation",
    )(cvec, w_mod, b_mod.reshape(DEPTH, 1, n))


def _rope_slot(x, tab):
    c, s_up, s_dn = tab[:, 0:SLOT], tab[:, SLOT:2 * SLOT], tab[:, 2 * SLOT:3 * SLOT]
    half = MLA_ROPE_DIM // 4
    x_up = pltpu.roll(x, SLOT - half, axis=1)
    x_dn = pltpu.roll(x, half, axis=1)
    return x * c + x_up * s_up + x_dn * s_dn


def _store_vo(vo_ref, v):
    tm = v.shape[0]
    for p in range(NA_HEADS // 2):
        vo_ref[:, p * VO_W:p * VO_W + SLOT] = v[:, p * SLOT:(p + 1) * SLOT].astype(BF16)
        vo_ref[:, p * VO_W + SLOT:(p + 1) * VO_W] = jnp.ones((tm, SLOT), BF16)


def _inproj_kernel(*refs, rope, ctx_out, stacked):
    n_in = 9 + (1 if rope else 0) + (4 if stacked else 0)
    ins, outs = refs[:n_in], refs[n_in:]
    x_ref, sh_ref, sc_ref, g_ref, win_ref, qn_ref, wuq_ref, kvn_ref, wukv_ref = ins[:9]
    tab_ref = ins[9] if rope else None
    prev = ins[n_in - 4:] if stacked else None
    if ctx_out:
        u_ref, nq_ref, nvo_ref, qm_ref, km_ref, vmo_ref, gate_ref, nk_ref, nv_ref, ckv_ref, kpe_ref = outs
    else:
        u_ref, nq_ref, nvo_ref, qm_ref, km_ref, vmo_ref, gate_ref, nk_ref = outs

    def put(ref, k, val):
        if stacked:
            ref[0] = prev[k][...]
            ref[1] = val
        else:
            ref[...] = val

    x = x_ref[...]
    h = (_rms(x, g_ref[...]) * (1.0 + sc_ref[...]) + sh_ref[...]).astype(BF16)

    def seg(a, b):
        return _dot(h, win_ref[:, a:b])

    u_ref[...] = seg(C_U, C_NQ).astype(u_ref.dtype)
    nq_ref[...] = (seg(C_NQ, C_NK) * (NA_HEAD_DIM ** -0.5 * LOG2E)).astype(BF16)
    nk = seg(C_NK, C_NV)
    nv = seg(C_NV, C_CQ)
    if ctx_out:
        put(nk_ref, 0, nk)
        put(nv_ref, 1, nv)
    else:
        nk_ref[...] = nk.astype(BF16)
    _store_vo(nvo_ref, nv)
    gate_ref[...] = jax.nn.sigmoid(seg(C_G, C_KPE)).astype(BF16)

    cq = _rms(seg(C_CQ, C_CKV), qn_ref[...]).astype(BF16)
    q = _dot(cq, wuq_ref[...])
    tab = tab_ref[...] if rope else None
    for hd in range(MLA_HEADS):
        qs = q[:, hd * SLOT:(hd + 1) * SLOT]
        if rope:
            qs = _rope_slot(qs, tab)
        qm_ref[:, hd * SLOT:(hd + 1) * SLOT] = (qs * (MLA_QK_DIM ** -0.5 * LOG2E)).astype(BF16)

    ckv = _rms(seg(C_CKV, C_G), kvn_ref[...])
    kpe_slot = seg(C_KPE, C_END)
    if ctx_out:
        put(ckv_ref, 2, ckv)
        put(kpe_ref, 3, kpe_slot[:, KPE_LANE:KPE_LANE + MLA_ROPE_DIM])
    if rope:
        kpe_slot = _rope_slot(kpe_slot, tab)
    kv = _dot(ckv.astype(BF16), wukv_ref[...])
    for hd in range(MLA_HEADS):
        km_ref[:, hd * SLOT:(hd + 1) * SLOT] = (kv[:, hd * SLOT:(hd + 1) * SLOT] + kpe_slot).astype(BF16)
    _store_vo(vmo_ref, kv[:, MLA_HEADS * SLOT:])


def _inproj(x, mod5, l, w, *, seq, rope_tab, ctx_out, tm, prev=None):
    n = x.shape[0]
    steps_per_seq = seq // tm if not ctx_out else 0
    stacked = prev is not None
    assert not stacked or (ctx_out and tm == seq and l == 1)

    def mrow(i):
        return 0 if ctx_out else 1 + i // steps_per_seq

    const = lambda i: (l, 0, 0)
    in_specs = [
        pl.BlockSpec((tm, D_MODEL), lambda i: (i, 0)),
        pl.BlockSpec((None, None, None, 1, D_MODEL), lambda i: (l, mrow(i), 0, 0, 0)),
        pl.BlockSpec((None, None, None, 1, D_MODEL), lambda i: (l, mrow(i), 1, 0, 0)),
        pl.BlockSpec((None, 1, D_MODEL), const),
        pl.BlockSpec((None, D_MODEL, C_END), const, pipeline_mode=pl.Buffered(1)),
        pl.BlockSpec((None, 1, MLA_Q_LORA), const),
        pl.BlockSpec((None, MLA_Q_LORA, MLA_HEADS * SLOT), const, pipeline_mode=pl.Buffered(1)),
        pl.BlockSpec((None, 1, MLA_KV_LORA), const),
        pl.BlockSpec((None, MLA_KV_LORA, MLA_HEADS * SLOT + 512), const, pipeline_mode=pl.Buffered(1)),
    ]
    args = [x, mod5, mod5, w["g_pre_mix"], w["w_in"], w["q_norm"], w["w_uq"], w["kv_norm"], w["w_ukv"]]
    rope = rope_tab is not None
    if rope:
        in_specs.append(pl.BlockSpec((tm, 3 * SLOT), lambda i: (i % steps_per_seq, 0)))
        args.append(rope_tab)
    row = lambda wd: pl.BlockSpec((tm, wd), lambda i: (i, 0))
    rows_of = lambda wd, dt: jax.ShapeDtypeStruct((n, wd), dt)
    widths = [512, 512, VO_ALL, MLA_HEADS * SLOT, MLA_HEADS * SLOT, VO_ALL, 3 * D_MODEL]
    out_specs = [row(wd) for wd in widths]
    out_shape = [rows_of(wd, BF16) for wd in widths]
    cache_w = [512, 512, MLA_KV_LORA, MLA_ROPE_DIM] if ctx_out else []
    if not ctx_out:
        out_specs.append(row(512))
        out_shape.append(rows_of(512, BF16))
    elif stacked:
        in_specs += [row(wd) for wd in cache_w]
        args += list(prev)
        out_specs += [pl.BlockSpec((None, DEPTH, seq, wd), lambda i: (i, 0, 0, 0)) for wd in cache_w]
        out_shape += [jax.ShapeDtypeStruct((n // seq, DEPTH, seq, wd), F32) for wd in cache_w]
    else:
        out_specs += [row(wd) for wd in cache_w]
        out_shape += [rows_of(wd, F32) for wd in cache_w]
    return pl.pallas_call(
        functools.partial(_inproj_kernel, rope=rope, ctx_out=ctx_out, stacked=stacked),
        grid=(n // tm,),
        in_specs=in_specs,
        out_specs=out_specs,
        out_shape=out_shape,
        compiler_params=_cparams(("arbitrary",)),
        name="inproj_ctx" if ctx_out else "inproj_lat",
    )(*args)


def _attend(q_h, segs):
    scores = []
    for k, _, bias in segs:
        s = _dot_nt(q_h, k)
        if bias is not None:
            s = s + bias
        scores.append(s)
    m = scores[0].max(axis=-1, keepdims=True)
    for s in scores[1:]:
        m = jnp.maximum(m, s.max(axis=-1, keepdims=True))
    acc = None
    for s, (_, vo, _) in zip(scores, segs):
        o = _dot(jnp.exp2(s - m).astype(BF16), vo)
        acc = o if acc is None else acc + o
    return acc[:, :SLOT] / acc[:, SLOT:]


def _attn_heads(out_ref, q_of, segs_of):
    lo = lax.broadcasted_iota(jnp.int32, (1, SLOT), 1) < NA_HEAD_DIM
    for p in range(NA_HEADS // 2):
        o0, o1 = [_attend(q_of(2 * p + e), segs_of(2 * p + e)) for e in range(2)]
        out_ref[:, p * SLOT:(p + 1) * SLOT] = jnp.where(lo, o0, o1).astype(out_ref.dtype)


def _pair_q(q_ref):
    lo = lax.broadcasted_iota(jnp.int32, (1, SLOT), 1) < NA_HEAD_DIM

    def q_of(h):
        q_pair = q_ref[:, (h // 2) * SLOT:(h // 2 + 1) * SLOT]
        keep = lo if h % 2 == 0 else jnp.logical_not(lo)
        return jnp.where(keep, q_pair, jnp.zeros_like(q_pair))
    return q_of


def _slot(ref, i, rows=slice(None)):
    return ref[rows, i * SLOT:(i + 1) * SLOT].astype(BF16)


def _vo(ref, h, rows=slice(None)):
    return ref[rows, (h // 2) * VO_W:(h // 2 + 1) * VO_W]


def _attn_ctx_kernel(nq_ref, nk_ref, nvo_ref, qm_ref, km_ref, vmo_ref, yna_ref, ymla_ref):
    _attn_heads(yna_ref, _pair_q(nq_ref), lambda h: [(_slot(nk_ref, h // 2), _vo(nvo_ref, h), None)])
    _attn_heads(ymla_ref, lambda h: _slot(qm_ref, h), lambda h: [(_slot(km_ref, h), _vo(vmo_ref, h), None)])


def _attn_ctx(nq, nk, nvo, qm, km, vmo, *, seq, nk_layer=None):
    n = nq.shape[0]
    spec = lambda wd: pl.BlockSpec((seq, wd), lambda b: (b, 0))
    nk_spec = spec(512) if nk_layer is None else pl.BlockSpec((None, None, seq, 512), lambda b: (b, nk_layer, 0, 0))
    return pl.pallas_call(
        _attn_ctx_kernel,
        grid=(n // seq,),
        in_specs=[spec(512), nk_spec, spec(VO_ALL), spec(1024), spec(1024), spec(VO_ALL)],
        out_specs=[spec(512), spec(512)],
        out_shape=[jax.ShapeDtypeStruct((n, 512), BF16)] * 2,
        compiler_params=_cparams(("arbitrary",)),
        name="attn_ctx",
    )(nq, nk, nvo, qm, km, vmo)


def _attn_na_kernel(nq_ref, nk_ref, nvo_ref, ck_ref, cvo_ref, bias_ref, out_ref, *, n_blocks):
    i = pl.program_id(1)
    ks = jnp.clip(NA_BLOCK_ROWS * i - NA_WIN_ROWS // 2, 0, n_blocks * NA_BLOCK_ROWS - NA_SLAB_ROWS)
    slab = pl.ds(pl.multiple_of(ks * GRID_W, GRID_W), NA_SLAB_ROWS * GRID_W)

    def segs_of(h):
        return [(_slot(nk_ref, h // 2, slab), _vo(nvo_ref, h, slab), bias_ref[h].astype(F32)),
                (_slot(ck_ref, h // 2), _vo(cvo_ref, h), None)]

    _attn_heads(out_ref, _pair_q(nq_ref), segs_of)


def _attn_na(nq, nk, nvo, cache_k, cvo, bias_tab, l, *, batch, seq):
    tq = NA_BLOCK_ROWS * GRID_W
    nb = seq // tq
    past = cache_k.shape[2]

    def cls(i):
        return jnp.where(i == 0, 0, jnp.where(i == nb - 1, 2, 1))

    return pl.pallas_call(
        functools.partial(_attn_na_kernel, n_blocks=nb),
        grid=(batch, nb),
        in_specs=[
            pl.BlockSpec((tq, 512), lambda b, i: (b * nb + i, 0)),
            pl.BlockSpec((seq, 512), lambda b, i: (b, 0)),
            pl.BlockSpec((seq, VO_ALL), lambda b, i: (b, 0)),
            pl.BlockSpec((None, None, past, 512), lambda b, i: (b, l, 0, 0)),
            pl.BlockSpec((None, past, VO_ALL), lambda b, i: (b, 0, 0)),
            pl.BlockSpec((None, None, NA_HEADS, tq, NA_SLAB_ROWS * GRID_W), lambda b, i: (l, cls(i), 0, 0, 0)),
        ],
        out_specs=pl.BlockSpec((tq, 512), lambda b, i: (b * nb + i, 0)),
        out_shape=jax.ShapeDtypeStruct((batch * seq, 512), BF16),
        compiler_params=_cparams(("arbitrary", "arbitrary")),
        name="attn_na_lat",
    )(nq, nk, nvo, cache_k, cvo, bias_tab)


def _cache_prep_kernel(ckv_ref, kpe_ref, nv_ref, wukv_ref, place_ref, km_ref, vmo_ref, nvo_ref):
    kv = _dot(ckv_ref[...].astype(BF16), wukv_ref[...])
    kslots = kv[:, :MLA_HEADS * SLOT] + _dot(kpe_ref[...].astype(BF16), place_ref[...])
    km_ref[...] = kslots.astype(BF16)
    _store_vo(vmo_ref, kv[:, MLA_HEADS * SLOT:])
    _store_vo(nvo_ref, nv_ref[...])


def _cache_prep(cache_ckv, cache_kpe, cache_nv, w_ukv, place, l):
    batch, _, past, _ = cache_ckv.shape
    tout = pl.BlockSpec((None, past, VO_ALL), lambda b: (b, 0, 0))
    tshape = jax.ShapeDtypeStruct((batch, past, VO_ALL), BF16)
    return pl.pallas_call(
        _cache_prep_kernel,
        grid=(batch,),
        in_specs=[
            pl.BlockSpec((None, None, past, MLA_KV_LORA), lambda b: (b, l, 0, 0)),
            pl.BlockSpec((None, None, past, MLA_ROPE_DIM), lambda b: (b, l, 0, 0)),
            pl.BlockSpec((None, None, past, 512), lambda b: (b, l, 0, 0)),
            pl.BlockSpec((None, MLA_KV_LORA, MLA_HEADS * SLOT + 512), lambda b: (l, 0, 0)),
            pl.BlockSpec((MLA_ROPE_DIM, MLA_HEADS * SLOT), lambda b: (0, 0)),
        ],
        out_specs=[pl.BlockSpec((None, past, MLA_HEADS * SLOT), lambda b: (b, 0, 0)), tout, tout],
        out_shape=[jax.ShapeDtypeStruct((batch, past, MLA_HEADS * SLOT), BF16), tshape, tshape],
        compiler_params=_cparams(("arbitrary",)),
        name="cache_prep",
    )(cache_ckv, cache_kpe, cache_nv, w_ukv, place)


def _attn_mla_kernel(qm_ref, km_ref, vmo_ref, ckm_ref, cvmo_ref, out_ref):
    _attn_heads(out_ref, lambda h: _slot(qm_ref, h),
                lambda h: [(_slot(km_ref, h), _vo(vmo_ref, h), None), (_slot(ckm_ref, h), _vo(cvmo_ref, h), None)])


def _attn_mla(qm, km, vmo, ckm, cvmo, *, batch, seq, tq):
    nb = seq // tq
    past = ckm.shape[1]
    return pl.pallas_call(
        _attn_mla_kernel,
        grid=(batch, nb),
        in_specs=[
            pl.BlockSpec((tq, MLA_HEADS * SLOT), lambda b, i: (b * nb + i, 0)),
            pl.BlockSpec((seq, MLA_HEADS * SLOT), lambda b, i: (b, 0)),
            pl.BlockSpec((seq, VO_ALL), lambda b, i: (b, 0)),
            pl.BlockSpec((None, past, MLA_HEADS * SLOT), lambda b, i: (b, 0, 0)),
            pl.BlockSpec((None, past, VO_ALL), lambda b, i: (b, 0, 0)),
        ],
        out_specs=pl.BlockSpec((tq, 512), lambda b, i: (b * nb + i, 0)),
        out_shape=jax.ShapeDtypeStruct((batch * seq, 512), BF16),
        compiler_params=_cparams(("arbitrary", "arbitrary")),
        name="attn_mla_lat",
    )(qm, km, vmo, ckm, cvmo)


POOL_HALO = 16


GROUP_LANE = N_EXPERTS


def _route(logits):
    lane = lax.broadcasted_iota(jnp.int32, logits.shape, 1).astype(F32)
    big = jnp.float32(1 << 20)
    is_g = lane < N_GROUPS
    gl = jnp.where(is_g, logits, NEG_INF)
    gmax = gl.max(axis=-1, keepdims=True)
    ge = jnp.where(is_g, jnp.exp(gl - gmax), 0.0)
    gp = ge / ge.sum(axis=-1, keepdims=True)
    g_w = jnp.where(is_g, gp, -1.0).max(axis=-1, keepdims=True)
    g_idx = jnp.where(is_g & (gp == g_w), lane, big).min(axis=-1, keepdims=True)
    e_lane = lane - N_GROUPS
    in_grp = (e_lane >= g_idx * EXPERTS_PER_GROUP) & (e_lane < (g_idx + 1) * EXPERTS_PER_GROUP)
    el = jnp.where(in_grp, logits, NEG_INF)
    emax = el.max(axis=-1, keepdims=True)
    ee = jnp.where(in_grp, jnp.exp(el - emax), 0.0)
    ep = ee / ee.sum(axis=-1, keepdims=True)
    p1 = jnp.where(in_grp, ep, -1.0).max(axis=-1, keepdims=True)
    i1 = jnp.where(in_grp & (ep == p1), lane, big).min(axis=-1, keepdims=True)
    rest = in_grp & (lane != i1)
    p2 = jnp.where(rest, ep, -1.0).max(axis=-1, keepdims=True)
    i2 = jnp.where(rest & (ep == p2), lane, big).min(axis=-1, keepdims=True)
    tot = p1 + p2
    gates = jnp.where(lane == i1, g_w * (p1 / tot), 0.0) + jnp.where(lane == i2, g_w * (p2 / tot), 0.0)
    return jnp.where(lane == GROUP_LANE, g_idx, pltpu.roll(gates, LANES - N_GROUPS, axis=1))


def _merge_kernel(x_ref, u_ref, yna_ref, ymla_ref, gate_ref, gt1_ref, sh2_ref, sc2_ref, gpost_ref, gpre_ref,
                  wpool_ref, pscale_ref, wbr_ref, wout_ref, wr2_ref, wr1_ref,
                  xo_ref, h2_ref, rg_ref, *, seq, tm, steps_per_seq):
    def pool_diff(uwin, utile, r0, s0):
        rows, nwin = utile.shape[0], uwin.shape[0]
        t = r0 + lax.broadcasted_iota(jnp.int32, (rows, nwin), 0)
        s = s0 + lax.broadcasted_iota(jnp.int32, (rows, nwin), 1)
        t1 = r0 + lax.broadcasted_iota(jnp.int32, (rows, 1), 0)
        ds = []
        for g, w in enumerate(POOL_WINDOWS):
            lo = jnp.maximum(t - w // 2, 0)
            hi = jnp.minimum(t + (w - w // 2), seq)
            band = ((s >= lo) & (s < hi)).astype(BF16)
            cnt = (jnp.minimum(t1 + (w - w // 2), seq) - jnp.maximum(t1 - w // 2, 0)).astype(F32)
            sl = slice(g * POOL_GC, (g + 1) * POOL_GC)
            ds.append(_dot(band, uwin[:, sl]) / cnt - utile[:, sl].astype(F32))
        return jnp.concatenate(ds, axis=-1).astype(BF16)

    if steps_per_seq == 0:
        d = jnp.concatenate([pool_diff(u_ref[si], u_ref[si], 0, 0) for si in range(tm // seq)], axis=0)
    else:
        nwin = min(tm + 2 * POOL_HALO, seq)
        r0 = (pl.program_id(0) % steps_per_seq) * tm
        s0 = pl.multiple_of(jnp.clip(r0 - POOL_HALO, 0, seq - nwin), POOL_HALO)
        d = pool_diff(u_ref[0, pl.ds(s0, nwin), :], u_ref[0, pl.ds(pl.multiple_of(r0, POOL_HALO), tm), :], r0, s0)
    y_pool = (_dot(d, wpool_ref[...]) * pscale_ref[...]).astype(BF16)

    g = gate_ref[...]
    merged = (g[:, 0:D_MODEL].astype(F32) * _dot(y_pool, wbr_ref[0])
              + g[:, D_MODEL:2 * D_MODEL].astype(F32) * _dot(yna_ref[...], wbr_ref[1])
              + g[:, 2 * D_MODEL:].astype(F32) * _dot(ymla_ref[...], wbr_ref[2]))
    y = _dot(merged.astype(BF16), wout_ref[...])
    xn = x_ref[...] + gt1_ref[...] * _rms(y, gpost_ref[...])
    xo_ref[...] = xn

    h2 = _rms(xn, gpre_ref[...]) * (1.0 + sc2_ref[...]) + sh2_ref[...]
    h_hi = h2.astype(BF16)
    h2_ref[...] = h2.T.astype(BF16)
    h_lo = (h2 - h_hi.astype(F32)).astype(BF16)
    two = _dot(h_hi, wr2_ref[...])
    logits = two[:, :LANES] + two[:, LANES:] + _dot(h_lo, wr1_ref[...])
    rg_ref[...] = _route(logits).T


def _merge(x, u, yna, ymla, gate, mod5, l, w, *, seq, ctx, tm):
    n = x.shape[0]
    nseq = n // seq
    if tm >= seq:
        assert ctx and tm % seq == 0
        steps_per_seq = 0
        u_spec = pl.BlockSpec((tm // seq, seq, 512), lambda i: (i, 0, 0))
    else:
        steps_per_seq = seq // tm
        u_spec = pl.BlockSpec((1, seq, 512), lambda i: (i // steps_per_seq, 0, 0))

    def mrow(i):
        return 0 if ctx else 1 + i // steps_per_seq

    const = lambda i: (l, 0, 0)
    modspec = lambda k: pl.BlockSpec((None, None, None, 1, D_MODEL), lambda i: (l, mrow(i), k, 0, 0))
    row = lambda wd: pl.BlockSpec((tm, wd), lambda i: (i, 0))
    return pl.pallas_call(
        functools.partial(_merge_kernel, seq=seq, tm=tm, steps_per_seq=steps_per_seq),
        grid=(n // tm,),
        in_specs=[
            row(D_MODEL),
            u_spec,
            row(512), row(512), row(3 * D_MODEL),
            modspec(2), modspec(3), modspec(4),
            pl.BlockSpec((None, 1, D_MODEL), const),
            pl.BlockSpec((None, 1, D_MODEL), const),
            pl.BlockSpec((None, 512, 512), const),
            pl.BlockSpec((None, 1, 512), const),
            pl.BlockSpec((None, 3, 512, D_MODEL), lambda i: (l, 0, 0, 0)),
            pl.BlockSpec((None, D_MODEL, D_MODEL), const),
            pl.BlockSpec((None, D_MODEL, 2 * LANES), const),
            pl.BlockSpec((None, D_MODEL, LANES), const),
        ],
        out_specs=[row(D_MODEL), pl.BlockSpec((D_MODEL, tm), lambda i: (0, i)),
                   pl.BlockSpec((LANES, tm), lambda i: (0, i))],
        out_shape=[jax.ShapeDtypeStruct((n, D_MODEL), F32), jax.ShapeDtypeStruct((D_MODEL, n), BF16),
                   jax.ShapeDtypeStruct((LANES, n), F32)],
        compiler_params=_cparams(("arbitrary",)),
        name="merge_ctx" if ctx else "merge_lat",
    )(x, u.reshape(nseq, seq, 512), yna, ymla, gate, mod5, mod5, mod5, w["g_post_mix"], w["g_pre_ffn"],
      w["w_pool_bd"], w["pool_scale"], w["w_branch"], w["w_out"], w["w_r2"], w["w_r1"])


MOE_TM = 512
MOE_COLS = 256
FF_PAIR = 2 * EXPERT_FF


def _moe_kernel(ht_ref, rgt_ref, x_ref, gt2_ref, gpost_ref, w13_ref, w2_ref, o_ref, acc_ref):
    tm = ht_ref.shape[1]
    ht = ht_ref[...]
    rgt = rgt_ref[...]
    grp = rgt[GROUP_LANE:GROUP_LANE + 1, :]
    gid = lax.broadcasted_iota(jnp.int32, (8, tm), 0).astype(F32)
    member = (gid == grp).astype(F32)
    t_row = lax.broadcasted_iota(jnp.int32, (tm, tm), 0)
    t_col = lax.broadcasted_iota(jnp.int32, (tm, tm), 1)
    rank = _dot(member.astype(BF16), (t_row < t_col).astype(BF16))
    count = member.sum(axis=1, keepdims=True)
    rg_hi = rgt.astype(BF16)
    rg_lo = (rgt - rg_hi.astype(F32)).astype(BF16)
    slot_id = lax.broadcasted_iota(jnp.int32, (MOE_COLS, 1), 0).astype(F32)
    sub1 = lax.broadcasted_iota(jnp.int32, (8, 1), 0)
    acc_ref[...] = jnp.zeros_like(acc_ref)

    for g in range(N_GROUPS):
        rr, mr = rank[g:g + 1, :], member[g:g + 1, :]
        n_g = jnp.max(jnp.where(sub1 == g, count, 0.0)).astype(jnp.int32)

        def chunk(c, carry, g=g, rr=rr, mr=mr):
            base = (c * MOE_COLS).astype(F32)
            take = ((rr - base == slot_id) & (mr > 0.0)).astype(BF16)
            hs = _dot_nt(ht, take).astype(BF16)
            gs = _dot_nt(rg_hi, take) + _dot_nt(rg_lo, take)
            ab = _dot(w13_ref[g], hs)
            hid = []
            for e in range(EXPERTS_PER_GROUP):
                k = g * EXPERTS_PER_GROUP + e
                a = ab[e * FF_PAIR:e * FF_PAIR + EXPERT_FF]
                b = ab[e * FF_PAIR + EXPERT_FF:(e + 1) * FF_PAIR]
                hid.append(((a * jax.nn.sigmoid(a)) * b * gs[k:k + 1, :]).astype(BF16))
            y = _dot(w2_ref[g], jnp.concatenate(hid, axis=0))
            acc_ref[...] += _dot(y.astype(BF16), take)
            return carry

        chunk(jnp.int32(0), 0)
        lax.fori_loop(1, (n_g + MOE_COLS - 1) // MOE_COLS, chunk, 0)

    o_ref[...] = x_ref[...] + gt2_ref[...] * _rms(acc_ref[...].T, gpost_ref[...])


def _moe(h2t, rgt, x, mod5, l, w, *, seq, ctx):
    n = x.shape[0]
    tm = MOE_TM
    steps_per_seq = seq // tm if not ctx else 0

    def mrow(i):
        return 0 if ctx else 1 + i // steps_per_seq

    row = lambda wd: pl.BlockSpec((tm, wd), lambda i: (i, 0))
    col = lambda ht: pl.BlockSpec((ht, tm), lambda i: (0, i))
    once = pl.Buffered(1)
    wspec = lambda a, b: pl.BlockSpec((None, N_GROUPS, a, b), lambda i: (l, 0, 0, 0), pipeline_mode=once)
    return pl.pallas_call(
        _moe_kernel,
        grid=(n // tm,),
        in_specs=[
            col(D_MODEL), col(LANES), row(D_MODEL),
            pl.BlockSpec((None, None, None, 1, D_MODEL), lambda i: (l, mrow(i), 5, 0, 0)),
            pl.BlockSpec((None, 1, D_MODEL), lambda i: (l, 0, 0)),
            wspec(EXPERTS_PER_GROUP * FF_PAIR, D_MODEL), wspec(D_MODEL, EXPERTS_PER_GROUP * EXPERT_FF),
        ],
        out_specs=row(D_MODEL),
        out_shape=jax.ShapeDtypeStruct((n, D_MODEL), F32),
        scratch_shapes=[pltpu.VMEM((D_MODEL, tm), F32)],
        compiler_params=_cparams(("arbitrary",)),
        name="moe_ctx" if ctx else "moe_lat",
    )(h2t, rgt, x, mod5, w["g_post_ffn"], w["w13t"], w["w2t"])


def _rope_table(seq):
    t = np.arange(seq)
    n_freq = MLA_ROPE_DIM // 4
    inv = jnp.asarray(ROPE_BASE, F32) ** (-jnp.arange(n_freq, dtype=F32) / n_freq)
    ang_r = jnp.asarray(t // GRID_W, F32)[:, None] * inv
    ang_c = jnp.asarray(t % GRID_W, F32)[:, None] * inv
    cos = jnp.cos(jnp.stack([ang_r, ang_c], axis=1))
    sin = jnp.sin(jnp.stack([ang_r, ang_c], axis=1))
    zeros = jnp.zeros_like(sin)
    c32 = jnp.stack([cos, cos], axis=2).reshape(seq, MLA_ROPE_DIM)
    up32 = jnp.stack([-sin, zeros], axis=2).reshape(seq, MLA_ROPE_DIM)
    dn32 = jnp.stack([zeros, sin], axis=2).reshape(seq, MLA_ROPE_DIM)

    def slot(v, fill):
        return jnp.concatenate([jnp.full((seq, KPE_LANE), fill, F32), v,
                                jnp.full((seq, SLOT - KPE_LANE - MLA_ROPE_DIM), fill, F32)], axis=1)

    return jnp.concatenate([slot(c32, 1.0), slot(up32, 0.0), slot(dn32, 0.0)], axis=1)


def _na_row_classes(n_rows):
    i = np.arange(NA_BLOCK_ROWS)[:, None]
    j = np.arange(NA_SLAB_ROWS)[None, :]
    last_ks = n_rows - NA_SLAB_ROWS
    last_q0 = n_rows - NA_BLOCK_ROWS
    specs = [
        (j < NA_WIN_ROWS, j - i),
        ((j >= i) & (j < i + NA_WIN_ROWS), j - NA_WIN_ROWS // 2 - i),
        (last_ks + j >= n_rows - NA_WIN_ROWS, last_ks + j - last_q0 - i),
    ]
    ok = np.stack([np.broadcast_to(o, (NA_BLOCK_ROWS, NA_SLAB_ROWS)) for o, _ in specs])
    dr = np.stack([np.clip(d + NA_WIN_ROWS - 1, 0, 2 * NA_WIN_ROWS - 2) for _, d in specs])
    return ok, dr


def _bias_kernel(rpb_ref, out_ref, *, row_ok, row_dr):
    reach = NA_WIN_COLS - 1
    rows = rpb_ref[...] * LOG2E
    v_lo = pltpu.roll(rows, SLOT - reach, axis=1)
    v_hi = pltpu.roll(rows, GRID_W - reach, axis=1)
    lane1 = lax.broadcasted_iota(jnp.int32, (1, SLOT), 1)
    mid = (lane1 >= GRID_W - reach) & (lane1 <= GRID_W + reach)
    c = lax.broadcasted_iota(jnp.int32, (GRID_W, SLOT), 0)
    lane = lax.broadcasted_iota(jnp.int32, (GRID_W, SLOT), 1)
    kc = lane & (GRID_W - 1)
    c0 = jnp.clip(c - NA_WIN_COLS // 2, 0, GRID_W - NA_WIN_COLS)
    in_col = (kc >= c0) & (kc < c0 + NA_WIN_COLS)
    first = lane < GRID_W
    for k in range(3):
        for i in range(NA_BLOCK_ROWS):
            for jp in range(NA_SLAB_ROWS // 2):
                ok_a, ok_b = bool(row_ok[k, i, 2 * jp]), bool(row_ok[k, i, 2 * jp + 1])
                a, b = int(row_dr[k, i, 2 * jp]), int(row_dr[k, i, 2 * jp + 1])
                rs = slice(i * GRID_W, (i + 1) * GRID_W)
                cs = slice(jp * SLOT, (jp + 1) * SLOT)
                if not (ok_a or ok_b):
                    out_ref[k, rs, cs] = jnp.full((GRID_W, SLOT), NEG_INF, out_ref.dtype)
                    continue
                v = jnp.where(mid, v_hi[b:b + 1], v_lo[a:a + 1])
                tile = pltpu.roll(jnp.broadcast_to(v, (GRID_W, SLOT)), 0, axis=1, stride=1, stride_axis=0)
                ok = in_col
                if not ok_b:
                    ok = ok & first
                if not ok_a:
                    ok = ok & jnp.logical_not(first)
                out_ref[k, rs, cs] = jnp.where(ok, tile, NEG_INF).astype(out_ref.dtype)


def _na_bias_table(rpb, n_rows):
    n_dr, n_dc = rpb.shape[-2:]
    row_ok, row_dr = _na_row_classes(n_rows)
    rpb_p = jnp.pad(rpb, ((0, 0), (0, 0), (0, 0), (0, SLOT - n_dc)))
    tq, tk = NA_BLOCK_ROWS * GRID_W, NA_SLAB_ROWS * GRID_W
    return pl.pallas_call(
        functools.partial(_bias_kernel, row_ok=row_ok, row_dr=row_dr),
        grid=(DEPTH, NA_HEADS),
        in_specs=[pl.BlockSpec((None, None, n_dr, SLOT), lambda l, h: (l, h, 0, 0))],
        out_specs=pl.BlockSpec((None, 3, None, tq, tk), lambda l, h: (l, 0, h, 0, 0)),
        out_shape=jax.ShapeDtypeStruct((DEPTH, 3, NA_HEADS, tq, tk), BF16),
        compiler_params=_cparams(("arbitrary", "arbitrary")),
        name="na_bias_table",
    )(rpb_p)


def _prep(p):
    L = DEPTH
    w_in = p["w_in"]
    kpe0 = C_G
    kpe1 = C_G + MLA_ROPE_DIM
    w_in_p = jnp.concatenate(
        [w_in[..., :kpe0], w_in[..., kpe1:], jnp.zeros((L, D_MODEL, KPE_LANE), F32), w_in[..., kpe0:kpe1],
         jnp.zeros((L, D_MODEL, SLOT - KPE_LANE - MLA_ROPE_DIM), F32)], axis=-1).astype(BF16)
    w_uq = jnp.pad(p["mla_w_uq"].reshape(L, MLA_Q_LORA, MLA_HEADS, MLA_QK_DIM),
                   ((0, 0), (0, 0), (0, 0), (0, SLOT - MLA_QK_DIM))).reshape(L, MLA_Q_LORA, MLA_HEADS * SLOT)
    ukv = p["mla_w_ukv"].reshape(L, MLA_KV_LORA, MLA_HEADS, MLA_NOPE_DIM + MLA_V_DIM)
    uk = jnp.pad(ukv[..., :MLA_NOPE_DIM], ((0, 0), (0, 0), (0, 0), (0, SLOT - MLA_NOPE_DIM)))
    w_ukv = jnp.concatenate([uk.reshape(L, MLA_KV_LORA, MLA_HEADS * SLOT),
                             ukv[..., MLA_NOPE_DIM:].reshape(L, MLA_KV_LORA, MLA_HEADS * MLA_V_DIM)], axis=-1)
    w_pool_bd = jnp.einsum("lgcd,gh->lgchd", p["w_pool"], jnp.eye(4, dtype=F32)).reshape(L, POOL_WIDTH, POOL_WIDTH)
    w_r = jnp.concatenate([p["w_group_router"], p["w_expert_router"]], axis=-1)
    r_hi = w_r.astype(BF16)
    r_lo = (w_r - r_hi.astype(F32)).astype(BF16)
    padl = lambda a: jnp.pad(a, ((0, 0), (0, 0), (0, LANES - a.shape[-1])))
    vec = lambda a: a.reshape(L, 1, a.shape[-1])
    grouped = lambda a: a.reshape((L, N_GROUPS, EXPERTS_PER_GROUP) + a.shape[2:])
    w13t = jnp.stack([jnp.swapaxes(grouped(p["w_exp_gate"]), -1, -2), jnp.swapaxes(grouped(p["w_exp_up"]), -1, -2)],
                     axis=3).reshape(L, N_GROUPS, EXPERTS_PER_GROUP * FF_PAIR, D_MODEL)
    w2t = jnp.transpose(grouped(p["w_exp_down"]), (0, 1, 4, 2, 3)).reshape(L, N_GROUPS, D_MODEL, EXPERTS_PER_GROUP * EXPERT_FF)
    return {
        "w_in": w_in_p,
        "w_uq": w_uq.astype(BF16),
        "w_ukv": w_ukv.astype(BF16),
        "w_pool_bd": w_pool_bd.astype(BF16),
        "w_branch": p["w_branch"].astype(BF16),
        "w_out": p["w_out"].astype(BF16),
        "w_r2": jnp.concatenate([padl(r_hi), padl(r_lo)], axis=-1),
        "w_r1": padl(r_hi),
        "w13t": w13t.astype(BF16),
        "w2t": w2t.astype(BF16),
        "g_pre_mix": vec(p["g_pre_mix"]), "g_post_mix": vec(p["g_post_mix"]),
        "g_pre_ffn": vec(p["g_pre_ffn"]), "g_post_ffn": vec(p["g_post_ffn"]),
        "pool_scale": vec(p["pool_scale"]), "q_norm": vec(p["mla_q_norm"]), "kv_norm": vec(p["mla_kv_norm"]),
    }


def kernel(x_prompt, x_sample, cache_na_k, cache_na_v, cache_mla_ckv, cache_mla_kpe, c, c_ctx, w_mod, b_mod,
           g_pre_mix, g_post_mix, g_pre_ffn, g_post_ffn, w_in, w_pool, pool_scale, na_rpb, mla_q_norm, mla_w_uq,
           mla_kv_norm, mla_w_ukv, w_branch, w_out, w_group_router, w_expert_router, w_exp_gate, w_exp_up,
           w_exp_down):
    batch, seq, _ = x_prompt.shape
    dbatch, dseq, _ = x_sample.shape
    past = cache_na_k.shape[2]
    assert dbatch + 1 <= 8 and dseq % (NA_BLOCK_ROWS * GRID_W) == 0

    w = _prep(dict(w_in=w_in, w_pool=w_pool, pool_scale=pool_scale, mla_q_norm=mla_q_norm, mla_w_uq=mla_w_uq,
                   mla_kv_norm=mla_kv_norm, mla_w_ukv=mla_w_ukv, w_branch=w_branch, w_out=w_out,
                   w_group_router=w_group_router, w_expert_router=w_expert_router, w_exp_gate=w_exp_gate,
                   w_exp_up=w_exp_up, w_exp_down=w_exp_down, g_pre_mix=g_pre_mix, g_post_mix=g_post_mix,
                   g_pre_ffn=g_pre_ffn, g_post_ffn=g_post_ffn))
    rope_tab = _rope_table(dseq)
    bias_tab = _na_bias_table(na_rpb, dseq // GRID_W)
    place = jnp.tile(jnp.pad(jnp.eye(MLA_ROPE_DIM, dtype=F32), ((0, 0), (KPE_LANE, SLOT - KPE_LANE - MLA_ROPE_DIM))),
                     (1, MLA_HEADS)).astype(BF16)

    cvec = jnp.concatenate([c_ctx[None], c, jnp.zeros((8 - 1 - dbatch, D_MODEL), F32)], axis=0)
    mod5 = _modulation(cvec, w_mod, b_mod).reshape(DEPTH, 8, 6, 1, D_MODEL)

    ck = cache_na_k.reshape(dbatch, DEPTH, past, NA_WIDTH)
    cv = cache_na_v.reshape(dbatch, DEPTH, past, NA_WIDTH)

    xp = x_prompt.reshape(batch * seq, D_MODEL)
    assert DEPTH == 2
    caches = None
    for l in range(DEPTH):
        u, nq, nvo, qm, km, vmo, gate, *caches = _inproj(xp, mod5, l, w, seq=seq, rope_tab=None, ctx_out=True,
                                                         tm=seq, prev=caches)
        yna, ymla = _attn_ctx(nq, caches[0], nvo, qm, km, vmo, seq=seq, nk_layer=l if l > 0 else None)
        xp, h2, rg = _merge(xp, u, yna, ymla, gate, mod5, l, w, seq=seq, ctx=True, tm=2 * seq)
        xp = _moe(h2, rg, xp, mod5, l, w, seq=seq, ctx=True)
    new_nk, new_nv, new_ckv, new_kpe = caches

    xs = x_sample.reshape(dbatch * dseq, D_MODEL)
    for l in range(DEPTH):
        u, nq, nvo, qm, km, vmo, gate, nk = _inproj(xs, mod5, l, w, seq=dseq, rope_tab=rope_tab, ctx_out=False, tm=512)
        ckm, cvmo, cnvo = _cache_prep(cache_mla_ckv, cache_mla_kpe, cv, w["w_ukv"], place, l)
        yna = _attn_na(nq, nk, nvo, ck, cnvo, bias_tab, l, batch=dbatch, seq=dseq)
        ymla = _attn_mla(qm, km, vmo, ckm, cvmo, batch=dbatch, seq=dseq, tq=256)
        xs, h2, rg = _merge(xs, u, yna, ymla, gate, mod5, l, w, seq=dseq, ctx=False, tm=512)
        xs = _moe(h2, rg, xs, mod5, l, w, seq=dseq, ctx=False)

    heads = (batch, DEPTH, seq, NA_HEADS, NA_HEAD_DIM)
    return (xp.reshape(batch, seq, D_MODEL), xs.reshape(dbatch, dseq, D_MODEL),
            new_nk.reshape(heads), new_nv.reshape(heads), new_ckv, new_kpe)
```

```python
import functools

import numpy as np
import jax
import jax.numpy as jnp
from jax import lax
from jax.experimental import pallas as pl
from jax.experimental.pallas import tpu as pltpu

F32 = jnp.float32
BF16 = jnp.bfloat16

D_MODEL = 1024
DEPTH = 2
GRID_W = 64
POOL_WINDOWS = (2, 4, 8, 16)
POOL_GC = 128
POOL_WIDTH = 512
NA_HEADS = 8
NA_HEAD_DIM = 64
NA_WIDTH = 512
NA_WIN_ROWS = 8
NA_WIN_COLS = 16
MLA_HEADS = 8
MLA_NOPE_DIM = 64
MLA_ROPE_DIM = 32
MLA_V_DIM = 64
MLA_QK_DIM = 96
MLA_Q_LORA = 384
MLA_KV_LORA = 256
N_GROUPS = 4
EXPERTS_PER_GROUP = 4
N_EXPERTS = 16
EXPERT_FF = 256
ROPE_BASE = 10000.0
EPS = 1e-6
NEG_INF = -1e30

LANES = 128
SLOT = LANES
VMEM_LIMIT = 56 * 1024 * 1024

C_U, C_NQ, C_NK, C_NV, C_CQ, C_CKV, C_G, C_KPE, C_END = 0, 512, 1024, 1536, 2048, 2432, 2688, 5760, 5888
KPE_LANE = MLA_NOPE_DIM

LOG2E = 1.4426950408889634
VO_W = 2 * SLOT
VO_ALL = (NA_HEADS // 2) * VO_W

NA_BLOCK_ROWS = 4
NA_SLAB_ROWS = 12


def _cparams(sem):
    return pltpu.CompilerParams(dimension_semantics=sem, vmem_limit_bytes=VMEM_LIMIT)


def _rms(x, g):
    return x * lax.rsqrt(jnp.mean(x * x, axis=-1, keepdims=True) + EPS) * g


def _dot(a, b):
    return jnp.dot(a, b, preferred_element_type=F32)


def _dot_nt(a, b):
    return lax.dot_general(a, b, (((1,), (1,)), ((), ())), preferred_element_type=F32)


def _mod_kernel(c_ref, w_ref, b_ref, o_ref):
    c = c_ref[...]
    s = (c * jax.nn.sigmoid(c)).astype(BF16)
    o_ref[...] = _dot(s, w_ref[...].astype(BF16)) + b_ref[...]


def _modulation(cvec, w_mod, b_mod):
    tn = 1536
    n = w_mod.shape[-1]
    return pl.pallas_call(
        _mod_kernel,
        grid=(DEPTH, n // tn),
        in_specs=[
            pl.BlockSpec((8, D_MODEL), lambda l, j: (0, 0)),
            pl.BlockSpec((None, D_MODEL, tn), lambda l, j: (l, 0, j)),
            pl.BlockSpec((None, 1, tn), lambda l, j: (l, 0, j)),
        ],
        out_specs=pl.BlockSpec((None, 8, tn), lambda l, j: (l, 0, j)),
        out_shape=jax.ShapeDtypeStruct((DEPTH, 8, n), F32),
        compiler_params=_cparams(("arbitrary", "arbitrary")),
        name="modulation",
    )(cvec, w_mod, b_mod.reshape(DEPTH, 1, n))


def _rope_slot(x, tab):
    c, s_up, s_dn = tab[:, 0:SLOT], tab[:, SLOT:2 * SLOT], tab[:, 2 * SLOT:3 * SLOT]
    half = MLA_ROPE_DIM // 4
    x_up = pltpu.roll(x, SLOT - half, axis=1)
    x_dn = pltpu.roll(x, half, axis=1)
    return x * c + x_up * s_up + x_dn * s_dn


def _store_vo(vo_ref, v):
    tm = v.shape[0]
    for p in range(NA_HEADS // 2):
        vo_ref[:, p * VO_W:p * VO_W + SLOT] = v[:, p * SLOT:(p + 1) * SLOT].astype(BF16)
        vo_ref[:, p * VO_W + SLOT:(p + 1) * VO_W] = jnp.ones((tm, SLOT), BF16)


def _inproj_kernel(*refs, rope, ctx_out, stacked):
    n_in = 9 + (1 if rope else 0) + (4 if stacked else 0)
    ins, outs = refs[:n_in], refs[n_in:]
    x_ref, sh_ref, sc_ref, g_ref, win_ref, qn_ref, wuq_ref, kvn_ref, wukv_ref = ins[:9]
    tab_ref = ins[9] if rope else None
    prev = ins[n_in - 4:] if stacked else None
    if ctx_out:
        u_ref, nq_ref, nvo_ref, qm_ref, km_ref, vmo_ref, gate_ref, nk_ref, nv_ref, ckv_ref, kpe_ref = outs
    else:
        u_ref, nq_ref, nvo_ref, qm_ref, km_ref, vmo_ref, gate_ref, nk_ref = outs

    def put(ref, k, val):
        if stacked:
            ref[0] = prev[k][...]
            ref[1] = val
        else:
            ref[...] = val

    x = x_ref[...]
    h = (_rms(x, g_ref[...]) * (1.0 + sc_ref[...]) + sh_ref[...]).astype(BF16)

    def seg(a, b):
        return _dot(h, win_ref[:, a:b])

    u_ref[...] = seg(C_U, C_NQ).astype(u_ref.dtype)
    nq_ref[...] = (seg(C_NQ, C_NK) * (NA_HEAD_DIM ** -0.5 * LOG2E)).astype(BF16)
    nk = seg(C_NK, C_NV)
    nv = seg(C_NV, C_CQ)
    if ctx_out:
        put(nk_ref, 0, nk)
        put(nv_ref, 1, nv)
    else:
        nk_ref[...] = nk.astype(BF16)
    _store_vo(nvo_ref, nv)
    gate_ref[...] = jax.nn.sigmoid(seg(C_G, C_KPE)).astype(BF16)

    cq = _rms(seg(C_CQ, C_CKV), qn_ref[...]).astype(BF16)
    q = _dot(cq, wuq_ref[...])
    tab = tab_ref[...] if rope else None
    for hd in range(MLA_HEADS):
        qs = q[:, hd * SLOT:(hd + 1) * SLOT]
        if rope:
            qs = _rope_slot(qs, tab)
        qm_ref[:, hd * SLOT:(hd + 1) * SLOT] = (qs * (MLA_QK_DIM ** -0.5 * LOG2E)).astype(BF16)

    ckv = _rms(seg(C_CKV, C_G), kvn_ref[...])
    kpe_slot = seg(C_KPE, C_END)
    if ctx_out:
        put(ckv_ref, 2, ckv)
        put(kpe_ref, 3, kpe_slot[:, KPE_LANE:KPE_LANE + MLA_ROPE_DIM])
    if rope:
        kpe_slot = _rope_slot(kpe_slot, tab)
    kv = _dot(ckv.astype(BF16), wukv_ref[...])
    for hd in range(MLA_HEADS):
        km_ref[:, hd * SLOT:(hd + 1) * SLOT] = (kv[:, hd * SLOT:(hd + 1) * SLOT] + kpe_slot).astype(BF16)
    _store_vo(vmo_ref, kv[:, MLA_HEADS * SLOT:])


def _inproj(x, mod5, l, w, *, seq, rope_tab, ctx_out, tm, prev=None):
    n = x.shape[0]
    steps_per_seq = seq // tm if not ctx_out else 0
    stacked = prev is not None
    assert not stacked or (ctx_out and tm == seq and l == 1)

    def mrow(i):
        return 0 if ctx_out else 1 + i // steps_per_seq

    const = lambda i: (l, 0, 0)
    in_specs = [
        pl.BlockSpec((tm, D_MODEL), lambda i: (i, 0)),
        pl.BlockSpec((None, None, None, 1, D_MODEL), lambda i: (l, mrow(i), 0, 0, 0)),
        pl.BlockSpec((None, None, None, 1, D_MODEL), lambda i: (l, mrow(i), 1, 0, 0)),
        pl.BlockSpec((None, 1, D_MODEL), const),
        pl.BlockSpec((None, D_MODEL, C_END), const, pipeline_mode=pl.Buffered(1)),
        pl.BlockSpec((None, 1, MLA_Q_LORA), const),
        pl.BlockSpec((None, MLA_Q_LORA, MLA_HEADS * SLOT), const, pipeline_mode=pl.Buffered(1)),
        pl.BlockSpec((None, 1, MLA_KV_LORA), const),
        pl.BlockSpec((None, MLA_KV_LORA, MLA_HEADS * SLOT + 512), const, pipeline_mode=pl.Buffered(1)),
    ]
    args = [x, mod5, mod5, w["g_pre_mix"], w["w_in"], w["q_norm"], w["w_uq"], w["kv_norm"], w["w_ukv"]]
    rope = rope_tab is not None
    if rope:
        in_specs.append(pl.BlockSpec((tm, 3 * SLOT), lambda i: (i % steps_per_seq, 0)))
        args.append(rope_tab)
    row = lambda wd: pl.BlockSpec((tm, wd), lambda i: (i, 0))
    rows_of = lambda wd, dt: jax.ShapeDtypeStruct((n, wd), dt)
    widths = [512, 512, VO_ALL, MLA_HEADS * SLOT, MLA_HEADS * SLOT, VO_ALL, 3 * D_MODEL]
    out_specs = [row(wd) for wd in widths]
    out_shape = [rows_of(wd, BF16) for wd in widths]
    cache_w = [512, 512, MLA_KV_LORA, MLA_ROPE_DIM] if ctx_out else []
    if not ctx_out:
        out_specs.append(row(512))
        out_shape.append(rows_of(512, BF16))
    elif stacked:
        in_specs += [row(wd) for wd in cache_w]
        args += list(prev)
        out_specs += [pl.BlockSpec((None, DEPTH, seq, wd), lambda i: (i, 0, 0, 0)) for wd in cache_w]
        out_shape += [jax.ShapeDtypeStruct((n // seq, DEPTH, seq, wd), F32) for wd in cache_w]
    else:
        out_specs += [row(wd) for wd in cache_w]
        out_shape += [rows_of(wd, F32) for wd in cache_w]
    return pl.pallas_call(
        functools.partial(_inproj_kernel, rope=rope, ctx_out=ctx_out, stacked=stacked),
        grid=(n // tm,),
        in_specs=in_specs,
        out_specs=out_specs,
        out_shape=out_shape,
        compiler_params=_cparams(("arbitrary",)),
        name="inproj_ctx" if ctx_out else "inproj_lat",
    )(*args)


def _attend(q_h, segs):
    scores = []
    for k, _, bias in segs:
        s = _dot_nt(q_h, k)
        if bias is not None:
            s = s + bias
        scores.append(s)
    m = scores[0].max(axis=-1, keepdims=True)
    for s in scores[1:]:
        m = jnp.maximum(m, s.max(axis=-1, keepdims=True))
    acc = None
    for s, (_, vo, _) in zip(scores, segs):
        o = _dot(jnp.exp2(s - m).astype(BF16), vo)
        acc = o if acc is None else acc + o
    return acc[:, :SLOT] / acc[:, SLOT:]


def _attn_heads(out_ref, q_of, segs_of):
    lo = lax.broadcasted_iota(jnp.int32, (1, SLOT), 1) < NA_HEAD_DIM
    for p in range(NA_HEADS // 2):
        o0, o1 = [_attend(q_of(2 * p + e), segs_of(2 * p + e)) for e in range(2)]
        out_ref[:, p * SLOT:(p + 1) * SLOT] = jnp.where(lo, o0, o1).astype(out_ref.dtype)


def _pair_q(q_ref):
    lo = lax.broadcasted_iota(jnp.int32, (1, SLOT), 1) < NA_HEAD_DIM

    def q_of(h):
        q_pair = q_ref[:, (h // 2) * SLOT:(h // 2 + 1) * SLOT]
        keep = lo if h % 2 == 0 else jnp.logical_not(lo)
        return jnp.where(keep, q_pair, jnp.zeros_like(q_pair))
    return q_of


def _slot(ref, i, rows=slice(None)):
    return ref[rows, i * SLOT:(i + 1) * SLOT].astype(BF16)


def _vo(ref, h, rows=slice(None)):
    return ref[rows, (h // 2) * VO_W:(h // 2 + 1) * VO_W]


def _attn_ctx_kernel(nq_ref, nk_ref, nvo_ref, qm_ref, km_ref, vmo_ref, yna_ref, ymla_ref):
    _attn_heads(yna_ref, _pair_q(nq_ref), lambda h: [(_slot(nk_ref, h // 2), _vo(nvo_ref, h), None)])
    _attn_heads(ymla_ref, lambda h: _slot(qm_ref, h), lambda h: [(_slot(km_ref, h), _vo(vmo_ref, h), None)])


def _attn_ctx(nq, nk, nvo, qm, km, vmo, *, seq, nk_layer=None):
    n = nq.shape[0]
    spec = lambda wd: pl.BlockSpec((seq, wd), lambda b: (b, 0))
    nk_spec = spec(512) if nk_layer is None else pl.BlockSpec((None, None, seq, 512), lambda b: (b, nk_layer, 0, 0))
    return pl.pallas_call(
        _attn_ctx_kernel,
        grid=(n // seq,),
        in_specs=[spec(512), nk_spec, spec(VO_ALL), spec(1024), spec(1024), spec(VO_ALL)],
        out_specs=[spec(512), spec(512)],
        out_shape=[jax.ShapeDtypeStruct((n, 512), BF16)] * 2,
        compiler_params=_cparams(("arbitrary",)),
        name="attn_ctx",
    )(nq, nk, nvo, qm, km, vmo)


def _attn_na_kernel(nq_ref, nk_ref, nvo_ref, ck_ref, cvo_ref, bias_ref, out_ref, *, n_blocks):
    i = pl.program_id(1)
    ks = jnp.clip(NA_BLOCK_ROWS * i - NA_WIN_ROWS // 2, 0, n_blocks * NA_BLOCK_ROWS - NA_SLAB_ROWS)
    slab = pl.ds(pl.multiple_of(ks * GRID_W, GRID_W), NA_SLAB_ROWS * GRID_W)

    def segs_of(h):
        return [(_slot(nk_ref, h // 2, slab), _vo(nvo_ref, h, slab), bias_ref[h].astype(F32)),
                (_slot(ck_ref, h // 2), _vo(cvo_ref, h), None)]

    _attn_heads(out_ref, _pair_q(nq_ref), segs_of)


def _attn_na(nq, nk, nvo, cache_k, cvo, bias_tab, l, *, batch, seq):
    tq = NA_BLOCK_ROWS * GRID_W
    nb = seq // tq
    past = cache_k.shape[2]

    def cls(i):
        return jnp.where(i == 0, 0, jnp.where(i == nb - 1, 2, 1))

    return pl.pallas_call(
        functools.partial(_attn_na_kernel, n_blocks=nb),
        grid=(batch, nb),
        in_specs=[
            pl.BlockSpec((tq, 512), lambda b, i: (b * nb + i, 0)),
            pl.BlockSpec((seq, 512), lambda b, i: (b, 0)),
            pl.BlockSpec((seq, VO_ALL), lambda b, i: (b, 0)),
            pl.BlockSpec((None, None, past, 512), lambda b, i: (b, l, 0, 0)),
            pl.BlockSpec((None, past, VO_ALL), lambda b, i: (b, 0, 0)),
            pl.BlockSpec((None, None, NA_HEADS, tq, NA_SLAB_ROWS * GRID_W), lambda b, i: (l, cls(i), 0, 0, 0)),
        ],
        out_specs=pl.BlockSpec((tq, 512), lambda b, i: (b * nb + i, 0)),
        out_shape=jax.ShapeDtypeStruct((batch * seq, 512), BF16),
        compiler_params=_cparams(("arbitrary", "arbitrary")),
        name="attn_na_lat",
    )(nq, nk, nvo, cache_k, cvo, bias_tab)


def _cache_prep_kernel(ckv_ref, kpe_ref, nv_ref, wukv_ref, place_ref, km_ref, vmo_ref, nvo_ref):
    kv = _dot(ckv_ref[...].astype(BF16), wukv_ref[...])
    kslots = kv[:, :MLA_HEADS * SLOT] + _dot(kpe_ref[...].astype(BF16), place_ref[...])
    km_ref[...] = kslots.astype(BF16)
    _store_vo(vmo_ref, kv[:, MLA_HEADS * SLOT:])
    _store_vo(nvo_ref, nv_ref[...])


def _cache_prep(cache_ckv, cache_kpe, cache_nv, w_ukv, place, l):
    batch, _, past, _ = cache_ckv.shape
    tout = pl.BlockSpec((None, past, VO_ALL), lambda b: (b, 0, 0))
    tshape = jax.ShapeDtypeStruct((batch, past, VO_ALL), BF16)
    return pl.pallas_call(
        _cache_prep_kernel,
        grid=(batch,),
        in_specs=[
            pl.BlockSpec((None, None, past, MLA_KV_LORA), lambda b: (b, l, 0, 0)),
            pl.BlockSpec((None, None, past, MLA_ROPE_DIM), lambda b: (b, l, 0, 0)),
            pl.BlockSpec((None, None, past, 512), lambda b: (b, l, 0, 0)),
            pl.BlockSpec((None, MLA_KV_LORA, MLA_HEADS * SLOT + 512), lambda b: (l, 0, 0)),
            pl.BlockSpec((MLA_ROPE_DIM, MLA_HEADS * SLOT), lambda b: (0, 0)),
        ],
        out_specs=[pl.BlockSpec((None, past, MLA_HEADS * SLOT), lambda b: (b, 0, 0)), tout, tout],
        out_shape=[jax.ShapeDtypeStruct((batch, past, MLA_HEADS * SLOT), BF16), tshape, tshape],
        compiler_params=_cparams(("arbitrary",)),
        name="cache_prep",
    )(cache_ckv, cache_kpe, cache_nv, w_ukv, place)


def _attn_mla_kernel(qm_ref, km_ref, vmo_ref, ckm_ref, cvmo_ref, out_ref):
    _attn_heads(out_ref, lambda h: _slot(qm_ref, h),
                lambda h: [(_slot(km_ref, h), _vo(vmo_ref, h), None), (_slot(ckm_ref, h), _vo(cvmo_ref, h), None)])


def _attn_mla(qm, km, vmo, ckm, cvmo, *, batch, seq, tq):
    nb = seq // tq
    past = ckm.shape[1]
    return pl.pallas_call(
        _attn_mla_kernel,
        grid=(batch, nb),
        in_specs=[
            pl.BlockSpec((tq, MLA_HEADS * SLOT), lambda b, i: (b * nb + i, 0)),
            pl.BlockSpec((seq, MLA_HEADS * SLOT), lambda b, i: (b, 0)),
            pl.BlockSpec((seq, VO_ALL), lambda b, i: (b, 0)),
            pl.BlockSpec((None, past, MLA_HEADS * SLOT), lambda b, i: (b, 0, 0)),
            pl.BlockSpec((None, past, VO_ALL), lambda b, i: (b, 0, 0)),
        ],
        out_specs=pl.BlockSpec((tq, 512), lambda b, i: (b * nb + i, 0)),
        out_shape=jax.ShapeDtypeStruct((batch * seq, 512), BF16),
        compiler_params=_cparams(("arbitrary", "arbitrary")),
        name="attn_mla_lat",
    )(qm, km, vmo, ckm, cvmo)


POOL_HALO = 16


GROUP_LANE = N_EXPERTS


def _route(logits):
    lane = lax.broadcasted_iota(jnp.int32, logits.shape, 1).astype(F32)
    big = jnp.float32(1 << 20)
    is_g = lane < N_GROUPS
    gl = jnp.where(is_g, logits, NEG_INF)
    gmax = gl.max(axis=-1, keepdims=True)
    ge = jnp.where(is_g, jnp.exp(gl - gmax), 0.0)
    gp = ge / ge.sum(axis=-1, keepdims=True)
    g_w = jnp.where(is_g, gp, -1.0).max(axis=-1, keepdims=True)
    g_idx = jnp.where(is_g & (gp == g_w), lane, big).min(axis=-1, keepdims=True)
    e_lane = lane - N_GROUPS
    in_grp = (e_lane >= g_idx * EXPERTS_PER_GROUP) & (e_lane < (g_idx + 1) * EXPERTS_PER_GROUP)
    el = jnp.where(in_grp, logits, NEG_INF)
    emax = el.max(axis=-1, keepdims=True)
    ee = jnp.where(in_grp, jnp.exp(el - emax), 0.0)
    ep = ee / ee.sum(axis=-1, keepdims=True)
    p1 = jnp.where(in_grp, ep, -1.0).max(axis=-1, keepdims=True)
    i1 = jnp.where(in_grp & (ep == p1), lane, big).min(axis=-1, keepdims=True)
    rest = in_grp & (lane != i1)
    p2 = jnp.where(rest, ep, -1.0).max(axis=-1, keepdims=True)
    i2 = jnp.where(rest & (ep == p2), lane, big).min(axis=-1, keepdims=True)
    tot = p1 + p2
    gates = jnp.where(lane == i1, g_w * (p1 / tot), 0.0) + jnp.where(lane == i2, g_w * (p2 / tot), 0.0)
    return jnp.where(lane == GROUP_LANE, g_idx, pltpu.roll(gates, LANES - N_GROUPS, axis=1))


def _merge_kernel(x_ref, u_ref, yna_ref, ymla_ref, gate_ref, gt1_ref, sh2_ref, sc2_ref, gpost_ref, gpre_ref,
                  wpool_ref, pscale_ref, wbr_ref, wout_ref, wr2_ref, wr1_ref,
                  xo_ref, h2_ref, rg_ref, *, seq, tm, steps_per_seq):
    def pool_diff(uwin, utile, r0, s0):
        rows, nwin = utile.shape[0], uwin.shape[0]
        t = r0 + lax.broadcasted_iota(jnp.int32, (rows, nwin), 0)
        s = s0 + lax.broadcasted_iota(jnp.int32, (rows, nwin), 1)
        t1 = r0 + lax.broadcasted_iota(jnp.int32, (rows, 1), 0)
        ds = []
        for g, w in enumerate(POOL_WINDOWS):
            lo = jnp.maximum(t - w // 2, 0)
            hi = jnp.minimum(t + (w - w // 2), seq)
            band = ((s >= lo) & (s < hi)).astype(BF16)
            cnt = (jnp.minimum(t1 + (w - w // 2), seq) - jnp.maximum(t1 - w // 2, 0)).astype(F32)
            sl = slice(g * POOL_GC, (g + 1) * POOL_GC)
            ds.append(_dot(band, uwin[:, sl]) / cnt - utile[:, sl].astype(F32))
        return jnp.concatenate(ds, axis=-1).astype(BF16)

    if steps_per_seq == 0:
        d = jnp.concatenate([pool_diff(u_ref[si], u_ref[si], 0, 0) for si in range(tm // seq)], axis=0)
    else:
        nwin = min(tm + 2 * POOL_HALO, seq)
        r0 = (pl.program_id(0) % steps_per_seq) * tm
        s0 = pl.multiple_of(jnp.clip(r0 - POOL_HALO, 0, seq - nwin), POOL_HALO)
        d = pool_diff(u_ref[0, pl.ds(s0, nwin), :], u_ref[0, pl.ds(pl.multiple_of(r0, POOL_HALO), tm), :], r0, s0)
    y_pool = (_dot(d, wpool_ref[...]) * pscale_ref[...]).astype(BF16)

    g = gate_ref[...]
    merged = (g[:, 0:D_MODEL].astype(F32) * _dot(y_pool, wbr_ref[0])
              + g[:, D_MODEL:2 * D_MODEL].astype(F32) * _dot(yna_ref[...], wbr_ref[1])
              + g[:, 2 * D_MODEL:].astype(F32) * _dot(ymla_ref[...], wbr_ref[2]))
    y = _dot(merged.astype(BF16), wout_ref[...])
    xn = x_ref[...] + gt1_ref[...] * _rms(y, gpost_ref[...])
    xo_ref[...] = xn

    h2 = _rms(xn, gpre_ref[...]) * (1.0 + sc2_ref[...]) + sh2_ref[...]
    h_hi = h2.astype(BF16)
    h2_ref[...] = h_hi
    h_lo = (h2 - h_hi.astype(F32)).astype(BF16)
    two = _dot(h_hi, wr2_ref[...])
    logits = two[:, :LANES] + two[:, LANES:] + _dot(h_lo, wr1_ref[...])
    rg_ref[...] = _route(logits)


def _merge(x, u, yna, ymla, gate, mod5, l, w, *, seq, ctx, tm):
    n = x.shape[0]
    nseq = n // seq
    if tm >= seq:
        assert ctx and tm % seq == 0
        steps_per_seq = 0
        u_spec = pl.BlockSpec((tm // seq, seq, 512), lambda i: (i, 0, 0))
    else:
        steps_per_seq = seq // tm
        u_spec = pl.BlockSpec((1, seq, 512), lambda i: (i // steps_per_seq, 0, 0))

    def mrow(i):
        return 0 if ctx else 1 + i // steps_per_seq

    const = lambda i: (l, 0, 0)
    modspec = lambda k: pl.BlockSpec((None, None, None, 1, D_MODEL), lambda i: (l, mrow(i), k, 0, 0))
    row = lambda wd: pl.BlockSpec((tm, wd), lambda i: (i, 0))
    return pl.pallas_call(
        functools.partial(_merge_kernel, seq=seq, tm=tm, steps_per_seq=steps_per_seq),
        grid=(n // tm,),
        in_specs=[
            row(D_MODEL),
            u_spec,
            row(512), row(512), row(3 * D_MODEL),
            modspec(2), modspec(3), modspec(4),
            pl.BlockSpec((None, 1, D_MODEL), const),
            pl.BlockSpec((None, 1, D_MODEL), const),
            pl.BlockSpec((None, 512, 512), const),
            pl.BlockSpec((None, 1, 512), const),
            pl.BlockSpec((None, 3, 512, D_MODEL), lambda i: (l, 0, 0, 0)),
            pl.BlockSpec((None, D_MODEL, D_MODEL), const),
            pl.BlockSpec((None, D_MODEL, 2 * LANES), const),
            pl.BlockSpec((None, D_MODEL, LANES), const),
        ],
        out_specs=[row(D_MODEL), row(D_MODEL), row(LANES)],
        out_shape=[jax.ShapeDtypeStruct((n, D_MODEL), F32), jax.ShapeDtypeStruct((n, D_MODEL), BF16),
                   jax.ShapeDtypeStruct((n, LANES), F32)],
        compiler_params=_cparams(("arbitrary",)),
        name="merge_ctx" if ctx else "merge_lat",
    )(x, u.reshape(nseq, seq, 512), yna, ymla, gate, mod5, mod5, mod5, w["g_post_mix"], w["g_pre_ffn"],
      w["w_pool_bd"], w["pool_scale"], w["w_branch"], w["w_out"], w["w_r2"], w["w_r1"])


MOE_TM = 1024
MOE_SUB = 512
MOE_ROWS = 160
MOE_ROWS_PAD = 256


def _moe_kernel(h_ref, rg_ref, x_ref, gt2_ref, gpost_ref, w1_ref, w3_ref, w2_ref, o_ref):
    tm = h_ref.shape[0]
    subs = [slice(s * MOE_SUB, (s + 1) * MOE_SUB) for s in range(tm // MOE_SUB)]
    acc_ref = o_ref
    rg = rg_ref[...]
    lane = lax.broadcasted_iota(jnp.int32, rg.shape, 1)
    grp = jnp.where(lane == GROUP_LANE, rg, 0.0).sum(axis=-1, keepdims=True)
    member = ((lane.astype(F32) == grp) & (lane < N_GROUPS)).astype(F32)
    t_row = lax.broadcasted_iota(jnp.int32, (tm, tm), 0)
    t_col = lax.broadcasted_iota(jnp.int32, (tm, tm), 1)
    same_sub = (t_row // MOE_SUB) == (t_col // MOE_SUB)
    rank_col = _dot(((t_col < t_row) & same_sub).astype(BF16), member.astype(BF16))
    member_t = member.T[:8]
    rank_row = _dot(member_t.astype(BF16), ((t_row < t_col) & same_sub).astype(BF16))
    count = functools.reduce(jnp.maximum, [member[sl].sum(axis=0, keepdims=True) for sl in subs])
    rg_hi = rg.astype(BF16)
    rg_lo = (rg - rg_hi.astype(F32)).astype(BF16)
    row_id = lax.broadcasted_iota(jnp.int32, (MOE_ROWS, 1), 0).astype(F32)
    col_id = lax.broadcasted_iota(jnp.int32, (1, MOE_ROWS_PAD), 1).astype(F32)
    lane1 = lax.broadcasted_iota(jnp.int32, (1, LANES), 1)
    acc_ref[...] = jnp.zeros_like(acc_ref)

    for g in range(N_GROUPS):
        rr, mr = rank_row[g:g + 1, :], member_t[g:g + 1, :]
        rc, mc = rank_col[:, g:g + 1], member[:, g:g + 1]
        n_g = jnp.max(jnp.where(lane1 == g, count, 0.0)).astype(jnp.int32)

        def chunk(c, carry, g=g, rr=rr, mr=mr, rc=rc, mc=mc):
            base = (c * MOE_ROWS).astype(F32)
            hs, gs = [], []
            for sl in subs:
                take = ((rr[:, sl] - base == row_id) & (mr[:, sl] > 0.0)).astype(BF16)
                hs.append(_dot(take, h_ref[sl, :]).astype(BF16))
                gs.append(_dot(take, rg_hi[sl]) + _dot(take, rg_lo[sl]))
            hs = jnp.concatenate(hs, axis=0)
            gs = jnp.concatenate(gs, axis=0)
            part = None
            for e in range(EXPERTS_PER_GROUP):
                k = g * EXPERTS_PER_GROUP + e
                a = _dot(hs, w1_ref[k])
                b = _dot(hs, w3_ref[k])
                hid = (a * jax.nn.sigmoid(a)) * b * gs[:, k:k + 1]
                y = _dot(hid.astype(BF16), w2_ref[k])
                part = y if part is None else part + y
            part = part.astype(BF16)
            pad = jnp.zeros((MOE_ROWS_PAD - MOE_ROWS, D_MODEL), BF16)
            for s, sl in enumerate(subs):
                rel = rc[sl] - base
                give = ((rel == col_id) & (rel < MOE_ROWS) & (mc[sl] > 0.0)).astype(BF16)
                rows = jnp.concatenate([part[s * MOE_ROWS:(s + 1) * MOE_ROWS], pad], axis=0)
                acc_ref[sl, :] += _dot(give, rows)
            return carry

        chunk(jnp.int32(0), 0)
        lax.fori_loop(1, (n_g + MOE_ROWS - 1) // MOE_ROWS, chunk, 0)

    o_ref[...] = x_ref[...] + gt2_ref[...] * _rms(acc_ref[...], gpost_ref[...])


def _moe(h2, rg, x, mod5, l, w, *, seq, ctx):
    n = x.shape[0]
    tm = MOE_TM
    steps_per_seq = seq // tm if not ctx else 0

    def mrow(i):
        return 0 if ctx else 1 + i // steps_per_seq

    row = lambda wd: pl.BlockSpec((tm, wd), lambda i: (i, 0))
    once = pl.Buffered(1)
    wspec = lambda a, b: pl.BlockSpec((None, N_EXPERTS, a, b), lambda i: (l, 0, 0, 0), pipeline_mode=once)
    return pl.pallas_call(
        _moe_kernel,
        grid=(n // tm,),
        in_specs=[
            row(D_MODEL), row(LANES), row(D_MODEL),
            pl.BlockSpec((None, None, None, 1, D_MODEL), lambda i: (l, mrow(i), 5, 0, 0)),
            pl.BlockSpec((None, 1, D_MODEL), lambda i: (l, 0, 0)),
            wspec(D_MODEL, EXPERT_FF), wspec(D_MODEL, EXPERT_FF), wspec(EXPERT_FF, D_MODEL),
        ],
        out_specs=row(D_MODEL),
        out_shape=jax.ShapeDtypeStruct((n, D_MODEL), F32),
        compiler_params=_cparams(("arbitrary",)),
        name="moe_ctx" if ctx else "moe_lat",
    )(h2, rg, x, mod5, w["g_post_ffn"], w["w1"], w["w3"], w["w2"])


def _rope_table(seq):
    t = np.arange(seq)
    n_freq = MLA_ROPE_DIM // 4
    inv = jnp.asarray(ROPE_BASE, F32) ** (-jnp.arange(n_freq, dtype=F32) / n_freq)
    ang_r = jnp.asarray(t // GRID_W, F32)[:, None] * inv
    ang_c = jnp.asarray(t % GRID_W, F32)[:, None] * inv
    cos = jnp.cos(jnp.stack([ang_r, ang_c], axis=1))
    sin = jnp.sin(jnp.stack([ang_r, ang_c], axis=1))
    zeros = jnp.zeros_like(sin)
    c32 = jnp.stack([cos, cos], axis=2).reshape(seq, MLA_ROPE_DIM)
    up32 = jnp.stack([-sin, zeros], axis=2).reshape(seq, MLA_ROPE_DIM)
    dn32 = jnp.stack([zeros, sin], axis=2).reshape(seq, MLA_ROPE_DIM)

    def slot(v, fill):
        return jnp.concatenate([jnp.full((seq, KPE_LANE), fill, F32), v,
                                jnp.full((seq, SLOT - KPE_LANE - MLA_ROPE_DIM), fill, F32)], axis=1)

    return jnp.concatenate([slot(c32, 1.0), slot(up32, 0.0), slot(dn32, 0.0)], axis=1)


def _na_row_classes(n_rows):
    i = np.arange(NA_BLOCK_ROWS)[:, None]
    j = np.arange(NA_SLAB_ROWS)[None, :]
    last_ks = n_rows - NA_SLAB_ROWS
    last_q0 = n_rows - NA_BLOCK_ROWS
    specs = [
        (j < NA_WIN_ROWS, j - i),
        ((j >= i) & (j < i + NA_WIN_ROWS), j - NA_WIN_ROWS // 2 - i),
        (last_ks + j >= n_rows - NA_WIN_ROWS, last_ks + j - last_q0 - i),
    ]
    ok = np.stack([np.broadcast_to(o, (NA_BLOCK_ROWS, NA_SLAB_ROWS)) for o, _ in specs])
    dr = np.stack([np.clip(d + NA_WIN_ROWS - 1, 0, 2 * NA_WIN_ROWS - 2) for _, d in specs])
    return ok, dr


def _bias_kernel(rpb_ref, out_ref, *, row_ok, row_dr):
    reach = NA_WIN_COLS - 1
    rows = rpb_ref[...] * LOG2E
    v_lo = pltpu.roll(rows, SLOT - reach, axis=1)
    v_hi = pltpu.roll(rows, GRID_W - reach, axis=1)
    lane1 = lax.broadcasted_iota(jnp.int32, (1, SLOT), 1)
    mid = (lane1 >= GRID_W - reach) & (lane1 <= GRID_W + reach)
    c = lax.broadcasted_iota(jnp.int32, (GRID_W, SLOT), 0)
    lane = lax.broadcasted_iota(jnp.int32, (GRID_W, SLOT), 1)
    kc = lane & (GRID_W - 1)
    c0 = jnp.clip(c - NA_WIN_COLS // 2, 0, GRID_W - NA_WIN_COLS)
    in_col = (kc >= c0) & (kc < c0 + NA_WIN_COLS)
    first = lane < GRID_W
    for k in range(3):
        for i in range(NA_BLOCK_ROWS):
            for jp in range(NA_SLAB_ROWS // 2):
                ok_a, ok_b = bool(row_ok[k, i, 2 * jp]), bool(row_ok[k, i, 2 * jp + 1])
                a, b = int(row_dr[k, i, 2 * jp]), int(row_dr[k, i, 2 * jp + 1])
                rs = slice(i * GRID_W, (i + 1) * GRID_W)
                cs = slice(jp * SLOT, (jp + 1) * SLOT)
                if not (ok_a or ok_b):
                    out_ref[k, rs, cs] = jnp.full((GRID_W, SLOT), NEG_INF, out_ref.dtype)
                    continue
                v = jnp.where(mid, v_hi[b:b + 1], v_lo[a:a + 1])
                tile = pltpu.roll(jnp.broadcast_to(v, (GRID_W, SLOT)), 0, axis=1, stride=1, stride_axis=0)
                ok = in_col
                if not ok_b:
                    ok = ok & first
                if not ok_a:
                    ok = ok & jnp.logical_not(first)
                out_ref[k, rs, cs] = jnp.where(ok, tile, NEG_INF).astype(out_ref.dtype)


def _na_bias_table(rpb, n_rows):
    n_dr, n_dc = rpb.shape[-2:]
    row_ok, row_dr = _na_row_classes(n_rows)
    rpb_p = jnp.pad(rpb, ((0, 0), (0, 0), (0, 0), (0, SLOT - n_dc)))
    tq, tk = NA_BLOCK_ROWS * GRID_W, NA_SLAB_ROWS * GRID_W
    return pl.pallas_call(
        functools.partial(_bias_kernel, row_ok=row_ok, row_dr=row_dr),
        grid=(DEPTH, NA_HEADS),
        in_specs=[pl.BlockSpec((None, None, n_dr, SLOT), lambda l, h: (l, h, 0, 0))],
        out_specs=pl.BlockSpec((None, 3, None, tq, tk), lambda l, h: (l, 0, h, 0, 0)),
        out_shape=jax.ShapeDtypeStruct((DEPTH, 3, NA_HEADS, tq, tk), BF16),
        compiler_params=_cparams(("arbitrary", "arbitrary")),
        name="na_bias_table",
    )(rpb_p)


def _prep(p):
    L = DEPTH
    w_in = p["w_in"]
    kpe0 = C_G
    kpe1 = C_G + MLA_ROPE_DIM
    w_in_p = jnp.concatenate(
        [w_in[..., :kpe0], w_in[..., kpe1:], jnp.zeros((L, D_MODEL, KPE_LANE), F32), w_in[..., kpe0:kpe1],
         jnp.zeros((L, D_MODEL, SLOT - KPE_LANE - MLA_ROPE_DIM), F32)], axis=-1).astype(BF16)
    w_uq = jnp.pad(p["mla_w_uq"].reshape(L, MLA_Q_LORA, MLA_HEADS, MLA_QK_DIM),
                   ((0, 0), (0, 0), (0, 0), (0, SLOT - MLA_QK_DIM))).reshape(L, MLA_Q_LORA, MLA_HEADS * SLOT)
    ukv = p["mla_w_ukv"].reshape(L, MLA_KV_LORA, MLA_HEADS, MLA_NOPE_DIM + MLA_V_DIM)
    uk = jnp.pad(ukv[..., :MLA_NOPE_DIM], ((0, 0), (0, 0), (0, 0), (0, SLOT - MLA_NOPE_DIM)))
    w_ukv = jnp.concatenate([uk.reshape(L, MLA_KV_LORA, MLA_HEADS * SLOT),
                             ukv[..., MLA_NOPE_DIM:].reshape(L, MLA_KV_LORA, MLA_HEADS * MLA_V_DIM)], axis=-1)
    w_pool_bd = jnp.einsum("lgcd,gh->lgchd", p["w_pool"], jnp.eye(4, dtype=F32)).reshape(L, POOL_WIDTH, POOL_WIDTH)
    w_r = jnp.concatenate([p["w_group_router"], p["w_expert_router"]], axis=-1)
    r_hi = w_r.astype(BF16)
    r_lo = (w_r - r_hi.astype(F32)).astype(BF16)
    padl = lambda a: jnp.pad(a, ((0, 0), (0, 0), (0, LANES - a.shape[-1])))
    vec = lambda a: a.reshape(L, 1, a.shape[-1])
    return {
        "w_in": w_in_p,
        "w_uq": w_uq.astype(BF16),
        "w_ukv": w_ukv.astype(BF16),
        "w_pool_bd": w_pool_bd.astype(BF16),
        "w_branch": p["w_branch"].astype(BF16),
        "w_out": p["w_out"].astype(BF16),
        "w_r2": jnp.concatenate([padl(r_hi), padl(r_lo)], axis=-1),
        "w_r1": padl(r_hi),
        "w1": p["w_exp_gate"].astype(BF16),
        "w3": p["w_exp_up"].astype(BF16),
        "w2": p["w_exp_down"].astype(BF16),
        "g_pre_mix": vec(p["g_pre_mix"]), "g_post_mix": vec(p["g_post_mix"]),
        "g_pre_ffn": vec(p["g_pre_ffn"]), "g_post_ffn": vec(p["g_post_ffn"]),
        "pool_scale": vec(p["pool_scale"]), "q_norm": vec(p["mla_q_norm"]), "kv_norm": vec(p["mla_kv_norm"]),
    }


def kernel(x_prompt, x_sample, cache_na_k, cache_na_v, cache_mla_ckv, cache_mla_kpe, c, c_ctx, w_mod, b_mod,
           g_pre_mix, g_post_mix, g_pre_ffn, g_post_ffn, w_in, w_pool, pool_scale, na_rpb, mla_q_norm, mla_w_uq,
           mla_kv_norm, mla_w_ukv, w_branch, w_out, w_group_router, w_expert_router, w_exp_gate, w_exp_up,
           w_exp_down):
    batch, seq, _ = x_prompt.shape
    dbatch, dseq, _ = x_sample.shape
    past = cache_na_k.shape[2]
    assert dbatch + 1 <= 8 and dseq % (NA_BLOCK_ROWS * GRID_W) == 0

    w = _prep(dict(w_in=w_in, w_pool=w_pool, pool_scale=pool_scale, mla_q_norm=mla_q_norm, mla_w_uq=mla_w_uq,
                   mla_kv_norm=mla_kv_norm, mla_w_ukv=mla_w_ukv, w_branch=w_branch, w_out=w_out,
                   w_group_router=w_group_router, w_expert_router=w_expert_router, w_exp_gate=w_exp_gate,
                   w_exp_up=w_exp_up, w_exp_down=w_exp_down, g_pre_mix=g_pre_mix, g_post_mix=g_post_mix,
                   g_pre_ffn=g_pre_ffn, g_post_ffn=g_post_ffn))
    rope_tab = _rope_table(dseq)
    bias_tab = _na_bias_table(na_rpb, dseq // GRID_W)
    place = jnp.tile(jnp.pad(jnp.eye(MLA_ROPE_DIM, dtype=F32), ((0, 0), (KPE_LANE, SLOT - KPE_LANE - MLA_ROPE_DIM))),
                     (1, MLA_HEADS)).astype(BF16)

    cvec = jnp.concatenate([c_ctx[None], c, jnp.zeros((8 - 1 - dbatch, D_MODEL), F32)], axis=0)
    mod5 = _modulation(cvec, w_mod, b_mod).reshape(DEPTH, 8, 6, 1, D_MODEL)

    ck = cache_na_k.reshape(dbatch, DEPTH, past, NA_WIDTH)
    cv = cache_na_v.reshape(dbatch, DEPTH, past, NA_WIDTH)

    xp = x_prompt.reshape(batch * seq, D_MODEL)
    assert DEPTH == 2
    caches = None
    for l in range(DEPTH):
        u, nq, nvo, qm, km, vmo, gate, *caches = _inproj(xp, mod5, l, w, seq=seq, rope_tab=None, ctx_out=True,
                                                         tm=seq, prev=caches)
        yna, ymla = _attn_ctx(nq, caches[0], nvo, qm, km, vmo, seq=seq, nk_layer=l if l > 0 else None)
        xp, h2, rg = _merge(xp, u, yna, ymla, gate, mod5, l, w, seq=seq, ctx=True, tm=2 * seq)
        xp = _moe(h2, rg, xp, mod5, l, w, seq=seq, ctx=True)
    new_nk, new_nv, new_ckv, new_kpe = caches

    xs = x_sample.reshape(dbatch * dseq, D_MODEL)
    for l in range(DEPTH):
        u, nq, nvo, qm, km, vmo, gate, nk = _inproj(xs, mod5, l, w, seq=dseq, rope_tab=rope_tab, ctx_out=False, tm=512)
        ckm, cvmo, cnvo = _cache_prep(cache_mla_ckv, cache_mla_kpe, cv, w["w_ukv"], place, l)
        yna = _attn_na(nq, nk, nvo, ck, cnvo, bias_tab, l, batch=dbatch, seq=dseq)
        ymla = _attn_mla(qm, km, vmo, ckm, cvmo, batch=dbatch, seq=dseq, tq=512)
        xs, h2, rg = _merge(xs, u, yna, ymla, gate, mod5, l, w, seq=dseq, ctx=False, tm=512)
        xs = _moe(h2, rg, xs, mod5, l, w, seq=dseq, ctx=False)

    heads = (batch, DEPTH, seq, NA_HEADS, NA_HEAD_DIM)
    return (xp.reshape(batch, seq, D_MODEL), xs.reshape(dbatch, dseq, D_MODEL),
            new_nk.reshape(heads), new_nv.reshape(heads), new_ckv, new_kpe)
```

```python
import functools

import numpy as np
import jax
import jax.numpy as jnp
from jax import lax
from jax.experimental import pallas as pl
from jax.experimental.pallas import tpu as pltpu

F32 = jnp.float32
BF16 = jnp.bfloat16

D_MODEL = 1024
DEPTH = 2
GRID_W = 64
POOL_WINDOWS = (2, 4, 8, 16)
POOL_GC = 128
POOL_WIDTH = 512
NA_HEADS = 8
NA_HEAD_DIM = 64
NA_WIDTH = 512
NA_WIN_ROWS = 8
NA_WIN_COLS = 16
MLA_HEADS = 8
MLA_NOPE_DIM = 64
MLA_ROPE_DIM = 32
MLA_V_DIM = 64
MLA_QK_DIM = 96
MLA_Q_LORA = 384
MLA_KV_LORA = 256
N_GROUPS = 4
EXPERTS_PER_GROUP = 4
N_EXPERTS = 16
EXPERT_FF = 256
ROPE_BASE = 10000.0
EPS = 1e-6
NEG_INF = -1e30

LANES = 128
SLOT = LANES
VMEM_LIMIT = 56 * 1024 * 1024

C_U, C_NQ, C_NK, C_NV, C_CQ, C_CKV, C_KPE, C_G, C_END = 0, 512, 1024, 1536, 2048, 2432, 2688, 2720, 5888
G_WIDTH = 3 * D_MODEL
KPE_LANE = MLA_NOPE_DIM

LOG2E = 1.4426950408889634
VO_W = 2 * SLOT
VO_ALL = (NA_HEADS // 2) * VO_W

NA_BLOCK_ROWS = 4
NA_SLAB_ROWS = 12


def _cparams(sem):
    return pltpu.CompilerParams(dimension_semantics=sem, vmem_limit_bytes=VMEM_LIMIT)


def _rms(x, g):
    return x * lax.rsqrt(jnp.mean(x * x, axis=-1, keepdims=True) + EPS) * g


def _dot(a, b):
    return jnp.dot(a, b, preferred_element_type=F32)


def _dot_nt(a, b):
    return lax.dot_general(a, b, (((1,), (1,)), ((), ())), preferred_element_type=F32)


def _mod_kernel(c_ref, w_ref, b_ref, o_ref):
    c = c_ref[...]
    s = (c * jax.nn.sigmoid(c)).astype(BF16)
    o_ref[...] = _dot(s, w_ref[...].astype(BF16)) + b_ref[...]


def _modulation(cvec, w_mod, b_mod):
    tn = 1536
    n = w_mod.shape[-1]
    return pl.pallas_call(
        _mod_kernel,
        grid=(DEPTH, n // tn),
        in_specs=[
            pl.BlockSpec((8, D_MODEL), lambda l, j: (0, 0)),
            pl.BlockSpec((None, D_MODEL, tn), lambda l, j: (l, 0, j)),
            pl.BlockSpec((None, 1, tn), lambda l, j: (l, 0, j)),
        ],
        out_specs=pl.BlockSpec((None, 8, tn), lambda l, j: (l, 0, j)),
        out_shape=jax.ShapeDtypeStruct((DEPTH, 8, n), F32),
        compiler_params=_cparams(("arbitrary", "arbitrary")),
        name="modulation",
    )(cvec, w_mod, b_mod.reshape(DEPTH, 1, n))


def _rope_slot(x, tab):
    c, s_up, s_dn = tab[:, 0:SLOT], tab[:, SLOT:2 * SLOT], tab[:, 2 * SLOT:3 * SLOT]
    half = MLA_ROPE_DIM // 4
    x_up = pltpu.roll(x, SLOT - half, axis=1)
    x_dn = pltpu.roll(x, half, axis=1)
    return x * c + x_up * s_up + x_dn * s_dn


def _store_vo(vo_ref, v):
    tm = v.shape[0]
    for p in range(NA_HEADS // 2):
        vo_ref[:, p * VO_W:p * VO_W + SLOT] = v[:, p * SLOT:(p + 1) * SLOT].astype(BF16)
        vo_ref[:, p * VO_W + SLOT:(p + 1) * VO_W] = jnp.ones((tm, SLOT), BF16)


def _realign_gate_columns(win_ref, wg_ref, wk_ref):
    off = C_G % LANES
    lane = lax.broadcasted_iota(jnp.int32, (1, LANES), 1)
    base = C_G - off
    nxt = win_ref[:, base:base + LANES].astype(F32)
    kpe = pltpu.roll(nxt, KPE_LANE - (C_KPE - base), axis=1)
    wk_ref[...] = jnp.where((lane >= KPE_LANE) & (lane < KPE_LANE + MLA_ROPE_DIM), kpe, 0.0).astype(BF16)
    for j in range(G_WIDTH // LANES):
        cur = nxt
        nxt = win_ref[:, base + (j + 1) * LANES:base + (j + 2) * LANES].astype(F32)
        tile = pltpu.roll(jnp.where(lane >= off, cur, nxt), LANES - off, axis=1)
        wg_ref[:, j * LANES:(j + 1) * LANES] = tile.astype(BF16)


def _inproj_kernel(*refs, rope, ctx_out, stacked):
    n_in = 9 + (1 if rope else 0) + (4 if stacked else 0)
    ins, outs, (wg_ref, wk_ref) = refs[:n_in], refs[n_in:-2], refs[-2:]
    x_ref, sh_ref, sc_ref, g_ref, win_ref, qn_ref, wuq_ref, kvn_ref, wukv_ref = ins[:9]
    tab_ref = ins[9] if rope else None
    prev = ins[n_in - 4:] if stacked else None
    if ctx_out:
        u_ref, nq_ref, nvo_ref, qm_ref, km_ref, vmo_ref, gate_ref, nk_ref, nv_ref, ckv_ref, kpe_ref = outs
    else:
        u_ref, nq_ref, nvo_ref, qm_ref, km_ref, vmo_ref, gate_ref, nk_ref = outs

    def put(ref, k, val):
        if stacked:
            ref[0] = prev[k][...]
            ref[1] = val
        else:
            ref[...] = val

    @pl.when(pl.program_id(0) == 0)
    def _():
        _realign_gate_columns(win_ref, wg_ref, wk_ref)

    x = x_ref[...]
    h = (_rms(x, g_ref[...]) * (1.0 + sc_ref[...]) + sh_ref[...]).astype(BF16)

    def seg(a, b):
        return _dot(h, win_ref[:, a:b])

    u_ref[...] = seg(C_U, C_NQ).astype(u_ref.dtype)
    nq_ref[...] = (seg(C_NQ, C_NK) * (NA_HEAD_DIM ** -0.5 * LOG2E)).astype(BF16)
    nk = seg(C_NK, C_NV)
    nv = seg(C_NV, C_CQ)
    if ctx_out:
        put(nk_ref, 0, nk)
        put(nv_ref, 1, nv)
    else:
        nk_ref[...] = nk.astype(BF16)
    _store_vo(nvo_ref, nv)
    gate_ref[...] = jax.nn.sigmoid(_dot(h, wg_ref[...])).astype(BF16)

    cq = _rms(seg(C_CQ, C_CKV), qn_ref[...]).astype(BF16)
    q = _dot(cq, wuq_ref[...])
    tab = tab_ref[...] if rope else None
    for hd in range(MLA_HEADS):
        qs = q[:, hd * SLOT:(hd + 1) * SLOT]
        if rope:
            qs = _rope_slot(qs, tab)
        qm_ref[:, hd * SLOT:(hd + 1) * SLOT] = (qs * (MLA_QK_DIM ** -0.5 * LOG2E)).astype(BF16)

    ckv = _rms(seg(C_CKV, C_KPE), kvn_ref[...])
    kpe_slot = _dot(h, wk_ref[...])
    if ctx_out:
        put(ckv_ref, 2, ckv)
        put(kpe_ref, 3, kpe_slot[:, KPE_LANE:KPE_LANE + MLA_ROPE_DIM])
    if rope:
        kpe_slot = _rope_slot(kpe_slot, tab)
    kv = _dot(ckv.astype(BF16), wukv_ref[...])
    for hd in range(MLA_HEADS):
        km_ref[:, hd * SLOT:(hd + 1) * SLOT] = (kv[:, hd * SLOT:(hd + 1) * SLOT] + kpe_slot).astype(BF16)
    _store_vo(vmo_ref, kv[:, MLA_HEADS * SLOT:])


def _inproj(x, mod5, l, w, *, seq, rope_tab, ctx_out, tm, prev=None):
    n = x.shape[0]
    steps_per_seq = seq // tm if not ctx_out else 0
    stacked = prev is not None
    assert not stacked or (ctx_out and tm == seq and l == 1)

    def mrow(i):
        return 0 if ctx_out else 1 + i // steps_per_seq

    const = lambda i: (l, 0, 0)
    in_specs = [
        pl.BlockSpec((tm, D_MODEL), lambda i: (i, 0)),
        pl.BlockSpec((None, None, None, 1, D_MODEL), lambda i: (l, mrow(i), 0, 0, 0)),
        pl.BlockSpec((None, None, None, 1, D_MODEL), lambda i: (l, mrow(i), 1, 0, 0)),
        pl.BlockSpec((None, 1, D_MODEL), const),
        pl.BlockSpec((None, D_MODEL, C_END), const, pipeline_mode=pl.Buffered(1)),
        pl.BlockSpec((None, 1, MLA_Q_LORA), const),
        pl.BlockSpec((None, MLA_Q_LORA, MLA_HEADS * SLOT), const, pipeline_mode=pl.Buffered(1)),
        pl.BlockSpec((None, 1, MLA_KV_LORA), const),
        pl.BlockSpec((None, MLA_KV_LORA, MLA_HEADS * SLOT + 512), const, pipeline_mode=pl.Buffered(1)),
    ]
    args = [x, mod5, mod5, w["g_pre_mix"], w["w_in"], w["q_norm"], w["w_uq"], w["kv_norm"], w["w_ukv"]]
    rope = rope_tab is not None
    if rope:
        in_specs.append(pl.BlockSpec((tm, 3 * SLOT), lambda i: (i % steps_per_seq, 0)))
        args.append(rope_tab)
    row = lambda wd: pl.BlockSpec((tm, wd), lambda i: (i, 0))
    rows_of = lambda wd, dt: jax.ShapeDtypeStruct((n, wd), dt)
    widths = [512, 512, VO_ALL, MLA_HEADS * SLOT, MLA_HEADS * SLOT, VO_ALL, 3 * D_MODEL]
    out_specs = [row(wd) for wd in widths]
    out_shape = [rows_of(wd, BF16) for wd in widths]
    cache_w = [512, 512, MLA_KV_LORA, MLA_ROPE_DIM] if ctx_out else []
    if not ctx_out:
        out_specs.append(row(512))
        out_shape.append(rows_of(512, BF16))
    elif stacked:
        in_specs += [row(wd) for wd in cache_w]
        args += list(prev)
        out_specs += [pl.BlockSpec((None, DEPTH, seq, wd), lambda i: (i, 0, 0, 0)) for wd in cache_w]
        out_shape += [jax.ShapeDtypeStruct((n // seq, DEPTH, seq, wd), F32) for wd in cache_w]
    else:
        out_specs += [row(wd) for wd in cache_w]
        out_shape += [rows_of(wd, F32) for wd in cache_w]
    return pl.pallas_call(
        functools.partial(_inproj_kernel, rope=rope, ctx_out=ctx_out, stacked=stacked),
        grid=(n // tm,),
        in_specs=in_specs,
        out_specs=out_specs,
        out_shape=out_shape,
        scratch_shapes=[pltpu.VMEM((D_MODEL, G_WIDTH), BF16), pltpu.VMEM((D_MODEL, SLOT), BF16)],
        compiler_params=_cparams(("arbitrary",)),
        name="inproj_ctx" if ctx_out else "inproj_lat",
    )(*args)


def _attend(q_h, segs):
    scores = []
    for k, _, bias in segs:
        s = _dot_nt(q_h, k)
        if bias is not None:
            s = s + bias
        scores.append(s)
    m = scores[0].max(axis=-1, keepdims=True)
    for s in scores[1:]:
        m = jnp.maximum(m, s.max(axis=-1, keepdims=True))
    acc = None
    for s, (_, vo, _) in zip(scores, segs):
        o = _dot(jnp.exp2(s - m).astype(BF16), vo)
        acc = o if acc is None else acc + o
    return acc[:, :SLOT] / acc[:, SLOT:]


def _attn_heads(out_ref, q_of, segs_of):
    lo = lax.broadcasted_iota(jnp.int32, (1, SLOT), 1) < NA_HEAD_DIM
    for p in range(NA_HEADS // 2):
        o0, o1 = [_attend(q_of(2 * p + e), segs_of(2 * p + e)) for e in range(2)]
        out_ref[:, p * SLOT:(p + 1) * SLOT] = jnp.where(lo, o0, o1).astype(out_ref.dtype)


def _pair_q(q_ref):
    lo = lax.broadcasted_iota(jnp.int32, (1, SLOT), 1) < NA_HEAD_DIM

    def q_of(h):
        q_pair = q_ref[:, (h // 2) * SLOT:(h // 2 + 1) * SLOT]
        keep = lo if h % 2 == 0 else jnp.logical_not(lo)
        return jnp.where(keep, q_pair, jnp.zeros_like(q_pair))
    return q_of


def _slot(ref, i, rows=slice(None)):
    return ref[rows, i * SLOT:(i + 1) * SLOT].astype(BF16)


def _vo(ref, h, rows=slice(None)):
    return ref[rows, (h // 2) * VO_W:(h // 2 + 1) * VO_W]


def _attn_ctx_kernel(nq_ref, nk_ref, nvo_ref, qm_ref, km_ref, vmo_ref, yna_ref, ymla_ref):
    _attn_heads(yna_ref, _pair_q(nq_ref), lambda h: [(_slot(nk_ref, h // 2), _vo(nvo_ref, h), None)])
    _attn_heads(ymla_ref, lambda h: _slot(qm_ref, h), lambda h: [(_slot(km_ref, h), _vo(vmo_ref, h), None)])


def _attn_ctx(nq, nk, nvo, qm, km, vmo, *, seq, nk_layer=None):
    n = nq.shape[0]
    spec = lambda wd: pl.BlockSpec((seq, wd), lambda b: (b, 0))
    nk_spec = spec(512) if nk_layer is None else pl.BlockSpec((None, None, seq, 512), lambda b: (b, nk_layer, 0, 0))
    return pl.pallas_call(
        _attn_ctx_kernel,
        grid=(n // seq,),
        in_specs=[spec(512), nk_spec, spec(VO_ALL), spec(1024), spec(1024), spec(VO_ALL)],
        out_specs=[spec(512), spec(512)],
        out_shape=[jax.ShapeDtypeStruct((n, 512), BF16)] * 2,
        compiler_params=_cparams(("arbitrary",)),
        name="attn_ctx",
    )(nq, nk, nvo, qm, km, vmo)


def _attn_na_kernel(nq_ref, nk_ref, nvo_ref, ck_ref, cvo_ref, bias_ref, out_ref, *, n_blocks):
    i = pl.program_id(1)
    ks = jnp.clip(NA_BLOCK_ROWS * i - NA_WIN_ROWS // 2, 0, n_blocks * NA_BLOCK_ROWS - NA_SLAB_ROWS)
    slab = pl.ds(pl.multiple_of(ks * GRID_W, GRID_W), NA_SLAB_ROWS * GRID_W)

    def segs_of(h):
        return [(_slot(nk_ref, h // 2, slab), _vo(nvo_ref, h, slab), bias_ref[h].astype(F32)),
                (_slot(ck_ref, h // 2), _vo(cvo_ref, h), None)]

    _attn_heads(out_ref, _pair_q(nq_ref), segs_of)


def _attn_na(nq, nk, nvo, cache_k, cvo, bias_tab, l, *, batch, seq):
    tq = NA_BLOCK_ROWS * GRID_W
    nb = seq // tq
    past = cache_k.shape[2]

    def cls(i):
        return jnp.where(i == 0, 0, jnp.where(i == nb - 1, 2, 1))

    return pl.pallas_call(
        functools.partial(_attn_na_kernel, n_blocks=nb),
        grid=(batch, nb),
        in_specs=[
            pl.BlockSpec((tq, 512), lambda b, i: (b * nb + i, 0)),
            pl.BlockSpec((seq, 512), lambda b, i: (b, 0)),
            pl.BlockSpec((seq, VO_ALL), lambda b, i: (b, 0)),
            pl.BlockSpec((None, None, past, 512), lambda b, i: (b, l, 0, 0)),
            pl.BlockSpec((None, past, VO_ALL), lambda b, i: (b, 0, 0)),
            pl.BlockSpec((None, None, NA_HEADS, tq, NA_SLAB_ROWS * GRID_W), lambda b, i: (l, cls(i), 0, 0, 0)),
        ],
        out_specs=pl.BlockSpec((tq, 512), lambda b, i: (b * nb + i, 0)),
        out_shape=jax.ShapeDtypeStruct((batch * seq, 512), BF16),
        compiler_params=_cparams(("arbitrary", "arbitrary")),
        name="attn_na_lat",
    )(nq, nk, nvo, cache_k, cvo, bias_tab)


def _cache_prep_kernel(ckv_ref, kpe_ref, nv_ref, wukv_ref, place_ref, km_ref, vmo_ref, nvo_ref):
    kv = _dot(ckv_ref[...].astype(BF16), wukv_ref[...])
    kslots = kv[:, :MLA_HEADS * SLOT] + _dot(kpe_ref[...].astype(BF16), place_ref[...])
    km_ref[...] = kslots.astype(BF16)
    _store_vo(vmo_ref, kv[:, MLA_HEADS * SLOT:])
    _store_vo(nvo_ref, nv_ref[...])


def _cache_prep(cache_ckv, cache_kpe, cache_nv, w_ukv, place, l):
    batch, _, past, _ = cache_ckv.shape
    tout = pl.BlockSpec((None, past, VO_ALL), lambda b: (b, 0, 0))
    tshape = jax.ShapeDtypeStruct((batch, past, VO_ALL), BF16)
    return pl.pallas_call(
        _cache_prep_kernel,
        grid=(batch,),
        in_specs=[
            pl.BlockSpec((None, None, past, MLA_KV_LORA), lambda b: (b, l, 0, 0)),
            pl.BlockSpec((None, None, past, MLA_ROPE_DIM), lambda b: (b, l, 0, 0)),
            pl.BlockSpec((None, None, past, 512), lambda b: (b, l, 0, 0)),
            pl.BlockSpec((None, MLA_KV_LORA, MLA_HEADS * SLOT + 512), lambda b: (l, 0, 0)),
            pl.BlockSpec((MLA_ROPE_DIM, MLA_HEADS * SLOT), lambda b: (0, 0)),
        ],
        out_specs=[pl.BlockSpec((None, past, MLA_HEADS * SLOT), lambda b: (b, 0, 0)), tout, tout],
        out_shape=[jax.ShapeDtypeStruct((batch, past, MLA_HEADS * SLOT), BF16), tshape, tshape],
        compiler_params=_cparams(("arbitrary",)),
        name="cache_prep",
    )(cache_ckv, cache_kpe, cache_nv, w_ukv, place)


def _attn_mla_kernel(qm_ref, km_ref, vmo_ref, ckm_ref, cvmo_ref, out_ref):
    _attn_heads(out_ref, lambda h: _slot(qm_ref, h),
                lambda h: [(_slot(km_ref, h), _vo(vmo_ref, h), None), (_slot(ckm_ref, h), _vo(cvmo_ref, h), None)])


def _attn_mla(qm, km, vmo, ckm, cvmo, *, batch, seq, tq):
    nb = seq // tq
    past = ckm.shape[1]
    return pl.pallas_call(
        _attn_mla_kernel,
        grid=(batch, nb),
        in_specs=[
            pl.BlockSpec((tq, MLA_HEADS * SLOT), lambda b, i: (b * nb + i, 0)),
            pl.BlockSpec((seq, MLA_HEADS * SLOT), lambda b, i: (b, 0)),
            pl.BlockSpec((seq, VO_ALL), lambda b, i: (b, 0)),
            pl.BlockSpec((None, past, MLA_HEADS * SLOT), lambda b, i: (b, 0, 0)),
            pl.BlockSpec((None, past, VO_ALL), lambda b, i: (b, 0, 0)),
        ],
        out_specs=pl.BlockSpec((tq, 512), lambda b, i: (b * nb + i, 0)),
        out_shape=jax.ShapeDtypeStruct((batch * seq, 512), BF16),
        compiler_params=_cparams(("arbitrary", "arbitrary")),
        name="attn_mla_lat",
    )(qm, km, vmo, ckm, cvmo)


POOL_HALO = 16


GROUP_LANE = N_EXPERTS


def _route(logits):
    lane = lax.broadcasted_iota(jnp.int32, logits.shape, 1).astype(F32)
    big = jnp.float32(1 << 20)
    is_g = lane < N_GROUPS
    gl = jnp.where(is_g, logits, NEG_INF)
    gmax = gl.max(axis=-1, keepdims=True)
    ge = jnp.where(is_g, jnp.exp(gl - gmax), 0.0)
    gp = ge / ge.sum(axis=-1, keepdims=True)
    g_w = jnp.where(is_g, gp, -1.0).max(axis=-1, keepdims=True)
    g_idx = jnp.where(is_g & (gp == g_w), lane, big).min(axis=-1, keepdims=True)
    e_lane = lane - N_GROUPS
    in_grp = (e_lane >= g_idx * EXPERTS_PER_GROUP) & (e_lane < (g_idx + 1) * EXPERTS_PER_GROUP)
    el = jnp.where(in_grp, logits, NEG_INF)
    emax = el.max(axis=-1, keepdims=True)
    ee = jnp.where(in_grp, jnp.exp(el - emax), 0.0)
    ep = ee / ee.sum(axis=-1, keepdims=True)
    p1 = jnp.where(in_grp, ep, -1.0).max(axis=-1, keepdims=True)
    i1 = jnp.where(in_grp & (ep == p1), lane, big).min(axis=-1, keepdims=True)
    rest = in_grp & (lane != i1)
    p2 = jnp.where(rest, ep, -1.0).max(axis=-1, keepdims=True)
    i2 = jnp.where(rest & (ep == p2), lane, big).min(axis=-1, keepdims=True)
    tot = p1 + p2
    gates = jnp.where(lane == i1, g_w * (p1 / tot), 0.0) + jnp.where(lane == i2, g_w * (p2 / tot), 0.0)
    return jnp.where(lane == GROUP_LANE, g_idx, pltpu.roll(gates, LANES - N_GROUPS, axis=1))


def _merge_kernel(x_ref, u_ref, yna_ref, ymla_ref, gate_ref, gt1_ref, sh2_ref, sc2_ref, gpost_ref, gpre_ref,
                  wpool_ref, pscale_ref, wbr_ref, wout_ref, wr2_ref, wr1_ref,
                  xo_ref, h2_ref, rg_ref, *, seq, tm, steps_per_seq):
    def pool_diff(uwin, utile, r0, s0):
        rows, nwin = utile.shape[0], uwin.shape[0]
        t = r0 + lax.broadcasted_iota(jnp.int32, (rows, nwin), 0)
        s = s0 + lax.broadcasted_iota(jnp.int32, (rows, nwin), 1)
        t1 = r0 + lax.broadcasted_iota(jnp.int32, (rows, 1), 0)
        ds = []
        for g, w in enumerate(POOL_WINDOWS):
            lo = jnp.maximum(t - w // 2, 0)
            hi = jnp.minimum(t + (w - w // 2), seq)
            band = ((s >= lo) & (s < hi)).astype(BF16)
            cnt = (jnp.minimum(t1 + (w - w // 2), seq) - jnp.maximum(t1 - w // 2, 0)).astype(F32)
            sl = slice(g * POOL_GC, (g + 1) * POOL_GC)
            ds.append(_dot(band, uwin[:, sl]) / cnt - utile[:, sl].astype(F32))
        return jnp.concatenate(ds, axis=-1).astype(BF16)

    if steps_per_seq == 0:
        d = jnp.concatenate([pool_diff(u_ref[si], u_ref[si], 0, 0) for si in range(tm // seq)], axis=0)
    else:
        nwin = min(tm + 2 * POOL_HALO, seq)
        r0 = (pl.program_id(0) % steps_per_seq) * tm
        s0 = pl.multiple_of(jnp.clip(r0 - POOL_HALO, 0, seq - nwin), POOL_HALO)
        d = pool_diff(u_ref[0, pl.ds(s0, nwin), :], u_ref[0, pl.ds(pl.multiple_of(r0, POOL_HALO), tm), :], r0, s0)
    y_pool = (_dot(d, wpool_ref[...]) * pscale_ref[...]).astype(BF16)

    g = gate_ref[...]
    merged = (g[:, 0:D_MODEL].astype(F32) * _dot(y_pool, wbr_ref[0])
              + g[:, D_MODEL:2 * D_MODEL].astype(F32) * _dot(yna_ref[...], wbr_ref[1])
              + g[:, 2 * D_MODEL:].astype(F32) * _dot(ymla_ref[...], wbr_ref[2]))
    y = _dot(merged.astype(BF16), wout_ref[...])
    xn = x_ref[...] + gt1_ref[...] * _rms(y, gpost_ref[...])
    xo_ref[...] = xn

    h2 = _rms(xn, gpre_ref[...]) * (1.0 + sc2_ref[...]) + sh2_ref[...]
    h_hi = h2.astype(BF16)
    h2_ref[...] = h_hi
    h_lo = (h2 - h_hi.astype(F32)).astype(BF16)
    two = _dot(h_hi, wr2_ref[...])
    logits = two[:, :LANES] + two[:, LANES:] + _dot(h_lo, wr1_ref[...])
    rg_ref[...] = _route(logits)


def _merge(x, u, yna, ymla, gate, mod5, l, w, *, seq, ctx, tm):
    n = x.shape[0]
    nseq = n // seq
    if tm >= seq:
        assert ctx and tm % seq == 0
        steps_per_seq = 0
        u_spec = pl.BlockSpec((tm // seq, seq, 512), lambda i: (i, 0, 0))
    else:
        steps_per_seq = seq // tm
        u_spec = pl.BlockSpec((1, seq, 512), lambda i: (i // steps_per_seq, 0, 0))

    def mrow(i):
        return 0 if ctx else 1 + i // steps_per_seq

    const = lambda i: (l, 0, 0)
    modspec = lambda k: pl.BlockSpec((None, None, None, 1, D_MODEL), lambda i: (l, mrow(i), k, 0, 0))
    row = lambda wd: pl.BlockSpec((tm, wd), lambda i: (i, 0))
    return pl.pallas_call(
        functools.partial(_merge_kernel, seq=seq, tm=tm, steps_per_seq=steps_per_seq),
        grid=(n // tm,),
        in_specs=[
            row(D_MODEL),
            u_spec,
            row(512), row(512), row(3 * D_MODEL),
            modspec(2), modspec(3), modspec(4),
            pl.BlockSpec((None, 1, D_MODEL), const),
            pl.BlockSpec((None, 1, D_MODEL), const),
            pl.BlockSpec((None, 512, 512), const),
            pl.BlockSpec((None, 1, 512), const),
            pl.BlockSpec((None, 3, 512, D_MODEL), lambda i: (l, 0, 0, 0)),
            pl.BlockSpec((None, D_MODEL, D_MODEL), const),
            pl.BlockSpec((None, D_MODEL, 2 * LANES), const),
            pl.BlockSpec((None, D_MODEL, LANES), const),
        ],
        out_specs=[row(D_MODEL), row(D_MODEL), row(LANES)],
        out_shape=[jax.ShapeDtypeStruct((n, D_MODEL), F32), jax.ShapeDtypeStruct((n, D_MODEL), BF16),
                   jax.ShapeDtypeStruct((n, LANES), F32)],
        compiler_params=_cparams(("arbitrary",)),
        name="merge_ctx" if ctx else "merge_lat",
    )(x, u.reshape(nseq, seq, 512), yna, ymla, gate, mod5, mod5, mod5, w["g_post_mix"], w["g_pre_ffn"],
      w["w_pool_bd"], w["pool_scale"], w["w_branch"], w["w_out"], w["w_r2"], w["w_r1"])


MOE_TM = 1024
MOE_SUB = 512
MOE_ROWS = 160
MOE_ROWS_PAD = 256


def _moe_kernel(h_ref, rg_ref, x_ref, gt2_ref, gpost_ref, w1_ref, w3_ref, w2_ref, o_ref):
    tm = h_ref.shape[0]
    subs = [slice(s * MOE_SUB, (s + 1) * MOE_SUB) for s in range(tm // MOE_SUB)]
    acc_ref = o_ref
    rg = rg_ref[...]
    lane = lax.broadcasted_iota(jnp.int32, rg.shape, 1)
    grp = jnp.where(lane == GROUP_LANE, rg, 0.0).sum(axis=-1, keepdims=True)
    member = ((lane.astype(F32) == grp) & (lane < N_GROUPS)).astype(F32)
    t_row = lax.broadcasted_iota(jnp.int32, (tm, tm), 0)
    t_col = lax.broadcasted_iota(jnp.int32, (tm, tm), 1)
    same_sub = (t_row // MOE_SUB) == (t_col // MOE_SUB)
    rank_col = _dot(((t_col < t_row) & same_sub).astype(BF16), member.astype(BF16))
    member_t = member.T[:8]
    rank_row = _dot(member_t.astype(BF16), ((t_row < t_col) & same_sub).astype(BF16))
    count = functools.reduce(jnp.maximum, [member[sl].sum(axis=0, keepdims=True) for sl in subs])
    rg_hi = rg.astype(BF16)
    rg_lo = (rg - rg_hi.astype(F32)).astype(BF16)
    row_id = lax.broadcasted_iota(jnp.int32, (MOE_ROWS, 1), 0).astype(F32)
    col_id = lax.broadcasted_iota(jnp.int32, (1, MOE_ROWS_PAD), 1).astype(F32)
    lane1 = lax.broadcasted_iota(jnp.int32, (1, LANES), 1)
    acc_ref[...] = jnp.zeros_like(acc_ref)

    for g in range(N_GROUPS):
        rr, mr = rank_row[g:g + 1, :], member_t[g:g + 1, :]
        rc, mc = rank_col[:, g:g + 1], member[:, g:g + 1]
        n_g = jnp.max(jnp.where(lane1 == g, count, 0.0)).astype(jnp.int32)

        def chunk(c, carry, g=g, rr=rr, mr=mr, rc=rc, mc=mc):
            base = (c * MOE_ROWS).astype(F32)
            hs, gs = [], []
            for sl in subs:
                take = ((rr[:, sl] - base == row_id) & (mr[:, sl] > 0.0)).astype(BF16)
                hs.append(_dot(take, h_ref[sl, :]).astype(BF16))
                gs.append(_dot(take, rg_hi[sl]) + _dot(take, rg_lo[sl]))
            hs = jnp.concatenate(hs, axis=0)
            gs = jnp.concatenate(gs, axis=0)
            part = None
            for e in range(EXPERTS_PER_GROUP):
                k = g * EXPERTS_PER_GROUP + e
                a = _dot(hs, w1_ref[k])
                b = _dot(hs, w3_ref[k])
                hid = (a * jax.nn.sigmoid(a)) * b * gs[:, k:k + 1]
                y = _dot(hid.astype(BF16), w2_ref[k])
                part = y if part is None else part + y
            part = part.astype(BF16)
            pad = jnp.zeros((MOE_ROWS_PAD - MOE_ROWS, D_MODEL), BF16)
            for s, sl in enumerate(subs):
                rel = rc[sl] - base
                give = ((rel == col_id) & (rel < MOE_ROWS) & (mc[sl] > 0.0)).astype(BF16)
                rows = jnp.concatenate([part[s * MOE_ROWS:(s + 1) * MOE_ROWS], pad], axis=0)
                acc_ref[sl, :] += _dot(give, rows)
            return carry

        chunk(jnp.int32(0), 0)
        lax.fori_loop(1, (n_g + MOE_ROWS - 1) // MOE_ROWS, chunk, 0)

    o_ref[...] = x_ref[...] + gt2_ref[...] * _rms(acc_ref[...], gpost_ref[...])


def _moe(h2, rg, x, mod5, l, w, *, seq, ctx):
    n = x.shape[0]
    tm = MOE_TM
    steps_per_seq = seq // tm if not ctx else 0

    def mrow(i):
        return 0 if ctx else 1 + i // steps_per_seq

    row = lambda wd: pl.BlockSpec((tm, wd), lambda i: (i, 0))
    once = pl.Buffered(1)
    wspec = lambda a, b: pl.BlockSpec((None, N_EXPERTS, a, b), lambda i: (l, 0, 0, 0), pipeline_mode=once)
    return pl.pallas_call(
        _moe_kernel,
        grid=(n // tm,),
        in_specs=[
            row(D_MODEL), row(LANES), row(D_MODEL),
            pl.BlockSpec((None, None, None, 1, D_MODEL), lambda i: (l, mrow(i), 5, 0, 0)),
            pl.BlockSpec((None, 1, D_MODEL), lambda i: (l, 0, 0)),
            wspec(D_MODEL, EXPERT_FF), wspec(D_MODEL, EXPERT_FF), wspec(EXPERT_FF, D_MODEL),
        ],
        out_specs=row(D_MODEL),
        out_shape=jax.ShapeDtypeStruct((n, D_MODEL), F32),
        compiler_params=_cparams(("arbitrary",)),
        name="moe_ctx" if ctx else "moe_lat",
    )(h2, rg, x, mod5, w["g_post_ffn"], w["w1"], w["w3"], w["w2"])


def _rope_table(seq):
    t = np.arange(seq)
    n_freq = MLA_ROPE_DIM // 4
    inv = jnp.asarray(ROPE_BASE, F32) ** (-jnp.arange(n_freq, dtype=F32) / n_freq)
    ang_r = jnp.asarray(t // GRID_W, F32)[:, None] * inv
    ang_c = jnp.asarray(t % GRID_W, F32)[:, None] * inv
    cos = jnp.cos(jnp.stack([ang_r, ang_c], axis=1))
    sin = jnp.sin(jnp.stack([ang_r, ang_c], axis=1))
    zeros = jnp.zeros_like(sin)
    c32 = jnp.stack([cos, cos], axis=2).reshape(seq, MLA_ROPE_DIM)
    up32 = jnp.stack([-sin, zeros], axis=2).reshape(seq, MLA_ROPE_DIM)
    dn32 = jnp.stack([zeros, sin], axis=2).reshape(seq, MLA_ROPE_DIM)

    def slot(v, fill):
        return jnp.concatenate([jnp.full((seq, KPE_LANE), fill, F32), v,
                                jnp.full((seq, SLOT - KPE_LANE - MLA_ROPE_DIM), fill, F32)], axis=1)

    return jnp.concatenate([slot(c32, 1.0), slot(up32, 0.0), slot(dn32, 0.0)], axis=1)


def _na_row_classes(n_rows):
    i = np.arange(NA_BLOCK_ROWS)[:, None]
    j = np.arange(NA_SLAB_ROWS)[None, :]
    last_ks = n_rows - NA_SLAB_ROWS
    last_q0 = n_rows - NA_BLOCK_ROWS
    specs = [
        (j < NA_WIN_ROWS, j - i),
        ((j >= i) & (j < i + NA_WIN_ROWS), j - NA_WIN_ROWS // 2 - i),
        (last_ks + j >= n_rows - NA_WIN_ROWS, last_ks + j - last_q0 - i),
    ]
    ok = np.stack([np.broadcast_to(o, (NA_BLOCK_ROWS, NA_SLAB_ROWS)) for o, _ in specs])
    dr = np.stack([np.clip(d + NA_WIN_ROWS - 1, 0, 2 * NA_WIN_ROWS - 2) for _, d in specs])
    return ok, dr


def _bias_kernel(rpb_ref, out_ref, *, row_ok, row_dr):
    reach = NA_WIN_COLS - 1
    rows = rpb_ref[...] * LOG2E
    v_lo = pltpu.roll(rows, SLOT - reach, axis=1)
    v_hi = pltpu.roll(rows, GRID_W - reach, axis=1)
    lane1 = lax.broadcasted_iota(jnp.int32, (1, SLOT), 1)
    mid = (lane1 >= GRID_W - reach) & (lane1 <= GRID_W + reach)
    c = lax.broadcasted_iota(jnp.int32, (GRID_W, SLOT), 0)
    lane = lax.broadcasted_iota(jnp.int32, (GRID_W, SLOT), 1)
    kc = lane & (GRID_W - 1)
    c0 = jnp.clip(c - NA_WIN_COLS // 2, 0, GRID_W - NA_WIN_COLS)
    in_col = (kc >= c0) & (kc < c0 + NA_WIN_COLS)
    first = lane < GRID_W
    for k in range(3):
        for i in range(NA_BLOCK_ROWS):
            for jp in range(NA_SLAB_ROWS // 2):
                ok_a, ok_b = bool(row_ok[k, i, 2 * jp]), bool(row_ok[k, i, 2 * jp + 1])
                a, b = int(row_dr[k, i, 2 * jp]), int(row_dr[k, i, 2 * jp + 1])
                rs = slice(i * GRID_W, (i + 1) * GRID_W)
                cs = slice(jp * SLOT, (jp + 1) * SLOT)
                if not (ok_a or ok_b):
                    out_ref[k, rs, cs] = jnp.full((GRID_W, SLOT), NEG_INF, out_ref.dtype)
                    continue
                v = jnp.where(mid, v_hi[b:b + 1], v_lo[a:a + 1])
                tile = pltpu.roll(jnp.broadcast_to(v, (GRID_W, SLOT)), 0, axis=1, stride=1, stride_axis=0)
                ok = in_col
                if not ok_b:
                    ok = ok & first
                if not ok_a:
                    ok = ok & jnp.logical_not(first)
                out_ref[k, rs, cs] = jnp.where(ok, tile, NEG_INF).astype(out_ref.dtype)


def _na_bias_table(rpb, n_rows):
    n_dr, n_dc = rpb.shape[-2:]
    row_ok, row_dr = _na_row_classes(n_rows)
    rpb_p = jnp.pad(rpb, ((0, 0), (0, 0), (0, 0), (0, SLOT - n_dc)))
    tq, tk = NA_BLOCK_ROWS * GRID_W, NA_SLAB_ROWS * GRID_W
    return pl.pallas_call(
        functools.partial(_bias_kernel, row_ok=row_ok, row_dr=row_dr),
        grid=(DEPTH, NA_HEADS),
        in_specs=[pl.BlockSpec((None, None, n_dr, SLOT), lambda l, h: (l, h, 0, 0))],
        out_specs=pl.BlockSpec((None, 3, None, tq, tk), lambda l, h: (l, 0, h, 0, 0)),
        out_shape=jax.ShapeDtypeStruct((DEPTH, 3, NA_HEADS, tq, tk), BF16),
        compiler_params=_cparams(("arbitrary", "arbitrary")),
        name="na_bias_table",
    )(rpb_p)


def _prep(p):
    L = DEPTH
    w_in = p["w_in"]
    w_in_p = jnp.pad(w_in, ((0, 0), (0, 0), (0, C_END - w_in.shape[-1]))).astype(BF16)
    w_uq = jnp.pad(p["mla_w_uq"].reshape(L, MLA_Q_LORA, MLA_HEADS, MLA_QK_DIM),
                   ((0, 0), (0, 0), (0, 0), (0, SLOT - MLA_QK_DIM))).reshape(L, MLA_Q_LORA, MLA_HEADS * SLOT)
    ukv = p["mla_w_ukv"].reshape(L, MLA_KV_LORA, MLA_HEADS, MLA_NOPE_DIM + MLA_V_DIM)
    uk = jnp.pad(ukv[..., :MLA_NOPE_DIM], ((0, 0), (0, 0), (0, 0), (0, SLOT - MLA_NOPE_DIM)))
    w_ukv = jnp.concatenate([uk.reshape(L, MLA_KV_LORA, MLA_HEADS * SLOT),
                             ukv[..., MLA_NOPE_DIM:].reshape(L, MLA_KV_LORA, MLA_HEADS * MLA_V_DIM)], axis=-1)
    w_pool_bd = jnp.einsum("lgcd,gh->lgchd", p["w_pool"], jnp.eye(4, dtype=F32)).reshape(L, POOL_WIDTH, POOL_WIDTH)
    w_r = jnp.concatenate([p["w_group_router"], p["w_expert_router"]], axis=-1)
    r_hi = w_r.astype(BF16)
    r_lo = (w_r - r_hi.astype(F32)).astype(BF16)
    padl = lambda a: jnp.pad(a, ((0, 0), (0, 0), (0, LANES - a.shape[-1])))
    vec = lambda a: a.reshape(L, 1, a.shape[-1])
    return {
        "w_in": w_in_p,
        "w_uq": w_uq.astype(BF16),
        "w_ukv": w_ukv.astype(BF16),
        "w_pool_bd": w_pool_bd.astype(BF16),
        "w_branch": p["w_branch"].astype(BF16),
        "w_out": p["w_out"].astype(BF16),
        "w_r2": jnp.concatenate([padl(r_hi), padl(r_lo)], axis=-1),
        "w_r1": padl(r_hi),
        "w1": p["w_exp_gate"].astype(BF16),
        "w3": p["w_exp_up"].astype(BF16),
        "w2": p["w_exp_down"].astype(BF16),
        "g_pre_mix": vec(p["g_pre_mix"]), "g_post_mix": vec(p["g_post_mix"]),
        "g_pre_ffn": vec(p["g_pre_ffn"]), "g_post_ffn": vec(p["g_post_ffn"]),
        "pool_scale": vec(p["pool_scale"]), "q_norm": vec(p["mla_q_norm"]), "kv_norm": vec(p["mla_kv_norm"]),
    }


def kernel(x_prompt, x_sample, cache_na_k, cache_na_v, cache_mla_ckv, cache_mla_kpe, c, c_ctx, w_mod, b_mod,
           g_pre_mix, g_post_mix, g_pre_ffn, g_post_ffn, w_in, w_pool, pool_scale, na_rpb, mla_q_norm, mla_w_uq,
           mla_kv_norm, mla_w_ukv, w_branch, w_out, w_group_router, w_expert_router, w_exp_gate, w_exp_up,
           w_exp_down):
    batch, seq, _ = x_prompt.shape
    dbatch, dseq, _ = x_sample.shape
    past = cache_na_k.shape[2]
    assert dbatch + 1 <= 8 and dseq % (NA_BLOCK_ROWS * GRID_W) == 0

    w = _prep(dict(w_in=w_in, w_pool=w_pool, pool_scale=pool_scale, mla_q_norm=mla_q_norm, mla_w_uq=mla_w_uq,
                   mla_kv_norm=mla_kv_norm, mla_w_ukv=mla_w_ukv, w_branch=w_branch, w_out=w_out,
                   w_group_router=w_group_router, w_expert_router=w_expert_router, w_exp_gate=w_exp_gate,
                   w_exp_up=w_exp_up, w_exp_down=w_exp_down, g_pre_mix=g_pre_mix, g_post_mix=g_post_mix,
                   g_pre_ffn=g_pre_ffn, g_post_ffn=g_post_ffn))
    rope_tab = _rope_table(dseq)
    bias_tab = _na_bias_table(na_rpb, dseq // GRID_W)
    place = jnp.tile(jnp.pad(jnp.eye(MLA_ROPE_DIM, dtype=F32), ((0, 0), (KPE_LANE, SLOT - KPE_LANE - MLA_ROPE_DIM))),
                     (1, MLA_HEADS)).astype(BF16)

    cvec = jnp.concatenate([c_ctx[None], c, jnp.zeros((8 - 1 - dbatch, D_MODEL), F32)], axis=0)
    mod5 = _modulation(cvec, w_mod, b_mod).reshape(DEPTH, 8, 6, 1, D_MODEL)

    ck = cache_na_k.reshape(dbatch, DEPTH, past, NA_WIDTH)
    cv = cache_na_v.reshape(dbatch, DEPTH, past, NA_WIDTH)

    xp = x_prompt.reshape(batch * seq, D_MODEL)
    assert DEPTH == 2
    caches = None
    for l in range(DEPTH):
        u, nq, nvo, qm, km, vmo, gate, *caches = _inproj(xp, mod5, l, w, seq=seq, rope_tab=None, ctx_out=True,
                                                         tm=seq, prev=caches)
        yna, ymla = _attn_ctx(nq, caches[0], nvo, qm, km, vmo, seq=seq, nk_layer=l if l > 0 else None)
        xp, h2, rg = _merge(xp, u, yna, ymla, gate, mod5, l, w, seq=seq, ctx=True, tm=2 * seq)
        xp = _moe(h2, rg, xp, mod5, l, w, seq=seq, ctx=True)
    new_nk, new_nv, new_ckv, new_kpe = caches

    xs = x_sample.reshape(dbatch * dseq, D_MODEL)
    for l in range(DEPTH):
        u, nq, nvo, qm, km, vmo, gate, nk = _inproj(xs, mod5, l, w, seq=dseq, rope_tab=rope_tab, ctx_out=False, tm=512)
        ckm, cvmo, cnvo = _cache_prep(cache_mla_ckv, cache_mla_kpe, cv, w["w_ukv"], place, l)
        yna = _attn_na(nq, nk, nvo, ck, cnvo, bias_tab, l, batch=dbatch, seq=dseq)
        ymla = _attn_mla(qm, km, vmo, ckm, cvmo, batch=dbatch, seq=dseq, tq=512)
        xs, h2, rg = _merge(xs, u, yna, ymla, gate, mod5, l, w, seq=dseq, ctx=False, tm=512)
        xs = _moe(h2, rg, xs, mod5, l, w, seq=dseq, ctx=False)

    heads = (batch, DEPTH, seq, NA_HEADS, NA_HEAD_DIM)
    return (xp.reshape(batch, seq, D_MODEL), xs.reshape(dbatch, dseq, D_MODEL),
            new_nk.reshape(heads), new_nv.reshape(heads), new_ckv, new_kpe)
```

```python
import functools

import numpy as np
import jax
import jax.numpy as jnp
from jax import lax
from jax.experimental import pallas as pl
from jax.experimental.pallas import tpu as pltpu

F32 = jnp.float32
BF16 = jnp.bfloat16

D_MODEL = 1024
DEPTH = 2
GRID_W = 64
POOL_WINDOWS = (2, 4, 8, 16)
POOL_GC = 128
POOL_WIDTH = 512
NA_HEADS = 8
NA_HEAD_DIM = 64
NA_WIDTH = 512
NA_WIN_ROWS = 8
NA_WIN_COLS = 16
MLA_HEADS = 8
MLA_NOPE_DIM = 64
MLA_ROPE_DIM = 32
MLA_V_DIM = 64
MLA_QK_DIM = 96
MLA_Q_LORA = 384
MLA_KV_LORA = 256
N_GROUPS = 4
EXPERTS_PER_GROUP = 4
N_EXPERTS = 16
EXPERT_FF = 256
ROPE_BASE = 10000.0
EPS = 1e-6
NEG_INF = -1e30

LANES = 128
SLOT = LANES
VMEM_LIMIT = 56 * 1024 * 1024

C_U, C_NQ, C_NK, C_NV, C_CQ, C_CKV, C_KPE, C_G, C_END = 0, 512, 1024, 1536, 2048, 2432, 2688, 2720, 5888
G_WIDTH = 3 * D_MODEL
KPE_LANE = MLA_NOPE_DIM

LOG2E = 1.4426950408889634
VO_W = 2 * SLOT
VO_ALL = (NA_HEADS // 2) * VO_W

NA_BLOCK_ROWS = 4
NA_SLAB_ROWS = 12


def _cparams(sem):
    return pltpu.CompilerParams(dimension_semantics=sem, vmem_limit_bytes=VMEM_LIMIT)


def _rms(x, g):
    return x * lax.rsqrt(jnp.mean(x * x, axis=-1, keepdims=True) + EPS) * g


def _dot(a, b):
    return jnp.dot(a, b, preferred_element_type=F32)


def _dot_nt(a, b):
    return lax.dot_general(a, b, (((1,), (1,)), ((), ())), preferred_element_type=F32)


def _mod_kernel(c_ref, w_ref, b_ref, o_ref):
    c = c_ref[...]
    s = (c * jax.nn.sigmoid(c)).astype(BF16)
    o_ref[...] = _dot(s, w_ref[...].astype(BF16)) + b_ref[...]


def _modulation(cvec, w_mod, b_mod):
    tn = 1536
    n = w_mod.shape[-1]
    return pl.pallas_call(
        _mod_kernel,
        grid=(DEPTH, n // tn),
        in_specs=[
            pl.BlockSpec((8, D_MODEL), lambda l, j: (0, 0)),
            pl.BlockSpec((None, D_MODEL, tn), lambda l, j: (l, 0, j)),
            pl.BlockSpec((None, 1, tn), lambda l, j: (l, 0, j)),
        ],
        out_specs=pl.BlockSpec((None, 8, tn), lambda l, j: (l, 0, j)),
        out_shape=jax.ShapeDtypeStruct((DEPTH, 8, n), F32),
        compiler_params=_cparams(("arbitrary", "arbitrary")),
        name="modulation",
    )(cvec, w_mod, b_mod.reshape(DEPTH, 1, n))


def _rope_slot(x, tab):
    c, s_up, s_dn = tab[:, 0:SLOT], tab[:, SLOT:2 * SLOT], tab[:, 2 * SLOT:3 * SLOT]
    half = MLA_ROPE_DIM // 4
    x_up = pltpu.roll(x, SLOT - half, axis=1)
    x_dn = pltpu.roll(x, half, axis=1)
    return x * c + x_up * s_up + x_dn * s_dn


def _store_vo(vo_ref, v):
    tm = v.shape[0]
    for p in range(NA_HEADS // 2):
        vo_ref[:, p * VO_W:p * VO_W + SLOT] = v[:, p * SLOT:(p + 1) * SLOT].astype(BF16)
        vo_ref[:, p * VO_W + SLOT:(p + 1) * VO_W] = jnp.ones((tm, SLOT), BF16)


def _realign_gate_columns(win_ref, wg_ref, wk_ref):
    off = C_G % LANES
    lane = lax.broadcasted_iota(jnp.int32, (1, LANES), 1)
    base = C_G - off
    nxt = win_ref[:, base:base + LANES].astype(F32)
    kpe = pltpu.roll(nxt, KPE_LANE - (C_KPE - base), axis=1)
    wk_ref[...] = jnp.where((lane >= KPE_LANE) & (lane < KPE_LANE + MLA_ROPE_DIM), kpe, 0.0).astype(BF16)
    for j in range(G_WIDTH // LANES):
        cur = nxt
        nxt = win_ref[:, base + (j + 1) * LANES:base + (j + 2) * LANES].astype(F32)
        tile = pltpu.roll(jnp.where(lane >= off, cur, nxt), LANES - off, axis=1)
        wg_ref[:, j * LANES:(j + 1) * LANES] = tile.astype(BF16)


def _inproj_kernel(*refs, rope, ctx_out, stacked):
    n_in = 9 + (1 if rope else 0) + (4 if stacked else 0)
    ins, outs, (wg_ref, wk_ref) = refs[:n_in], refs[n_in:-2], refs[-2:]
    x_ref, sh_ref, sc_ref, g_ref, win_ref, qn_ref, wuq_ref, kvn_ref, wukv_ref = ins[:9]
    tab_ref = ins[9] if rope else None
    prev = ins[n_in - 4:] if stacked else None
    if ctx_out:
        u_ref, nq_ref, nvo_ref, qm_ref, km_ref, vmo_ref, gate_ref, nk_ref, nv_ref, ckv_ref, kpe_ref = outs
    else:
        u_ref, nq_ref, nvo_ref, qm_ref, km_ref, vmo_ref, gate_ref, nk_ref = outs

    def put(ref, k, val):
        if stacked:
            ref[0] = prev[k][...]
            ref[1] = val
        else:
            ref[...] = val

    @pl.when(pl.program_id(0) == 0)
    def _():
        _realign_gate_columns(win_ref, wg_ref, wk_ref)

    x = x_ref[...]
    h = (_rms(x, g_ref[...]) * (1.0 + sc_ref[...]) + sh_ref[...]).astype(BF16)

    def seg(a, b):
        return _dot(h, win_ref[:, a:b])

    u_ref[...] = seg(C_U, C_NQ).astype(u_ref.dtype)
    nq_ref[...] = (seg(C_NQ, C_NK) * (NA_HEAD_DIM ** -0.5 * LOG2E)).astype(BF16)
    nk = seg(C_NK, C_NV)
    nv = seg(C_NV, C_CQ)
    if ctx_out:
        put(nk_ref, 0, nk)
        put(nv_ref, 1, nv)
    else:
        nk_ref[...] = nk.astype(BF16)
    _store_vo(nvo_ref, nv)
    gate_ref[...] = jax.nn.sigmoid(_dot(h, wg_ref[...])).astype(BF16)

    cq = _rms(seg(C_CQ, C_CKV), qn_ref[...]).astype(BF16)
    q = _dot(cq, wuq_ref[...])
    tab = tab_ref[...] if rope else None
    for hd in range(MLA_HEADS):
        qs = q[:, hd * SLOT:(hd + 1) * SLOT]
        if rope:
            qs = _rope_slot(qs, tab)
        qm_ref[:, hd * SLOT:(hd + 1) * SLOT] = (qs * (MLA_QK_DIM ** -0.5 * LOG2E)).astype(BF16)

    ckv = _rms(seg(C_CKV, C_KPE), kvn_ref[...])
    kpe_slot = _dot(h, wk_ref[...])
    if ctx_out:
        put(ckv_ref, 2, ckv)
        put(kpe_ref, 3, kpe_slot[:, KPE_LANE:KPE_LANE + MLA_ROPE_DIM])
    if rope:
        kpe_slot = _rope_slot(kpe_slot, tab)
    kv = _dot(ckv.astype(BF16), wukv_ref[...])
    for hd in range(MLA_HEADS):
        km_ref[:, hd * SLOT:(hd + 1) * SLOT] = (kv[:, hd * SLOT:(hd + 1) * SLOT] + kpe_slot).astype(BF16)
    _store_vo(vmo_ref, kv[:, MLA_HEADS * SLOT:])


def _inproj(x, mod5, l, w, *, seq, rope_tab, ctx_out, tm, prev=None):
    n = x.shape[0]
    steps_per_seq = seq // tm if not ctx_out else 0
    stacked = prev is not None
    assert not stacked or (ctx_out and tm == seq and l == 1)

    def mrow(i):
        return 0 if ctx_out else 1 + i // steps_per_seq

    const = lambda i: (l, 0, 0)
    in_specs = [
        pl.BlockSpec((tm, D_MODEL), lambda i: (i, 0)),
        pl.BlockSpec((None, None, None, 1, D_MODEL), lambda i: (l, mrow(i), 0, 0, 0)),
        pl.BlockSpec((None, None, None, 1, D_MODEL), lambda i: (l, mrow(i), 1, 0, 0)),
        pl.BlockSpec((None, 1, D_MODEL), const),
        pl.BlockSpec((None, D_MODEL, C_END), const, pipeline_mode=pl.Buffered(1)),
        pl.BlockSpec((None, 1, MLA_Q_LORA), const),
        pl.BlockSpec((None, MLA_Q_LORA, MLA_HEADS * SLOT), const, pipeline_mode=pl.Buffered(1)),
        pl.BlockSpec((None, 1, MLA_KV_LORA), const),
        pl.BlockSpec((None, MLA_KV_LORA, MLA_HEADS * SLOT + 512), const, pipeline_mode=pl.Buffered(1)),
    ]
    args = [x, mod5, mod5, w["g_pre_mix"], w["w_in"], w["q_norm"], w["w_uq"], w["kv_norm"], w["w_ukv"]]
    rope = rope_tab is not None
    if rope:
        in_specs.append(pl.BlockSpec((tm, 3 * SLOT), lambda i: (i % steps_per_seq, 0)))
        args.append(rope_tab)
    row = lambda wd: pl.BlockSpec((tm, wd), lambda i: (i, 0))
    rows_of = lambda wd, dt: jax.ShapeDtypeStruct((n, wd), dt)
    widths = [512, 512, VO_ALL, MLA_HEADS * SLOT, MLA_HEADS * SLOT, VO_ALL, 3 * D_MODEL]
    out_specs = [row(wd) for wd in widths]
    out_shape = [rows_of(wd, BF16) for wd in widths]
    cache_w = [512, 512, MLA_KV_LORA, MLA_ROPE_DIM] if ctx_out else []
    if not ctx_out:
        out_specs.append(row(512))
        out_shape.append(rows_of(512, BF16))
    elif stacked:
        in_specs += [row(wd) for wd in cache_w]
        args += list(prev)
        out_specs += [pl.BlockSpec((None, DEPTH, seq, wd), lambda i: (i, 0, 0, 0)) for wd in cache_w]
        out_shape += [jax.ShapeDtypeStruct((n // seq, DEPTH, seq, wd), F32) for wd in cache_w]
    else:
        out_specs += [row(wd) for wd in cache_w]
        out_shape += [rows_of(wd, F32) for wd in cache_w]
    return pl.pallas_call(
        functools.partial(_inproj_kernel, rope=rope, ctx_out=ctx_out, stacked=stacked),
        grid=(n // tm,),
        in_specs=in_specs,
        out_specs=out_specs,
        out_shape=out_shape,
        scratch_shapes=[pltpu.VMEM((D_MODEL, G_WIDTH), BF16), pltpu.VMEM((D_MODEL, SLOT), BF16)],
        compiler_params=_cparams(("arbitrary",)),
        name="inproj_ctx" if ctx_out else "inproj_lat",
    )(*args)


def _attend(q_h, segs):
    scores = []
    for k, _, bias in segs:
        s = _dot_nt(q_h, k)
        if bias is not None:
            s = s + bias
        scores.append(s)
    m = scores[0].max(axis=-1, keepdims=True)
    for s in scores[1:]:
        m = jnp.maximum(m, s.max(axis=-1, keepdims=True))
    acc = None
    for s, (_, vo, _) in zip(scores, segs):
        o = _dot(jnp.exp2(s - m).astype(BF16), vo)
        acc = o if acc is None else acc + o
    return acc[:, :SLOT] / acc[:, SLOT:]


def _attn_heads(out_ref, q_of, segs_of):
    lo = lax.broadcasted_iota(jnp.int32, (1, SLOT), 1) < NA_HEAD_DIM
    for p in range(NA_HEADS // 2):
        o0, o1 = [_attend(q_of(2 * p + e), segs_of(2 * p + e)) for e in range(2)]
        out_ref[:, p * SLOT:(p + 1) * SLOT] = jnp.where(lo, o0, o1).astype(out_ref.dtype)


def _pair_q(q_ref):
    lo = lax.broadcasted_iota(jnp.int32, (1, SLOT), 1) < NA_HEAD_DIM

    def q_of(h):
        q_pair = q_ref[:, (h // 2) * SLOT:(h // 2 + 1) * SLOT]
        keep = lo if h % 2 == 0 else jnp.logical_not(lo)
        return jnp.where(keep, q_pair, jnp.zeros_like(q_pair))
    return q_of


def _slot(ref, i, rows=slice(None)):
    return ref[rows, i * SLOT:(i + 1) * SLOT].astype(BF16)


def _vo(ref, h, rows=slice(None)):
    return ref[rows, (h // 2) * VO_W:(h // 2 + 1) * VO_W]


def _attn_ctx_kernel(nq_ref, nk_ref, nvo_ref, qm_ref, km_ref, vmo_ref, yna_ref, ymla_ref):
    _attn_heads(yna_ref, _pair_q(nq_ref), lambda h: [(_slot(nk_ref, h // 2), _vo(nvo_ref, h), None)])
    _attn_heads(ymla_ref, lambda h: _slot(qm_ref, h), lambda h: [(_slot(km_ref, h), _vo(vmo_ref, h), None)])


def _attn_ctx(nq, nk, nvo, qm, km, vmo, *, seq, nk_layer=None):
    n = nq.shape[0]
    spec = lambda wd: pl.BlockSpec((seq, wd), lambda b: (b, 0))
    nk_spec = spec(512) if nk_layer is None else pl.BlockSpec((None, None, seq, 512), lambda b: (b, nk_layer, 0, 0))
    return pl.pallas_call(
        _attn_ctx_kernel,
        grid=(n // seq,),
        in_specs=[spec(512), nk_spec, spec(VO_ALL), spec(1024), spec(1024), spec(VO_ALL)],
        out_specs=[spec(512), spec(512)],
        out_shape=[jax.ShapeDtypeStruct((n, 512), BF16)] * 2,
        compiler_params=_cparams(("arbitrary",)),
        name="attn_ctx",
    )(nq, nk, nvo, qm, km, vmo)


def _attn_na_kernel(nq_ref, nk_ref, nvo_ref, ck_ref, cvo_ref, bias_ref, out_ref, *, n_blocks):
    i = pl.program_id(1)
    ks = jnp.clip(NA_BLOCK_ROWS * i - NA_WIN_ROWS // 2, 0, n_blocks * NA_BLOCK_ROWS - NA_SLAB_ROWS)
    slab = pl.ds(pl.multiple_of(ks * GRID_W, GRID_W), NA_SLAB_ROWS * GRID_W)

    def segs_of(h):
        return [(_slot(nk_ref, h // 2, slab), _vo(nvo_ref, h, slab), bias_ref[h].astype(F32)),
                (_slot(ck_ref, h // 2), _vo(cvo_ref, h), None)]

    _attn_heads(out_ref, _pair_q(nq_ref), segs_of)


def _attn_na(nq, nk, nvo, cache_k, cvo, bias_tab, l, *, batch, seq):
    tq = NA_BLOCK_ROWS * GRID_W
    nb = seq // tq
    past = cache_k.shape[2]

    def cls(i):
        return jnp.where(i == 0, 0, jnp.where(i == nb - 1, 2, 1))

    return pl.pallas_call(
        functools.partial(_attn_na_kernel, n_blocks=nb),
        grid=(batch, nb),
        in_specs=[
            pl.BlockSpec((tq, 512), lambda b, i: (b * nb + i, 0)),
            pl.BlockSpec((seq, 512), lambda b, i: (b, 0)),
            pl.BlockSpec((seq, VO_ALL), lambda b, i: (b, 0)),
            pl.BlockSpec((None, None, past, 512), lambda b, i: (b, l, 0, 0)),
            pl.BlockSpec((None, past, VO_ALL), lambda b, i: (b, 0, 0)),
            pl.BlockSpec((None, None, NA_HEADS, tq, NA_SLAB_ROWS * GRID_W), lambda b, i: (l, cls(i), 0, 0, 0)),
        ],
        out_specs=pl.BlockSpec((tq, 512), lambda b, i: (b * nb + i, 0)),
        out_shape=jax.ShapeDtypeStruct((batch * seq, 512), BF16),
        compiler_params=_cparams(("arbitrary", "arbitrary")),
        name="attn_na_lat",
    )(nq, nk, nvo, cache_k, cvo, bias_tab)


def _cache_prep_kernel(ckv_ref, kpe_ref, nv_ref, wukv_ref, place_ref, km_ref, vmo_ref, nvo_ref):
    kv = _dot(ckv_ref[...].astype(BF16), wukv_ref[...])
    kslots = kv[:, :MLA_HEADS * SLOT] + _dot(kpe_ref[...].astype(BF16), place_ref[...])
    km_ref[...] = kslots.astype(BF16)
    _store_vo(vmo_ref, kv[:, MLA_HEADS * SLOT:])
    _store_vo(nvo_ref, nv_ref[...])


def _cache_prep(cache_ckv, cache_kpe, cache_nv, w_ukv, place, l):
    batch, _, past, _ = cache_ckv.shape
    tout = pl.BlockSpec((None, past, VO_ALL), lambda b: (b, 0, 0))
    tshape = jax.ShapeDtypeStruct((batch, past, VO_ALL), BF16)
    return pl.pallas_call(
        _cache_prep_kernel,
        grid=(batch,),
        in_specs=[
            pl.BlockSpec((None, None, past, MLA_KV_LORA), lambda b: (b, l, 0, 0)),
            pl.BlockSpec((None, None, past, MLA_ROPE_DIM), lambda b: (b, l, 0, 0)),
            pl.BlockSpec((None, None, past, 512), lambda b: (b, l, 0, 0)),
            pl.BlockSpec((None, MLA_KV_LORA, MLA_HEADS * SLOT + 512), lambda b: (l, 0, 0)),
            pl.BlockSpec((MLA_ROPE_DIM, MLA_HEADS * SLOT), lambda b: (0, 0)),
        ],
        out_specs=[pl.BlockSpec((None, past, MLA_HEADS * SLOT), lambda b: (b, 0, 0)), tout, tout],
        out_shape=[jax.ShapeDtypeStruct((batch, past, MLA_HEADS * SLOT), BF16), tshape, tshape],
        compiler_params=_cparams(("arbitrary",)),
        name="cache_prep",
    )(cache_ckv, cache_kpe, cache_nv, w_ukv, place)


def _attn_mla_kernel(qm_ref, km_ref, vmo_ref, ckm_ref, cvmo_ref, out_ref):
    _attn_heads(out_ref, lambda h: _slot(qm_ref, h),
                lambda h: [(_slot(km_ref, h), _vo(vmo_ref, h), None), (_slot(ckm_ref, h), _vo(cvmo_ref, h), None)])


def _attn_mla(qm, km, vmo, ckm, cvmo, *, batch, seq, tq):
    nb = seq // tq
    past = ckm.shape[1]
    return pl.pallas_call(
        _attn_mla_kernel,
        grid=(batch, nb),
        in_specs=[
            pl.BlockSpec((tq, MLA_HEADS * SLOT), lambda b, i: (b * nb + i, 0)),
            pl.BlockSpec((seq, MLA_HEADS * SLOT), lambda b, i: (b, 0)),
            pl.BlockSpec((seq, VO_ALL), lambda b, i: (b, 0)),
            pl.BlockSpec((None, past, MLA_HEADS * SLOT), lambda b, i: (b, 0, 0)),
            pl.BlockSpec((None, past, VO_ALL), lambda b, i: (b, 0, 0)),
        ],
        out_specs=pl.BlockSpec((tq, 512), lambda b, i: (b * nb + i, 0)),
        out_shape=jax.ShapeDtypeStruct((batch * seq, 512), BF16),
        compiler_params=_cparams(("arbitrary", "arbitrary")),
        name="attn_mla_lat",
    )(qm, km, vmo, ckm, cvmo)


POOL_HALO = 16


GROUP_LANE = N_EXPERTS


def _route(logits):
    lane = lax.broadcasted_iota(jnp.int32, logits.shape, 1).astype(F32)
    big = jnp.float32(1 << 20)
    is_g = lane < N_GROUPS
    gl = jnp.where(is_g, logits, NEG_INF)
    gmax = gl.max(axis=-1, keepdims=True)
    ge = jnp.where(is_g, jnp.exp(gl - gmax), 0.0)
    gp = ge / ge.sum(axis=-1, keepdims=True)
    g_w = jnp.where(is_g, gp, -1.0).max(axis=-1, keepdims=True)
    g_idx = jnp.where(is_g & (gp == g_w), lane, big).min(axis=-1, keepdims=True)
    e_lane = lane - N_GROUPS
    in_grp = (e_lane >= g_idx * EXPERTS_PER_GROUP) & (e_lane < (g_idx + 1) * EXPERTS_PER_GROUP)
    el = jnp.where(in_grp, logits, NEG_INF)
    emax = el.max(axis=-1, keepdims=True)
    ee = jnp.where(in_grp, jnp.exp(el - emax), 0.0)
    ep = ee / ee.sum(axis=-1, keepdims=True)
    p1 = jnp.where(in_grp, ep, -1.0).max(axis=-1, keepdims=True)
    i1 = jnp.where(in_grp & (ep == p1), lane, big).min(axis=-1, keepdims=True)
    rest = in_grp & (lane != i1)
    p2 = jnp.where(rest, ep, -1.0).max(axis=-1, keepdims=True)
    i2 = jnp.where(rest & (ep == p2), lane, big).min(axis=-1, keepdims=True)
    tot = p1 + p2
    gates = jnp.where(lane == i1, g_w * (p1 / tot), 0.0) + jnp.where(lane == i2, g_w * (p2 / tot), 0.0)
    return jnp.where(lane == GROUP_LANE, g_idx, pltpu.roll(gates, LANES - N_GROUPS, axis=1))


def _merge_kernel(x_ref, u_ref, yna_ref, ymla_ref, gate_ref, gt1_ref, sh2_ref, sc2_ref, gpost_ref, gpre_ref,
                  wpool_ref, pscale_ref, wbr_ref, wout_ref, wr2_ref, wr1_ref,
                  xo_ref, h2_ref, rg_ref, *, seq, tm, steps_per_seq):
    def pool_diff(uwin, utile, r0, s0):
        rows, nwin = utile.shape[0], uwin.shape[0]
        t = r0 + lax.broadcasted_iota(jnp.int32, (rows, nwin), 0)
        s = s0 + lax.broadcasted_iota(jnp.int32, (rows, nwin), 1)
        t1 = r0 + lax.broadcasted_iota(jnp.int32, (rows, 1), 0)
        ds = []
        for g, w in enumerate(POOL_WINDOWS):
            lo = jnp.maximum(t - w // 2, 0)
            hi = jnp.minimum(t + (w - w // 2), seq)
            band = ((s >= lo) & (s < hi)).astype(BF16)
            cnt = (jnp.minimum(t1 + (w - w // 2), seq) - jnp.maximum(t1 - w // 2, 0)).astype(F32)
            sl = slice(g * POOL_GC, (g + 1) * POOL_GC)
            ds.append(_dot(band, uwin[:, sl]) / cnt - utile[:, sl].astype(F32))
        return jnp.concatenate(ds, axis=-1).astype(BF16)

    if steps_per_seq == 0:
        d = jnp.concatenate([pool_diff(u_ref[si], u_ref[si], 0, 0) for si in range(tm // seq)], axis=0)
    else:
        nwin = min(tm + 2 * POOL_HALO, seq)
        r0 = (pl.program_id(0) % steps_per_seq) * tm
        s0 = pl.multiple_of(jnp.clip(r0 - POOL_HALO, 0, seq - nwin), POOL_HALO)
        d = pool_diff(u_ref[0, pl.ds(s0, nwin), :], u_ref[0, pl.ds(pl.multiple_of(r0, POOL_HALO), tm), :], r0, s0)
    y_pool = (_dot(d, wpool_ref[...]) * pscale_ref[...]).astype(BF16)

    g = gate_ref[...]
    merged = (g[:, 0:D_MODEL].astype(F32) * _dot(y_pool, wbr_ref[0])
              + g[:, D_MODEL:2 * D_MODEL].astype(F32) * _dot(yna_ref[...], wbr_ref[1])
              + g[:, 2 * D_MODEL:].astype(F32) * _dot(ymla_ref[...], wbr_ref[2]))
    y = _dot(merged.astype(BF16), wout_ref[...])
    xn = x_ref[...] + gt1_ref[...] * _rms(y, gpost_ref[...])
    xo_ref[...] = xn

    h2 = _rms(xn, gpre_ref[...]) * (1.0 + sc2_ref[...]) + sh2_ref[...]
    h_hi = h2.astype(BF16)
    h2_ref[...] = h_hi
    h_lo = (h2 - h_hi.astype(F32)).astype(BF16)
    two = _dot(h_hi, wr2_ref[...])
    logits = two[:, :LANES] + two[:, LANES:] + _dot(h_lo, wr1_ref[...])
    rg_ref[...] = _route(logits)


def _merge(x, u, yna, ymla, gate, mod5, l, w, *, seq, ctx, tm):
    n = x.shape[0]
    nseq = n // seq
    if tm >= seq:
        assert ctx and tm % seq == 0
        steps_per_seq = 0
        u_spec = pl.BlockSpec((tm // seq, seq, 512), lambda i: (i, 0, 0))
    else:
        steps_per_seq = seq // tm
        u_spec = pl.BlockSpec((1, seq, 512), lambda i: (i // steps_per_seq, 0, 0))

    def mrow(i):
        return 0 if ctx else 1 + i // steps_per_seq

    const = lambda i: (l, 0, 0)
    modspec = lambda k: pl.BlockSpec((None, None, None, 1, D_MODEL), lambda i: (l, mrow(i), k, 0, 0))
    row = lambda wd: pl.BlockSpec((tm, wd), lambda i: (i, 0))
    return pl.pallas_call(
        functools.partial(_merge_kernel, seq=seq, tm=tm, steps_per_seq=steps_per_seq),
        grid=(n // tm,),
        in_specs=[
            row(D_MODEL),
            u_spec,
            row(512), row(512), row(3 * D_MODEL),
            modspec(2), modspec(3), modspec(4),
            pl.BlockSpec((None, 1, D_MODEL), const),
            pl.BlockSpec((None, 1, D_MODEL), const),
            pl.BlockSpec((None, 512, 512), const),
            pl.BlockSpec((None, 1, 512), const),
            pl.BlockSpec((None, 3, 512, D_MODEL), lambda i: (l, 0, 0, 0)),
            pl.BlockSpec((None, D_MODEL, D_MODEL), const),
            pl.BlockSpec((None, D_MODEL, 2 * LANES), const),
            pl.BlockSpec((None, D_MODEL, LANES), const),
        ],
        out_specs=[row(D_MODEL), row(D_MODEL), row(LANES)],
        out_shape=[jax.ShapeDtypeStruct((n, D_MODEL), F32), jax.ShapeDtypeStruct((n, D_MODEL), BF16),
                   jax.ShapeDtypeStruct((n, LANES), F32)],
        compiler_params=_cparams(("arbitrary",)),
        name="merge_ctx" if ctx else "merge_lat",
    )(x, u.reshape(nseq, seq, 512), yna, ymla, gate, mod5, mod5, mod5, w["g_post_mix"], w["g_pre_ffn"],
      w["w_pool_bd"], w["pool_scale"], w["w_branch"], w["w_out"], w["w_r2"], w["w_r1"])


MOE_TM = 1024
MOE_SUB = 512
MOE_ROWS = 160
MOE_ROWS_PAD = 256


def _moe_kernel(h_ref, rg_ref, x_ref, gt2_ref, gpost_ref, w1_ref, w3_ref, w2_ref, o_ref):
    tm = h_ref.shape[0]
    subs = [slice(s * MOE_SUB, (s + 1) * MOE_SUB) for s in range(tm // MOE_SUB)]
    acc_ref = o_ref
    rg = rg_ref[...]
    lane = lax.broadcasted_iota(jnp.int32, rg.shape, 1)
    grp = jnp.where(lane == GROUP_LANE, rg, 0.0).sum(axis=-1, keepdims=True)
    member = ((lane.astype(F32) == grp) & (lane < N_GROUPS)).astype(F32)
    t_row = lax.broadcasted_iota(jnp.int32, (tm, tm), 0)
    t_col = lax.broadcasted_iota(jnp.int32, (tm, tm), 1)
    same_sub = (t_row // MOE_SUB) == (t_col // MOE_SUB)
    rank_col = _dot(((t_col < t_row) & same_sub).astype(BF16), member.astype(BF16))
    member_t = member.T[:8]
    rank_row = _dot(member_t.astype(BF16), ((t_row < t_col) & same_sub).astype(BF16))
    count = functools.reduce(jnp.maximum, [member[sl].sum(axis=0, keepdims=True) for sl in subs])
    rg_hi = rg.astype(BF16)
    rg_lo = (rg - rg_hi.astype(F32)).astype(BF16)
    row_id = lax.broadcasted_iota(jnp.int32, (MOE_ROWS, 1), 0).astype(F32)
    col_id = lax.broadcasted_iota(jnp.int32, (1, MOE_ROWS_PAD), 1).astype(F32)
    lane1 = lax.broadcasted_iota(jnp.int32, (1, LANES), 1)
    acc_ref[...] = jnp.zeros_like(acc_ref)

    for g in range(N_GROUPS):
        rr, mr = rank_row[g:g + 1, :], member_t[g:g + 1, :]
        rc, mc = rank_col[:, g:g + 1], member[:, g:g + 1]
        n_g = jnp.max(jnp.where(lane1 == g, count, 0.0)).astype(jnp.int32)

        def chunk(c, carry, g=g, rr=rr, mr=mr, rc=rc, mc=mc):
            base = (c * MOE_ROWS).astype(F32)
            hs, gs = [], []
            for sl in subs:
                take = ((rr[:, sl] - base == row_id) & (mr[:, sl] > 0.0)).astype(BF16)
                hs.append(_dot(take, h_ref[sl, :]).astype(BF16))
                gs.append(_dot(take, rg_hi[sl]) + _dot(take, rg_lo[sl]))
            hs = jnp.concatenate(hs, axis=0)
            gs = jnp.concatenate(gs, axis=0)
            hid = []
            for e in range(EXPERTS_PER_GROUP):
                k = g * EXPERTS_PER_GROUP + e
                a = _dot(hs, w1_ref[k])
                b = _dot(hs, w3_ref[k])
                hid.append(((a * jax.nn.sigmoid(a)) * b * gs[:, k:k + 1]).astype(BF16))
            part = _dot(jnp.concatenate(hid, axis=-1), w2_ref[g]).astype(BF16)
            pad = jnp.zeros((MOE_ROWS_PAD - MOE_ROWS, D_MODEL), BF16)
            for s, sl in enumerate(subs):
                rel = rc[sl] - base
                give = ((rel == col_id) & (rel < MOE_ROWS) & (mc[sl] > 0.0)).astype(BF16)
                rows = jnp.concatenate([part[s * MOE_ROWS:(s + 1) * MOE_ROWS], pad], axis=0)
                acc_ref[sl, :] += _dot(give, rows)
            return carry

        chunk(jnp.int32(0), 0)
        lax.fori_loop(1, (n_g + MOE_ROWS - 1) // MOE_ROWS, chunk, 0)

    o_ref[...] = x_ref[...] + gt2_ref[...] * _rms(acc_ref[...], gpost_ref[...])


def _moe(h2, rg, x, mod5, l, w, *, seq, ctx):
    n = x.shape[0]
    tm = MOE_TM
    steps_per_seq = seq // tm if not ctx else 0

    def mrow(i):
        return 0 if ctx else 1 + i // steps_per_seq

    row = lambda wd: pl.BlockSpec((tm, wd), lambda i: (i, 0))
    once = pl.Buffered(1)
    wspec = lambda a, b: pl.BlockSpec((None, N_EXPERTS, a, b), lambda i: (l, 0, 0, 0), pipeline_mode=once)
    return pl.pallas_call(
        _moe_kernel,
        grid=(n // tm,),
        in_specs=[
            row(D_MODEL), row(LANES), row(D_MODEL),
            pl.BlockSpec((None, None, None, 1, D_MODEL), lambda i: (l, mrow(i), 5, 0, 0)),
            pl.BlockSpec((None, 1, D_MODEL), lambda i: (l, 0, 0)),
            wspec(D_MODEL, EXPERT_FF), wspec(D_MODEL, EXPERT_FF),
            pl.BlockSpec((None, N_GROUPS, EXPERTS_PER_GROUP * EXPERT_FF, D_MODEL), lambda i: (l, 0, 0, 0),
                         pipeline_mode=once),
        ],
        out_specs=row(D_MODEL),
        out_shape=jax.ShapeDtypeStruct((n, D_MODEL), F32),
        compiler_params=_cparams(("arbitrary",)),
        name="moe_ctx" if ctx else "moe_lat",
    )(h2, rg, x, mod5, w["g_post_ffn"], w["w1"], w["w3"], w["w2"])


def _rope_table(seq):
    t = np.arange(seq)
    n_freq = MLA_ROPE_DIM // 4
    inv = jnp.asarray(ROPE_BASE, F32) ** (-jnp.arange(n_freq, dtype=F32) / n_freq)
    ang_r = jnp.asarray(t // GRID_W, F32)[:, None] * inv
    ang_c = jnp.asarray(t % GRID_W, F32)[:, None] * inv
    cos = jnp.cos(jnp.stack([ang_r, ang_c], axis=1))
    sin = jnp.sin(jnp.stack([ang_r, ang_c], axis=1))
    zeros = jnp.zeros_like(sin)
    c32 = jnp.stack([cos, cos], axis=2).reshape(seq, MLA_ROPE_DIM)
    up32 = jnp.stack([-sin, zeros], axis=2).reshape(seq, MLA_ROPE_DIM)
    dn32 = jnp.stack([zeros, sin], axis=2).reshape(seq, MLA_ROPE_DIM)

    def slot(v, fill):
        return jnp.concatenate([jnp.full((seq, KPE_LANE), fill, F32), v,
                                jnp.full((seq, SLOT - KPE_LANE - MLA_ROPE_DIM), fill, F32)], axis=1)

    return jnp.concatenate([slot(c32, 1.0), slot(up32, 0.0), slot(dn32, 0.0)], axis=1)


def _na_row_classes(n_rows):
    i = np.arange(NA_BLOCK_ROWS)[:, None]
    j = np.arange(NA_SLAB_ROWS)[None, :]
    last_ks = n_rows - NA_SLAB_ROWS
    last_q0 = n_rows - NA_BLOCK_ROWS
    specs = [
        (j < NA_WIN_ROWS, j - i),
        ((j >= i) & (j < i + NA_WIN_ROWS), j - NA_WIN_ROWS // 2 - i),
        (last_ks + j >= n_rows - NA_WIN_ROWS, last_ks + j - last_q0 - i),
    ]
    ok = np.stack([np.broadcast_to(o, (NA_BLOCK_ROWS, NA_SLAB_ROWS)) for o, _ in specs])
    dr = np.stack([np.clip(d + NA_WIN_ROWS - 1, 0, 2 * NA_WIN_ROWS - 2) for _, d in specs])
    return ok, dr


def _bias_kernel(rpb_ref, out_ref, *, row_ok, row_dr):
    reach = NA_WIN_COLS - 1
    rows = rpb_ref[...] * LOG2E
    v_lo = pltpu.roll(rows, SLOT - reach, axis=1)
    v_hi = pltpu.roll(rows, GRID_W - reach, axis=1)
    lane1 = lax.broadcasted_iota(jnp.int32, (1, SLOT), 1)
    mid = (lane1 >= GRID_W - reach) & (lane1 <= GRID_W + reach)
    c = lax.broadcasted_iota(jnp.int32, (GRID_W, SLOT), 0)
    lane = lax.broadcasted_iota(jnp.int32, (GRID_W, SLOT), 1)
    kc = lane & (GRID_W - 1)
    c0 = jnp.clip(c - NA_WIN_COLS // 2, 0, GRID_W - NA_WIN_COLS)
    in_col = (kc >= c0) & (kc < c0 + NA_WIN_COLS)
    first = lane < GRID_W
    for k in range(3):
        for i in range(NA_BLOCK_ROWS):
            for jp in range(NA_SLAB_ROWS // 2):
                ok_a, ok_b = bool(row_ok[k, i, 2 * jp]), bool(row_ok[k, i, 2 * jp + 1])
                a, b = int(row_dr[k, i, 2 * jp]), int(row_dr[k, i, 2 * jp + 1])
                rs = slice(i * GRID_W, (i + 1) * GRID_W)
                cs = slice(jp * SLOT, (jp + 1) * SLOT)
                if not (ok_a or ok_b):
                    out_ref[k, rs, cs] = jnp.full((GRID_W, SLOT), NEG_INF, out_ref.dtype)
                    continue
                v = jnp.where(mid, v_hi[b:b + 1], v_lo[a:a + 1])
                tile = pltpu.roll(jnp.broadcast_to(v, (GRID_W, SLOT)), 0, axis=1, stride=1, stride_axis=0)
                ok = in_col
                if not ok_b:
                    ok = ok & first
                if not ok_a:
                    ok = ok & jnp.logical_not(first)
                out_ref[k, rs, cs] = jnp.where(ok, tile, NEG_INF).astype(out_ref.dtype)


def _na_bias_table(rpb, n_rows):
    n_dr, n_dc = rpb.shape[-2:]
    row_ok, row_dr = _na_row_classes(n_rows)
    rpb_p = jnp.pad(rpb, ((0, 0), (0, 0), (0, 0), (0, SLOT - n_dc)))
    tq, tk = NA_BLOCK_ROWS * GRID_W, NA_SLAB_ROWS * GRID_W
    return pl.pallas_call(
        functools.partial(_bias_kernel, row_ok=row_ok, row_dr=row_dr),
        grid=(DEPTH, NA_HEADS),
        in_specs=[pl.BlockSpec((None, None, n_dr, SLOT), lambda l, h: (l, h, 0, 0))],
        out_specs=pl.BlockSpec((None, 3, None, tq, tk), lambda l, h: (l, 0, h, 0, 0)),
        out_shape=jax.ShapeDtypeStruct((DEPTH, 3, NA_HEADS, tq, tk), BF16),
        compiler_params=_cparams(("arbitrary", "arbitrary")),
        name="na_bias_table",
    )(rpb_p)


def _prep(p):
    L = DEPTH
    w_in = p["w_in"]
    w_in_p = jnp.pad(w_in.astype(BF16), ((0, 0), (0, 0), (0, C_END - w_in.shape[-1])))
    w_uq = jnp.pad(p["mla_w_uq"].reshape(L, MLA_Q_LORA, MLA_HEADS, MLA_QK_DIM),
                   ((0, 0), (0, 0), (0, 0), (0, SLOT - MLA_QK_DIM))).reshape(L, MLA_Q_LORA, MLA_HEADS * SLOT)
    ukv = p["mla_w_ukv"].reshape(L, MLA_KV_LORA, MLA_HEADS, MLA_NOPE_DIM + MLA_V_DIM)
    uk = jnp.pad(ukv[..., :MLA_NOPE_DIM], ((0, 0), (0, 0), (0, 0), (0, SLOT - MLA_NOPE_DIM)))
    w_ukv = jnp.concatenate([uk.reshape(L, MLA_KV_LORA, MLA_HEADS * SLOT),
                             ukv[..., MLA_NOPE_DIM:].reshape(L, MLA_KV_LORA, MLA_HEADS * MLA_V_DIM)], axis=-1)
    w_pool_bd = jnp.einsum("lgcd,gh->lgchd", p["w_pool"], jnp.eye(4, dtype=F32)).reshape(L, POOL_WIDTH, POOL_WIDTH)
    w_r = jnp.concatenate([p["w_group_router"], p["w_expert_router"]], axis=-1)
    r_hi = w_r.astype(BF16)
    r_lo = (w_r - r_hi.astype(F32)).astype(BF16)
    padl = lambda a: jnp.pad(a, ((0, 0), (0, 0), (0, LANES - a.shape[-1])))
    vec = lambda a: a.reshape(L, 1, a.shape[-1])
    return {
        "w_in": w_in_p,
        "w_uq": w_uq.astype(BF16),
        "w_ukv": w_ukv.astype(BF16),
        "w_pool_bd": w_pool_bd.astype(BF16),
        "w_branch": p["w_branch"].astype(BF16),
        "w_out": p["w_out"].astype(BF16),
        "w_r2": jnp.concatenate([padl(r_hi), padl(r_lo)], axis=-1),
        "w_r1": padl(r_hi),
        "w1": p["w_exp_gate"].astype(BF16),
        "w3": p["w_exp_up"].astype(BF16),
        "w2": p["w_exp_down"].astype(BF16).reshape(L, N_GROUPS, EXPERTS_PER_GROUP * EXPERT_FF, D_MODEL),
        "g_pre_mix": vec(p["g_pre_mix"]), "g_post_mix": vec(p["g_post_mix"]),
        "g_pre_ffn": vec(p["g_pre_ffn"]), "g_post_ffn": vec(p["g_post_ffn"]),
        "pool_scale": vec(p["pool_scale"]), "q_norm": vec(p["mla_q_norm"]), "kv_norm": vec(p["mla_kv_norm"]),
    }


def kernel(x_prompt, x_sample, cache_na_k, cache_na_v, cache_mla_ckv, cache_mla_kpe, c, c_ctx, w_mod, b_mod,
           g_pre_mix, g_post_mix, g_pre_ffn, g_post_ffn, w_in, w_pool, pool_scale, na_rpb, mla_q_norm, mla_w_uq,
           mla_kv_norm, mla_w_ukv, w_branch, w_out, w_group_router, w_expert_router, w_exp_gate, w_exp_up,
           w_exp_down):
    batch, seq, _ = x_prompt.shape
    dbatch, dseq, _ = x_sample.shape
    past = cache_na_k.shape[2]
    assert dbatch + 1 <= 8 and dseq % (NA_BLOCK_ROWS * GRID_W) == 0

    w = _prep(dict(w_in=w_in, w_pool=w_pool, pool_scale=pool_scale, mla_q_norm=mla_q_norm, mla_w_uq=mla_w_uq,
                   mla_kv_norm=mla_kv_norm, mla_w_ukv=mla_w_ukv, w_branch=w_branch, w_out=w_out,
                   w_group_router=w_group_router, w_expert_router=w_expert_router, w_exp_gate=w_exp_gate,
                   w_exp_up=w_exp_up, w_exp_down=w_exp_down, g_pre_mix=g_pre_mix, g_post_mix=g_post_mix,
                   g_pre_ffn=g_pre_ffn, g_post_ffn=g_post_ffn))
    rope_tab = _rope_table(dseq)
    bias_tab = _na_bias_table(na_rpb, dseq // GRID_W)
    place = jnp.tile(jnp.pad(jnp.eye(MLA_ROPE_DIM, dtype=F32), ((0, 0), (KPE_LANE, SLOT - KPE_LANE - MLA_ROPE_DIM))),
                     (1, MLA_HEADS)).astype(BF16)

    cvec = jnp.concatenate([c_ctx[None], c, jnp.zeros((8 - 1 - dbatch, D_MODEL), F32)], axis=0)
    mod5 = _modulation(cvec, w_mod, b_mod).reshape(DEPTH, 8, 6, 1, D_MODEL)

    ck = cache_na_k.reshape(dbatch, DEPTH, past, NA_WIDTH)
    cv = cache_na_v.reshape(dbatch, DEPTH, past, NA_WIDTH)

    xp = x_prompt.reshape(batch * seq, D_MODEL)
    assert DEPTH == 2
    caches = None
    for l in range(DEPTH):
        u, nq, nvo, qm, km, vmo, gate, *caches = _inproj(xp, mod5, l, w, seq=seq, rope_tab=None, ctx_out=True,
                                                         tm=seq, prev=caches)
        yna, ymla = _attn_ctx(nq, caches[0], nvo, qm, km, vmo, seq=seq, nk_layer=l if l > 0 else None)
        xp, h2, rg = _merge(xp, u, yna, ymla, gate, mod5, l, w, seq=seq, ctx=True, tm=2 * seq)
        xp = _moe(h2, rg, xp, mod5, l, w, seq=seq, ctx=True)
    new_nk, new_nv, new_ckv, new_kpe = caches

    xs = x_sample.reshape(dbatch * dseq, D_MODEL)
    for l in range(DEPTH):
        u, nq, nvo, qm, km, vmo, gate, nk = _inproj(xs, mod5, l, w, seq=dseq, rope_tab=rope_tab, ctx_out=False, tm=512)
        ckm, cvmo, cnvo = _cache_prep(cache_mla_ckv, cache_mla_kpe, cv, w["w_ukv"], place, l)
        yna = _attn_na(nq, nk, nvo, ck, cnvo, bias_tab, l, batch=dbatch, seq=dseq)
        ymla = _attn_mla(qm, km, vmo, ckm, cvmo, batch=dbatch, seq=dseq, tq=512)
        xs, h2, rg = _merge(xs, u, yna, ymla, gate, mod5, l, w, seq=dseq, ctx=False, tm=512)
        xs = _moe(h2, rg, xs, mod5, l, w, seq=dseq, ctx=False)

    heads = (batch, DEPTH, seq, NA_HEADS, NA_HEAD_DIM)
    return (xp.reshape(batch, seq, D_MODEL), xs.reshape(dbatch, dseq, D_MODEL),
            new_nk.reshape(heads), new_nv.reshape(heads), new_ckv, new_kpe)
```

```python
import functools

import numpy as np
import jax
import jax.numpy as jnp
from jax import lax
from jax.experimental import pallas as pl
from jax.experimental.pallas import tpu as pltpu

F32 = jnp.float32
BF16 = jnp.bfloat16

D_MODEL = 1024
DEPTH = 2
GRID_W = 64
POOL_WINDOWS = (2, 4, 8, 16)
POOL_GC = 128
POOL_WIDTH = 512
NA_HEADS = 8
NA_HEAD_DIM = 64
NA_WIDTH = 512
NA_WIN_ROWS = 8
NA_WIN_COLS = 16
MLA_HEADS = 8
MLA_NOPE_DIM = 64
MLA_ROPE_DIM = 32
MLA_V_DIM = 64
MLA_QK_DIM = 96
MLA_Q_LORA = 384
MLA_KV_LORA = 256
N_GROUPS = 4
EXPERTS_PER_GROUP = 4
N_EXPERTS = 16
EXPERT_FF = 256
ROPE_BASE = 10000.0
EPS = 1e-6
NEG_INF = -1e30

LANES = 128
SLOT = LANES
VMEM_LIMIT = 56 * 1024 * 1024

C_U, C_NQ, C_NK, C_NV, C_CQ, C_CKV, C_KPE, C_G, C_END = 0, 512, 1024, 1536, 2048, 2432, 2688, 2720, 5888
G_WIDTH = 3 * D_MODEL
KPE_LANE = MLA_NOPE_DIM

LOG2E = 1.4426950408889634
VO_W = 2 * SLOT
VO_ALL = (NA_HEADS // 2) * VO_W

NA_BLOCK_ROWS = 4
NA_SLAB_ROWS = 12


def _cparams(sem):
    return pltpu.CompilerParams(dimension_semantics=sem, vmem_limit_bytes=VMEM_LIMIT)


def _rms(x, g):
    return x * lax.rsqrt(jnp.mean(x * x, axis=-1, keepdims=True) + EPS) * g


def _dot(a, b):
    return jnp.dot(a, b, preferred_element_type=F32)


def _dot_nt(a, b):
    return lax.dot_general(a, b, (((1,), (1,)), ((), ())), preferred_element_type=F32)


def _mod_kernel(c_ref, w_ref, b_ref, o_ref):
    c = c_ref[...]
    s = (c * jax.nn.sigmoid(c)).astype(BF16)
    o_ref[...] = _dot(s, w_ref[...].astype(BF16)) + b_ref[...]


def _modulation(cvec, w_mod, b_mod):
    tn = 1536
    n = w_mod.shape[-1]
    return pl.pallas_call(
        _mod_kernel,
        grid=(DEPTH, n // tn),
        in_specs=[
            pl.BlockSpec((8, D_MODEL), lambda l, j: (0, 0)),
            pl.BlockSpec((None, D_MODEL, tn), lambda l, j: (l, 0, j)),
            pl.BlockSpec((None, 1, tn), lambda l, j: (l, 0, j)),
        ],
        out_specs=pl.BlockSpec((None, 8, tn), lambda l, j: (l, 0, j)),
        out_shape=jax.ShapeDtypeStruct((DEPTH, 8, n), F32),
        compiler_params=_cparams(("arbitrary", "arbitrary")),
        name="modulation",
    )(cvec, w_mod, b_mod.reshape(DEPTH, 1, n))


def _rope_slot(x, tab):
    c, s_up, s_dn = tab[:, 0:SLOT], tab[:, SLOT:2 * SLOT], tab[:, 2 * SLOT:3 * SLOT]
    half = MLA_ROPE_DIM // 4
    x_up = pltpu.roll(x, SLOT - half, axis=1)
    x_dn = pltpu.roll(x, half, axis=1)
    return x * c + x_up * s_up + x_dn * s_dn


def _store_vo(vo_ref, v):
    tm = v.shape[0]
    for p in range(NA_HEADS // 2):
        vo_ref[:, p * VO_W:p * VO_W + SLOT] = v[:, p * SLOT:(p + 1) * SLOT].astype(BF16)
        vo_ref[:, p * VO_W + SLOT:(p + 1) * VO_W] = jnp.ones((tm, SLOT), BF16)


def _realign_gate_columns(win_ref, wg_ref, wqk_ref):
    off = C_G % LANES
    lane = lax.broadcasted_iota(jnp.int32, (1, LANES), 1)
    base = C_G - off
    nxt = win_ref[:, base:base + LANES].astype(F32)
    kpe = pltpu.roll(nxt, KPE_LANE - (C_KPE - base), axis=1)
    wqk_ref[:, :MLA_Q_LORA] = win_ref[:, C_CQ:C_CKV]
    wqk_ref[:, MLA_Q_LORA:] = jnp.where((lane >= KPE_LANE) & (lane < KPE_LANE + MLA_ROPE_DIM), kpe, 0.0).astype(BF16)
    for j in range(G_WIDTH // LANES):
        cur = nxt
        nxt = win_ref[:, base + (j + 1) * LANES:base + (j + 2) * LANES].astype(F32)
        tile = pltpu.roll(jnp.where(lane >= off, cur, nxt), LANES - off, axis=1)
        wg_ref[:, j * LANES:(j + 1) * LANES] = tile.astype(BF16)


def _inproj_kernel(*refs, rope, ctx_out, stacked):
    n_in = 9 + (1 if rope else 0) + (4 if stacked else 0)
    ins, outs, (wg_ref, wqk_ref) = refs[:n_in], refs[n_in:-2], refs[-2:]
    x_ref, sh_ref, sc_ref, g_ref, win_ref, qn_ref, wuq_ref, kvn_ref, wukv_ref = ins[:9]
    tab_ref = ins[9] if rope else None
    prev = ins[n_in - 4:] if stacked else None
    if ctx_out:
        u_ref, nq_ref, nvo_ref, qm_ref, km_ref, vmo_ref, gate_ref, nk_ref, nv_ref, ckv_ref, kpe_ref = outs
    else:
        u_ref, nq_ref, nvo_ref, qm_ref, km_ref, vmo_ref, gate_ref, nk_ref = outs

    def put(ref, k, val):
        if stacked:
            ref[0] = prev[k][...]
            ref[1] = val
        else:
            ref[...] = val

    @pl.when(pl.program_id(0) == 0)
    def _():
        _realign_gate_columns(win_ref, wg_ref, wqk_ref)

    x = x_ref[...]
    h = (_rms(x, g_ref[...]) * (1.0 + sc_ref[...]) + sh_ref[...]).astype(BF16)

    def seg(a, b):
        return _dot(h, win_ref[:, a:b])

    u_ref[...] = seg(C_U, C_NQ).astype(u_ref.dtype)
    nq_ref[...] = (seg(C_NQ, C_NK) * (NA_HEAD_DIM ** -0.5 * LOG2E)).astype(BF16)
    nk = seg(C_NK, C_NV)
    nv = seg(C_NV, C_CQ)
    if ctx_out:
        put(nk_ref, 0, nk)
        put(nv_ref, 1, nv)
    else:
        nk_ref[...] = nk.astype(BF16)
    _store_vo(nvo_ref, nv)
    gate_ref[...] = jax.nn.sigmoid(_dot(h, wg_ref[...])).astype(BF16)

    cqk = _dot(h, wqk_ref[...])
    cq = _rms(cqk[:, :MLA_Q_LORA], qn_ref[...]).astype(BF16)
    q = _dot(cq, wuq_ref[...])
    tab = tab_ref[...] if rope else None
    for hd in range(MLA_HEADS):
        qs = q[:, hd * SLOT:(hd + 1) * SLOT]
        if rope:
            qs = _rope_slot(qs, tab)
        qm_ref[:, hd * SLOT:(hd + 1) * SLOT] = (qs * (MLA_QK_DIM ** -0.5 * LOG2E)).astype(BF16)

    ckv = _rms(seg(C_CKV, C_KPE), kvn_ref[...])
    kpe_slot = cqk[:, MLA_Q_LORA:]
    if ctx_out:
        put(ckv_ref, 2, ckv)
        put(kpe_ref, 3, kpe_slot[:, KPE_LANE:KPE_LANE + MLA_ROPE_DIM])
    if rope:
        kpe_slot = _rope_slot(kpe_slot, tab)
    kv = _dot(ckv.astype(BF16), wukv_ref[...])
    for hd in range(MLA_HEADS):
        km_ref[:, hd * SLOT:(hd + 1) * SLOT] = (kv[:, hd * SLOT:(hd + 1) * SLOT] + kpe_slot).astype(BF16)
    _store_vo(vmo_ref, kv[:, MLA_HEADS * SLOT:])


def _inproj(x, mod5, l, w, *, seq, rope_tab, ctx_out, tm, prev=None):
    n = x.shape[0]
    steps_per_seq = seq // tm if not ctx_out else 0
    stacked = prev is not None
    assert not stacked or (ctx_out and tm == seq and l == 1)

    def mrow(i):
        return 0 if ctx_out else 1 + i // steps_per_seq

    const = lambda i: (l, 0, 0)
    in_specs = [
        pl.BlockSpec((tm, D_MODEL), lambda i: (i, 0)),
        pl.BlockSpec((None, None, None, 1, D_MODEL), lambda i: (l, mrow(i), 0, 0, 0)),
        pl.BlockSpec((None, None, None, 1, D_MODEL), lambda i: (l, mrow(i), 1, 0, 0)),
        pl.BlockSpec((None, 1, D_MODEL), const),
        pl.BlockSpec((None, D_MODEL, C_END), const, pipeline_mode=pl.Buffered(1)),
        pl.BlockSpec((None, 1, MLA_Q_LORA), const),
        pl.BlockSpec((None, MLA_Q_LORA, MLA_HEADS * SLOT), const, pipeline_mode=pl.Buffered(1)),
        pl.BlockSpec((None, 1, MLA_KV_LORA), const),
        pl.BlockSpec((None, MLA_KV_LORA, MLA_HEADS * SLOT + 512), const, pipeline_mode=pl.Buffered(1)),
    ]
    args = [x, mod5, mod5, w["g_pre_mix"], w["w_in"], w["q_norm"], w["w_uq"], w["kv_norm"], w["w_ukv"]]
    rope = rope_tab is not None
    if rope:
        in_specs.append(pl.BlockSpec((tm, 3 * SLOT), lambda i: (i % steps_per_seq, 0)))
        args.append(rope_tab)
    row = lambda wd: pl.BlockSpec((tm, wd), lambda i: (i, 0))
    rows_of = lambda wd, dt: jax.ShapeDtypeStruct((n, wd), dt)
    widths = [512, 512, VO_ALL, MLA_HEADS * SLOT, MLA_HEADS * SLOT, VO_ALL, 3 * D_MODEL]
    out_specs = [row(wd) for wd in widths]
    out_shape = [rows_of(wd, BF16) for wd in widths]
    cache_w = [512, 512, MLA_KV_LORA, MLA_ROPE_DIM] if ctx_out else []
    if not ctx_out:
        out_specs.append(row(512))
        out_shape.append(rows_of(512, BF16))
    elif stacked:
        in_specs += [row(wd) for wd in cache_w]
        args += list(prev)
        out_specs += [pl.BlockSpec((None, DEPTH, seq, wd), lambda i: (i, 0, 0, 0)) for wd in cache_w]
        out_shape += [jax.ShapeDtypeStruct((n // seq, DEPTH, seq, wd), F32) for wd in cache_w]
    else:
        out_specs += [row(wd) for wd in cache_w]
        out_shape += [rows_of(wd, F32) for wd in cache_w]
    return pl.pallas_call(
        functools.partial(_inproj_kernel, rope=rope, ctx_out=ctx_out, stacked=stacked),
        grid=(n // tm,),
        in_specs=in_specs,
        out_specs=out_specs,
        out_shape=out_shape,
        scratch_shapes=[pltpu.VMEM((D_MODEL, G_WIDTH), BF16), pltpu.VMEM((D_MODEL, MLA_Q_LORA + SLOT), BF16)],
        compiler_params=_cparams(("arbitrary",)),
        name="inproj_ctx" if ctx_out else "inproj_lat",
    )(*args)


def _attend(q_h, segs):
    scores = []
    for k, _, bias in segs:
        s = _dot_nt(q_h, k)
        if bias is not None:
            s = s + bias
        scores.append(s)
    m = scores[0].max(axis=-1, keepdims=True)
    for s in scores[1:]:
        m = jnp.maximum(m, s.max(axis=-1, keepdims=True))
    acc = None
    for s, (_, vo, _) in zip(scores, segs):
        o = _dot(jnp.exp2(s - m).astype(BF16), vo)
        acc = o if acc is None else acc + o
    return acc[:, :SLOT] / acc[:, SLOT:]


def _attn_heads(out_ref, q_of, segs_of):
    lo = lax.broadcasted_iota(jnp.int32, (1, SLOT), 1) < NA_HEAD_DIM
    for p in range(NA_HEADS // 2):
        o0, o1 = [_attend(q_of(2 * p + e), segs_of(2 * p + e)) for e in range(2)]
        out_ref[:, p * SLOT:(p + 1) * SLOT] = jnp.where(lo, o0, o1).astype(out_ref.dtype)


def _pair_q(q_ref):
    lo = lax.broadcasted_iota(jnp.int32, (1, SLOT), 1) < NA_HEAD_DIM

    def q_of(h):
        q_pair = q_ref[:, (h // 2) * SLOT:(h // 2 + 1) * SLOT]
        keep = lo if h % 2 == 0 else jnp.logical_not(lo)
        return jnp.where(keep, q_pair, jnp.zeros_like(q_pair))
    return q_of


def _slot(ref, i, rows=slice(None)):
    return ref[rows, i * SLOT:(i + 1) * SLOT].astype(BF16)


def _vo(ref, h, rows=slice(None)):
    return ref[rows, (h // 2) * VO_W:(h // 2 + 1) * VO_W]


def _attn_ctx_kernel(nq_ref, nk_ref, nvo_ref, qm_ref, km_ref, vmo_ref, yna_ref, ymla_ref):
    _attn_heads(yna_ref, _pair_q(nq_ref), lambda h: [(_slot(nk_ref, h // 2), _vo(nvo_ref, h), None)])
    _attn_heads(ymla_ref, lambda h: _slot(qm_ref, h), lambda h: [(_slot(km_ref, h), _vo(vmo_ref, h), None)])


def _attn_ctx(nq, nk, nvo, qm, km, vmo, *, seq, nk_layer=None):
    n = nq.shape[0]
    spec = lambda wd: pl.BlockSpec((seq, wd), lambda b: (b, 0))
    nk_spec = spec(512) if nk_layer is None else pl.BlockSpec((None, None, seq, 512), lambda b: (b, nk_layer, 0, 0))
    return pl.pallas_call(
        _attn_ctx_kernel,
        grid=(n // seq,),
        in_specs=[spec(512), nk_spec, spec(VO_ALL), spec(1024), spec(1024), spec(VO_ALL)],
        out_specs=[spec(512), spec(512)],
        out_shape=[jax.ShapeDtypeStruct((n, 512), BF16)] * 2,
        compiler_params=_cparams(("arbitrary",)),
        name="attn_ctx",
    )(nq, nk, nvo, qm, km, vmo)


def _attn_na_kernel(nq_ref, nk_ref, nvo_ref, ck_ref, cvo_ref, bias_ref, out_ref, *, n_blocks):
    i = pl.program_id(1)
    ks = jnp.clip(NA_BLOCK_ROWS * i - NA_WIN_ROWS // 2, 0, n_blocks * NA_BLOCK_ROWS - NA_SLAB_ROWS)
    slab = pl.ds(pl.multiple_of(ks * GRID_W, GRID_W), NA_SLAB_ROWS * GRID_W)

    def segs_of(h):
        return [(_slot(nk_ref, h // 2, slab), _vo(nvo_ref, h, slab), bias_ref[h].astype(F32)),
                (_slot(ck_ref, h // 2), _vo(cvo_ref, h), None)]

    _attn_heads(out_ref, _pair_q(nq_ref), segs_of)


def _attn_na(nq, nk, nvo, cache_k, cvo, bias_tab, l, *, batch, seq):
    tq = NA_BLOCK_ROWS * GRID_W
    nb = seq // tq
    past = cache_k.shape[2]

    def cls(i):
        return jnp.where(i == 0, 0, jnp.where(i == nb - 1, 2, 1))

    return pl.pallas_call(
        functools.partial(_attn_na_kernel, n_blocks=nb),
        grid=(batch, nb),
        in_specs=[
            pl.BlockSpec((tq, 512), lambda b, i: (b * nb + i, 0)),
            pl.BlockSpec((seq, 512), lambda b, i: (b, 0)),
            pl.BlockSpec((seq, VO_ALL), lambda b, i: (b, 0)),
            pl.BlockSpec((None, None, past, 512), lambda b, i: (b, l, 0, 0)),
            pl.BlockSpec((None, past, VO_ALL), lambda b, i: (b, 0, 0)),
            pl.BlockSpec((None, None, NA_HEADS, tq, NA_SLAB_ROWS * GRID_W), lambda b, i: (l, cls(i), 0, 0, 0)),
        ],
        out_specs=pl.BlockSpec((tq, 512), lambda b, i: (b * nb + i, 0)),
        out_shape=jax.ShapeDtypeStruct((batch * seq, 512), BF16),
        compiler_params=_cparams(("arbitrary", "arbitrary")),
        name="attn_na_lat",
    )(nq, nk, nvo, cache_k, cvo, bias_tab)


def _cache_prep_kernel(ckv_ref, kpe_ref, nv_ref, wukv_ref, place_ref, km_ref, vmo_ref, nvo_ref):
    kv = _dot(ckv_ref[...].astype(BF16), wukv_ref[...])
    kslots = kv[:, :MLA_HEADS * SLOT] + _dot(kpe_ref[...].astype(BF16), place_ref[...])
    km_ref[...] = kslots.astype(BF16)
    _store_vo(vmo_ref, kv[:, MLA_HEADS * SLOT:])
    _store_vo(nvo_ref, nv_ref[...])


def _cache_prep(cache_ckv, cache_kpe, cache_nv, w_ukv, place, l):
    batch, _, past, _ = cache_ckv.shape
    tout = pl.BlockSpec((None, past, VO_ALL), lambda b: (b, 0, 0))
    tshape = jax.ShapeDtypeStruct((batch, past, VO_ALL), BF16)
    return pl.pallas_call(
        _cache_prep_kernel,
        grid=(batch,),
        in_specs=[
            pl.BlockSpec((None, None, past, MLA_KV_LORA), lambda b: (b, l, 0, 0)),
            pl.BlockSpec((None, None, past, MLA_ROPE_DIM), lambda b: (b, l, 0, 0)),
            pl.BlockSpec((None, None, past, 512), lambda b: (b, l, 0, 0)),
            pl.BlockSpec((None, MLA_KV_LORA, MLA_HEADS * SLOT + 512), lambda b: (l, 0, 0)),
            pl.BlockSpec((MLA_ROPE_DIM, MLA_HEADS * SLOT), lambda b: (0, 0)),
        ],
        out_specs=[pl.BlockSpec((None, past, MLA_HEADS * SLOT), lambda b: (b, 0, 0)), tout, tout],
        out_shape=[jax.ShapeDtypeStruct((batch, past, MLA_HEADS * SLOT), BF16), tshape, tshape],
        compiler_params=_cparams(("arbitrary",)),
        name="cache_prep",
    )(cache_ckv, cache_kpe, cache_nv, w_ukv, place)


def _attn_mla_kernel(qm_ref, km_ref, vmo_ref, ckm_ref, cvmo_ref, out_ref):
    _attn_heads(out_ref, lambda h: _slot(qm_ref, h),
                lambda h: [(_slot(km_ref, h), _vo(vmo_ref, h), None), (_slot(ckm_ref, h), _vo(cvmo_ref, h), None)])


def _attn_mla(qm, km, vmo, ckm, cvmo, *, batch, seq, tq):
    nb = seq // tq
    past = ckm.shape[1]
    return pl.pallas_call(
        _attn_mla_kernel,
        grid=(batch, nb),
        in_specs=[
            pl.BlockSpec((tq, MLA_HEADS * SLOT), lambda b, i: (b * nb + i, 0)),
            pl.BlockSpec((seq, MLA_HEADS * SLOT), lambda b, i: (b, 0)),
            pl.BlockSpec((seq, VO_ALL), lambda b, i: (b, 0)),
            pl.BlockSpec((None, past, MLA_HEADS * SLOT), lambda b, i: (b, 0, 0)),
            pl.BlockSpec((None, past, VO_ALL), lambda b, i: (b, 0, 0)),
        ],
        out_specs=pl.BlockSpec((tq, 512), lambda b, i: (b * nb + i, 0)),
        out_shape=jax.ShapeDtypeStruct((batch * seq, 512), BF16),
        compiler_params=_cparams(("arbitrary", "arbitrary")),
        name="attn_mla_lat",
    )(qm, km, vmo, ckm, cvmo)


POOL_HALO = 16


GROUP_LANE = N_EXPERTS


def _route(logits):
    lane = lax.broadcasted_iota(jnp.int32, logits.shape, 1).astype(F32)
    big = jnp.float32(1 << 20)
    is_g = lane < N_GROUPS
    gl = jnp.where(is_g, logits, NEG_INF)
    gmax = gl.max(axis=-1, keepdims=True)
    ge = jnp.where(is_g, jnp.exp(gl - gmax), 0.0)
    gp = ge / ge.sum(axis=-1, keepdims=True)
    g_w = jnp.where(is_g, gp, -1.0).max(axis=-1, keepdims=True)
    g_idx = jnp.where(is_g & (gp == g_w), lane, big).min(axis=-1, keepdims=True)
    e_lane = lane - N_GROUPS
    in_grp = (e_lane >= g_idx * EXPERTS_PER_GROUP) & (e_lane < (g_idx + 1) * EXPERTS_PER_GROUP)
    el = jnp.where(in_grp, logits, NEG_INF)
    emax = el.max(axis=-1, keepdims=True)
    ee = jnp.where(in_grp, jnp.exp(el - emax), 0.0)
    ep = ee / ee.sum(axis=-1, keepdims=True)
    p1 = jnp.where(in_grp, ep, -1.0).max(axis=-1, keepdims=True)
    i1 = jnp.where(in_grp & (ep == p1), lane, big).min(axis=-1, keepdims=True)
    rest = in_grp & (lane != i1)
    p2 = jnp.where(rest, ep, -1.0).max(axis=-1, keepdims=True)
    i2 = jnp.where(rest & (ep == p2), lane, big).min(axis=-1, keepdims=True)
    tot = p1 + p2
    gates = jnp.where(lane == i1, g_w * (p1 / tot), 0.0) + jnp.where(lane == i2, g_w * (p2 / tot), 0.0)
    return jnp.where(lane == GROUP_LANE, g_idx, pltpu.roll(gates, LANES - N_GROUPS, axis=1))


def _merge_kernel(x_ref, u_ref, yna_ref, ymla_ref, gate_ref, gt1_ref, sh2_ref, sc2_ref, gpost_ref, gpre_ref,
                  wpool_ref, pscale_ref, wbr_ref, wout_ref, wr2_ref, wr1_ref,
                  xo_ref, h2_ref, rg_ref, *, seq, tm, steps_per_seq):
    def pool_diff(uwin, utile, r0, s0):
        rows, nwin = utile.shape[0], uwin.shape[0]
        t = r0 + lax.broadcasted_iota(jnp.int32, (rows, nwin), 0)
        s = s0 + lax.broadcasted_iota(jnp.int32, (rows, nwin), 1)
        t1 = r0 + lax.broadcasted_iota(jnp.int32, (rows, 1), 0)
        ds = []
        for g, w in enumerate(POOL_WINDOWS):
            lo = jnp.maximum(t - w // 2, 0)
            hi = jnp.minimum(t + (w - w // 2), seq)
            band = ((s >= lo) & (s < hi)).astype(BF16)
            cnt = (jnp.minimum(t1 + (w - w // 2), seq) - jnp.maximum(t1 - w // 2, 0)).astype(F32)
            sl = slice(g * POOL_GC, (g + 1) * POOL_GC)
            ds.append(_dot(band, uwin[:, sl]) / cnt - utile[:, sl].astype(F32))
        return jnp.concatenate(ds, axis=-1).astype(BF16)

    if steps_per_seq == 0:
        d = jnp.concatenate([pool_diff(u_ref[si], u_ref[si], 0, 0) for si in range(tm // seq)], axis=0)
    else:
        nwin = min(tm + 2 * POOL_HALO, seq)
        r0 = (pl.program_id(0) % steps_per_seq) * tm
        s0 = pl.multiple_of(jnp.clip(r0 - POOL_HALO, 0, seq - nwin), POOL_HALO)
        d = pool_diff(u_ref[0, pl.ds(s0, nwin), :], u_ref[0, pl.ds(pl.multiple_of(r0, POOL_HALO), tm), :], r0, s0)
    y_pool = (_dot(d, wpool_ref[...]) * pscale_ref[...]).astype(BF16)

    g = gate_ref[...]
    merged = (g[:, 0:D_MODEL].astype(F32) * _dot(y_pool, wbr_ref[0])
              + g[:, D_MODEL:2 * D_MODEL].astype(F32) * _dot(yna_ref[...], wbr_ref[1])
              + g[:, 2 * D_MODEL:].astype(F32) * _dot(ymla_ref[...], wbr_ref[2]))
    y = _dot(merged.astype(BF16), wout_ref[...])
    xn = x_ref[...] + gt1_ref[...] * _rms(y, gpost_ref[...])
    xo_ref[...] = xn

    h2 = _rms(xn, gpre_ref[...]) * (1.0 + sc2_ref[...]) + sh2_ref[...]
    h_hi = h2.astype(BF16)
    h2_ref[...] = h_hi
    h_lo = (h2 - h_hi.astype(F32)).astype(BF16)
    two = _dot(h_hi, wr2_ref[...])
    logits = two[:, :LANES] + two[:, LANES:] + _dot(h_lo, wr1_ref[...])
    rg_ref[...] = _route(logits)


def _merge(x, u, yna, ymla, gate, mod5, l, w, *, seq, ctx, tm):
    n = x.shape[0]
    nseq = n // seq
    if tm >= seq:
        assert ctx and tm % seq == 0
        steps_per_seq = 0
        u_spec = pl.BlockSpec((tm // seq, seq, 512), lambda i: (i, 0, 0))
    else:
        steps_per_seq = seq // tm
        u_spec = pl.BlockSpec((1, seq, 512), lambda i: (i // steps_per_seq, 0, 0))

    def mrow(i):
        return 0 if ctx else 1 + i // steps_per_seq

    const = lambda i: (l, 0, 0)
    modspec = lambda k: pl.BlockSpec((None, None, None, 1, D_MODEL), lambda i: (l, mrow(i), k, 0, 0))
    row = lambda wd: pl.BlockSpec((tm, wd), lambda i: (i, 0))
    return pl.pallas_call(
        functools.partial(_merge_kernel, seq=seq, tm=tm, steps_per_seq=steps_per_seq),
        grid=(n // tm,),
        in_specs=[
            row(D_MODEL),
            u_spec,
            row(512), row(512), row(3 * D_MODEL),
            modspec(2), modspec(3), modspec(4),
            pl.BlockSpec((None, 1, D_MODEL), const),
            pl.BlockSpec((None, 1, D_MODEL), const),
            pl.BlockSpec((None, 512, 512), const),
            pl.BlockSpec((None, 1, 512), const),
            pl.BlockSpec((None, 3, 512, D_MODEL), lambda i: (l, 0, 0, 0)),
            pl.BlockSpec((None, D_MODEL, D_MODEL), const),
            pl.BlockSpec((None, D_MODEL, 2 * LANES), const),
            pl.BlockSpec((None, D_MODEL, LANES), const),
        ],
        out_specs=[row(D_MODEL), row(D_MODEL), row(LANES)],
        out_shape=[jax.ShapeDtypeStruct((n, D_MODEL), F32), jax.ShapeDtypeStruct((n, D_MODEL), BF16),
                   jax.ShapeDtypeStruct((n, LANES), F32)],
        compiler_params=_cparams(("arbitrary",)),
        name="merge_ctx" if ctx else "merge_lat",
    )(x, u.reshape(nseq, seq, 512), yna, ymla, gate, mod5, mod5, mod5, w["g_post_mix"], w["g_pre_ffn"],
      w["w_pool_bd"], w["pool_scale"], w["w_branch"], w["w_out"], w["w_r2"], w["w_r1"])


MOE_TM = 1024
MOE_SUB = 512
MOE_ROWS = 160
MOE_ROWS_PAD = 256


def _moe_kernel(h_ref, rg_ref, x_ref, gt2_ref, gpost_ref, w1_ref, w3_ref, w2_ref, o_ref):
    tm = h_ref.shape[0]
    subs = [slice(s * MOE_SUB, (s + 1) * MOE_SUB) for s in range(tm // MOE_SUB)]
    acc_ref = o_ref
    rg = rg_ref[...]
    lane = lax.broadcasted_iota(jnp.int32, rg.shape, 1)
    grp = jnp.where(lane == GROUP_LANE, rg, 0.0).sum(axis=-1, keepdims=True)
    member = ((lane.astype(F32) == grp) & (lane < N_GROUPS)).astype(F32)
    t_row = lax.broadcasted_iota(jnp.int32, (MOE_SUB, MOE_SUB), 0)
    t_col = lax.broadcasted_iota(jnp.int32, (MOE_SUB, MOE_SUB), 1)
    earlier, later = (t_col < t_row).astype(BF16), (t_row < t_col).astype(BF16)
    member_t = member.T[:8]
    rank_col = jnp.concatenate([_dot(earlier, member[sl].astype(BF16)) for sl in subs], axis=0)
    rank_row = jnp.concatenate([_dot(member_t[:, sl].astype(BF16), later) for sl in subs], axis=1)
    count = functools.reduce(jnp.maximum, [member[sl].sum(axis=0, keepdims=True) for sl in subs])
    rg_hi = rg.astype(BF16)
    rg_lo = (rg - rg_hi.astype(F32)).astype(BF16)
    row_id = lax.broadcasted_iota(jnp.int32, (MOE_ROWS, 1), 0).astype(F32)
    col_id = lax.broadcasted_iota(jnp.int32, (1, MOE_ROWS_PAD), 1).astype(F32)
    lane1 = lax.broadcasted_iota(jnp.int32, (1, LANES), 1)
    acc_ref[...] = jnp.zeros_like(acc_ref)

    for g in range(N_GROUPS):
        rr, mr = rank_row[g:g + 1, :], member_t[g:g + 1, :]
        rc, mc = rank_col[:, g:g + 1], member[:, g:g + 1]
        n_g = jnp.max(jnp.where(lane1 == g, count, 0.0)).astype(jnp.int32)

        def chunk(c, carry, g=g, rr=rr, mr=mr, rc=rc, mc=mc):
            base = (c * MOE_ROWS).astype(F32)
            hs, gs = [], []
            for sl in subs:
                take = ((rr[:, sl] - base == row_id) & (mr[:, sl] > 0.0)).astype(BF16)
                hs.append(_dot(take, h_ref[sl, :]).astype(BF16))
                gs.append(_dot(take, rg_hi[sl]) + _dot(take, rg_lo[sl]))
            hs = jnp.concatenate(hs, axis=0)
            gs = jnp.concatenate(gs, axis=0)
            hid = []
            for e in range(EXPERTS_PER_GROUP):
                k = g * EXPERTS_PER_GROUP + e
                a = _dot(hs, w1_ref[k])
                b = _dot(hs, w3_ref[k])
                hid.append(((a * jax.nn.sigmoid(a)) * b * gs[:, k:k + 1]).astype(BF16))
            part = _dot(jnp.concatenate(hid, axis=-1), w2_ref[g]).astype(BF16)
            pad = jnp.zeros((MOE_ROWS_PAD - MOE_ROWS, D_MODEL), BF16)
            for s, sl in enumerate(subs):
                rel = rc[sl] - base
                give = ((rel == col_id) & (rel < MOE_ROWS) & (mc[sl] > 0.0)).astype(BF16)
                rows = jnp.concatenate([part[s * MOE_ROWS:(s + 1) * MOE_ROWS], pad], axis=0)
                acc_ref[sl, :] += _dot(give, rows)
            return carry

        chunk(jnp.int32(0), 0)
        lax.fori_loop(1, (n_g + MOE_ROWS - 1) // MOE_ROWS, chunk, 0)

    o_ref[...] = x_ref[...] + gt2_ref[...] * _rms(acc_ref[...], gpost_ref[...])


def _moe(h2, rg, x, mod5, l, w, *, seq, ctx):
    n = x.shape[0]
    tm = MOE_TM
    steps_per_seq = seq // tm if not ctx else 0

    def mrow(i):
        return 0 if ctx else 1 + i // steps_per_seq

    row = lambda wd: pl.BlockSpec((tm, wd), lambda i: (i, 0))
    once = pl.Buffered(1)
    wspec = lambda a, b: pl.BlockSpec((None, N_EXPERTS, a, b), lambda i: (l, 0, 0, 0), pipeline_mode=once)
    return pl.pallas_call(
        _moe_kernel,
        grid=(n // tm,),
        in_specs=[
            row(D_MODEL), row(LANES), row(D_MODEL),
            pl.BlockSpec((None, None, None, 1, D_MODEL), lambda i: (l, mrow(i), 5, 0, 0)),
            pl.BlockSpec((None, 1, D_MODEL), lambda i: (l, 0, 0)),
            wspec(D_MODEL, EXPERT_FF), wspec(D_MODEL, EXPERT_FF),
            pl.BlockSpec((None, N_GROUPS, EXPERTS_PER_GROUP * EXPERT_FF, D_MODEL), lambda i: (l, 0, 0, 0),
                         pipeline_mode=once),
        ],
        out_specs=row(D_MODEL),
        out_shape=jax.ShapeDtypeStruct((n, D_MODEL), F32),
        compiler_params=_cparams(("arbitrary",)),
        name="moe_ctx" if ctx else "moe_lat",
    )(h2, rg, x, mod5, w["g_post_ffn"], w["w1"], w["w3"], w["w2"])


def _rope_table(seq):
    t = np.arange(seq)
    n_freq = MLA_ROPE_DIM // 4
    inv = jnp.asarray(ROPE_BASE, F32) ** (-jnp.arange(n_freq, dtype=F32) / n_freq)
    ang_r = jnp.asarray(t // GRID_W, F32)[:, None] * inv
    ang_c = jnp.asarray(t % GRID_W, F32)[:, None] * inv
    cos = jnp.cos(jnp.stack([ang_r, ang_c], axis=1))
    sin = jnp.sin(jnp.stack([ang_r, ang_c], axis=1))
    zeros = jnp.zeros_like(sin)
    c32 = jnp.stack([cos, cos], axis=2).reshape(seq, MLA_ROPE_DIM)
    up32 = jnp.stack([-sin, zeros], axis=2).reshape(seq, MLA_ROPE_DIM)
    dn32 = jnp.stack([zeros, sin], axis=2).reshape(seq, MLA_ROPE_DIM)

    def slot(v, fill):
        return jnp.concatenate([jnp.full((seq, KPE_LANE), fill, F32), v,
                                jnp.full((seq, SLOT - KPE_LANE - MLA_ROPE_DIM), fill, F32)], axis=1)

    return jnp.concatenate([slot(c32, 1.0), slot(up32, 0.0), slot(dn32, 0.0)], axis=1)


def _na_row_classes(n_rows):
    i = np.arange(NA_BLOCK_ROWS)[:, None]
    j = np.arange(NA_SLAB_ROWS)[None, :]
    last_ks = n_rows - NA_SLAB_ROWS
    last_q0 = n_rows - NA_BLOCK_ROWS
    specs = [
        (j < NA_WIN_ROWS, j - i),
        ((j >= i) & (j < i + NA_WIN_ROWS), j - NA_WIN_ROWS // 2 - i),
        (last_ks + j >= n_rows - NA_WIN_ROWS, last_ks + j - last_q0 - i),
    ]
    ok = np.stack([np.broadcast_to(o, (NA_BLOCK_ROWS, NA_SLAB_ROWS)) for o, _ in specs])
    dr = np.stack([np.clip(d + NA_WIN_ROWS - 1, 0, 2 * NA_WIN_ROWS - 2) for _, d in specs])
    return ok, dr


def _bias_kernel(rpb_ref, out_ref, *, row_ok, row_dr):
    reach = NA_WIN_COLS - 1
    rows = rpb_ref[...] * LOG2E
    v_lo = pltpu.roll(rows, SLOT - reach, axis=1)
    v_hi = pltpu.roll(rows, GRID_W - reach, axis=1)
    lane1 = lax.broadcasted_iota(jnp.int32, (1, SLOT), 1)
    mid = (lane1 >= GRID_W - reach) & (lane1 <= GRID_W + reach)
    c = lax.broadcasted_iota(jnp.int32, (GRID_W, SLOT), 0)
    lane = lax.broadcasted_iota(jnp.int32, (GRID_W, SLOT), 1)
    kc = lane & (GRID_W - 1)
    c0 = jnp.clip(c - NA_WIN_COLS // 2, 0, GRID_W - NA_WIN_COLS)
    in_col = (kc >= c0) & (kc < c0 + NA_WIN_COLS)
    first = lane < GRID_W
    for k in range(3):
        for i in range(NA_BLOCK_ROWS):
            for jp in range(NA_SLAB_ROWS // 2):
                ok_a, ok_b = bool(row_ok[k, i, 2 * jp]), bool(row_ok[k, i, 2 * jp + 1])
                a, b = int(row_dr[k, i, 2 * jp]), int(row_dr[k, i, 2 * jp + 1])
                rs = slice(i * GRID_W, (i + 1) * GRID_W)
                cs = slice(jp * SLOT, (jp + 1) * SLOT)
                if not (ok_a or ok_b):
                    out_ref[k, rs, cs] = jnp.full((GRID_W, SLOT), NEG_INF, out_ref.dtype)
                    continue
                v = jnp.where(mid, v_hi[b:b + 1], v_lo[a:a + 1])
                tile = pltpu.roll(jnp.broadcast_to(v, (GRID_W, SLOT)), 0, axis=1, stride=1, stride_axis=0)
                ok = in_col
                if not ok_b:
                    ok = ok & first
                if not ok_a:
                    ok = ok & jnp.logical_not(first)
                out_ref[k, rs, cs] = jnp.where(ok, tile, NEG_INF).astype(out_ref.dtype)


def _na_bias_table(rpb, n_rows):
    n_dr, n_dc = rpb.shape[-2:]
    row_ok, row_dr = _na_row_classes(n_rows)
    rpb_p = jnp.pad(rpb, ((0, 0), (0, 0), (0, 0), (0, SLOT - n_dc)))
    tq, tk = NA_BLOCK_ROWS * GRID_W, NA_SLAB_ROWS * GRID_W
    return pl.pallas_call(
        functools.partial(_bias_kernel, row_ok=row_ok, row_dr=row_dr),
        grid=(DEPTH, NA_HEADS),
        in_specs=[pl.BlockSpec((None, None, n_dr, SLOT), lambda l, h: (l, h, 0, 0))],
        out_specs=pl.BlockSpec((None, 3, None, tq, tk), lambda l, h: (l, 0, h, 0, 0)),
        out_shape=jax.ShapeDtypeStruct((DEPTH, 3, NA_HEADS, tq, tk), BF16),
        compiler_params=_cparams(("arbitrary", "arbitrary")),
        name="na_bias_table",
    )(rpb_p)


def _prep(p):
    L = DEPTH
    w_in = p["w_in"]
    w_in_p = jnp.pad(w_in.astype(BF16), ((0, 0), (0, 0), (0, C_END - w_in.shape[-1])))
    w_uq = jnp.pad(p["mla_w_uq"].reshape(L, MLA_Q_LORA, MLA_HEADS, MLA_QK_DIM),
                   ((0, 0), (0, 0), (0, 0), (0, SLOT - MLA_QK_DIM))).reshape(L, MLA_Q_LORA, MLA_HEADS * SLOT)
    ukv = p["mla_w_ukv"].reshape(L, MLA_KV_LORA, MLA_HEADS, MLA_NOPE_DIM + MLA_V_DIM)
    uk = jnp.pad(ukv[..., :MLA_NOPE_DIM], ((0, 0), (0, 0), (0, 0), (0, SLOT - MLA_NOPE_DIM)))
    w_ukv = jnp.concatenate([uk.reshape(L, MLA_KV_LORA, MLA_HEADS * SLOT),
                             ukv[..., MLA_NOPE_DIM:].reshape(L, MLA_KV_LORA, MLA_HEADS * MLA_V_DIM)], axis=-1)
    w_pool_bd = jnp.einsum("lgcd,gh->lgchd", p["w_pool"], jnp.eye(4, dtype=F32)).reshape(L, POOL_WIDTH, POOL_WIDTH)
    w_r = jnp.concatenate([p["w_group_router"], p["w_expert_router"]], axis=-1)
    r_hi = w_r.astype(BF16)
    r_lo = (w_r - r_hi.astype(F32)).astype(BF16)
    padl = lambda a: jnp.pad(a, ((0, 0), (0, 0), (0, LANES - a.shape[-1])))
    vec = lambda a: a.reshape(L, 1, a.shape[-1])
    return {
        "w_in": w_in_p,
        "w_uq": w_uq.astype(BF16),
        "w_ukv": w_ukv.astype(BF16),
        "w_pool_bd": w_pool_bd.astype(BF16),
        "w_branch": p["w_branch"].astype(BF16),
        "w_out": p["w_out"].astype(BF16),
        "w_r2": jnp.concatenate([padl(r_hi), padl(r_lo)], axis=-1),
        "w_r1": padl(r_hi),
        "w1": p["w_exp_gate"].astype(BF16),
        "w3": p["w_exp_up"].astype(BF16),
        "w2": p["w_exp_down"].astype(BF16).reshape(L, N_GROUPS, EXPERTS_PER_GROUP * EXPERT_FF, D_MODEL),
        "g_pre_mix": vec(p["g_pre_mix"]), "g_post_mix": vec(p["g_post_mix"]),
        "g_pre_ffn": vec(p["g_pre_ffn"]), "g_post_ffn": vec(p["g_post_ffn"]),
        "pool_scale": vec(p["pool_scale"]), "q_norm": vec(p["mla_q_norm"]), "kv_norm": vec(p["mla_kv_norm"]),
    }


def kernel(x_prompt, x_sample, cache_na_k, cache_na_v, cache_mla_ckv, cache_mla_kpe, c, c_ctx, w_mod, b_mod,
           g_pre_mix, g_post_mix, g_pre_ffn, g_post_ffn, w_in, w_pool, pool_scale, na_rpb, mla_q_norm, mla_w_uq,
           mla_kv_norm, mla_w_ukv, w_branch, w_out, w_group_router, w_expert_router, w_exp_gate, w_exp_up,
           w_exp_down):
    batch, seq, _ = x_prompt.shape
    dbatch, dseq, _ = x_sample.shape
    past = cache_na_k.shape[2]
    assert dbatch + 1 <= 8 and dseq % (NA_BLOCK_ROWS * GRID_W) == 0

    w = _prep(dict(w_in=w_in, w_pool=w_pool, pool_scale=pool_scale, mla_q_norm=mla_q_norm, mla_w_uq=mla_w_uq,
                   mla_kv_norm=mla_kv_norm, mla_w_ukv=mla_w_ukv, w_branch=w_branch, w_out=w_out,
                   w_group_router=w_group_router, w_expert_router=w_expert_router, w_exp_gate=w_exp_gate,
                   w_exp_up=w_exp_up, w_exp_down=w_exp_down, g_pre_mix=g_pre_mix, g_post_mix=g_post_mix,
                   g_pre_ffn=g_pre_ffn, g_post_ffn=g_post_ffn))
    rope_tab = _rope_table(dseq)
    bias_tab = _na_bias_table(na_rpb, dseq // GRID_W)
    place = jnp.tile(jnp.pad(jnp.eye(MLA_ROPE_DIM, dtype=F32), ((0, 0), (KPE_LANE, SLOT - KPE_LANE - MLA_ROPE_DIM))),
                     (1, MLA_HEADS)).astype(BF16)

    cvec = jnp.concatenate([c_ctx[None], c, jnp.zeros((8 - 1 - dbatch, D_MODEL), F32)], axis=0)
    mod5 = _modulation(cvec, w_mod, b_mod).reshape(DEPTH, 8, 6, 1, D_MODEL)

    ck = cache_na_k.reshape(dbatch, DEPTH, past, NA_WIDTH)
    cv = cache_na_v.reshape(dbatch, DEPTH, past, NA_WIDTH)

    xp = x_prompt.reshape(batch * seq, D_MODEL)
    assert DEPTH == 2
    caches = None
    for l in range(DEPTH):
        u, nq, nvo, qm, km, vmo, gate, *caches = _inproj(xp, mod5, l, w, seq=seq, rope_tab=None, ctx_out=True,
                                                         tm=seq, prev=caches)
        yna, ymla = _attn_ctx(nq, caches[0], nvo, qm, km, vmo, seq=seq, nk_layer=l if l > 0 else None)
        xp, h2, rg = _merge(xp, u, yna, ymla, gate, mod5, l, w, seq=seq, ctx=True, tm=2 * seq)
        xp = _moe(h2, rg, xp, mod5, l, w, seq=seq, ctx=True)
    new_nk, new_nv, new_ckv, new_kpe = caches

    xs = x_sample.reshape(dbatch * dseq, D_MODEL)
    for l in range(DEPTH):
        u, nq, nvo, qm, km, vmo, gate, nk = _inproj(xs, mod5, l, w, seq=dseq, rope_tab=rope_tab, ctx_out=False, tm=512)
        ckm, cvmo, cnvo = _cache_prep(cache_mla_ckv, cache_mla_kpe, cv, w["w_ukv"], place, l)
        yna = _attn_na(nq, nk, nvo, ck, cnvo, bias_tab, l, batch=dbatch, seq=dseq)
        ymla = _attn_mla(qm, km, vmo, ckm, cvmo, batch=dbatch, seq=dseq, tq=512)
        xs, h2, rg = _merge(xs, u, yna, ymla, gate, mod5, l, w, seq=dseq, ctx=False, tm=512)
        xs = _moe(h2, rg, xs, mod5, l, w, seq=dseq, ctx=False)

    heads = (batch, DEPTH, seq, NA_HEADS, NA_HEAD_DIM)
    return (xp.reshape(batch, seq, D_MODEL), xs.reshape(dbatch, dseq, D_MODEL),
            new_nk.reshape(heads), new_nv.reshape(heads), new_ckv, new_kpe)
```

```python
import functools

import numpy as np
import jax
import jax.numpy as jnp
from jax import lax
from jax.experimental import pallas as pl
from jax.experimental.pallas import tpu as pltpu

F32 = jnp.float32
BF16 = jnp.bfloat16

D_MODEL = 1024
DEPTH = 2
GRID_W = 64
POOL_WINDOWS = (2, 4, 8, 16)
POOL_GC = 128
POOL_WIDTH = 512
NA_HEADS = 8
NA_HEAD_DIM = 64
NA_WIDTH = 512
NA_WIN_ROWS = 8
NA_WIN_COLS = 16
MLA_HEADS = 8
MLA_NOPE_DIM = 64
MLA_ROPE_DIM = 32
MLA_V_DIM = 64
MLA_QK_DIM = 96
MLA_Q_LORA = 384
MLA_KV_LORA = 256
N_GROUPS = 4
EXPERTS_PER_GROUP = 4
N_EXPERTS = 16
EXPERT_FF = 256
ROPE_BASE = 10000.0
EPS = 1e-6
NEG_INF = -1e30

LANES = 128
SLOT = LANES
VMEM_LIMIT = 56 * 1024 * 1024

C_U, C_NQ, C_NK, C_NV, C_CQ, C_CKV, C_KPE, C_G, C_END = 0, 512, 1024, 1536, 2048, 2432, 2688, 2720, 5888
G_WIDTH = 3 * D_MODEL
KPE_LANE = MLA_NOPE_DIM

LOG2E = 1.4426950408889634
VO_W = 2 * SLOT
VO_ALL = (NA_HEADS // 2) * VO_W

NA_BLOCK_ROWS = 4
NA_SLAB_ROWS = 12


def _cparams(sem):
    return pltpu.CompilerParams(dimension_semantics=sem, vmem_limit_bytes=VMEM_LIMIT)


def _rms(x, g):
    return x * lax.rsqrt(jnp.mean(x * x, axis=-1, keepdims=True) + EPS) * g


def _dot(a, b):
    return jnp.dot(a, b, preferred_element_type=F32)


def _dot_nt(a, b):
    return lax.dot_general(a, b, (((1,), (1,)), ((), ())), preferred_element_type=F32)


def _mod_kernel(c_ref, w_ref, b_ref, o_ref):
    c = c_ref[...]
    s = (c * jax.nn.sigmoid(c)).astype(BF16)
    o_ref[...] = _dot(s, w_ref[...].astype(BF16)) + b_ref[...]


def _modulation(cvec, w_mod, b_mod):
    tn = 1536
    n = w_mod.shape[-1]
    return pl.pallas_call(
        _mod_kernel,
        grid=(DEPTH, n // tn),
        in_specs=[
            pl.BlockSpec((8, D_MODEL), lambda l, j: (0, 0)),
            pl.BlockSpec((None, D_MODEL, tn), lambda l, j: (l, 0, j)),
            pl.BlockSpec((None, 1, tn), lambda l, j: (l, 0, j)),
        ],
        out_specs=pl.BlockSpec((None, 8, tn), lambda l, j: (l, 0, j)),
        out_shape=jax.ShapeDtypeStruct((DEPTH, 8, n), F32),
        compiler_params=_cparams(("arbitrary", "arbitrary")),
        name="modulation",
    )(cvec, w_mod, b_mod.reshape(DEPTH, 1, n))


def _rope_slot(x, tab):
    c, s_up, s_dn = tab[:, 0:SLOT], tab[:, SLOT:2 * SLOT], tab[:, 2 * SLOT:3 * SLOT]
    half = MLA_ROPE_DIM // 4
    x_up = pltpu.roll(x, SLOT - half, axis=1)
    x_dn = pltpu.roll(x, half, axis=1)
    return x * c + x_up * s_up + x_dn * s_dn


def _store_vo(vo_ref, v):
    tm = v.shape[0]
    for p in range(NA_HEADS // 2):
        vo_ref[:, p * VO_W:p * VO_W + SLOT] = v[:, p * SLOT:(p + 1) * SLOT].astype(BF16)
        vo_ref[:, p * VO_W + SLOT:(p + 1) * VO_W] = jnp.ones((tm, SLOT), BF16)


def _realign_gate_columns(win_ref, wg_ref, wqk_ref):
    off = C_G % LANES
    lane = lax.broadcasted_iota(jnp.int32, (1, LANES), 1)
    base = C_G - off
    nxt = win_ref[:, base:base + LANES].astype(F32)
    kpe = pltpu.roll(nxt, KPE_LANE - (C_KPE - base), axis=1)
    wqk_ref[:, :MLA_Q_LORA] = win_ref[:, C_CQ:C_CKV]
    wqk_ref[:, MLA_Q_LORA:] = jnp.where((lane >= KPE_LANE) & (lane < KPE_LANE + MLA_ROPE_DIM), kpe, 0.0).astype(BF16)
    for j in range(G_WIDTH // LANES):
        cur = nxt
        nxt = win_ref[:, base + (j + 1) * LANES:base + (j + 2) * LANES].astype(F32)
        tile = pltpu.roll(jnp.where(lane >= off, cur, nxt), LANES - off, axis=1)
        wg_ref[:, j * LANES:(j + 1) * LANES] = tile.astype(BF16)


def _inproj_kernel(*refs, rope, ctx_out, stacked):
    n_in = 9 + (1 if rope else 0) + (4 if stacked else 0)
    ins, outs, (wg_ref, wqk_ref) = refs[:n_in], refs[n_in:-2], refs[-2:]
    x_ref, sh_ref, sc_ref, g_ref, win_ref, qn_ref, wuq_ref, kvn_ref, wukv_ref = ins[:9]
    tab_ref = ins[9] if rope else None
    prev = ins[n_in - 4:] if stacked else None
    if ctx_out:
        u_ref, nq_ref, nvo_ref, qm_ref, km_ref, vmo_ref, gate_ref, nk_ref, nv_ref, ckv_ref, kpe_ref = outs
    else:
        u_ref, nq_ref, nvo_ref, qm_ref, km_ref, vmo_ref, gate_ref, nk_ref = outs

    def put(ref, k, val):
        if stacked:
            ref[0] = prev[k][...]
            ref[1] = val
        else:
            ref[...] = val

    @pl.when(pl.program_id(0) == 0)
    def _():
        _realign_gate_columns(win_ref, wg_ref, wqk_ref)

    x = x_ref[...]
    h = (_rms(x, g_ref[...]) * (1.0 + sc_ref[...]) + sh_ref[...]).astype(BF16)

    def seg(a, b):
        return _dot(h, win_ref[:, a:b])

    u_ref[...] = seg(C_U, C_NQ).astype(u_ref.dtype)
    nq_ref[...] = (seg(C_NQ, C_NK) * (NA_HEAD_DIM ** -0.5 * LOG2E)).astype(BF16)
    nk = seg(C_NK, C_NV)
    nv = seg(C_NV, C_CQ)
    if ctx_out:
        put(nk_ref, 0, nk)
        put(nv_ref, 1, nv)
    else:
        nk_ref[...] = nk.astype(BF16)
    _store_vo(nvo_ref, nv)
    gate_ref[...] = jax.nn.sigmoid(_dot(h, wg_ref[...])).astype(BF16)

    cqk = _dot(h, wqk_ref[...])
    cq = _rms(cqk[:, :MLA_Q_LORA], qn_ref[...]).astype(BF16)
    q = _dot(cq, wuq_ref[...])
    tab = tab_ref[...] if rope else None
    for hd in range(MLA_HEADS):
        qs = q[:, hd * SLOT:(hd + 1) * SLOT]
        if rope:
            qs = _rope_slot(qs, tab)
        qm_ref[:, hd * SLOT:(hd + 1) * SLOT] = (qs * (MLA_QK_DIM ** -0.5 * LOG2E)).astype(BF16)

    ckv = _rms(seg(C_CKV, C_KPE), kvn_ref[...])
    kpe_slot = cqk[:, MLA_Q_LORA:]
    if ctx_out:
        put(ckv_ref, 2, ckv)
        put(kpe_ref, 3, kpe_slot[:, KPE_LANE:KPE_LANE + MLA_ROPE_DIM])
    if rope:
        kpe_slot = _rope_slot(kpe_slot, tab)
    kv = _dot(ckv.astype(BF16), wukv_ref[...])
    for hd in range(MLA_HEADS):
        km_ref[:, hd * SLOT:(hd + 1) * SLOT] = (kv[:, hd * SLOT:(hd + 1) * SLOT] + kpe_slot).astype(BF16)
    _store_vo(vmo_ref, kv[:, MLA_HEADS * SLOT:])


def _inproj(x, mod5, l, w, *, seq, rope_tab, ctx_out, tm, prev=None):
    n = x.shape[0]
    steps_per_seq = seq // tm if not ctx_out else 0
    stacked = prev is not None
    assert not stacked or (ctx_out and tm == seq and l == 1)

    def mrow(i):
        return 0 if ctx_out else 1 + i // steps_per_seq

    const = lambda i: (l, 0, 0)
    in_specs = [
        pl.BlockSpec((tm, D_MODEL), lambda i: (i, 0)),
        pl.BlockSpec((None, None, None, 1, D_MODEL), lambda i: (l, mrow(i), 0, 0, 0)),
        pl.BlockSpec((None, None, None, 1, D_MODEL), lambda i: (l, mrow(i), 1, 0, 0)),
        pl.BlockSpec((None, 1, D_MODEL), const),
        pl.BlockSpec((None, D_MODEL, C_END), const, pipeline_mode=pl.Buffered(1)),
        pl.BlockSpec((None, 1, MLA_Q_LORA), const),
        pl.BlockSpec((None, MLA_Q_LORA, MLA_HEADS * SLOT), const, pipeline_mode=pl.Buffered(1)),
        pl.BlockSpec((None, 1, MLA_KV_LORA), const),
        pl.BlockSpec((None, MLA_KV_LORA, MLA_HEADS * SLOT + 512), const, pipeline_mode=pl.Buffered(1)),
    ]
    args = [x, mod5, mod5, w["g_pre_mix"], w["w_in"], w["q_norm"], w["w_uq"], w["kv_norm"], w["w_ukv"]]
    rope = rope_tab is not None
    if rope:
        in_specs.append(pl.BlockSpec((tm, 3 * SLOT), lambda i: (i % steps_per_seq, 0)))
        args.append(rope_tab)
    row = lambda wd: pl.BlockSpec((tm, wd), lambda i: (i, 0))
    rows_of = lambda wd, dt: jax.ShapeDtypeStruct((n, wd), dt)
    widths = [512, 512, VO_ALL, MLA_HEADS * SLOT, MLA_HEADS * SLOT, VO_ALL, 3 * D_MODEL]
    out_specs = [row(wd) for wd in widths]
    out_shape = [rows_of(wd, BF16) for wd in widths]
    cache_w = [512, 512, MLA_KV_LORA, MLA_ROPE_DIM] if ctx_out else []
    if not ctx_out:
        out_specs.append(row(512))
        out_shape.append(rows_of(512, BF16))
    elif stacked:
        in_specs += [row(wd) for wd in cache_w]
        args += list(prev)
        out_specs += [pl.BlockSpec((None, DEPTH, seq, wd), lambda i: (i, 0, 0, 0)) for wd in cache_w]
        out_shape += [jax.ShapeDtypeStruct((n // seq, DEPTH, seq, wd), F32) for wd in cache_w]
    else:
        out_specs += [row(wd) for wd in cache_w]
        out_shape += [rows_of(wd, F32) for wd in cache_w]
    return pl.pallas_call(
        functools.partial(_inproj_kernel, rope=rope, ctx_out=ctx_out, stacked=stacked),
        grid=(n // tm,),
        in_specs=in_specs,
        out_specs=out_specs,
        out_shape=out_shape,
        scratch_shapes=[pltpu.VMEM((D_MODEL, G_WIDTH), BF16), pltpu.VMEM((D_MODEL, MLA_Q_LORA + SLOT), BF16)],
        compiler_params=_cparams(("arbitrary",)),
        name="inproj_ctx" if ctx_out else "inproj_lat",
    )(*args)


def _attend(q_h, segs):
    scores = []
    for k, _, bias in segs:
        s = _dot_nt(q_h, k)
        if bias is not None:
            s = s + bias
        scores.append(s)
    m = scores[0].max(axis=-1, keepdims=True)
    for s in scores[1:]:
        m = jnp.maximum(m, s.max(axis=-1, keepdims=True))
    acc = None
    for s, (_, vo, _) in zip(scores, segs):
        o = _dot(jnp.exp2(s - m).astype(BF16), vo)
        acc = o if acc is None else acc + o
    return acc[:, :SLOT] / acc[:, SLOT:]


def _attn_heads(out_ref, q_of, segs_of):
    lo = lax.broadcasted_iota(jnp.int32, (1, SLOT), 1) < NA_HEAD_DIM
    for p in range(NA_HEADS // 2):
        o0, o1 = [_attend(q_of(2 * p + e), segs_of(2 * p + e)) for e in range(2)]
        out_ref[:, p * SLOT:(p + 1) * SLOT] = jnp.where(lo, o0, o1).astype(out_ref.dtype)


def _pair_q(q_ref):
    lo = lax.broadcasted_iota(jnp.int32, (1, SLOT), 1) < NA_HEAD_DIM

    def q_of(h):
        q_pair = q_ref[:, (h // 2) * SLOT:(h // 2 + 1) * SLOT]
        keep = lo if h % 2 == 0 else jnp.logical_not(lo)
        return jnp.where(keep, q_pair, jnp.zeros_like(q_pair))
    return q_of


def _slot(ref, i, rows=slice(None)):
    return ref[rows, i * SLOT:(i + 1) * SLOT].astype(BF16)


def _vo(ref, h, rows=slice(None)):
    return ref[rows, (h // 2) * VO_W:(h // 2 + 1) * VO_W]


def _attn_ctx_kernel(nq_ref, nk_ref, nvo_ref, qm_ref, km_ref, vmo_ref, yna_ref, ymla_ref):
    _attn_heads(yna_ref, _pair_q(nq_ref), lambda h: [(_slot(nk_ref, h // 2), _vo(nvo_ref, h), None)])
    _attn_heads(ymla_ref, lambda h: _slot(qm_ref, h), lambda h: [(_slot(km_ref, h), _vo(vmo_ref, h), None)])


def _attn_ctx(nq, nk, nvo, qm, km, vmo, *, seq, nk_layer=None):
    n = nq.shape[0]
    spec = lambda wd: pl.BlockSpec((seq, wd), lambda b: (b, 0))
    nk_spec = spec(512) if nk_layer is None else pl.BlockSpec((None, None, seq, 512), lambda b: (b, nk_layer, 0, 0))
    return pl.pallas_call(
        _attn_ctx_kernel,
        grid=(n // seq,),
        in_specs=[spec(512), nk_spec, spec(VO_ALL), spec(1024), spec(1024), spec(VO_ALL)],
        out_specs=[spec(512), spec(512)],
        out_shape=[jax.ShapeDtypeStruct((n, 512), BF16)] * 2,
        compiler_params=_cparams(("arbitrary",)),
        name="attn_ctx",
    )(nq, nk, nvo, qm, km, vmo)


def _attn_na_kernel(nq_ref, nk_ref, nvo_ref, ck_ref, cvo_ref, bias_ref, out_ref, *, n_blocks):
    i = pl.program_id(1)
    ks = jnp.clip(NA_BLOCK_ROWS * i - NA_WIN_ROWS // 2, 0, n_blocks * NA_BLOCK_ROWS - NA_SLAB_ROWS)
    slab = pl.ds(pl.multiple_of(ks * GRID_W, GRID_W), NA_SLAB_ROWS * GRID_W)

    def segs_of(h):
        return [(_slot(nk_ref, h // 2, slab), _vo(nvo_ref, h, slab), bias_ref[h].astype(F32)),
                (_slot(ck_ref, h // 2), _vo(cvo_ref, h), None)]

    _attn_heads(out_ref, _pair_q(nq_ref), segs_of)


def _attn_na(nq, nk, nvo, cache_k, cvo, bias_tab, l, *, batch, seq):
    tq = NA_BLOCK_ROWS * GRID_W
    nb = seq // tq
    past = cache_k.shape[2]

    def cls(i):
        return jnp.where(i == 0, 0, jnp.where(i == nb - 1, 2, 1))

    return pl.pallas_call(
        functools.partial(_attn_na_kernel, n_blocks=nb),
        grid=(batch, nb),
        in_specs=[
            pl.BlockSpec((tq, 512), lambda b, i: (b * nb + i, 0)),
            pl.BlockSpec((seq, 512), lambda b, i: (b, 0)),
            pl.BlockSpec((seq, VO_ALL), lambda b, i: (b, 0)),
            pl.BlockSpec((None, None, past, 512), lambda b, i: (b, l, 0, 0)),
            pl.BlockSpec((None, past, VO_ALL), lambda b, i: (b, 0, 0)),
            pl.BlockSpec((None, None, NA_HEADS, tq, NA_SLAB_ROWS * GRID_W), lambda b, i: (l, cls(i), 0, 0, 0)),
        ],
        out_specs=pl.BlockSpec((tq, 512), lambda b, i: (b * nb + i, 0)),
        out_shape=jax.ShapeDtypeStruct((batch * seq, 512), BF16),
        compiler_params=_cparams(("arbitrary", "arbitrary")),
        name="attn_na_lat",
    )(nq, nk, nvo, cache_k, cvo, bias_tab)


def _cache_prep_kernel(ckv_ref, kpe_ref, nv_ref, wukv_ref, place_ref, km_ref, vmo_ref, nvo_ref):
    kv = _dot(ckv_ref[...].astype(BF16), wukv_ref[...])
    kslots = kv[:, :MLA_HEADS * SLOT] + _dot(kpe_ref[...].astype(BF16), place_ref[...])
    km_ref[...] = kslots.astype(BF16)
    _store_vo(vmo_ref, kv[:, MLA_HEADS * SLOT:])
    _store_vo(nvo_ref, nv_ref[...])


def _cache_prep(cache_ckv, cache_kpe, cache_nv, w_ukv, place, l):
    batch, _, past, _ = cache_ckv.shape
    tout = pl.BlockSpec((None, past, VO_ALL), lambda b: (b, 0, 0))
    tshape = jax.ShapeDtypeStruct((batch, past, VO_ALL), BF16)
    return pl.pallas_call(
        _cache_prep_kernel,
        grid=(batch,),
        in_specs=[
            pl.BlockSpec((None, None, past, MLA_KV_LORA), lambda b: (b, l, 0, 0)),
            pl.BlockSpec((None, None, past, MLA_ROPE_DIM), lambda b: (b, l, 0, 0)),
            pl.BlockSpec((None, None, past, 512), lambda b: (b, l, 0, 0)),
            pl.BlockSpec((None, MLA_KV_LORA, MLA_HEADS * SLOT + 512), lambda b: (l, 0, 0)),
            pl.BlockSpec((MLA_ROPE_DIM, MLA_HEADS * SLOT), lambda b: (0, 0)),
        ],
        out_specs=[pl.BlockSpec((None, past, MLA_HEADS * SLOT), lambda b: (b, 0, 0)), tout, tout],
        out_shape=[jax.ShapeDtypeStruct((batch, past, MLA_HEADS * SLOT), BF16), tshape, tshape],
        compiler_params=_cparams(("arbitrary",)),
        name="cache_prep",
    )(cache_ckv, cache_kpe, cache_nv, w_ukv, place)


def _attn_mla_kernel(qm_ref, km_ref, vmo_ref, ckm_ref, cvmo_ref, out_ref):
    _attn_heads(out_ref, lambda h: _slot(qm_ref, h),
                lambda h: [(_slot(km_ref, h), _vo(vmo_ref, h), None), (_slot(ckm_ref, h), _vo(cvmo_ref, h), None)])


def _attn_mla(qm, km, vmo, ckm, cvmo, *, batch, seq, tq):
    nb = seq // tq
    past = ckm.shape[1]
    return pl.pallas_call(
        _attn_mla_kernel,
        grid=(batch, nb),
        in_specs=[
            pl.BlockSpec((tq, MLA_HEADS * SLOT), lambda b, i: (b * nb + i, 0)),
            pl.BlockSpec((seq, MLA_HEADS * SLOT), lambda b, i: (b, 0)),
            pl.BlockSpec((seq, VO_ALL), lambda b, i: (b, 0)),
            pl.BlockSpec((None, past, MLA_HEADS * SLOT), lambda b, i: (b, 0, 0)),
            pl.BlockSpec((None, past, VO_ALL), lambda b, i: (b, 0, 0)),
        ],
        out_specs=pl.BlockSpec((tq, 512), lambda b, i: (b * nb + i, 0)),
        out_shape=jax.ShapeDtypeStruct((batch * seq, 512), BF16),
        compiler_params=_cparams(("arbitrary", "arbitrary")),
        name="attn_mla_lat",
    )(qm, km, vmo, ckm, cvmo)


POOL_HALO = 16


GROUP_LANE = N_EXPERTS


def _route(logits):
    lane = lax.broadcasted_iota(jnp.int32, logits.shape, 1).astype(F32)
    big = jnp.float32(1 << 20)
    is_g = lane < N_GROUPS
    gl = jnp.where(is_g, logits, NEG_INF)
    gmax = gl.max(axis=-1, keepdims=True)
    ge = jnp.where(is_g, jnp.exp(gl - gmax), 0.0)
    gp = ge / ge.sum(axis=-1, keepdims=True)
    g_w = jnp.where(is_g, gp, -1.0).max(axis=-1, keepdims=True)
    g_idx = jnp.where(is_g & (gp == g_w), lane, big).min(axis=-1, keepdims=True)
    e_lane = lane - N_GROUPS
    in_grp = (e_lane >= g_idx * EXPERTS_PER_GROUP) & (e_lane < (g_idx + 1) * EXPERTS_PER_GROUP)
    el = jnp.where(in_grp, logits, NEG_INF)
    emax = el.max(axis=-1, keepdims=True)
    ee = jnp.where(in_grp, jnp.exp(el - emax), 0.0)
    ep = ee / ee.sum(axis=-1, keepdims=True)
    p1 = jnp.where(in_grp, ep, -1.0).max(axis=-1, keepdims=True)
    i1 = jnp.where(in_grp & (ep == p1), lane, big).min(axis=-1, keepdims=True)
    rest = in_grp & (lane != i1)
    p2 = jnp.where(rest, ep, -1.0).max(axis=-1, keepdims=True)
    i2 = jnp.where(rest & (ep == p2), lane, big).min(axis=-1, keepdims=True)
    tot = p1 + p2
    gates = jnp.where(lane == i1, g_w * (p1 / tot), 0.0) + jnp.where(lane == i2, g_w * (p2 / tot), 0.0)
    return jnp.where(lane == GROUP_LANE, g_idx, pltpu.roll(gates, LANES - N_GROUPS, axis=1))


def _merge_kernel(x_ref, u_ref, yna_ref, ymla_ref, gate_ref, gt1_ref, sh2_ref, sc2_ref, gpost_ref, gpre_ref,
                  wpool_ref, pscale_ref, wbr_ref, wout_ref, wr2_ref, wr1_ref,
                  xo_ref, h2_ref, rg_ref, *, seq, tm, steps_per_seq):
    def pool_diff(uwin, utile, r0, s0):
        rows, nwin = utile.shape[0], uwin.shape[0]
        t = r0 + lax.broadcasted_iota(jnp.int32, (rows, nwin), 0)
        s = s0 + lax.broadcasted_iota(jnp.int32, (rows, nwin), 1)
        t1 = r0 + lax.broadcasted_iota(jnp.int32, (rows, 1), 0)
        ds = []
        for g, w in enumerate(POOL_WINDOWS):
            lo = jnp.maximum(t - w // 2, 0)
            hi = jnp.minimum(t + (w - w // 2), seq)
            band = ((s >= lo) & (s < hi)).astype(BF16)
            cnt = (jnp.minimum(t1 + (w - w // 2), seq) - jnp.maximum(t1 - w // 2, 0)).astype(F32)
            sl = slice(g * POOL_GC, (g + 1) * POOL_GC)
            ds.append(_dot(band, uwin[:, sl]) / cnt - utile[:, sl].astype(F32))
        return jnp.concatenate(ds, axis=-1).astype(BF16)

    if steps_per_seq == 0:
        d = jnp.concatenate([pool_diff(u_ref[si], u_ref[si], 0, 0) for si in range(tm // seq)], axis=0)
    else:
        nwin = min(tm + 2 * POOL_HALO, seq)
        r0 = (pl.program_id(0) % steps_per_seq) * tm
        s0 = pl.multiple_of(jnp.clip(r0 - POOL_HALO, 0, seq - nwin), POOL_HALO)
        d = pool_diff(u_ref[0, pl.ds(s0, nwin), :], u_ref[0, pl.ds(pl.multiple_of(r0, POOL_HALO), tm), :], r0, s0)
    y_pool = (_dot(d, wpool_ref[...]) * pscale_ref[...]).astype(BF16)

    g = gate_ref[...]
    merged = (g[:, 0:D_MODEL].astype(F32) * _dot(y_pool, wbr_ref[0])
              + g[:, D_MODEL:2 * D_MODEL].astype(F32) * _dot(yna_ref[...], wbr_ref[1])
              + g[:, 2 * D_MODEL:].astype(F32) * _dot(ymla_ref[...], wbr_ref[2]))
    y = _dot(merged.astype(BF16), wout_ref[...])
    xn = x_ref[...] + gt1_ref[...] * _rms(y, gpost_ref[...])
    xo_ref[...] = xn

    h2 = _rms(xn, gpre_ref[...]) * (1.0 + sc2_ref[...]) + sh2_ref[...]
    h_hi = h2.astype(BF16)
    h2_ref[...] = h_hi
    h_lo = (h2 - h_hi.astype(F32)).astype(BF16)
    two = _dot(h_hi, wr2_ref[...])
    logits = two[:, :LANES] + two[:, LANES:] + _dot(h_lo, wr1_ref[...])
    rg_ref[...] = _route(logits)


def _merge(x, u, yna, ymla, gate, mod5, l, w, *, seq, ctx, tm):
    n = x.shape[0]
    nseq = n // seq
    if tm >= seq:
        assert ctx and tm % seq == 0
        steps_per_seq = 0
        u_spec = pl.BlockSpec((tm // seq, seq, 512), lambda i: (i, 0, 0))
    else:
        steps_per_seq = seq // tm
        u_spec = pl.BlockSpec((1, seq, 512), lambda i: (i // steps_per_seq, 0, 0))

    def mrow(i):
        return 0 if ctx else 1 + i // steps_per_seq

    const = lambda i: (l, 0, 0)
    modspec = lambda k: pl.BlockSpec((None, None, None, 1, D_MODEL), lambda i: (l, mrow(i), k, 0, 0))
    row = lambda wd: pl.BlockSpec((tm, wd), lambda i: (i, 0))
    return pl.pallas_call(
        functools.partial(_merge_kernel, seq=seq, tm=tm, steps_per_seq=steps_per_seq),
        grid=(n // tm,),
        in_specs=[
            row(D_MODEL),
            u_spec,
            row(512), row(512), row(3 * D_MODEL),
            modspec(2), modspec(3), modspec(4),
            pl.BlockSpec((None, 1, D_MODEL), const),
            pl.BlockSpec((None, 1, D_MODEL), const),
            pl.BlockSpec((None, 512, 512), const),
            pl.BlockSpec((None, 1, 512), const),
            pl.BlockSpec((None, 3, 512, D_MODEL), lambda i: (l, 0, 0, 0)),
            pl.BlockSpec((None, D_MODEL, D_MODEL), const),
            pl.BlockSpec((None, D_MODEL, 2 * LANES), const),
            pl.BlockSpec((None, D_MODEL, LANES), const),
        ],
        out_specs=[row(D_MODEL), row(D_MODEL), row(LANES)],
        out_shape=[jax.ShapeDtypeStruct((n, D_MODEL), F32), jax.ShapeDtypeStruct((n, D_MODEL), BF16),
                   jax.ShapeDtypeStruct((n, LANES), F32)],
        compiler_params=_cparams(("arbitrary",)),
        name="merge_ctx" if ctx else "merge_lat",
    )(x, u.reshape(nseq, seq, 512), yna, ymla, gate, mod5, mod5, mod5, w["g_post_mix"], w["g_pre_ffn"],
      w["w_pool_bd"], w["pool_scale"], w["w_branch"], w["w_out"], w["w_r2"], w["w_r1"])


MOE_TM = 1024
MOE_SUB = 512
MOE_ROWS = 160
MOE_ROWS_PAD = 256


def _moe_kernel(h_ref, rg_ref, x_ref, gt2_ref, gpost_ref, w1_ref, w3_ref, w2_ref, o_ref):
    tm = h_ref.shape[0]
    subs = [slice(s * MOE_SUB, (s + 1) * MOE_SUB) for s in range(tm // MOE_SUB)]
    acc_ref = o_ref
    rg = rg_ref[...]
    lane = lax.broadcasted_iota(jnp.int32, rg.shape, 1)
    grp = jnp.where(lane == GROUP_LANE, rg, 0.0).sum(axis=-1, keepdims=True)
    member = ((lane.astype(F32) == grp) & (lane < N_GROUPS)).astype(F32)
    t_row = lax.broadcasted_iota(jnp.int32, (MOE_SUB, MOE_SUB), 0)
    t_col = lax.broadcasted_iota(jnp.int32, (MOE_SUB, MOE_SUB), 1)
    earlier, later = (t_col < t_row).astype(BF16), (t_row < t_col).astype(BF16)
    member_t = member.T[:8]
    rank_col = jnp.concatenate([_dot(earlier, member[sl].astype(BF16)) for sl in subs], axis=0)
    rank_row = jnp.concatenate([_dot(member_t[:, sl].astype(BF16), later) for sl in subs], axis=1)
    count = functools.reduce(jnp.maximum, [member[sl].sum(axis=0, keepdims=True) for sl in subs])
    rg_hi = rg.astype(BF16)
    rg_lo = (rg - rg_hi.astype(F32)).astype(BF16)
    row_id = lax.broadcasted_iota(jnp.int32, (MOE_ROWS, 1), 0).astype(F32)
    col_id = lax.broadcasted_iota(jnp.int32, (1, MOE_ROWS_PAD), 1).astype(F32)
    lane1 = lax.broadcasted_iota(jnp.int32, (1, LANES), 1)

    def experts(g, hs, gs):
        hid = []
        for e in range(EXPERTS_PER_GROUP):
            k = g * EXPERTS_PER_GROUP + e
            a = _dot(hs, w1_ref[k])
            b = _dot(hs, w3_ref[k])
            hid.append(((a * jax.nn.sigmoid(a)) * b * gs[:, k:k + 1]).astype(BF16))
        return _dot(jnp.concatenate(hid, axis=-1), w2_ref[g]).astype(BF16)

    n_slots = N_GROUPS * MOE_ROWS
    slots_pad = -(-n_slots // LANES) * LANES
    slot_row = lax.broadcasted_iota(jnp.int32, (n_slots, 1), 0).astype(F32)
    slot_col = lax.broadcasted_iota(jnp.int32, (1, slots_pad), 1).astype(F32)
    dest_row = sum(jnp.where((member_t[g:g + 1] > 0.0) & (rank_row[g:g + 1] < MOE_ROWS),
                             rank_row[g:g + 1] + (g * MOE_ROWS + 1), 0.0) for g in range(N_GROUPS)) - 1.0
    dest_col = sum(jnp.where((member[:, g:g + 1] > 0.0) & (rank_col[:, g:g + 1] < MOE_ROWS),
                             rank_col[:, g:g + 1] + (g * MOE_ROWS + 1), 0.0) for g in range(N_GROUPS)) - 1.0
    hs_sub, gs_sub = [], []
    for sl in subs:
        take = (dest_row[:, sl] == slot_row).astype(BF16)
        hs_sub.append(_dot(take, h_ref[sl, :]).astype(BF16))
        gs_sub.append(_dot(take, rg_hi[sl]) + _dot(take, rg_lo[sl]))
    parts = []
    for g in range(N_GROUPS):
        rows = slice(g * MOE_ROWS, (g + 1) * MOE_ROWS)
        parts.append(experts(g, jnp.concatenate([h[rows] for h in hs_sub], axis=0),
                             jnp.concatenate([q[rows] for q in gs_sub], axis=0)))
    pad = [jnp.zeros((slots_pad - n_slots, D_MODEL), BF16)] if slots_pad > n_slots else []
    for s, sl in enumerate(subs):
        give = (dest_col[sl] == slot_col).astype(BF16)
        rows = jnp.concatenate([p[s * MOE_ROWS:(s + 1) * MOE_ROWS] for p in parts] + pad, axis=0)
        acc_ref[sl, :] = _dot(give, rows)

    pad = jnp.zeros((MOE_ROWS_PAD - MOE_ROWS, D_MODEL), BF16)
    for g in range(N_GROUPS):
        rr, mr = rank_row[g:g + 1, :], member_t[g:g + 1, :]
        rc, mc = rank_col[:, g:g + 1], member[:, g:g + 1]
        n_g = jnp.max(jnp.where(lane1 == g, count, 0.0)).astype(jnp.int32)

        def chunk(c, carry, g=g, rr=rr, mr=mr, rc=rc, mc=mc):
            base = (c * MOE_ROWS).astype(F32)
            hs, gs = [], []
            for sl in subs:
                take = ((rr[:, sl] - base == row_id) & (mr[:, sl] > 0.0)).astype(BF16)
                hs.append(_dot(take, h_ref[sl, :]).astype(BF16))
                gs.append(_dot(take, rg_hi[sl]) + _dot(take, rg_lo[sl]))
            part = experts(g, jnp.concatenate(hs, axis=0), jnp.concatenate(gs, axis=0))
            for s, sl in enumerate(subs):
                rel = rc[sl] - base
                give = ((rel == col_id) & (rel < MOE_ROWS) & (mc[sl] > 0.0)).astype(BF16)
                rows = jnp.concatenate([part[s * MOE_ROWS:(s + 1) * MOE_ROWS], pad], axis=0)
                acc_ref[sl, :] += _dot(give, rows)
            return carry

        lax.fori_loop(1, (n_g + MOE_ROWS - 1) // MOE_ROWS, chunk, 0)

    o_ref[...] = x_ref[...] + gt2_ref[...] * _rms(acc_ref[...], gpost_ref[...])


def _moe(h2, rg, x, mod5, l, w, *, seq, ctx):
    n = x.shape[0]
    tm = MOE_TM
    steps_per_seq = seq // tm if not ctx else 0

    def mrow(i):
        return 0 if ctx else 1 + i // steps_per_seq

    row = lambda wd: pl.BlockSpec((tm, wd), lambda i: (i, 0))
    once = pl.Buffered(1)
    wspec = lambda a, b: pl.BlockSpec((None, N_EXPERTS, a, b), lambda i: (l, 0, 0, 0), pipeline_mode=once)
    return pl.pallas_call(
        _moe_kernel,
        grid=(n // tm,),
        in_specs=[
            row(D_MODEL), row(LANES), row(D_MODEL),
            pl.BlockSpec((None, None, None, 1, D_MODEL), lambda i: (l, mrow(i), 5, 0, 0)),
            pl.BlockSpec((None, 1, D_MODEL), lambda i: (l, 0, 0)),
            wspec(D_MODEL, EXPERT_FF), wspec(D_MODEL, EXPERT_FF),
            pl.BlockSpec((None, N_GROUPS, EXPERTS_PER_GROUP * EXPERT_FF, D_MODEL), lambda i: (l, 0, 0, 0),
                         pipeline_mode=once),
        ],
        out_specs=row(D_MODEL),
        out_shape=jax.ShapeDtypeStruct((n, D_MODEL), F32),
        compiler_params=_cparams(("arbitrary",)),
        name="moe_ctx" if ctx else "moe_lat",
    )(h2, rg, x, mod5, w["g_post_ffn"], w["w1"], w["w3"], w["w2"])


def _rope_table(seq):
    t = np.arange(seq)
    n_freq = MLA_ROPE_DIM // 4
    inv = jnp.asarray(ROPE_BASE, F32) ** (-jnp.arange(n_freq, dtype=F32) / n_freq)
    ang_r = jnp.asarray(t // GRID_W, F32)[:, None] * inv
    ang_c = jnp.asarray(t % GRID_W, F32)[:, None] * inv
    cos = jnp.cos(jnp.stack([ang_r, ang_c], axis=1))
    sin = jnp.sin(jnp.stack([ang_r, ang_c], axis=1))
    zeros = jnp.zeros_like(sin)
    c32 = jnp.stack([cos, cos], axis=2).reshape(seq, MLA_ROPE_DIM)
    up32 = jnp.stack([-sin, zeros], axis=2).reshape(seq, MLA_ROPE_DIM)
    dn32 = jnp.stack([zeros, sin], axis=2).reshape(seq, MLA_ROPE_DIM)

    def slot(v, fill):
        return jnp.concatenate([jnp.full((seq, KPE_LANE), fill, F32), v,
                                jnp.full((seq, SLOT - KPE_LANE - MLA_ROPE_DIM), fill, F32)], axis=1)

    return jnp.concatenate([slot(c32, 1.0), slot(up32, 0.0), slot(dn32, 0.0)], axis=1)


def _na_row_classes(n_rows):
    i = np.arange(NA_BLOCK_ROWS)[:, None]
    j = np.arange(NA_SLAB_ROWS)[None, :]
    last_ks = n_rows - NA_SLAB_ROWS
    last_q0 = n_rows - NA_BLOCK_ROWS
    specs = [
        (j < NA_WIN_ROWS, j - i),
        ((j >= i) & (j < i + NA_WIN_ROWS), j - NA_WIN_ROWS // 2 - i),
        (last_ks + j >= n_rows - NA_WIN_ROWS, last_ks + j - last_q0 - i),
    ]
    ok = np.stack([np.broadcast_to(o, (NA_BLOCK_ROWS, NA_SLAB_ROWS)) for o, _ in specs])
    dr = np.stack([np.clip(d + NA_WIN_ROWS - 1, 0, 2 * NA_WIN_ROWS - 2) for _, d in specs])
    return ok, dr


def _bias_kernel(rpb_ref, out_ref, *, row_ok, row_dr):
    reach = NA_WIN_COLS - 1
    rows = rpb_ref[...] * LOG2E
    v_lo = pltpu.roll(rows, SLOT - reach, axis=1)
    v_hi = pltpu.roll(rows, GRID_W - reach, axis=1)
    lane1 = lax.broadcasted_iota(jnp.int32, (1, SLOT), 1)
    mid = (lane1 >= GRID_W - reach) & (lane1 <= GRID_W + reach)
    c = lax.broadcasted_iota(jnp.int32, (GRID_W, SLOT), 0)
    lane = lax.broadcasted_iota(jnp.int32, (GRID_W, SLOT), 1)
    kc = lane & (GRID_W - 1)
    c0 = jnp.clip(c - NA_WIN_COLS // 2, 0, GRID_W - NA_WIN_COLS)
    in_col = (kc >= c0) & (kc < c0 + NA_WIN_COLS)
    first = lane < GRID_W
    for k in range(3):
        for i in range(NA_BLOCK_ROWS):
            for jp in range(NA_SLAB_ROWS // 2):
                ok_a, ok_b = bool(row_ok[k, i, 2 * jp]), bool(row_ok[k, i, 2 * jp + 1])
                a, b = int(row_dr[k, i, 2 * jp]), int(row_dr[k, i, 2 * jp + 1])
                rs = slice(i * GRID_W, (i + 1) * GRID_W)
                cs = slice(jp * SLOT, (jp + 1) * SLOT)
                if not (ok_a or ok_b):
                    out_ref[k, rs, cs] = jnp.full((GRID_W, SLOT), NEG_INF, out_ref.dtype)
                    continue
                v = jnp.where(mid, v_hi[b:b + 1], v_lo[a:a + 1])
                tile = pltpu.roll(jnp.broadcast_to(v, (GRID_W, SLOT)), 0, axis=1, stride=1, stride_axis=0)
                ok = in_col
                if not ok_b:
                    ok = ok & first
                if not ok_a:
                    ok = ok & jnp.logical_not(first)
                out_ref[k, rs, cs] = jnp.where(ok, tile, NEG_INF).astype(out_ref.dtype)


def _na_bias_table(rpb, n_rows):
    n_dr, n_dc = rpb.shape[-2:]
    row_ok, row_dr = _na_row_classes(n_rows)
    rpb_p = jnp.pad(rpb, ((0, 0), (0, 0), (0, 0), (0, SLOT - n_dc)))
    tq, tk = NA_BLOCK_ROWS * GRID_W, NA_SLAB_ROWS * GRID_W
    return pl.pallas_call(
        functools.partial(_bias_kernel, row_ok=row_ok, row_dr=row_dr),
        grid=(DEPTH, NA_HEADS),
        in_specs=[pl.BlockSpec((None, None, n_dr, SLOT), lambda l, h: (l, h, 0, 0))],
        out_specs=pl.BlockSpec((None, 3, None, tq, tk), lambda l, h: (l, 0, h, 0, 0)),
        out_shape=jax.ShapeDtypeStruct((DEPTH, 3, NA_HEADS, tq, tk), BF16),
        compiler_params=_cparams(("arbitrary", "arbitrary")),
        name="na_bias_table",
    )(rpb_p)


def _prep(p):
    L = DEPTH
    w_in = p["w_in"]
    w_in_p = jnp.pad(w_in.astype(BF16), ((0, 0), (0, 0), (0, C_END - w_in.shape[-1])))
    w_uq = jnp.pad(p["mla_w_uq"].reshape(L, MLA_Q_LORA, MLA_HEADS, MLA_QK_DIM),
                   ((0, 0), (0, 0), (0, 0), (0, SLOT - MLA_QK_DIM))).reshape(L, MLA_Q_LORA, MLA_HEADS * SLOT)
    ukv = p["mla_w_ukv"].reshape(L, MLA_KV_LORA, MLA_HEADS, MLA_NOPE_DIM + MLA_V_DIM)
    uk = jnp.pad(ukv[..., :MLA_NOPE_DIM], ((0, 0), (0, 0), (0, 0), (0, SLOT - MLA_NOPE_DIM)))
    w_ukv = jnp.concatenate([uk.reshape(L, MLA_KV_LORA, MLA_HEADS * SLOT),
                             ukv[..., MLA_NOPE_DIM:].reshape(L, MLA_KV_LORA, MLA_HEADS * MLA_V_DIM)], axis=-1)
    w_pool_bd = jnp.einsum("lgcd,gh->lgchd", p["w_pool"], jnp.eye(4, dtype=F32)).reshape(L, POOL_WIDTH, POOL_WIDTH)
    w_r = jnp.concatenate([p["w_group_router"], p["w_expert_router"]], axis=-1)
    r_hi = w_r.astype(BF16)
    r_lo = (w_r - r_hi.astype(F32)).astype(BF16)
    padl = lambda a: jnp.pad(a, ((0, 0), (0, 0), (0, LANES - a.shape[-1])))
    vec = lambda a: a.reshape(L, 1, a.shape[-1])
    return {
        "w_in": w_in_p,
        "w_uq": w_uq.astype(BF16),
        "w_ukv": w_ukv.astype(BF16),
        "w_pool_bd": w_pool_bd.astype(BF16),
        "w_branch": p["w_branch"].astype(BF16),
        "w_out": p["w_out"].astype(BF16),
        "w_r2": jnp.concatenate([padl(r_hi), padl(r_lo)], axis=-1),
        "w_r1": padl(r_hi),
        "w1": p["w_exp_gate"].astype(BF16),
        "w3": p["w_exp_up"].astype(BF16),
        "w2": p["w_exp_down"].astype(BF16).reshape(L, N_GROUPS, EXPERTS_PER_GROUP * EXPERT_FF, D_MODEL),
        "g_pre_mix": vec(p["g_pre_mix"]), "g_post_mix": vec(p["g_post_mix"]),
        "g_pre_ffn": vec(p["g_pre_ffn"]), "g_post_ffn": vec(p["g_post_ffn"]),
        "pool_scale": vec(p["pool_scale"]), "q_norm": vec(p["mla_q_norm"]), "kv_norm": vec(p["mla_kv_norm"]),
    }


def kernel(x_prompt, x_sample, cache_na_k, cache_na_v, cache_mla_ckv, cache_mla_kpe, c, c_ctx, w_mod, b_mod,
           g_pre_mix, g_post_mix, g_pre_ffn, g_post_ffn, w_in, w_pool, pool_scale, na_rpb, mla_q_norm, mla_w_uq,
           mla_kv_norm, mla_w_ukv, w_branch, w_out, w_group_router, w_expert_router, w_exp_gate, w_exp_up,
           w_exp_down):
    batch, seq, _ = x_prompt.shape
    dbatch, dseq, _ = x_sample.shape
    past = cache_na_k.shape[2]
    assert dbatch + 1 <= 8 and dseq % (NA_BLOCK_ROWS * GRID_W) == 0

    w = _prep(dict(w_in=w_in, w_pool=w_pool, pool_scale=pool_scale, mla_q_norm=mla_q_norm, mla_w_uq=mla_w_uq,
                   mla_kv_norm=mla_kv_norm, mla_w_ukv=mla_w_ukv, w_branch=w_branch, w_out=w_out,
                   w_group_router=w_group_router, w_expert_router=w_expert_router, w_exp_gate=w_exp_gate,
                   w_exp_up=w_exp_up, w_exp_down=w_exp_down, g_pre_mix=g_pre_mix, g_post_mix=g_post_mix,
                   g_pre_ffn=g_pre_ffn, g_post_ffn=g_post_ffn))
    rope_tab = _rope_table(dseq)
    bias_tab = _na_bias_table(na_rpb, dseq // GRID_W)
    place = jnp.tile(jnp.pad(jnp.eye(MLA_ROPE_DIM, dtype=F32), ((0, 0), (KPE_LANE, SLOT - KPE_LANE - MLA_ROPE_DIM))),
                     (1, MLA_HEADS)).astype(BF16)

    cvec = jnp.concatenate([c_ctx[None], c, jnp.zeros((8 - 1 - dbatch, D_MODEL), F32)], axis=0)
    mod5 = _modulation(cvec, w_mod, b_mod).reshape(DEPTH, 8, 6, 1, D_MODEL)

    ck = cache_na_k.reshape(dbatch, DEPTH, past, NA_WIDTH)
    cv = cache_na_v.reshape(dbatch, DEPTH, past, NA_WIDTH)

    xp = x_prompt.reshape(batch * seq, D_MODEL)
    assert DEPTH == 2
    caches = None
    for l in range(DEPTH):
        u, nq, nvo, qm, km, vmo, gate, *caches = _inproj(xp, mod5, l, w, seq=seq, rope_tab=None, ctx_out=True,
                                                         tm=seq, prev=caches)
        yna, ymla = _attn_ctx(nq, caches[0], nvo, qm, km, vmo, seq=seq, nk_layer=l if l > 0 else None)
        xp, h2, rg = _merge(xp, u, yna, ymla, gate, mod5, l, w, seq=seq, ctx=True, tm=2 * seq)
        xp = _moe(h2, rg, xp, mod5, l, w, seq=seq, ctx=True)
    new_nk, new_nv, new_ckv, new_kpe = caches

    xs = x_sample.reshape(dbatch * dseq, D_MODEL)
    for l in range(DEPTH):
        u, nq, nvo, qm, km, vmo, gate, nk = _inproj(xs, mod5, l, w, seq=dseq, rope_tab=rope_tab, ctx_out=False, tm=512)
        ckm, cvmo, cnvo = _cache_prep(cache_mla_ckv, cache_mla_kpe, cv, w["w_ukv"], place, l)
        yna = _attn_na(nq, nk, nvo, ck, cnvo, bias_tab, l, batch=dbatch, seq=dseq)
        ymla = _attn_mla(qm, km, vmo, ckm, cvmo, batch=dbatch, seq=dseq, tq=512)
        xs, h2, rg = _merge(xs, u, yna, ymla, gate, mod5, l, w, seq=dseq, ctx=False, tm=512)
        xs = _moe(h2, rg, xs, mod5, l, w, seq=dseq, ctx=False)

    heads = (batch, DEPTH, seq, NA_HEADS, NA_HEAD_DIM)
    return (xp.reshape(batch, seq, D_MODEL), xs.reshape(dbatch, dseq, D_MODEL),
            new_nk.reshape(heads), new_nv.reshape(heads), new_ckv, new_kpe)
```

```python
import functools

import numpy as np
import jax
import jax.numpy as jnp
from jax import lax
from jax.experimental import pallas as pl
from jax.experimental.pallas import tpu as pltpu

F32 = jnp.float32
BF16 = jnp.bfloat16

D_MODEL = 1024
DEPTH = 2
GRID_W = 64
POOL_WINDOWS = (2, 4, 8, 16)
POOL_GC = 128
POOL_WIDTH = 512
NA_HEADS = 8
NA_HEAD_DIM = 64
NA_WIDTH = 512
NA_WIN_ROWS = 8
NA_WIN_COLS = 16
MLA_HEADS = 8
MLA_NOPE_DIM = 64
MLA_ROPE_DIM = 32
MLA_V_DIM = 64
MLA_QK_DIM = 96
MLA_Q_LORA = 384
MLA_KV_LORA = 256
N_GROUPS = 4
EXPERTS_PER_GROUP = 4
N_EXPERTS = 16
EXPERT_FF = 256
ROPE_BASE = 10000.0
EPS = 1e-6
NEG_INF = -1e30

LANES = 128
SLOT = LANES
VMEM_LIMIT = 56 * 1024 * 1024

C_U, C_NQ, C_NK, C_NV, C_CQ, C_CKV, C_KPE, C_G, C_END = 0, 512, 1024, 1536, 2048, 2432, 2688, 2720, 5888
G_WIDTH = 3 * D_MODEL
KPE_LANE = MLA_NOPE_DIM

LOG2E = 1.4426950408889634
VO_W = 2 * SLOT
VO_ALL = (NA_HEADS // 2) * VO_W

NA_BLOCK_ROWS = 4
NA_SLAB_ROWS = 12


def _cparams(sem):
    return pltpu.CompilerParams(dimension_semantics=sem, vmem_limit_bytes=VMEM_LIMIT)


def _rms(x, g):
    return x * lax.rsqrt(jnp.mean(x * x, axis=-1, keepdims=True) + EPS) * g


def _dot(a, b):
    return jnp.dot(a, b, preferred_element_type=F32)


def _dot_nt(a, b):
    return lax.dot_general(a, b, (((1,), (1,)), ((), ())), preferred_element_type=F32)


def _mod_kernel(c_ref, w_ref, b_ref, o_ref):
    c = c_ref[...]
    s = (c * jax.nn.sigmoid(c)).astype(BF16)
    o_ref[...] = _dot(s, w_ref[...].astype(BF16)) + b_ref[...]


def _modulation(cvec, w_mod, b_mod):
    tn = 1536
    n = w_mod.shape[-1]
    return pl.pallas_call(
        _mod_kernel,
        grid=(DEPTH, n // tn),
        in_specs=[
            pl.BlockSpec((8, D_MODEL), lambda l, j: (0, 0)),
            pl.BlockSpec((None, D_MODEL, tn), lambda l, j: (l, 0, j)),
            pl.BlockSpec((None, 1, tn), lambda l, j: (l, 0, j)),
        ],
        out_specs=pl.BlockSpec((None, 8, tn), lambda l, j: (l, 0, j)),
        out_shape=jax.ShapeDtypeStruct((DEPTH, 8, n), F32),
        compiler_params=_cparams(("arbitrary", "arbitrary")),
        name="modulation",
    )(cvec, w_mod, b_mod.reshape(DEPTH, 1, n))


def _rope_slot(x, tab):
    c, s_up, s_dn = tab[:, 0:SLOT], tab[:, SLOT:2 * SLOT], tab[:, 2 * SLOT:3 * SLOT]
    half = MLA_ROPE_DIM // 4
    x_up = pltpu.roll(x, SLOT - half, axis=1)
    x_dn = pltpu.roll(x, half, axis=1)
    return x * c + x_up * s_up + x_dn * s_dn


def _store_vo(vo_ref, v):
    tm = v.shape[0]
    for p in range(NA_HEADS // 2):
        vo_ref[:, p * VO_W:p * VO_W + SLOT] = v[:, p * SLOT:(p + 1) * SLOT].astype(BF16)
        vo_ref[:, p * VO_W + SLOT:(p + 1) * VO_W] = jnp.ones((tm, SLOT), BF16)


def _realign_gate_columns(win_ref, wg_ref, wqk_ref):
    off = C_G % LANES
    lane = lax.broadcasted_iota(jnp.int32, (1, LANES), 1)
    base = C_G - off
    nxt = win_ref[:, base:base + LANES].astype(F32)
    kpe = pltpu.roll(nxt, KPE_LANE - (C_KPE - base), axis=1)
    wqk_ref[:, :MLA_Q_LORA] = win_ref[:, C_CQ:C_CKV]
    wqk_ref[:, MLA_Q_LORA:] = jnp.where((lane >= KPE_LANE) & (lane < KPE_LANE + MLA_ROPE_DIM), kpe, 0.0).astype(BF16)
    for j in range(G_WIDTH // LANES):
        cur = nxt
        nxt = win_ref[:, base + (j + 1) * LANES:base + (j + 2) * LANES].astype(F32)
        tile = pltpu.roll(jnp.where(lane >= off, cur, nxt), LANES - off, axis=1)
        wg_ref[:, j * LANES:(j + 1) * LANES] = tile.astype(BF16)


def _inproj_kernel(*refs, rope, ctx_out, stacked):
    n_in = 9 + (1 if rope else 0) + (4 if stacked else 0)
    ins, outs, (wg_ref, wqk_ref) = refs[:n_in], refs[n_in:-2], refs[-2:]
    x_ref, sh_ref, sc_ref, g_ref, win_ref, qn_ref, wuq_ref, kvn_ref, wukv_ref = ins[:9]
    tab_ref = ins[9] if rope else None
    prev = ins[n_in - 4:] if stacked else None
    if ctx_out:
        u_ref, nq_ref, nvo_ref, qm_ref, km_ref, vmo_ref, gate_ref, nk_ref, nv_ref, ckv_ref, kpe_ref = outs
    else:
        u_ref, nq_ref, nvo_ref, qm_ref, km_ref, vmo_ref, gate_ref, nk_ref = outs

    def put(ref, k, val):
        if stacked:
            ref[0] = prev[k][...]
            ref[1] = val
        else:
            ref[...] = val

    @pl.when(pl.program_id(0) == 0)
    def _():
        _realign_gate_columns(win_ref, wg_ref, wqk_ref)

    x = x_ref[...]
    h = (_rms(x, g_ref[...]) * (1.0 + sc_ref[...]) + sh_ref[...]).astype(BF16)

    def seg(a, b):
        return _dot(h, win_ref[:, a:b])

    u_ref[...] = seg(C_U, C_NQ).astype(u_ref.dtype)
    nq_ref[...] = (seg(C_NQ, C_NK) * (NA_HEAD_DIM ** -0.5 * LOG2E)).astype(BF16)
    nk = seg(C_NK, C_NV)
    nv = seg(C_NV, C_CQ)
    if ctx_out:
        put(nk_ref, 0, nk)
        put(nv_ref, 1, nv)
    else:
        nk_ref[...] = nk.astype(BF16)
    _store_vo(nvo_ref, nv)
    gate_ref[...] = jax.nn.sigmoid(_dot(h, wg_ref[...])).astype(BF16)

    cqk = _dot(h, wqk_ref[...])
    cq = _rms(cqk[:, :MLA_Q_LORA], qn_ref[...]).astype(BF16)
    q = _dot(cq, wuq_ref[...])
    tab = tab_ref[...] if rope else None
    for hd in range(MLA_HEADS):
        qs = q[:, hd * SLOT:(hd + 1) * SLOT]
        if rope:
            qs = _rope_slot(qs, tab)
        qm_ref[:, hd * SLOT:(hd + 1) * SLOT] = (qs * (MLA_QK_DIM ** -0.5 * LOG2E)).astype(BF16)

    ckv = _rms(seg(C_CKV, C_KPE), kvn_ref[...])
    kpe_slot = cqk[:, MLA_Q_LORA:]
    if ctx_out:
        put(ckv_ref, 2, ckv)
        put(kpe_ref, 3, kpe_slot[:, KPE_LANE:KPE_LANE + MLA_ROPE_DIM])
    if rope:
        kpe_slot = _rope_slot(kpe_slot, tab)
    kv = _dot(ckv.astype(BF16), wukv_ref[...])
    for hd in range(MLA_HEADS):
        km_ref[:, hd * SLOT:(hd + 1) * SLOT] = (kv[:, hd * SLOT:(hd + 1) * SLOT] + kpe_slot).astype(BF16)
    _store_vo(vmo_ref, kv[:, MLA_HEADS * SLOT:])


def _inproj(x, mod5, l, w, *, seq, rope_tab, ctx_out, tm, prev=None):
    n = x.shape[0]
    steps_per_seq = seq // tm if not ctx_out else 0
    stacked = prev is not None
    assert not stacked or (ctx_out and tm == seq and l == 1)

    def mrow(i):
        return 0 if ctx_out else 1 + i // steps_per_seq

    const = lambda i: (l, 0, 0)
    in_specs = [
        pl.BlockSpec((tm, D_MODEL), lambda i: (i, 0)),
        pl.BlockSpec((None, None, None, 1, D_MODEL), lambda i: (l, mrow(i), 0, 0, 0)),
        pl.BlockSpec((None, None, None, 1, D_MODEL), lambda i: (l, mrow(i), 1, 0, 0)),
        pl.BlockSpec((None, 1, D_MODEL), const),
        pl.BlockSpec((None, D_MODEL, C_END), const, pipeline_mode=pl.Buffered(1)),
        pl.BlockSpec((None, 1, MLA_Q_LORA), const),
        pl.BlockSpec((None, MLA_Q_LORA, MLA_HEADS * SLOT), const, pipeline_mode=pl.Buffered(1)),
        pl.BlockSpec((None, 1, MLA_KV_LORA), const),
        pl.BlockSpec((None, MLA_KV_LORA, MLA_HEADS * SLOT + 512), const, pipeline_mode=pl.Buffered(1)),
    ]
    args = [x, mod5, mod5, w["g_pre_mix"], w["w_in"], w["q_norm"], w["w_uq"], w["kv_norm"], w["w_ukv"]]
    rope = rope_tab is not None
    if rope:
        in_specs.append(pl.BlockSpec((tm, 3 * SLOT), lambda i: (i % steps_per_seq, 0)))
        args.append(rope_tab)
    row = lambda wd: pl.BlockSpec((tm, wd), lambda i: (i, 0))
    rows_of = lambda wd, dt: jax.ShapeDtypeStruct((n, wd), dt)
    widths = [512, 512, VO_ALL, MLA_HEADS * SLOT, MLA_HEADS * SLOT, VO_ALL, 3 * D_MODEL]
    out_specs = [row(wd) for wd in widths]
    out_shape = [rows_of(wd, BF16) for wd in widths]
    cache_w = [512, 512, MLA_KV_LORA, MLA_ROPE_DIM] if ctx_out else []
    if not ctx_out:
        out_specs.append(row(512))
        out_shape.append(rows_of(512, BF16))
    elif stacked:
        in_specs += [row(wd) for wd in cache_w]
        args += list(prev)
        out_specs += [pl.BlockSpec((None, DEPTH, seq, wd), lambda i: (i, 0, 0, 0)) for wd in cache_w]
        out_shape += [jax.ShapeDtypeStruct((n // seq, DEPTH, seq, wd), F32) for wd in cache_w]
    else:
        out_specs += [row(wd) for wd in cache_w]
        out_shape += [rows_of(wd, F32) for wd in cache_w]
    return pl.pallas_call(
        functools.partial(_inproj_kernel, rope=rope, ctx_out=ctx_out, stacked=stacked),
        grid=(n // tm,),
        in_specs=in_specs,
        out_specs=out_specs,
        out_shape=out_shape,
        scratch_shapes=[pltpu.VMEM((D_MODEL, G_WIDTH), BF16), pltpu.VMEM((D_MODEL, MLA_Q_LORA + SLOT), BF16)],
        compiler_params=_cparams(("arbitrary",)),
        name="inproj_ctx" if ctx_out else "inproj_lat",
    )(*args)


def _attend(q_h, segs):
    scores = []
    for k, _, bias in segs:
        s = _dot_nt(q_h, k)
        if bias is not None:
            s = s + bias
        scores.append(s)
    m = scores[0].max(axis=-1, keepdims=True)
    for s in scores[1:]:
        m = jnp.maximum(m, s.max(axis=-1, keepdims=True))
    acc = None
    for s, (_, vo, _) in zip(scores, segs):
        o = _dot(jnp.exp2(s - m).astype(BF16), vo)
        acc = o if acc is None else acc + o
    return acc[:, :SLOT] / acc[:, SLOT:]


def _attn_heads(out_ref, q_of, segs_of):
    lo = lax.broadcasted_iota(jnp.int32, (1, SLOT), 1) < NA_HEAD_DIM
    for p in range(NA_HEADS // 2):
        o0, o1 = [_attend(q_of(2 * p + e), segs_of(2 * p + e)) for e in range(2)]
        out_ref[:, p * SLOT:(p + 1) * SLOT] = jnp.where(lo, o0, o1).astype(out_ref.dtype)


def _pair_q(q_ref):
    lo = lax.broadcasted_iota(jnp.int32, (1, SLOT), 1) < NA_HEAD_DIM

    def q_of(h):
        q_pair = q_ref[:, (h // 2) * SLOT:(h // 2 + 1) * SLOT]
        keep = lo if h % 2 == 0 else jnp.logical_not(lo)
        return jnp.where(keep, q_pair, jnp.zeros_like(q_pair))
    return q_of


def _slot(ref, i, rows=slice(None)):
    return ref[rows, i * SLOT:(i + 1) * SLOT].astype(BF16)


def _vo(ref, h, rows=slice(None)):
    return ref[rows, (h // 2) * VO_W:(h // 2 + 1) * VO_W]


def _attn_ctx_kernel(nq_ref, nk_ref, nvo_ref, qm_ref, km_ref, vmo_ref, yna_ref, ymla_ref):
    _attn_heads(yna_ref, _pair_q(nq_ref), lambda h: [(_slot(nk_ref, h // 2), _vo(nvo_ref, h), None)])
    _attn_heads(ymla_ref, lambda h: _slot(qm_ref, h), lambda h: [(_slot(km_ref, h), _vo(vmo_ref, h), None)])


def _attn_ctx(nq, nk, nvo, qm, km, vmo, *, seq, nk_layer=None):
    n = nq.shape[0]
    spec = lambda wd: pl.BlockSpec((seq, wd), lambda b: (b, 0))
    nk_spec = spec(512) if nk_layer is None else pl.BlockSpec((None, None, seq, 512), lambda b: (b, nk_layer, 0, 0))
    return pl.pallas_call(
        _attn_ctx_kernel,
        grid=(n // seq,),
        in_specs=[spec(512), nk_spec, spec(VO_ALL), spec(1024), spec(1024), spec(VO_ALL)],
        out_specs=[spec(512), spec(512)],
        out_shape=[jax.ShapeDtypeStruct((n, 512), BF16)] * 2,
        compiler_params=_cparams(("arbitrary",)),
        name="attn_ctx",
    )(nq, nk, nvo, qm, km, vmo)


def _attn_na_kernel(nq_ref, nk_ref, nvo_ref, ck_ref, cvo_ref, bias_ref, out_ref, *, n_blocks):
    i = pl.program_id(1)
    ks = jnp.clip(NA_BLOCK_ROWS * i - NA_WIN_ROWS // 2, 0, n_blocks * NA_BLOCK_ROWS - NA_SLAB_ROWS)
    slab = pl.ds(pl.multiple_of(ks * GRID_W, GRID_W), NA_SLAB_ROWS * GRID_W)

    def segs_of(h):
        return [(_slot(nk_ref, h // 2, slab), _vo(nvo_ref, h, slab), bias_ref[h].astype(F32)),
                (_slot(ck_ref, h // 2), _vo(cvo_ref, h), None)]

    _attn_heads(out_ref, _pair_q(nq_ref), segs_of)


def _attn_na(nq, nk, nvo, cache_k, cvo, bias_tab, l, *, batch, seq):
    tq = NA_BLOCK_ROWS * GRID_W
    nb = seq // tq
    past = cache_k.shape[2]

    def cls(i):
        return jnp.where(i == 0, 0, jnp.where(i == nb - 1, 2, 1))

    return pl.pallas_call(
        functools.partial(_attn_na_kernel, n_blocks=nb),
        grid=(batch, nb),
        in_specs=[
            pl.BlockSpec((tq, 512), lambda b, i: (b * nb + i, 0)),
            pl.BlockSpec((seq, 512), lambda b, i: (b, 0)),
            pl.BlockSpec((seq, VO_ALL), lambda b, i: (b, 0)),
            pl.BlockSpec((None, None, past, 512), lambda b, i: (b, l, 0, 0)),
            pl.BlockSpec((None, past, VO_ALL), lambda b, i: (b, 0, 0)),
            pl.BlockSpec((None, None, NA_HEADS, tq, NA_SLAB_ROWS * GRID_W), lambda b, i: (l, cls(i), 0, 0, 0)),
        ],
        out_specs=pl.BlockSpec((tq, 512), lambda b, i: (b * nb + i, 0)),
        out_shape=jax.ShapeDtypeStruct((batch * seq, 512), BF16),
        compiler_params=_cparams(("arbitrary", "arbitrary")),
        name="attn_na_lat",
    )(nq, nk, nvo, cache_k, cvo, bias_tab)


def _cache_prep_kernel(ckv_ref, kpe_ref, nv_ref, wukv_ref, place_ref, km_ref, vmo_ref, nvo_ref):
    kv = _dot(ckv_ref[...].astype(BF16), wukv_ref[...])
    kslots = kv[:, :MLA_HEADS * SLOT] + _dot(kpe_ref[...].astype(BF16), place_ref[...])
    km_ref[...] = kslots.astype(BF16)
    _store_vo(vmo_ref, kv[:, MLA_HEADS * SLOT:])
    _store_vo(nvo_ref, nv_ref[...])


def _cache_prep(cache_ckv, cache_kpe, cache_nv, w_ukv, place, l):
    batch, _, past, _ = cache_ckv.shape
    tout = pl.BlockSpec((None, past, VO_ALL), lambda b: (b, 0, 0))
    tshape = jax.ShapeDtypeStruct((batch, past, VO_ALL), BF16)
    return pl.pallas_call(
        _cache_prep_kernel,
        grid=(batch,),
        in_specs=[
            pl.BlockSpec((None, None, past, MLA_KV_LORA), lambda b: (b, l, 0, 0)),
            pl.BlockSpec((None, None, past, MLA_ROPE_DIM), lambda b: (b, l, 0, 0)),
            pl.BlockSpec((None, None, past, 512), lambda b: (b, l, 0, 0)),
            pl.BlockSpec((None, MLA_KV_LORA, MLA_HEADS * SLOT + 512), lambda b: (l, 0, 0)),
            pl.BlockSpec((MLA_ROPE_DIM, MLA_HEADS * SLOT), lambda b: (0, 0)),
        ],
        out_specs=[pl.BlockSpec((None, past, MLA_HEADS * SLOT), lambda b: (b, 0, 0)), tout, tout],
        out_shape=[jax.ShapeDtypeStruct((batch, past, MLA_HEADS * SLOT), BF16), tshape, tshape],
        compiler_params=_cparams(("arbitrary",)),
        name="cache_prep",
    )(cache_ckv, cache_kpe, cache_nv, w_ukv, place)


def _attn_mla_kernel(qm_ref, km_ref, vmo_ref, ckm_ref, cvmo_ref, out_ref):
    _attn_heads(out_ref, lambda h: _slot(qm_ref, h),
                lambda h: [(_slot(km_ref, h), _vo(vmo_ref, h), None), (_slot(ckm_ref, h), _vo(cvmo_ref, h), None)])


def _attn_mla(qm, km, vmo, ckm, cvmo, *, batch, seq, tq):
    nb = seq // tq
    past = ckm.shape[1]
    return pl.pallas_call(
        _attn_mla_kernel,
        grid=(batch, nb),
        in_specs=[
            pl.BlockSpec((tq, MLA_HEADS * SLOT), lambda b, i: (b * nb + i, 0)),
            pl.BlockSpec((seq, MLA_HEADS * SLOT), lambda b, i: (b, 0)),
            pl.BlockSpec((seq, VO_ALL), lambda b, i: (b, 0)),
            pl.BlockSpec((None, past, MLA_HEADS * SLOT), lambda b, i: (b, 0, 0)),
            pl.BlockSpec((None, past, VO_ALL), lambda b, i: (b, 0, 0)),
        ],
        out_specs=pl.BlockSpec((tq, 512), lambda b, i: (b * nb + i, 0)),
        out_shape=jax.ShapeDtypeStruct((batch * seq, 512), BF16),
        compiler_params=_cparams(("arbitrary", "arbitrary")),
        name="attn_mla_lat",
    )(qm, km, vmo, ckm, cvmo)


POOL_HALO = 16


GROUP_LANE = N_EXPERTS


ROUTE_ROWS = 32


def _route(logits):
    lt = logits.T[:ROUTE_ROWS]
    idx = lax.broadcasted_iota(jnp.int32, lt.shape, 0).astype(F32)
    big = jnp.float32(1 << 20)
    is_g = idx < N_GROUPS
    gl = jnp.where(is_g, lt, NEG_INF)
    gmax = gl.max(axis=0, keepdims=True)
    ge = jnp.where(is_g, jnp.exp(gl - gmax), 0.0)
    gp = ge / ge.sum(axis=0, keepdims=True)
    g_w = jnp.where(is_g, gp, -1.0).max(axis=0, keepdims=True)
    g_idx = jnp.where(is_g & (gp == g_w), idx, big).min(axis=0, keepdims=True)
    e_idx = idx - N_GROUPS
    in_grp = (e_idx >= g_idx * EXPERTS_PER_GROUP) & (e_idx < (g_idx + 1) * EXPERTS_PER_GROUP)
    el = jnp.where(in_grp, lt, NEG_INF)
    emax = el.max(axis=0, keepdims=True)
    ee = jnp.where(in_grp, jnp.exp(el - emax), 0.0)
    ep = ee / ee.sum(axis=0, keepdims=True)
    p1 = jnp.where(in_grp, ep, -1.0).max(axis=0, keepdims=True)
    i1 = jnp.where(in_grp & (ep == p1), idx, big).min(axis=0, keepdims=True)
    rest = in_grp & (idx != i1)
    p2 = jnp.where(rest, ep, -1.0).max(axis=0, keepdims=True)
    i2 = jnp.where(rest & (ep == p2), idx, big).min(axis=0, keepdims=True)
    tot = p1 + p2
    out_t = (jnp.where(idx == i1 - N_GROUPS, g_w * (p1 / tot), 0.0) + jnp.where(idx == i2 - N_GROUPS, g_w * (p2 / tot), 0.0)
             + jnp.where(idx == GROUP_LANE, g_idx, 0.0))
    return jnp.concatenate([out_t, jnp.zeros((LANES - ROUTE_ROWS, lt.shape[1]), F32)], axis=0).T


def _merge_kernel(x_ref, u_ref, yna_ref, ymla_ref, gate_ref, gt1_ref, sh2_ref, sc2_ref, gpost_ref, gpre_ref,
                  wpool_ref, pscale_ref, wbr_ref, wout_ref, wr2_ref, wr1_ref,
                  xo_ref, h2_ref, rg_ref, *, seq, tm, steps_per_seq):
    def pool_diff(uwin, utile, r0, s0):
        rows, nwin = utile.shape[0], uwin.shape[0]
        t = r0 + lax.broadcasted_iota(jnp.int32, (rows, nwin), 0)
        s = s0 + lax.broadcasted_iota(jnp.int32, (rows, nwin), 1)
        t1 = r0 + lax.broadcasted_iota(jnp.int32, (rows, 1), 0)
        ds = []
        for g, w in enumerate(POOL_WINDOWS):
            lo = jnp.maximum(t - w // 2, 0)
            hi = jnp.minimum(t + (w - w // 2), seq)
            band = ((s >= lo) & (s < hi)).astype(BF16)
            cnt = (jnp.minimum(t1 + (w - w // 2), seq) - jnp.maximum(t1 - w // 2, 0)).astype(F32)
            sl = slice(g * POOL_GC, (g + 1) * POOL_GC)
            ds.append(_dot(band, uwin[:, sl]) / cnt - utile[:, sl].astype(F32))
        return jnp.concatenate(ds, axis=-1).astype(BF16)

    if steps_per_seq == 0:
        d = jnp.concatenate([pool_diff(u_ref[si], u_ref[si], 0, 0) for si in range(tm // seq)], axis=0)
    else:
        nwin = min(tm + 2 * POOL_HALO, seq)
        r0 = (pl.program_id(0) % steps_per_seq) * tm
        s0 = pl.multiple_of(jnp.clip(r0 - POOL_HALO, 0, seq - nwin), POOL_HALO)
        d = pool_diff(u_ref[0, pl.ds(s0, nwin), :], u_ref[0, pl.ds(pl.multiple_of(r0, POOL_HALO), tm), :], r0, s0)
    y_pool = (_dot(d, wpool_ref[...]) * pscale_ref[...]).astype(BF16)

    g = gate_ref[...]
    merged = (g[:, 0:D_MODEL].astype(F32) * _dot(y_pool, wbr_ref[0])
              + g[:, D_MODEL:2 * D_MODEL].astype(F32) * _dot(yna_ref[...], wbr_ref[1])
              + g[:, 2 * D_MODEL:].astype(F32) * _dot(ymla_ref[...], wbr_ref[2]))
    y = _dot(merged.astype(BF16), wout_ref[...])
    xn = x_ref[...] + gt1_ref[...] * _rms(y, gpost_ref[...])
    xo_ref[...] = xn

    h2 = _rms(xn, gpre_ref[...]) * (1.0 + sc2_ref[...]) + sh2_ref[...]
    h_hi = h2.astype(BF16)
    h2_ref[...] = h_hi
    h_lo = (h2 - h_hi.astype(F32)).astype(BF16)
    two = _dot(h_hi, wr2_ref[...])
    logits = two[:, :LANES] + two[:, LANES:] + _dot(h_lo, wr1_ref[...])
    rg_ref[...] = _route(logits)


def _merge(x, u, yna, ymla, gate, mod5, l, w, *, seq, ctx, tm):
    n = x.shape[0]
    nseq = n // seq
    if tm >= seq:
        assert ctx and tm % seq == 0
        steps_per_seq = 0
        u_spec = pl.BlockSpec((tm // seq, seq, 512), lambda i: (i, 0, 0))
    else:
        steps_per_seq = seq // tm
        u_spec = pl.BlockSpec((1, seq, 512), lambda i: (i // steps_per_seq, 0, 0))

    def mrow(i):
        return 0 if ctx else 1 + i // steps_per_seq

    const = lambda i: (l, 0, 0)
    modspec = lambda k: pl.BlockSpec((None, None, None, 1, D_MODEL), lambda i: (l, mrow(i), k, 0, 0))
    row = lambda wd: pl.BlockSpec((tm, wd), lambda i: (i, 0))
    return pl.pallas_call(
        functools.partial(_merge_kernel, seq=seq, tm=tm, steps_per_seq=steps_per_seq),
        grid=(n // tm,),
        in_specs=[
            row(D_MODEL),
            u_spec,
            row(512), row(512), row(3 * D_MODEL),
            modspec(2), modspec(3), modspec(4),
            pl.BlockSpec((None, 1, D_MODEL), const),
            pl.BlockSpec((None, 1, D_MODEL), const),
            pl.BlockSpec((None, 512, 512), const),
            pl.BlockSpec((None, 1, 512), const),
            pl.BlockSpec((None, 3, 512, D_MODEL), lambda i: (l, 0, 0, 0)),
            pl.BlockSpec((None, D_MODEL, D_MODEL), const),
            pl.BlockSpec((None, D_MODEL, 2 * LANES), const),
            pl.BlockSpec((None, D_MODEL, LANES), const),
        ],
        out_specs=[row(D_MODEL), row(D_MODEL), row(LANES)],
        out_shape=[jax.ShapeDtypeStruct((n, D_MODEL), F32), jax.ShapeDtypeStruct((n, D_MODEL), BF16),
                   jax.ShapeDtypeStruct((n, LANES), F32)],
        compiler_params=_cparams(("arbitrary",)),
        name="merge_ctx" if ctx else "merge_lat",
    )(x, u.reshape(nseq, seq, 512), yna, ymla, gate, mod5, mod5, mod5, w["g_post_mix"], w["g_pre_ffn"],
      w["w_pool_bd"], w["pool_scale"], w["w_branch"], w["w_out"], w["w_r2"], w["w_r1"])


MOE_TM = 1024
MOE_SUB = 512
MOE_ROWS = 160
MOE_ROWS_PAD = 256


def _moe_kernel(h_ref, rg_ref, x_ref, gt2_ref, gpost_ref, w1_ref, w3_ref, w2_ref, o_ref):
    tm = h_ref.shape[0]
    subs = [slice(s * MOE_SUB, (s + 1) * MOE_SUB) for s in range(tm // MOE_SUB)]
    acc_ref = o_ref
    rg = rg_ref[...]
    lane = lax.broadcasted_iota(jnp.int32, rg.shape, 1)
    grp = jnp.where(lane == GROUP_LANE, rg, 0.0).sum(axis=-1, keepdims=True)
    member = ((lane.astype(F32) == grp) & (lane < N_GROUPS)).astype(F32)
    t_row = lax.broadcasted_iota(jnp.int32, (MOE_SUB, MOE_SUB), 0)
    t_col = lax.broadcasted_iota(jnp.int32, (MOE_SUB, MOE_SUB), 1)
    earlier, later = (t_col < t_row).astype(BF16), (t_row < t_col).astype(BF16)
    member_t = member.T[:8]
    rank_col = jnp.concatenate([_dot(earlier, member[sl].astype(BF16)) for sl in subs], axis=0)
    rank_row = jnp.concatenate([_dot(member_t[:, sl].astype(BF16), later) for sl in subs], axis=1)
    count = functools.reduce(jnp.maximum, [member[sl].sum(axis=0, keepdims=True) for sl in subs])
    rg_hi = rg.astype(BF16)
    rg_lo = (rg - rg_hi.astype(F32)).astype(BF16)
    row_id = lax.broadcasted_iota(jnp.int32, (MOE_ROWS, 1), 0).astype(F32)
    col_id = lax.broadcasted_iota(jnp.int32, (1, MOE_ROWS_PAD), 1).astype(F32)
    lane1 = lax.broadcasted_iota(jnp.int32, (1, LANES), 1)

    def experts(g, hs, gs):
        hid = []
        for e in range(EXPERTS_PER_GROUP):
            k = g * EXPERTS_PER_GROUP + e
            a = _dot(hs, w1_ref[k])
            b = _dot(hs, w3_ref[k])
            hid.append(((a * jax.nn.sigmoid(a)) * b * gs[:, k:k + 1]).astype(BF16))
        return _dot(jnp.concatenate(hid, axis=-1), w2_ref[g]).astype(BF16)

    n_slots = N_GROUPS * MOE_ROWS
    slots_pad = -(-n_slots // LANES) * LANES
    slot_row = lax.broadcasted_iota(jnp.int32, (n_slots, 1), 0).astype(F32)
    slot_col = lax.broadcasted_iota(jnp.int32, (1, slots_pad), 1).astype(F32)
    dest_row = sum(jnp.where((member_t[g:g + 1] > 0.0) & (rank_row[g:g + 1] < MOE_ROWS),
                             rank_row[g:g + 1] + (g * MOE_ROWS + 1), 0.0) for g in range(N_GROUPS)) - 1.0
    dest_col = sum(jnp.where((member[:, g:g + 1] > 0.0) & (rank_col[:, g:g + 1] < MOE_ROWS),
                             rank_col[:, g:g + 1] + (g * MOE_ROWS + 1), 0.0) for g in range(N_GROUPS)) - 1.0
    hs_sub, gs_sub = [], []
    for sl in subs:
        take = (dest_row[:, sl] == slot_row).astype(BF16)
        hs_sub.append(_dot(take, h_ref[sl, :]).astype(BF16))
        gs_sub.append(_dot(take, rg_hi[sl]) + _dot(take, rg_lo[sl]))
    parts = []
    for g in range(N_GROUPS):
        rows = slice(g * MOE_ROWS, (g + 1) * MOE_ROWS)
        parts.append(experts(g, jnp.concatenate([h[rows] for h in hs_sub], axis=0),
                             jnp.concatenate([q[rows] for q in gs_sub], axis=0)))
    pad = [jnp.zeros((slots_pad - n_slots, D_MODEL), BF16)] if slots_pad > n_slots else []
    for s, sl in enumerate(subs):
        give = (dest_col[sl] == slot_col).astype(BF16)
        rows = jnp.concatenate([p[s * MOE_ROWS:(s + 1) * MOE_ROWS] for p in parts] + pad, axis=0)
        acc_ref[sl, :] = _dot(give, rows)

    pad = jnp.zeros((MOE_ROWS_PAD - MOE_ROWS, D_MODEL), BF16)
    for g in range(N_GROUPS):
        rr, mr = rank_row[g:g + 1, :], member_t[g:g + 1, :]
        rc, mc = rank_col[:, g:g + 1], member[:, g:g + 1]
        n_g = jnp.max(jnp.where(lane1 == g, count, 0.0)).astype(jnp.int32)

        def chunk(c, carry, g=g, rr=rr, mr=mr, rc=rc, mc=mc):
            base = (c * MOE_ROWS).astype(F32)
            hs, gs = [], []
            for sl in subs:
                take = ((rr[:, sl] - base == row_id) & (mr[:, sl] > 0.0)).astype(BF16)
                hs.append(_dot(take, h_ref[sl, :]).astype(BF16))
                gs.append(_dot(take, rg_hi[sl]) + _dot(take, rg_lo[sl]))
            part = experts(g, jnp.concatenate(hs, axis=0), jnp.concatenate(gs, axis=0))
            for s, sl in enumerate(subs):
                rel = rc[sl] - base
                give = ((rel == col_id) & (rel < MOE_ROWS) & (mc[sl] > 0.0)).astype(BF16)
                rows = jnp.concatenate([part[s * MOE_ROWS:(s + 1) * MOE_ROWS], pad], axis=0)
                acc_ref[sl, :] += _dot(give, rows)
            return carry

        lax.fori_loop(1, (n_g + MOE_ROWS - 1) // MOE_ROWS, chunk, 0)

    o_ref[...] = x_ref[...] + gt2_ref[...] * _rms(acc_ref[...], gpost_ref[...])


def _moe(h2, rg, x, mod5, l, w, *, seq, ctx):
    n = x.shape[0]
    tm = MOE_TM
    steps_per_seq = seq // tm if not ctx else 0

    def mrow(i):
        return 0 if ctx else 1 + i // steps_per_seq

    row = lambda wd: pl.BlockSpec((tm, wd), lambda i: (i, 0))
    once = pl.Buffered(1)
    wspec = lambda a, b: pl.BlockSpec((None, N_EXPERTS, a, b), lambda i: (l, 0, 0, 0), pipeline_mode=once)
    return pl.pallas_call(
        _moe_kernel,
        grid=(n // tm,),
        in_specs=[
            row(D_MODEL), row(LANES), row(D_MODEL),
            pl.BlockSpec((None, None, None, 1, D_MODEL), lambda i: (l, mrow(i), 5, 0, 0)),
            pl.BlockSpec((None, 1, D_MODEL), lambda i: (l, 0, 0)),
            wspec(D_MODEL, EXPERT_FF), wspec(D_MODEL, EXPERT_FF),
            pl.BlockSpec((None, N_GROUPS, EXPERTS_PER_GROUP * EXPERT_FF, D_MODEL), lambda i: (l, 0, 0, 0),
                         pipeline_mode=once),
        ],
        out_specs=row(D_MODEL),
        out_shape=jax.ShapeDtypeStruct((n, D_MODEL), F32),
        compiler_params=_cparams(("arbitrary",)),
        name="moe_ctx" if ctx else "moe_lat",
    )(h2, rg, x, mod5, w["g_post_ffn"], w["w1"], w["w3"], w["w2"])


def _rope_table(seq):
    t = np.arange(seq)
    n_freq = MLA_ROPE_DIM // 4
    inv = jnp.asarray(ROPE_BASE, F32) ** (-jnp.arange(n_freq, dtype=F32) / n_freq)
    ang_r = jnp.asarray(t // GRID_W, F32)[:, None] * inv
    ang_c = jnp.asarray(t % GRID_W, F32)[:, None] * inv
    cos = jnp.cos(jnp.stack([ang_r, ang_c], axis=1))
    sin = jnp.sin(jnp.stack([ang_r, ang_c], axis=1))
    zeros = jnp.zeros_like(sin)
    c32 = jnp.stack([cos, cos], axis=2).reshape(seq, MLA_ROPE_DIM)
    up32 = jnp.stack([-sin, zeros], axis=2).reshape(seq, MLA_ROPE_DIM)
    dn32 = jnp.stack([zeros, sin], axis=2).reshape(seq, MLA_ROPE_DIM)

    def slot(v, fill):
        return jnp.concatenate([jnp.full((seq, KPE_LANE), fill, F32), v,
                                jnp.full((seq, SLOT - KPE_LANE - MLA_ROPE_DIM), fill, F32)], axis=1)

    return jnp.concatenate([slot(c32, 1.0), slot(up32, 0.0), slot(dn32, 0.0)], axis=1)


def _na_row_classes(n_rows):
    i = np.arange(NA_BLOCK_ROWS)[:, None]
    j = np.arange(NA_SLAB_ROWS)[None, :]
    last_ks = n_rows - NA_SLAB_ROWS
    last_q0 = n_rows - NA_BLOCK_ROWS
    specs = [
        (j < NA_WIN_ROWS, j - i),
        ((j >= i) & (j < i + NA_WIN_ROWS), j - NA_WIN_ROWS // 2 - i),
        (last_ks + j >= n_rows - NA_WIN_ROWS, last_ks + j - last_q0 - i),
    ]
    ok = np.stack([np.broadcast_to(o, (NA_BLOCK_ROWS, NA_SLAB_ROWS)) for o, _ in specs])
    dr = np.stack([np.clip(d + NA_WIN_ROWS - 1, 0, 2 * NA_WIN_ROWS - 2) for _, d in specs])
    return ok, dr


def _bias_kernel(rpb_ref, out_ref, *, row_ok, row_dr):
    reach = NA_WIN_COLS - 1
    rows = rpb_ref[...] * LOG2E
    v_lo = pltpu.roll(rows, SLOT - reach, axis=1)
    v_hi = pltpu.roll(rows, GRID_W - reach, axis=1)
    lane1 = lax.broadcasted_iota(jnp.int32, (1, SLOT), 1)
    mid = (lane1 >= GRID_W - reach) & (lane1 <= GRID_W + reach)
    c = lax.broadcasted_iota(jnp.int32, (GRID_W, SLOT), 0)
    lane = lax.broadcasted_iota(jnp.int32, (GRID_W, SLOT), 1)
    kc = lane & (GRID_W - 1)
    c0 = jnp.clip(c - NA_WIN_COLS // 2, 0, GRID_W - NA_WIN_COLS)
    in_col = (kc >= c0) & (kc < c0 + NA_WIN_COLS)
    first = lane < GRID_W
    for k in range(3):
        for i in range(NA_BLOCK_ROWS):
            for jp in range(NA_SLAB_ROWS // 2):
                ok_a, ok_b = bool(row_ok[k, i, 2 * jp]), bool(row_ok[k, i, 2 * jp + 1])
                a, b = int(row_dr[k, i, 2 * jp]), int(row_dr[k, i, 2 * jp + 1])
                rs = slice(i * GRID_W, (i + 1) * GRID_W)
                cs = slice(jp * SLOT, (jp + 1) * SLOT)
                if not (ok_a or ok_b):
                    out_ref[k, rs, cs] = jnp.full((GRID_W, SLOT), NEG_INF, out_ref.dtype)
                    continue
                v = jnp.where(mid, v_hi[b:b + 1], v_lo[a:a + 1])
                tile = pltpu.roll(jnp.broadcast_to(v, (GRID_W, SLOT)), 0, axis=1, stride=1, stride_axis=0)
                ok = in_col
                if not ok_b:
                    ok = ok & first
                if not ok_a:
                    ok = ok & jnp.logical_not(first)
                out_ref[k, rs, cs] = jnp.where(ok, tile, NEG_INF).astype(out_ref.dtype)


def _na_bias_table(rpb, n_rows):
    n_dr, n_dc = rpb.shape[-2:]
    row_ok, row_dr = _na_row_classes(n_rows)
    rpb_p = jnp.pad(rpb, ((0, 0), (0, 0), (0, 0), (0, SLOT - n_dc)))
    tq, tk = NA_BLOCK_ROWS * GRID_W, NA_SLAB_ROWS * GRID_W
    return pl.pallas_call(
        functools.partial(_bias_kernel, row_ok=row_ok, row_dr=row_dr),
        grid=(DEPTH, NA_HEADS),
        in_specs=[pl.BlockSpec((None, None, n_dr, SLOT), lambda l, h: (l, h, 0, 0))],
        out_specs=pl.BlockSpec((None, 3, None, tq, tk), lambda l, h: (l, 0, h, 0, 0)),
        out_shape=jax.ShapeDtypeStruct((DEPTH, 3, NA_HEADS, tq, tk), BF16),
        compiler_params=_cparams(("arbitrary", "arbitrary")),
        name="na_bias_table",
    )(rpb_p)


def _prep(p):
    L = DEPTH
    w_in = p["w_in"]
    w_in_p = jnp.pad(w_in.astype(BF16), ((0, 0), (0, 0), (0, C_END - w_in.shape[-1])))
    w_uq = jnp.pad(p["mla_w_uq"].reshape(L, MLA_Q_LORA, MLA_HEADS, MLA_QK_DIM),
                   ((0, 0), (0, 0), (0, 0), (0, SLOT - MLA_QK_DIM))).reshape(L, MLA_Q_LORA, MLA_HEADS * SLOT)
    ukv = p["mla_w_ukv"].reshape(L, MLA_KV_LORA, MLA_HEADS, MLA_NOPE_DIM + MLA_V_DIM)
    uk = jnp.pad(ukv[..., :MLA_NOPE_DIM], ((0, 0), (0, 0), (0, 0), (0, SLOT - MLA_NOPE_DIM)))
    w_ukv = jnp.concatenate([uk.reshape(L, MLA_KV_LORA, MLA_HEADS * SLOT),
                             ukv[..., MLA_NOPE_DIM:].reshape(L, MLA_KV_LORA, MLA_HEADS * MLA_V_DIM)], axis=-1)
    w_pool_bd = jnp.einsum("lgcd,gh->lgchd", p["w_pool"], jnp.eye(4, dtype=F32)).reshape(L, POOL_WIDTH, POOL_WIDTH)
    w_r = jnp.concatenate([p["w_group_router"], p["w_expert_router"]], axis=-1)
    r_hi = w_r.astype(BF16)
    r_lo = (w_r - r_hi.astype(F32)).astype(BF16)
    padl = lambda a: jnp.pad(a, ((0, 0), (0, 0), (0, LANES - a.shape[-1])))
    vec = lambda a: a.reshape(L, 1, a.shape[-1])
    return {
        "w_in": w_in_p,
        "w_uq": w_uq.astype(BF16),
        "w_ukv": w_ukv.astype(BF16),
        "w_pool_bd": w_pool_bd.astype(BF16),
        "w_branch": p["w_branch"].astype(BF16),
        "w_out": p["w_out"].astype(BF16),
        "w_r2": jnp.concatenate([padl(r_hi), padl(r_lo)], axis=-1),
        "w_r1": padl(r_hi),
        "w1": p["w_exp_gate"].astype(BF16),
        "w3": p["w_exp_up"].astype(BF16),
        "w2": p["w_exp_down"].astype(BF16).reshape(L, N_GROUPS, EXPERTS_PER_GROUP * EXPERT_FF, D_MODEL),
        "g_pre_mix": vec(p["g_pre_mix"]), "g_post_mix": vec(p["g_post_mix"]),
        "g_pre_ffn": vec(p["g_pre_ffn"]), "g_post_ffn": vec(p["g_post_ffn"]),
        "pool_scale": vec(p["pool_scale"]), "q_norm": vec(p["mla_q_norm"]), "kv_norm": vec(p["mla_kv_norm"]),
    }


def kernel(x_prompt, x_sample, cache_na_k, cache_na_v, cache_mla_ckv, cache_mla_kpe, c, c_ctx, w_mod, b_mod,
           g_pre_mix, g_post_mix, g_pre_ffn, g_post_ffn, w_in, w_pool, pool_scale, na_rpb, mla_q_norm, mla_w_uq,
           mla_kv_norm, mla_w_ukv, w_branch, w_out, w_group_router, w_expert_router, w_exp_gate, w_exp_up,
           w_exp_down):
    batch, seq, _ = x_prompt.shape
    dbatch, dseq, _ = x_sample.shape
    past = cache_na_k.shape[2]
    assert dbatch + 1 <= 8 and dseq % (NA_BLOCK_ROWS * GRID_W) == 0

    w = _prep(dict(w_in=w_in, w_pool=w_pool, pool_scale=pool_scale, mla_q_norm=mla_q_norm, mla_w_uq=mla_w_uq,
                   mla_kv_norm=mla_kv_norm, mla_w_ukv=mla_w_ukv, w_branch=w_branch, w_out=w_out,
                   w_group_router=w_group_router, w_expert_router=w_expert_router, w_exp_gate=w_exp_gate,
                   w_exp_up=w_exp_up, w_exp_down=w_exp_down, g_pre_mix=g_pre_mix, g_post_mix=g_post_mix,
                   g_pre_ffn=g_pre_ffn, g_post_ffn=g_post_ffn))
    rope_tab = _rope_table(dseq)
    bias_tab = _na_bias_table(na_rpb, dseq // GRID_W)
    place = jnp.tile(jnp.pad(jnp.eye(MLA_ROPE_DIM, dtype=F32), ((0, 0), (KPE_LANE, SLOT - KPE_LANE - MLA_ROPE_DIM))),
                     (1, MLA_HEADS)).astype(BF16)

    cvec = jnp.concatenate([c_ctx[None], c, jnp.zeros((8 - 1 - dbatch, D_MODEL), F32)], axis=0)
    mod5 = _modulation(cvec, w_mod, b_mod).reshape(DEPTH, 8, 6, 1, D_MODEL)

    ck = cache_na_k.reshape(dbatch, DEPTH, past, NA_WIDTH)
    cv = cache_na_v.reshape(dbatch, DEPTH, past, NA_WIDTH)

    xp = x_prompt.reshape(batch * seq, D_MODEL)
    assert DEPTH == 2
    caches = None
    for l in range(DEPTH):
        u, nq, nvo, qm, km, vmo, gate, *caches = _inproj(xp, mod5, l, w, seq=seq, rope_tab=None, ctx_out=True,
                                                         tm=seq, prev=caches)
        yna, ymla = _attn_ctx(nq, caches[0], nvo, qm, km, vmo, seq=seq, nk_layer=l if l > 0 else None)
        xp, h2, rg = _merge(xp, u, yna, ymla, gate, mod5, l, w, seq=seq, ctx=True, tm=2 * seq)
        xp = _moe(h2, rg, xp, mod5, l, w, seq=seq, ctx=True)
    new_nk, new_nv, new_ckv, new_kpe = caches

    xs = x_sample.reshape(dbatch * dseq, D_MODEL)
    for l in range(DEPTH):
        u, nq, nvo, qm, km, vmo, gate, nk = _inproj(xs, mod5, l, w, seq=dseq, rope_tab=rope_tab, ctx_out=False, tm=512)
        ckm, cvmo, cnvo = _cache_prep(cache_mla_ckv, cache_mla_kpe, cv, w["w_ukv"], place, l)
        yna = _attn_na(nq, nk, nvo, ck, cnvo, bias_tab, l, batch=dbatch, seq=dseq)
        ymla = _attn_mla(qm, km, vmo, ckm, cvmo, batch=dbatch, seq=dseq, tq=512)
        xs, h2, rg = _merge(xs, u, yna, ymla, gate, mod5, l, w, seq=dseq, ctx=False, tm=512)
        xs = _moe(h2, rg, xs, mod5, l, w, seq=dseq, ctx=False)

    heads = (batch, DEPTH, seq, NA_HEADS, NA_HEAD_DIM)
    return (xp.reshape(batch, seq, D_MODEL), xs.reshape(dbatch, dseq, D_MODEL),
            new_nk.reshape(heads), new_nv.reshape(heads), new_ckv, new_kpe)
```

```python
import functools

import numpy as np
import jax
import jax.numpy as jnp
from jax import lax
from jax.experimental import pallas as pl
from jax.experimental.pallas import tpu as pltpu

F32 = jnp.float32
BF16 = jnp.bfloat16

D_MODEL = 1024
DEPTH = 2
GRID_W = 64
POOL_WINDOWS = (2, 4, 8, 16)
POOL_GC = 128
POOL_WIDTH = 512
NA_HEADS = 8
NA_HEAD_DIM = 64
NA_WIDTH = 512
NA_WIN_ROWS = 8
NA_WIN_COLS = 16
MLA_HEADS = 8
MLA_NOPE_DIM = 64
MLA_ROPE_DIM = 32
MLA_V_DIM = 64
MLA_QK_DIM = 96
MLA_Q_LORA = 384
MLA_KV_LORA = 256
N_GROUPS = 4
EXPERTS_PER_GROUP = 4
N_EXPERTS = 16
EXPERT_FF = 256
ROPE_BASE = 10000.0
EPS = 1e-6
NEG_INF = -1e30

LANES = 128
SLOT = LANES
VMEM_LIMIT = 56 * 1024 * 1024

C_U, C_NQ, C_NK, C_NV, C_CQ, C_CKV, C_KPE, C_G, C_END = 0, 512, 1024, 1536, 2048, 2432, 2688, 2720, 5888
G_WIDTH = 3 * D_MODEL
KPE_LANE = MLA_NOPE_DIM

LOG2E = 1.4426950408889634
VO_W = 2 * SLOT
VO_ALL = (NA_HEADS // 2) * VO_W

NA_BLOCK_ROWS = 4
NA_SLAB_ROWS = 12


def _cparams(sem):
    return pltpu.CompilerParams(dimension_semantics=sem, vmem_limit_bytes=VMEM_LIMIT)


def _rms(x, g):
    return x * lax.rsqrt(jnp.mean(x * x, axis=-1, keepdims=True) + EPS) * g


def _dot(a, b):
    return jnp.dot(a, b, preferred_element_type=F32)


def _dot_nt(a, b):
    return lax.dot_general(a, b, (((1,), (1,)), ((), ())), preferred_element_type=F32)


def _mod_kernel(c_ref, w_ref, b_ref, o_ref):
    c = c_ref[...]
    s = (c * jax.nn.sigmoid(c)).astype(BF16)
    o_ref[...] = _dot(s, w_ref[...].astype(BF16)) + b_ref[...]


def _modulation(cvec, w_mod, b_mod):
    tn = 1536
    n = w_mod.shape[-1]
    return pl.pallas_call(
        _mod_kernel,
        grid=(DEPTH, n // tn),
        in_specs=[
            pl.BlockSpec((8, D_MODEL), lambda l, j: (0, 0)),
            pl.BlockSpec((None, D_MODEL, tn), lambda l, j: (l, 0, j)),
            pl.BlockSpec((None, 1, tn), lambda l, j: (l, 0, j)),
        ],
        out_specs=pl.BlockSpec((None, 8, tn), lambda l, j: (l, 0, j)),
        out_shape=jax.ShapeDtypeStruct((DEPTH, 8, n), F32),
        compiler_params=_cparams(("arbitrary", "arbitrary")),
        name="modulation",
    )(cvec, w_mod, b_mod.reshape(DEPTH, 1, n))


def _rope_slot(x, tab):
    c, s_up, s_dn = tab[:, 0:SLOT], tab[:, SLOT:2 * SLOT], tab[:, 2 * SLOT:3 * SLOT]
    half = MLA_ROPE_DIM // 4
    x_up = pltpu.roll(x, SLOT - half, axis=1)
    x_dn = pltpu.roll(x, half, axis=1)
    return x * c + x_up * s_up + x_dn * s_dn


def _store_vo(vo_ref, v):
    tm = v.shape[0]
    for p in range(NA_HEADS // 2):
        vo_ref[:, p * VO_W:p * VO_W + SLOT] = v[:, p * SLOT:(p + 1) * SLOT].astype(BF16)
        vo_ref[:, p * VO_W + SLOT:(p + 1) * VO_W] = jnp.ones((tm, SLOT), BF16)


def _realign_gate_columns(win_ref, wg_ref, wqk_ref):
    off = C_G % LANES
    lane = lax.broadcasted_iota(jnp.int32, (1, LANES), 1)
    base = C_G - off
    nxt = win_ref[:, base:base + LANES].astype(F32)
    kpe = pltpu.roll(nxt, KPE_LANE - (C_KPE - base), axis=1)
    wqk_ref[:, :MLA_Q_LORA] = win_ref[:, C_CQ:C_CKV]
    wqk_ref[:, MLA_Q_LORA:] = jnp.where((lane >= KPE_LANE) & (lane < KPE_LANE + MLA_ROPE_DIM), kpe, 0.0).astype(BF16)
    for j in range(G_WIDTH // LANES):
        cur = nxt
        nxt = win_ref[:, base + (j + 1) * LANES:base + (j + 2) * LANES].astype(F32)
        tile = pltpu.roll(jnp.where(lane >= off, cur, nxt), LANES - off, axis=1)
        wg_ref[:, j * LANES:(j + 1) * LANES] = tile.astype(BF16)


def _inproj_kernel(*refs, rope, ctx_out, stacked):
    n_in = 9 + (1 if rope else 0) + (4 if stacked else 0)
    ins, outs, (wg_ref, wqk_ref) = refs[:n_in], refs[n_in:-2], refs[-2:]
    x_ref, sh_ref, sc_ref, g_ref, win_ref, qn_ref, wuq_ref, kvn_ref, wukv_ref = ins[:9]
    tab_ref = ins[9] if rope else None
    prev = ins[n_in - 4:] if stacked else None
    if ctx_out:
        u_ref, nq_ref, nvo_ref, qm_ref, km_ref, vmo_ref, gate_ref, nk_ref, nv_ref, ckv_ref, kpe_ref = outs
    else:
        u_ref, nq_ref, nvo_ref, qm_ref, km_ref, vmo_ref, gate_ref, nk_ref = outs

    def put(ref, k, val):
        if stacked:
            ref[0] = prev[k][...]
            ref[1] = val
        else:
            ref[...] = val

    @pl.when(pl.program_id(0) == 0)
    def _():
        _realign_gate_columns(win_ref, wg_ref, wqk_ref)

    x = x_ref[...]
    h = (_rms(x, g_ref[...]) * (1.0 + sc_ref[...]) + sh_ref[...]).astype(BF16)

    def seg(a, b):
        return _dot(h, win_ref[:, a:b])

    u_ref[...] = seg(C_U, C_NQ).astype(u_ref.dtype)
    nq_ref[...] = (seg(C_NQ, C_NK) * (NA_HEAD_DIM ** -0.5 * LOG2E)).astype(BF16)
    nk = seg(C_NK, C_NV)
    nv = seg(C_NV, C_CQ)
    if ctx_out:
        put(nk_ref, 0, nk)
        put(nv_ref, 1, nv)
    else:
        nk_ref[...] = nk.astype(BF16)
    _store_vo(nvo_ref, nv)
    gate_ref[...] = jax.nn.sigmoid(_dot(h, wg_ref[...])).astype(BF16)

    cqk = _dot(h, wqk_ref[...])
    cq = _rms(cqk[:, :MLA_Q_LORA], qn_ref[...]).astype(BF16)
    q = _dot(cq, wuq_ref[...])
    tab = tab_ref[...] if rope else None
    for hd in range(MLA_HEADS):
        qs = q[:, hd * SLOT:(hd + 1) * SLOT]
        if rope:
            qs = _rope_slot(qs, tab)
        qm_ref[:, hd * SLOT:(hd + 1) * SLOT] = (qs * (MLA_QK_DIM ** -0.5 * LOG2E)).astype(BF16)

    ckv = _rms(seg(C_CKV, C_KPE), kvn_ref[...])
    kpe_slot = cqk[:, MLA_Q_LORA:]
    if ctx_out:
        put(ckv_ref, 2, ckv)
        put(kpe_ref, 3, kpe_slot[:, KPE_LANE:KPE_LANE + MLA_ROPE_DIM])
    if rope:
        kpe_slot = _rope_slot(kpe_slot, tab)
    kv = _dot(ckv.astype(BF16), wukv_ref[...])
    for hd in range(MLA_HEADS):
        km_ref[:, hd * SLOT:(hd + 1) * SLOT] = (kv[:, hd * SLOT:(hd + 1) * SLOT] + kpe_slot).astype(BF16)
    _store_vo(vmo_ref, kv[:, MLA_HEADS * SLOT:])


def _inproj(x, mod5, l, w, *, seq, rope_tab, ctx_out, tm, prev=None):
    n = x.shape[0]
    steps_per_seq = seq // tm if not ctx_out else 0
    stacked = prev is not None
    assert not stacked or (ctx_out and tm == seq and l == 1)

    def mrow(i):
        return 0 if ctx_out else 1 + i // steps_per_seq

    const = lambda i: (l, 0, 0)
    in_specs = [
        pl.BlockSpec((tm, D_MODEL), lambda i: (i, 0)),
        pl.BlockSpec((None, None, None, 1, D_MODEL), lambda i: (l, mrow(i), 0, 0, 0)),
        pl.BlockSpec((None, None, None, 1, D_MODEL), lambda i: (l, mrow(i), 1, 0, 0)),
        pl.BlockSpec((None, 1, D_MODEL), const),
        pl.BlockSpec((None, D_MODEL, C_END), const, pipeline_mode=pl.Buffered(1)),
        pl.BlockSpec((None, 1, MLA_Q_LORA), const),
        pl.BlockSpec((None, MLA_Q_LORA, MLA_HEADS * SLOT), const, pipeline_mode=pl.Buffered(1)),
        pl.BlockSpec((None, 1, MLA_KV_LORA), const),
        pl.BlockSpec((None, MLA_KV_LORA, MLA_HEADS * SLOT + 512), const, pipeline_mode=pl.Buffered(1)),
    ]
    args = [x, mod5, mod5, w["g_pre_mix"], w["w_in"], w["q_norm"], w["w_uq"], w["kv_norm"], w["w_ukv"]]
    rope = rope_tab is not None
    if rope:
        in_specs.append(pl.BlockSpec((tm, 3 * SLOT), lambda i: (i % steps_per_seq, 0)))
        args.append(rope_tab)
    row = lambda wd: pl.BlockSpec((tm, wd), lambda i: (i, 0))
    rows_of = lambda wd, dt: jax.ShapeDtypeStruct((n, wd), dt)
    widths = [512, 512, VO_ALL, MLA_HEADS * SLOT, MLA_HEADS * SLOT, VO_ALL, 3 * D_MODEL]
    out_specs = [row(wd) for wd in widths]
    out_shape = [rows_of(wd, BF16) for wd in widths]
    cache_w = [512, 512, MLA_KV_LORA, MLA_ROPE_DIM] if ctx_out else []
    if not ctx_out:
        out_specs.append(row(512))
        out_shape.append(rows_of(512, BF16))
    elif stacked:
        in_specs += [row(wd) for wd in cache_w]
        args += list(prev)
        out_specs += [pl.BlockSpec((None, DEPTH, seq, wd), lambda i: (i, 0, 0, 0)) for wd in cache_w]
        out_shape += [jax.ShapeDtypeStruct((n // seq, DEPTH, seq, wd), F32) for wd in cache_w]
    else:
        out_specs += [row(wd) for wd in cache_w]
        out_shape += [rows_of(wd, F32) for wd in cache_w]
    return pl.pallas_call(
        functools.partial(_inproj_kernel, rope=rope, ctx_out=ctx_out, stacked=stacked),
        grid=(n // tm,),
        in_specs=in_specs,
        out_specs=out_specs,
        out_shape=out_shape,
        scratch_shapes=[pltpu.VMEM((D_MODEL, G_WIDTH), BF16), pltpu.VMEM((D_MODEL, MLA_Q_LORA + SLOT), BF16)],
        compiler_params=_cparams(("arbitrary",)),
        name="inproj_ctx" if ctx_out else "inproj_lat",
    )(*args)


def _attend(q_h, segs):
    scores = []
    for k, _, bias in segs:
        s = _dot_nt(q_h, k)
        if bias is not None:
            s = s + bias
        scores.append(s)
    m = scores[0].max(axis=-1, keepdims=True)
    for s in scores[1:]:
        m = jnp.maximum(m, s.max(axis=-1, keepdims=True))
    acc = None
    for s, (_, vo, _) in zip(scores, segs):
        o = _dot(jnp.exp2(s - m).astype(BF16), vo)
        acc = o if acc is None else acc + o
    return acc[:, :SLOT] / acc[:, SLOT:]


def _attn_heads(out_ref, q_of, segs_of):
    lo = lax.broadcasted_iota(jnp.int32, (1, SLOT), 1) < NA_HEAD_DIM
    for p in range(NA_HEADS // 2):
        o0, o1 = [_attend(q_of(2 * p + e), segs_of(2 * p + e)) for e in range(2)]
        out_ref[:, p * SLOT:(p + 1) * SLOT] = jnp.where(lo, o0, o1).astype(out_ref.dtype)


def _pair_q(q_ref):
    lo = lax.broadcasted_iota(jnp.int32, (1, SLOT), 1) < NA_HEAD_DIM

    def q_of(h):
        q_pair = q_ref[:, (h // 2) * SLOT:(h // 2 + 1) * SLOT]
        keep = lo if h % 2 == 0 else jnp.logical_not(lo)
        return jnp.where(keep, q_pair, jnp.zeros_like(q_pair))
    return q_of


def _slot(ref, i, rows=slice(None)):
    return ref[rows, i * SLOT:(i + 1) * SLOT].astype(BF16)


def _vo(ref, h, rows=slice(None)):
    return ref[rows, (h // 2) * VO_W:(h // 2 + 1) * VO_W]


def _attn_ctx_kernel(nq_ref, nk_ref, nvo_ref, qm_ref, km_ref, vmo_ref, yna_ref, ymla_ref):
    _attn_heads(yna_ref, _pair_q(nq_ref), lambda h: [(_slot(nk_ref, h // 2), _vo(nvo_ref, h), None)])
    _attn_heads(ymla_ref, lambda h: _slot(qm_ref, h), lambda h: [(_slot(km_ref, h), _vo(vmo_ref, h), None)])


def _attn_ctx(nq, nk, nvo, qm, km, vmo, *, seq, nk_layer=None):
    n = nq.shape[0]
    spec = lambda wd: pl.BlockSpec((seq, wd), lambda b: (b, 0))
    nk_spec = spec(512) if nk_layer is None else pl.BlockSpec((None, None, seq, 512), lambda b: (b, nk_layer, 0, 0))
    return pl.pallas_call(
        _attn_ctx_kernel,
        grid=(n // seq,),
        in_specs=[spec(512), nk_spec, spec(VO_ALL), spec(1024), spec(1024), spec(VO_ALL)],
        out_specs=[spec(512), spec(512)],
        out_shape=[jax.ShapeDtypeStruct((n, 512), BF16)] * 2,
        compiler_params=_cparams(("arbitrary",)),
        name="attn_ctx",
    )(nq, nk, nvo, qm, km, vmo)


def _attn_na_kernel(nq_ref, nk_ref, nvo_ref, ck_ref, cvo_ref, bias_ref, out_ref, *, n_blocks):
    i = pl.program_id(1)
    ks = jnp.clip(NA_BLOCK_ROWS * i - NA_WIN_ROWS // 2, 0, n_blocks * NA_BLOCK_ROWS - NA_SLAB_ROWS)
    slab = pl.ds(pl.multiple_of(ks * GRID_W, GRID_W), NA_SLAB_ROWS * GRID_W)

    def segs_of(h):
        return [(_slot(nk_ref, h // 2, slab), _vo(nvo_ref, h, slab), bias_ref[h].astype(F32)),
                (_slot(ck_ref, h // 2), _vo(cvo_ref, h), None)]

    _attn_heads(out_ref, _pair_q(nq_ref), segs_of)


def _attn_na(nq, nk, nvo, cache_k, cvo, bias_tab, l, *, batch, seq):
    tq = NA_BLOCK_ROWS * GRID_W
    nb = seq // tq
    past = cache_k.shape[2]

    def cls(i):
        return jnp.where(i == 0, 0, jnp.where(i == nb - 1, 2, 1))

    return pl.pallas_call(
        functools.partial(_attn_na_kernel, n_blocks=nb),
        grid=(batch, nb),
        in_specs=[
            pl.BlockSpec((tq, 512), lambda b, i: (b * nb + i, 0)),
            pl.BlockSpec((seq, 512), lambda b, i: (b, 0)),
            pl.BlockSpec((seq, VO_ALL), lambda b, i: (b, 0)),
            pl.BlockSpec((None, None, past, 512), lambda b, i: (b, l, 0, 0)),
            pl.BlockSpec((None, past, VO_ALL), lambda b, i: (b, 0, 0)),
            pl.BlockSpec((None, None, NA_HEADS, tq, NA_SLAB_ROWS * GRID_W), lambda b, i: (l, cls(i), 0, 0, 0)),
        ],
        out_specs=pl.BlockSpec((tq, 512), lambda b, i: (b * nb + i, 0)),
        out_shape=jax.ShapeDtypeStruct((batch * seq, 512), BF16),
        compiler_params=_cparams(("arbitrary", "arbitrary")),
        name="attn_na_lat",
    )(nq, nk, nvo, cache_k, cvo, bias_tab)


def _cache_prep_kernel(ckv_ref, kpe_ref, nv_ref, wukv_ref, place_ref, km_ref, vmo_ref, nvo_ref):
    kv = _dot(ckv_ref[...].astype(BF16), wukv_ref[...])
    kslots = kv[:, :MLA_HEADS * SLOT] + _dot(kpe_ref[...].astype(BF16), place_ref[...])
    km_ref[...] = kslots.astype(BF16)
    _store_vo(vmo_ref, kv[:, MLA_HEADS * SLOT:])
    _store_vo(nvo_ref, nv_ref[...])


def _cache_prep(cache_ckv, cache_kpe, cache_nv, w_ukv, place, l):
    batch, _, past, _ = cache_ckv.shape
    tout = pl.BlockSpec((None, past, VO_ALL), lambda b: (b, 0, 0))
    tshape = jax.ShapeDtypeStruct((batch, past, VO_ALL), BF16)
    return pl.pallas_call(
        _cache_prep_kernel,
        grid=(batch,),
        in_specs=[
            pl.BlockSpec((None, None, past, MLA_KV_LORA), lambda b: (b, l, 0, 0)),
            pl.BlockSpec((None, None, past, MLA_ROPE_DIM), lambda b: (b, l, 0, 0)),
            pl.BlockSpec((None, None, past, 512), lambda b: (b, l, 0, 0)),
            pl.BlockSpec((None, MLA_KV_LORA, MLA_HEADS * SLOT + 512), lambda b: (l, 0, 0)),
            pl.BlockSpec((MLA_ROPE_DIM, MLA_HEADS * SLOT), lambda b: (0, 0)),
        ],
        out_specs=[pl.BlockSpec((None, past, MLA_HEADS * SLOT), lambda b: (b, 0, 0)), tout, tout],
        out_shape=[jax.ShapeDtypeStruct((batch, past, MLA_HEADS * SLOT), BF16), tshape, tshape],
        compiler_params=_cparams(("arbitrary",)),
        name="cache_prep",
    )(cache_ckv, cache_kpe, cache_nv, w_ukv, place)


def _attn_mla_kernel(qm_ref, km_ref, vmo_ref, ckm_ref, cvmo_ref, out_ref):
    _attn_heads(out_ref, lambda h: _slot(qm_ref, h),
                lambda h: [(_slot(km_ref, h), _vo(vmo_ref, h), None), (_slot(ckm_ref, h), _vo(cvmo_ref, h), None)])


def _attn_mla(qm, km, vmo, ckm, cvmo, *, batch, seq, tq):
    nb = seq // tq
    past = ckm.shape[1]
    return pl.pallas_call(
        _attn_mla_kernel,
        grid=(batch, nb),
        in_specs=[
            pl.BlockSpec((tq, MLA_HEADS * SLOT), lambda b, i: (b * nb + i, 0)),
            pl.BlockSpec((seq, MLA_HEADS * SLOT), lambda b, i: (b, 0)),
            pl.BlockSpec((seq, VO_ALL), lambda b, i: (b, 0)),
            pl.BlockSpec((None, past, MLA_HEADS * SLOT), lambda b, i: (b, 0, 0)),
            pl.BlockSpec((None, past, VO_ALL), lambda b, i: (b, 0, 0)),
        ],
        out_specs=pl.BlockSpec((tq, 512), lambda b, i: (b * nb + i, 0)),
        out_shape=jax.ShapeDtypeStruct((batch * seq, 512), BF16),
        compiler_params=_cparams(("arbitrary", "arbitrary")),
        name="attn_mla_lat",
    )(qm, km, vmo, ckm, cvmo)


POOL_HALO = 16


GROUP_LANE = N_EXPERTS


ROUTE_ROWS = 32


def _route(logits):
    lt = logits.T[:ROUTE_ROWS]
    idx = lax.broadcasted_iota(jnp.int32, lt.shape, 0).astype(F32)
    big = jnp.float32(1 << 20)
    is_g = idx < N_GROUPS
    gl = jnp.where(is_g, lt, NEG_INF)
    gmax = gl.max(axis=0, keepdims=True)
    ge = jnp.where(is_g, jnp.exp(gl - gmax), 0.0)
    gp = ge / ge.sum(axis=0, keepdims=True)
    g_w = jnp.where(is_g, gp, -1.0).max(axis=0, keepdims=True)
    g_idx = jnp.where(is_g & (gp == g_w), idx, big).min(axis=0, keepdims=True)
    e_idx = idx - N_GROUPS
    in_grp = (e_idx >= g_idx * EXPERTS_PER_GROUP) & (e_idx < (g_idx + 1) * EXPERTS_PER_GROUP)
    el = jnp.where(in_grp, lt, NEG_INF)
    emax = el.max(axis=0, keepdims=True)
    ee = jnp.where(in_grp, jnp.exp(el - emax), 0.0)
    ep = ee / ee.sum(axis=0, keepdims=True)
    p1 = jnp.where(in_grp, ep, -1.0).max(axis=0, keepdims=True)
    i1 = jnp.where(in_grp & (ep == p1), idx, big).min(axis=0, keepdims=True)
    rest = in_grp & (idx != i1)
    p2 = jnp.where(rest, ep, -1.0).max(axis=0, keepdims=True)
    i2 = jnp.where(rest & (ep == p2), idx, big).min(axis=0, keepdims=True)
    tot = p1 + p2
    out_t = (jnp.where(idx == i1 - N_GROUPS, g_w * (p1 / tot), 0.0) + jnp.where(idx == i2 - N_GROUPS, g_w * (p2 / tot), 0.0)
             + jnp.where(idx == GROUP_LANE, g_idx, 0.0))
    return jnp.concatenate([out_t, jnp.zeros((LANES - ROUTE_ROWS, lt.shape[1]), F32)], axis=0).T


def _merge_kernel(x_ref, u_ref, yna_ref, ymla_ref, gate_ref, gt1_ref, sh2_ref, sc2_ref, gpost_ref, gpre_ref,
                  wpool_ref, pscale_ref, wbr_ref, wout_ref, wr2_ref, wr1_ref,
                  xo_ref, h2_ref, rg_ref, *, seq, tm, steps_per_seq):
    def pool_diff(uwin, utile, r0, s0):
        rows, nwin = utile.shape[0], uwin.shape[0]
        t = r0 + lax.broadcasted_iota(jnp.int32, (rows, nwin), 0)
        s = s0 + lax.broadcasted_iota(jnp.int32, (rows, nwin), 1)
        t1 = r0 + lax.broadcasted_iota(jnp.int32, (rows, 1), 0)
        ds = []
        for g, w in enumerate(POOL_WINDOWS):
            lo = jnp.maximum(t - w // 2, 0)
            hi = jnp.minimum(t + (w - w // 2), seq)
            band = ((s >= lo) & (s < hi)).astype(BF16)
            cnt = (jnp.minimum(t1 + (w - w // 2), seq) - jnp.maximum(t1 - w // 2, 0)).astype(F32)
            sl = slice(g * POOL_GC, (g + 1) * POOL_GC)
            ds.append(_dot(band, uwin[:, sl]) / cnt - utile[:, sl].astype(F32))
        return jnp.concatenate(ds, axis=-1).astype(BF16)

    if steps_per_seq == 0:
        d = jnp.concatenate([pool_diff(u_ref[si], u_ref[si], 0, 0) for si in range(tm // seq)], axis=0)
    else:
        nwin = min(tm + 2 * POOL_HALO, seq)
        r0 = (pl.program_id(0) % steps_per_seq) * tm
        s0 = pl.multiple_of(jnp.clip(r0 - POOL_HALO, 0, seq - nwin), POOL_HALO)
        d = pool_diff(u_ref[0, pl.ds(s0, nwin), :], u_ref[0, pl.ds(pl.multiple_of(r0, POOL_HALO), tm), :], r0, s0)
    y_pool = (_dot(d, wpool_ref[...]) * pscale_ref[...]).astype(BF16)

    g = gate_ref[...]
    merged = (g[:, 0:D_MODEL] * _dot(y_pool, wbr_ref[0]).astype(BF16)
              + g[:, D_MODEL:2 * D_MODEL] * _dot(yna_ref[...], wbr_ref[1]).astype(BF16)
              + g[:, 2 * D_MODEL:] * _dot(ymla_ref[...], wbr_ref[2]).astype(BF16))
    y = _dot(merged, wout_ref[...])
    xn = x_ref[...] + gt1_ref[...] * _rms(y, gpost_ref[...])
    xo_ref[...] = xn

    h2 = _rms(xn, gpre_ref[...]) * (1.0 + sc2_ref[...]) + sh2_ref[...]
    h_hi = h2.astype(BF16)
    h2_ref[...] = h_hi
    h_lo = (h2 - h_hi.astype(F32)).astype(BF16)
    two = _dot(h_hi, wr2_ref[...])
    logits = two[:, :LANES] + two[:, LANES:] + _dot(h_lo, wr1_ref[...])
    rg_ref[...] = _route(logits)


def _merge(x, u, yna, ymla, gate, mod5, l, w, *, seq, ctx, tm):
    n = x.shape[0]
    nseq = n // seq
    if tm >= seq:
        assert ctx and tm % seq == 0
        steps_per_seq = 0
        u_spec = pl.BlockSpec((tm // seq, seq, 512), lambda i: (i, 0, 0))
    else:
        steps_per_seq = seq // tm
        u_spec = pl.BlockSpec((1, seq, 512), lambda i: (i // steps_per_seq, 0, 0))

    def mrow(i):
        return 0 if ctx else 1 + i // steps_per_seq

    const = lambda i: (l, 0, 0)
    modspec = lambda k: pl.BlockSpec((None, None, None, 1, D_MODEL), lambda i: (l, mrow(i), k, 0, 0))
    row = lambda wd: pl.BlockSpec((tm, wd), lambda i: (i, 0))
    return pl.pallas_call(
        functools.partial(_merge_kernel, seq=seq, tm=tm, steps_per_seq=steps_per_seq),
        grid=(n // tm,),
        in_specs=[
            row(D_MODEL),
            u_spec,
            row(512), row(512), row(3 * D_MODEL),
            modspec(2), modspec(3), modspec(4),
            pl.BlockSpec((None, 1, D_MODEL), const),
            pl.BlockSpec((None, 1, D_MODEL), const),
            pl.BlockSpec((None, 512, 512), const),
            pl.BlockSpec((None, 1, 512), const),
            pl.BlockSpec((None, 3, 512, D_MODEL), lambda i: (l, 0, 0, 0)),
            pl.BlockSpec((None, D_MODEL, D_MODEL), const),
            pl.BlockSpec((None, D_MODEL, 2 * LANES), const),
            pl.BlockSpec((None, D_MODEL, LANES), const),
        ],
        out_specs=[row(D_MODEL), row(D_MODEL), row(LANES)],
        out_shape=[jax.ShapeDtypeStruct((n, D_MODEL), F32), jax.ShapeDtypeStruct((n, D_MODEL), BF16),
                   jax.ShapeDtypeStruct((n, LANES), F32)],
        compiler_params=_cparams(("arbitrary",)),
        name="merge_ctx" if ctx else "merge_lat",
    )(x, u.reshape(nseq, seq, 512), yna, ymla, gate, mod5, mod5, mod5, w["g_post_mix"], w["g_pre_ffn"],
      w["w_pool_bd"], w["pool_scale"], w["w_branch"], w["w_out"], w["w_r2"], w["w_r1"])


MOE_TM = 1024
MOE_SUB = 512
MOE_ROWS = 160
MOE_ROWS_PAD = 256


def _moe_kernel(h_ref, rg_ref, x_ref, gt2_ref, gpost_ref, w1_ref, w3_ref, w2_ref, o_ref):
    tm = h_ref.shape[0]
    subs = [slice(s * MOE_SUB, (s + 1) * MOE_SUB) for s in range(tm // MOE_SUB)]
    acc_ref = o_ref
    rg = rg_ref[...]
    lane = lax.broadcasted_iota(jnp.int32, rg.shape, 1)
    grp = jnp.where(lane == GROUP_LANE, rg, 0.0).sum(axis=-1, keepdims=True)
    member = ((lane.astype(F32) == grp) & (lane < N_GROUPS)).astype(F32)
    t_row = lax.broadcasted_iota(jnp.int32, (MOE_SUB, MOE_SUB), 0)
    t_col = lax.broadcasted_iota(jnp.int32, (MOE_SUB, MOE_SUB), 1)
    earlier, later = (t_col < t_row).astype(BF16), (t_row < t_col).astype(BF16)
    member_t = member.T[:8]
    rank_col = jnp.concatenate([_dot(earlier, member[sl].astype(BF16)) for sl in subs], axis=0)
    rank_row = jnp.concatenate([_dot(member_t[:, sl].astype(BF16), later) for sl in subs], axis=1)
    count = functools.reduce(jnp.maximum, [member[sl].sum(axis=0, keepdims=True) for sl in subs])
    rg_hi = rg.astype(BF16)
    rg_lo = (rg - rg_hi.astype(F32)).astype(BF16)
    row_id = lax.broadcasted_iota(jnp.int32, (MOE_ROWS, 1), 0).astype(F32)
    col_id = lax.broadcasted_iota(jnp.int32, (1, MOE_ROWS_PAD), 1).astype(F32)
    lane1 = lax.broadcasted_iota(jnp.int32, (1, LANES), 1)

    def experts(g, hs, gs):
        hid = []
        for e in range(EXPERTS_PER_GROUP):
            k = g * EXPERTS_PER_GROUP + e
            a = _dot(hs, w1_ref[k])
            b = _dot(hs, w3_ref[k])
            hid.append(((a * jax.nn.sigmoid(a)) * b * gs[:, k:k + 1]).astype(BF16))
        return _dot(jnp.concatenate(hid, axis=-1), w2_ref[g]).astype(BF16)

    n_slots = N_GROUPS * MOE_ROWS
    slots_pad = -(-n_slots // LANES) * LANES
    slot_row = lax.broadcasted_iota(jnp.int32, (n_slots, 1), 0).astype(F32)
    slot_col = lax.broadcasted_iota(jnp.int32, (1, slots_pad), 1).astype(F32)
    dest_row = sum(jnp.where((member_t[g:g + 1] > 0.0) & (rank_row[g:g + 1] < MOE_ROWS),
                             rank_row[g:g + 1] + (g * MOE_ROWS + 1), 0.0) for g in range(N_GROUPS)) - 1.0
    dest_col = sum(jnp.where((member[:, g:g + 1] > 0.0) & (rank_col[:, g:g + 1] < MOE_ROWS),
                             rank_col[:, g:g + 1] + (g * MOE_ROWS + 1), 0.0) for g in range(N_GROUPS)) - 1.0
    hs_sub, gs_sub = [], []
    for sl in subs:
        take = (dest_row[:, sl] == slot_row).astype(BF16)
        hs_sub.append(_dot(take, h_ref[sl, :]).astype(BF16))
        gs_sub.append(_dot(take, rg_hi[sl]) + _dot(take, rg_lo[sl]))
    parts = []
    for g in range(N_GROUPS):
        rows = slice(g * MOE_ROWS, (g + 1) * MOE_ROWS)
        parts.append(experts(g, jnp.concatenate([h[rows] for h in hs_sub], axis=0),
                             jnp.concatenate([q[rows] for q in gs_sub], axis=0)))
    pad = [jnp.zeros((slots_pad - n_slots, D_MODEL), BF16)] if slots_pad > n_slots else []
    for s, sl in enumerate(subs):
        give = (dest_col[sl] == slot_col).astype(BF16)
        rows = jnp.concatenate([p[s * MOE_ROWS:(s + 1) * MOE_ROWS] for p in parts] + pad, axis=0)
        acc_ref[sl, :] = _dot(give, rows)

    pad = jnp.zeros((MOE_ROWS_PAD - MOE_ROWS, D_MODEL), BF16)
    for g in range(N_GROUPS):
        rr, mr = rank_row[g:g + 1, :], member_t[g:g + 1, :]
        rc, mc = rank_col[:, g:g + 1], member[:, g:g + 1]
        n_g = jnp.max(jnp.where(lane1 == g, count, 0.0)).astype(jnp.int32)

        def chunk(c, carry, g=g, rr=rr, mr=mr, rc=rc, mc=mc):
            base = (c * MOE_ROWS).astype(F32)
            hs, gs = [], []
            for sl in subs:
                take = ((rr[:, sl] - base == row_id) & (mr[:, sl] > 0.0)).astype(BF16)
                hs.append(_dot(take, h_ref[sl, :]).astype(BF16))
                gs.append(_dot(take, rg_hi[sl]) + _dot(take, rg_lo[sl]))
            part = experts(g, jnp.concatenate(hs, axis=0), jnp.concatenate(gs, axis=0))
            for s, sl in enumerate(subs):
                rel = rc[sl] - base
                give = ((rel == col_id) & (rel < MOE_ROWS) & (mc[sl] > 0.0)).astype(BF16)
                rows = jnp.concatenate([part[s * MOE_ROWS:(s + 1) * MOE_ROWS], pad], axis=0)
                acc_ref[sl, :] += _dot(give, rows)
            return carry

        lax.fori_loop(1, (n_g + MOE_ROWS - 1) // MOE_ROWS, chunk, 0)

    o_ref[...] = x_ref[...] + gt2_ref[...] * _rms(acc_ref[...], gpost_ref[...])


def _moe(h2, rg, x, mod5, l, w, *, seq, ctx):
    n = x.shape[0]
    tm = MOE_TM
    steps_per_seq = seq // tm if not ctx else 0

    def mrow(i):
        return 0 if ctx else 1 + i // steps_per_seq

    row = lambda wd: pl.BlockSpec((tm, wd), lambda i: (i, 0))
    once = pl.Buffered(1)
    wspec = lambda a, b: pl.BlockSpec((None, N_EXPERTS, a, b), lambda i: (l, 0, 0, 0), pipeline_mode=once)
    return pl.pallas_call(
        _moe_kernel,
        grid=(n // tm,),
        in_specs=[
            row(D_MODEL), row(LANES), row(D_MODEL),
            pl.BlockSpec((None, None, None, 1, D_MODEL), lambda i: (l, mrow(i), 5, 0, 0)),
            pl.BlockSpec((None, 1, D_MODEL), lambda i: (l, 0, 0)),
            wspec(D_MODEL, EXPERT_FF), wspec(D_MODEL, EXPERT_FF),
            pl.BlockSpec((None, N_GROUPS, EXPERTS_PER_GROUP * EXPERT_FF, D_MODEL), lambda i: (l, 0, 0, 0),
                         pipeline_mode=once),
        ],
        out_specs=row(D_MODEL),
        out_shape=jax.ShapeDtypeStruct((n, D_MODEL), F32),
        compiler_params=_cparams(("arbitrary",)),
        name="moe_ctx" if ctx else "moe_lat",
    )(h2, rg, x, mod5, w["g_post_ffn"], w["w1"], w["w3"], w["w2"])


def _rope_table(seq):
    t = np.arange(seq)
    n_freq = MLA_ROPE_DIM // 4
    inv = jnp.asarray(ROPE_BASE, F32) ** (-jnp.arange(n_freq, dtype=F32) / n_freq)
    ang_r = jnp.asarray(t // GRID_W, F32)[:, None] * inv
    ang_c = jnp.asarray(t % GRID_W, F32)[:, None] * inv
    cos = jnp.cos(jnp.stack([ang_r, ang_c], axis=1))
    sin = jnp.sin(jnp.stack([ang_r, ang_c], axis=1))
    zeros = jnp.zeros_like(sin)
    c32 = jnp.stack([cos, cos], axis=2).reshape(seq, MLA_ROPE_DIM)
    up32 = jnp.stack([-sin, zeros], axis=2).reshape(seq, MLA_ROPE_DIM)
    dn32 = jnp.stack([zeros, sin], axis=2).reshape(seq, MLA_ROPE_DIM)

    def slot(v, fill):
        return jnp.concatenate([jnp.full((seq, KPE_LANE), fill, F32), v,
                                jnp.full((seq, SLOT - KPE_LANE - MLA_ROPE_DIM), fill, F32)], axis=1)

    return jnp.concatenate([slot(c32, 1.0), slot(up32, 0.0), slot(dn32, 0.0)], axis=1)


def _na_row_classes(n_rows):
    i = np.arange(NA_BLOCK_ROWS)[:, None]
    j = np.arange(NA_SLAB_ROWS)[None, :]
    last_ks = n_rows - NA_SLAB_ROWS
    last_q0 = n_rows - NA_BLOCK_ROWS
    specs = [
        (j < NA_WIN_ROWS, j - i),
        ((j >= i) & (j < i + NA_WIN_ROWS), j - NA_WIN_ROWS // 2 - i),
        (last_ks + j >= n_rows - NA_WIN_ROWS, last_ks + j - last_q0 - i),
    ]
    ok = np.stack([np.broadcast_to(o, (NA_BLOCK_ROWS, NA_SLAB_ROWS)) for o, _ in specs])
    dr = np.stack([np.clip(d + NA_WIN_ROWS - 1, 0, 2 * NA_WIN_ROWS - 2) for _, d in specs])
    return ok, dr


def _bias_kernel(rpb_ref, out_ref, *, row_ok, row_dr):
    reach = NA_WIN_COLS - 1
    rows = rpb_ref[...] * LOG2E
    v_lo = pltpu.roll(rows, SLOT - reach, axis=1)
    v_hi = pltpu.roll(rows, GRID_W - reach, axis=1)
    lane1 = lax.broadcasted_iota(jnp.int32, (1, SLOT), 1)
    mid = (lane1 >= GRID_W - reach) & (lane1 <= GRID_W + reach)
    c = lax.broadcasted_iota(jnp.int32, (GRID_W, SLOT), 0)
    lane = lax.broadcasted_iota(jnp.int32, (GRID_W, SLOT), 1)
    kc = lane & (GRID_W - 1)
    c0 = jnp.clip(c - NA_WIN_COLS // 2, 0, GRID_W - NA_WIN_COLS)
    in_col = (kc >= c0) & (kc < c0 + NA_WIN_COLS)
    first = lane < GRID_W
    for k in range(3):
        for i in range(NA_BLOCK_ROWS):
            for jp in range(NA_SLAB_ROWS // 2):
                ok_a, ok_b = bool(row_ok[k, i, 2 * jp]), bool(row_ok[k, i, 2 * jp + 1])
                a, b = int(row_dr[k, i, 2 * jp]), int(row_dr[k, i, 2 * jp + 1])
                rs = slice(i * GRID_W, (i + 1) * GRID_W)
                cs = slice(jp * SLOT, (jp + 1) * SLOT)
                if not (ok_a or ok_b):
                    out_ref[k, rs, cs] = jnp.full((GRID_W, SLOT), NEG_INF, out_ref.dtype)
                    continue
                v = jnp.where(mid, v_hi[b:b + 1], v_lo[a:a + 1])
                tile = pltpu.roll(jnp.broadcast_to(v, (GRID_W, SLOT)), 0, axis=1, stride=1, stride_axis=0)
                ok = in_col
                if not ok_b:
                    ok = ok & first
                if not ok_a:
                    ok = ok & jnp.logical_not(first)
                out_ref[k, rs, cs] = jnp.where(ok, tile, NEG_INF).astype(out_ref.dtype)


def _na_bias_table(rpb, n_rows):
    n_dr, n_dc = rpb.shape[-2:]
    row_ok, row_dr = _na_row_classes(n_rows)
    rpb_p = jnp.pad(rpb, ((0, 0), (0, 0), (0, 0), (0, SLOT - n_dc)))
    tq, tk = NA_BLOCK_ROWS * GRID_W, NA_SLAB_ROWS * GRID_W
    return pl.pallas_call(
        functools.partial(_bias_kernel, row_ok=row_ok, row_dr=row_dr),
        grid=(DEPTH, NA_HEADS),
        in_specs=[pl.BlockSpec((None, None, n_dr, SLOT), lambda l, h: (l, h, 0, 0))],
        out_specs=pl.BlockSpec((None, 3, None, tq, tk), lambda l, h: (l, 0, h, 0, 0)),
        out_shape=jax.ShapeDtypeStruct((DEPTH, 3, NA_HEADS, tq, tk), BF16),
        compiler_params=_cparams(("arbitrary", "arbitrary")),
        name="na_bias_table",
    )(rpb_p)


def _prep(p):
    L = DEPTH
    w_in = p["w_in"]
    w_in_p = jnp.pad(w_in.astype(BF16), ((0, 0), (0, 0), (0, C_END - w_in.shape[-1])))
    w_uq = jnp.pad(p["mla_w_uq"].reshape(L, MLA_Q_LORA, MLA_HEADS, MLA_QK_DIM),
                   ((0, 0), (0, 0), (0, 0), (0, SLOT - MLA_QK_DIM))).reshape(L, MLA_Q_LORA, MLA_HEADS * SLOT)
    ukv = p["mla_w_ukv"].reshape(L, MLA_KV_LORA, MLA_HEADS, MLA_NOPE_DIM + MLA_V_DIM)
    uk = jnp.pad(ukv[..., :MLA_NOPE_DIM], ((0, 0), (0, 0), (0, 0), (0, SLOT - MLA_NOPE_DIM)))
    w_ukv = jnp.concatenate([uk.reshape(L, MLA_KV_LORA, MLA_HEADS * SLOT),
                             ukv[..., MLA_NOPE_DIM:].reshape(L, MLA_KV_LORA, MLA_HEADS * MLA_V_DIM)], axis=-1)
    w_pool_bd = jnp.einsum("lgcd,gh->lgchd", p["w_pool"], jnp.eye(4, dtype=F32)).reshape(L, POOL_WIDTH, POOL_WIDTH)
    w_r = jnp.concatenate([p["w_group_router"], p["w_expert_router"]], axis=-1)
    r_hi = w_r.astype(BF16)
    r_lo = (w_r - r_hi.astype(F32)).astype(BF16)
    padl = lambda a: jnp.pad(a, ((0, 0), (0, 0), (0, LANES - a.shape[-1])))
    vec = lambda a: a.reshape(L, 1, a.shape[-1])
    return {
        "w_in": w_in_p,
        "w_uq": w_uq.astype(BF16),
        "w_ukv": w_ukv.astype(BF16),
        "w_pool_bd": w_pool_bd.astype(BF16),
        "w_branch": p["w_branch"].astype(BF16),
        "w_out": p["w_out"].astype(BF16),
        "w_r2": jnp.concatenate([padl(r_hi), padl(r_lo)], axis=-1),
        "w_r1": padl(r_hi),
        "w1": p["w_exp_gate"].astype(BF16),
        "w3": p["w_exp_up"].astype(BF16),
        "w2": p["w_exp_down"].astype(BF16).reshape(L, N_GROUPS, EXPERTS_PER_GROUP * EXPERT_FF, D_MODEL),
        "g_pre_mix": vec(p["g_pre_mix"]), "g_post_mix": vec(p["g_post_mix"]),
        "g_pre_ffn": vec(p["g_pre_ffn"]), "g_post_ffn": vec(p["g_post_ffn"]),
        "pool_scale": vec(p["pool_scale"]), "q_norm": vec(p["mla_q_norm"]), "kv_norm": vec(p["mla_kv_norm"]),
    }


def kernel(x_prompt, x_sample, cache_na_k, cache_na_v, cache_mla_ckv, cache_mla_kpe, c, c_ctx, w_mod, b_mod,
           g_pre_mix, g_post_mix, g_pre_ffn, g_post_ffn, w_in, w_pool, pool_scale, na_rpb, mla_q_norm, mla_w_uq,
           mla_kv_norm, mla_w_ukv, w_branch, w_out, w_group_router, w_expert_router, w_exp_gate, w_exp_up,
           w_exp_down):
    batch, seq, _ = x_prompt.shape
    dbatch, dseq, _ = x_sample.shape
    past = cache_na_k.shape[2]
    assert dbatch + 1 <= 8 and dseq % (NA_BLOCK_ROWS * GRID_W) == 0

    w = _prep(dict(w_in=w_in, w_pool=w_pool, pool_scale=pool_scale, mla_q_norm=mla_q_norm, mla_w_uq=mla_w_uq,
                   mla_kv_norm=mla_kv_norm, mla_w_ukv=mla_w_ukv, w_branch=w_branch, w_out=w_out,
                   w_group_router=w_group_router, w_expert_router=w_expert_router, w_exp_gate=w_exp_gate,
                   w_exp_up=w_exp_up, w_exp_down=w_exp_down, g_pre_mix=g_pre_mix, g_post_mix=g_post_mix,
                   g_pre_ffn=g_pre_ffn, g_post_ffn=g_post_ffn))
    rope_tab = _rope_table(dseq)
    bias_tab = _na_bias_table(na_rpb, dseq // GRID_W)
    place = jnp.tile(jnp.pad(jnp.eye(MLA_ROPE_DIM, dtype=F32), ((0, 0), (KPE_LANE, SLOT - KPE_LANE - MLA_ROPE_DIM))),
                     (1, MLA_HEADS)).astype(BF16)

    cvec = jnp.concatenate([c_ctx[None], c, jnp.zeros((8 - 1 - dbatch, D_MODEL), F32)], axis=0)
    mod5 = _modulation(cvec, w_mod, b_mod).reshape(DEPTH, 8, 6, 1, D_MODEL)

    ck = cache_na_k.reshape(dbatch, DEPTH, past, NA_WIDTH)
    cv = cache_na_v.reshape(dbatch, DEPTH, past, NA_WIDTH)

    xp = x_prompt.reshape(batch * seq, D_MODEL)
    assert DEPTH == 2
    caches = None
    for l in range(DEPTH):
        u, nq, nvo, qm, km, vmo, gate, *caches = _inproj(xp, mod5, l, w, seq=seq, rope_tab=None, ctx_out=True,
                                                         tm=seq, prev=caches)
        yna, ymla = _attn_ctx(nq, caches[0], nvo, qm, km, vmo, seq=seq, nk_layer=l if l > 0 else None)
        xp, h2, rg = _merge(xp, u, yna, ymla, gate, mod5, l, w, seq=seq, ctx=True, tm=2 * seq)
        xp = _moe(h2, rg, xp, mod5, l, w, seq=seq, ctx=True)
    new_nk, new_nv, new_ckv, new_kpe = caches

    xs = x_sample.reshape(dbatch * dseq, D_MODEL)
    for l in range(DEPTH):
        u, nq, nvo, qm, km, vmo, gate, nk = _inproj(xs, mod5, l, w, seq=dseq, rope_tab=rope_tab, ctx_out=False, tm=512)
        ckm, cvmo, cnvo = _cache_prep(cache_mla_ckv, cache_mla_kpe, cv, w["w_ukv"], place, l)
        yna = _attn_na(nq, nk, nvo, ck, cnvo, bias_tab, l, batch=dbatch, seq=dseq)
        ymla = _attn_mla(qm, km, vmo, ckm, cvmo, batch=dbatch, seq=dseq, tq=512)
        xs, h2, rg = _merge(xs, u, yna, ymla, gate, mod5, l, w, seq=dseq, ctx=False, tm=512)
        xs = _moe(h2, rg, xs, mod5, l, w, seq=dseq, ctx=False)

    heads = (batch, DEPTH, seq, NA_HEADS, NA_HEAD_DIM)
    return (xp.reshape(batch, seq, D_MODEL), xs.reshape(dbatch, dseq, D_MODEL),
            new_nk.reshape(heads), new_nv.reshape(heads), new_ckv, new_kpe)
```

```python
import functools

import numpy as np
import jax
import jax.numpy as jnp
from jax import lax
from jax.experimental import pallas as pl
from jax.experimental.pallas import tpu as pltpu

F32 = jnp.float32
BF16 = jnp.bfloat16

D_MODEL = 1024
DEPTH = 2
GRID_W = 64
POOL_WINDOWS = (2, 4, 8, 16)
POOL_GC = 128
POOL_WIDTH = 512
NA_HEADS = 8
NA_HEAD_DIM = 64
NA_WIDTH = 512
NA_WIN_ROWS = 8
NA_WIN_COLS = 16
MLA_HEADS = 8
MLA_NOPE_DIM = 64
MLA_ROPE_DIM = 32
MLA_V_DIM = 64
MLA_QK_DIM = 96
MLA_Q_LORA = 384
MLA_KV_LORA = 256
N_GROUPS = 4
EXPERTS_PER_GROUP = 4
N_EXPERTS = 16
EXPERT_FF = 256
ROPE_BASE = 10000.0
EPS = 1e-6
NEG_INF = -1e30

LANES = 128
SLOT = LANES
VMEM_LIMIT = 56 * 1024 * 1024

C_U, C_NQ, C_NK, C_NV, C_CQ, C_CKV, C_KPE, C_G, C_END = 0, 512, 1024, 1536, 2048, 2432, 2688, 2720, 5888
G_WIDTH = 3 * D_MODEL
KPE_LANE = MLA_NOPE_DIM

LOG2E = 1.4426950408889634
VO_W = 2 * SLOT
VO_ALL = (NA_HEADS // 2) * VO_W

NA_BLOCK_ROWS = 4
NA_SLAB_ROWS = 12


def _cparams(sem):
    return pltpu.CompilerParams(dimension_semantics=sem, vmem_limit_bytes=VMEM_LIMIT)


def _rms(x, g):
    return x * lax.rsqrt(jnp.mean(x * x, axis=-1, keepdims=True) + EPS) * g


def _dot(a, b):
    return jnp.dot(a, b, preferred_element_type=F32)


def _dot_nt(a, b):
    return lax.dot_general(a, b, (((1,), (1,)), ((), ())), preferred_element_type=F32)


def _mod_kernel(c_ref, w_ref, b_ref, o_ref):
    c = c_ref[...]
    s = (c * jax.nn.sigmoid(c)).astype(BF16)
    o_ref[...] = _dot(s, w_ref[...].astype(BF16)) + b_ref[...]


def _modulation(cvec, w_mod, b_mod):
    tn = 1536
    n = w_mod.shape[-1]
    return pl.pallas_call(
        _mod_kernel,
        grid=(DEPTH, n // tn),
        in_specs=[
            pl.BlockSpec((8, D_MODEL), lambda l, j: (0, 0)),
            pl.BlockSpec((None, D_MODEL, tn), lambda l, j: (l, 0, j)),
            pl.BlockSpec((None, 1, tn), lambda l, j: (l, 0, j)),
        ],
        out_specs=pl.BlockSpec((None, 8, tn), lambda l, j: (l, 0, j)),
        out_shape=jax.ShapeDtypeStruct((DEPTH, 8, n), F32),
        compiler_params=_cparams(("arbitrary", "arbitrary")),
        name="modulation",
    )(cvec, w_mod, b_mod.reshape(DEPTH, 1, n))


def _rope_slot(x, tab):
    c, s_up, s_dn = tab[:, 0:SLOT], tab[:, SLOT:2 * SLOT], tab[:, 2 * SLOT:3 * SLOT]
    half = MLA_ROPE_DIM // 4
    x_up = pltpu.roll(x, SLOT - half, axis=1)
    x_dn = pltpu.roll(x, half, axis=1)
    return x * c + x_up * s_up + x_dn * s_dn


def _store_vo(vo_ref, v):
    tm = v.shape[0]
    for p in range(NA_HEADS // 2):
        vo_ref[:, p * VO_W:p * VO_W + SLOT] = v[:, p * SLOT:(p + 1) * SLOT].astype(BF16)
        vo_ref[:, p * VO_W + SLOT:(p + 1) * VO_W] = jnp.ones((tm, SLOT), BF16)


def _realign_gate_columns(win_ref, wg_ref, wqk_ref):
    off = C_G % LANES
    lane = lax.broadcasted_iota(jnp.int32, (1, LANES), 1)
    base = C_G - off
    nxt = win_ref[:, base:base + LANES].astype(F32)
    kpe = pltpu.roll(nxt, KPE_LANE - (C_KPE - base), axis=1)
    wqk_ref[:, :MLA_Q_LORA] = win_ref[:, C_CQ:C_CKV]
    wqk_ref[:, MLA_Q_LORA:] = jnp.where((lane >= KPE_LANE) & (lane < KPE_LANE + MLA_ROPE_DIM), kpe, 0.0).astype(BF16)
    for j in range(G_WIDTH // LANES):
        cur = nxt
        nxt = win_ref[:, base + (j + 1) * LANES:base + (j + 2) * LANES].astype(F32)
        tile = pltpu.roll(jnp.where(lane >= off, cur, nxt), LANES - off, axis=1)
        wg_ref[:, j * LANES:(j + 1) * LANES] = tile.astype(BF16)


def _inproj_kernel(*refs, rope, ctx_out, stacked):
    n_in = 9 + (1 if rope else 0) + (4 if stacked else 0)
    ins, outs, (wg_ref, wqk_ref) = refs[:n_in], refs[n_in:-2], refs[-2:]
    x_ref, sh_ref, sc_ref, g_ref, win_ref, qn_ref, wuq_ref, kvn_ref, wukv_ref = ins[:9]
    tab_ref = ins[9] if rope else None
    prev = ins[n_in - 4:] if stacked else None
    if ctx_out:
        u_ref, nq_ref, nvo_ref, qm_ref, km_ref, vmo_ref, gate_ref, nk_ref, nv_ref, ckv_ref, kpe_ref = outs
    else:
        u_ref, nq_ref, nvo_ref, qm_ref, km_ref, vmo_ref, gate_ref, nk_ref = outs

    def put(ref, k, val):
        if stacked:
            ref[0] = prev[k][...]
            ref[1] = val
        else:
            ref[...] = val

    @pl.when(pl.program_id(0) == 0)
    def _():
        _realign_gate_columns(win_ref, wg_ref, wqk_ref)

    x = x_ref[...]
    h = (_rms(x, g_ref[...]) * (1.0 + sc_ref[...]) + sh_ref[...]).astype(BF16)

    def seg(a, b):
        return _dot(h, win_ref[:, a:b])

    u_ref[...] = seg(C_U, C_NQ).astype(u_ref.dtype)
    nq_ref[...] = (seg(C_NQ, C_NK) * (NA_HEAD_DIM ** -0.5 * LOG2E)).astype(BF16)
    nk = seg(C_NK, C_NV)
    nv = seg(C_NV, C_CQ)
    if ctx_out:
        put(nk_ref, 0, nk)
        put(nv_ref, 1, nv)
    else:
        nk_ref[...] = nk.astype(BF16)
    _store_vo(nvo_ref, nv)
    gate_ref[...] = jax.nn.sigmoid(_dot(h, wg_ref[...])).astype(BF16)

    cqk = _dot(h, wqk_ref[...])
    cq = _rms(cqk[:, :MLA_Q_LORA], qn_ref[...]).astype(BF16)
    q = _dot(cq, wuq_ref[...])
    tab = tab_ref[...] if rope else None
    for hd in range(MLA_HEADS):
        qs = q[:, hd * SLOT:(hd + 1) * SLOT]
        if rope:
            qs = _rope_slot(qs, tab)
        qm_ref[:, hd * SLOT:(hd + 1) * SLOT] = (qs * (MLA_QK_DIM ** -0.5 * LOG2E)).astype(BF16)

    ckv = _rms(seg(C_CKV, C_KPE), kvn_ref[...])
    kpe_slot = cqk[:, MLA_Q_LORA:]
    if ctx_out:
        put(ckv_ref, 2, ckv)
        put(kpe_ref, 3, kpe_slot[:, KPE_LANE:KPE_LANE + MLA_ROPE_DIM])
    if rope:
        kpe_slot = _rope_slot(kpe_slot, tab)
    kv = _dot(ckv.astype(BF16), wukv_ref[...])
    for hd in range(MLA_HEADS):
        km_ref[:, hd * SLOT:(hd + 1) * SLOT] = (kv[:, hd * SLOT:(hd + 1) * SLOT] + kpe_slot).astype(BF16)
    _store_vo(vmo_ref, kv[:, MLA_HEADS * SLOT:])


def _inproj(x, mod5, l, w, *, seq, rope_tab, ctx_out, tm, prev=None):
    n = x.shape[0]
    steps_per_seq = seq // tm if not ctx_out else 0
    stacked = prev is not None
    assert not stacked or (ctx_out and tm == seq and l == 1)

    def mrow(i):
        return 0 if ctx_out else 1 + i // steps_per_seq

    const = lambda i: (l, 0, 0)
    in_specs = [
        pl.BlockSpec((tm, D_MODEL), lambda i: (i, 0)),
        pl.BlockSpec((None, None, None, 1, D_MODEL), lambda i: (l, mrow(i), 0, 0, 0)),
        pl.BlockSpec((None, None, None, 1, D_MODEL), lambda i: (l, mrow(i), 1, 0, 0)),
        pl.BlockSpec((None, 1, D_MODEL), const),
        pl.BlockSpec((None, D_MODEL, C_END), const, pipeline_mode=pl.Buffered(1)),
        pl.BlockSpec((None, 1, MLA_Q_LORA), const),
        pl.BlockSpec((None, MLA_Q_LORA, MLA_HEADS * SLOT), const, pipeline_mode=pl.Buffered(1)),
        pl.BlockSpec((None, 1, MLA_KV_LORA), const),
        pl.BlockSpec((None, MLA_KV_LORA, MLA_HEADS * SLOT + 512), const, pipeline_mode=pl.Buffered(1)),
    ]
    args = [x, mod5, mod5, w["g_pre_mix"], w["w_in"], w["q_norm"], w["w_uq"], w["kv_norm"], w["w_ukv"]]
    rope = rope_tab is not None
    if rope:
        in_specs.append(pl.BlockSpec((tm, 3 * SLOT), lambda i: (i % steps_per_seq, 0)))
        args.append(rope_tab)
    row = lambda wd: pl.BlockSpec((tm, wd), lambda i: (i, 0))
    rows_of = lambda wd, dt: jax.ShapeDtypeStruct((n, wd), dt)
    widths = [512, 512, VO_ALL, MLA_HEADS * SLOT, MLA_HEADS * SLOT, VO_ALL, 3 * D_MODEL]
    out_specs = [row(wd) for wd in widths]
    out_shape = [rows_of(wd, BF16) for wd in widths]
    cache_w = [512, 512, MLA_KV_LORA, MLA_ROPE_DIM] if ctx_out else []
    if not ctx_out:
        out_specs.append(row(512))
        out_shape.append(rows_of(512, BF16))
    elif stacked:
        in_specs += [row(wd) for wd in cache_w]
        args += list(prev)
        out_specs += [pl.BlockSpec((None, DEPTH, seq, wd), lambda i: (i, 0, 0, 0)) for wd in cache_w]
        out_shape += [jax.ShapeDtypeStruct((n // seq, DEPTH, seq, wd), F32) for wd in cache_w]
    else:
        out_specs += [row(wd) for wd in cache_w]
        out_shape += [rows_of(wd, F32) for wd in cache_w]
    return pl.pallas_call(
        functools.partial(_inproj_kernel, rope=rope, ctx_out=ctx_out, stacked=stacked),
        grid=(n // tm,),
        in_specs=in_specs,
        out_specs=out_specs,
        out_shape=out_shape,
        scratch_shapes=[pltpu.VMEM((D_MODEL, G_WIDTH), BF16), pltpu.VMEM((D_MODEL, MLA_Q_LORA + SLOT), BF16)],
        compiler_params=_cparams(("arbitrary",)),
        name="inproj_ctx" if ctx_out else "inproj_lat",
    )(*args)


def _attend(q_h, segs):
    scores = []
    for k, _, bias in segs:
        s = _dot_nt(q_h, k)
        if bias is not None:
            s = s + bias
        scores.append(s)
    m = scores[0].max(axis=-1, keepdims=True)
    for s in scores[1:]:
        m = jnp.maximum(m, s.max(axis=-1, keepdims=True))
    acc = None
    for s, (_, vo, _) in zip(scores, segs):
        o = _dot(jnp.exp2(s - m).astype(BF16), vo)
        acc = o if acc is None else acc + o
    return acc[:, :SLOT] / acc[:, SLOT:]


def _attn_heads(out_ref, q_of, segs_of, rows=slice(None)):
    lo = lax.broadcasted_iota(jnp.int32, (1, SLOT), 1) < NA_HEAD_DIM
    for p in range(NA_HEADS // 2):
        o0, o1 = [_attend(q_of(2 * p + e), segs_of(2 * p + e)) for e in range(2)]
        out_ref[rows, p * SLOT:(p + 1) * SLOT] = jnp.where(lo, o0, o1).astype(out_ref.dtype)


def _pair_q(q_ref, rows=slice(None)):
    lo = lax.broadcasted_iota(jnp.int32, (1, SLOT), 1) < NA_HEAD_DIM

    def q_of(h):
        q_pair = q_ref[rows, (h // 2) * SLOT:(h // 2 + 1) * SLOT]
        keep = lo if h % 2 == 0 else jnp.logical_not(lo)
        return jnp.where(keep, q_pair, jnp.zeros_like(q_pair))
    return q_of


def _slot(ref, i, rows=slice(None)):
    return ref[rows, i * SLOT:(i + 1) * SLOT].astype(BF16)


def _vo(ref, h, rows=slice(None)):
    return ref[rows, (h // 2) * VO_W:(h // 2 + 1) * VO_W]


CTX_SEQS_PER_STEP = 4


def _attn_ctx_kernel(nq_ref, nk_ref, nvo_ref, qm_ref, km_ref, vmo_ref, yna_ref, ymla_ref, *, seq, nk_stacked):
    for bi in range(CTX_SEQS_PER_STEP):
        rows = slice(bi * seq, (bi + 1) * seq)
        if nk_stacked:
            nk_of = lambda p, bi=bi: nk_ref[bi, :, p * SLOT:(p + 1) * SLOT].astype(BF16)
        else:
            nk_of = lambda p, rows=rows: _slot(nk_ref, p, rows)
        _attn_heads(yna_ref, _pair_q(nq_ref, rows),
                    lambda h, rows=rows, nk_of=nk_of: [(nk_of(h // 2), _vo(nvo_ref, h, rows), None)], rows)
        _attn_heads(ymla_ref, lambda h, rows=rows: _slot(qm_ref, h, rows),
                    lambda h, rows=rows: [(_slot(km_ref, h, rows), _vo(vmo_ref, h, rows), None)], rows)


def _attn_ctx(nq, nk, nvo, qm, km, vmo, *, seq, nk_layer=None):
    n = nq.shape[0]
    nb = CTX_SEQS_PER_STEP
    spec = lambda wd: pl.BlockSpec((nb * seq, wd), lambda b: (b, 0))
    nk_spec = spec(512) if nk_layer is None else pl.BlockSpec((nb, None, seq, 512), lambda b: (b, nk_layer, 0, 0))
    return pl.pallas_call(
        functools.partial(_attn_ctx_kernel, seq=seq, nk_stacked=nk_layer is not None),
        grid=(n // (nb * seq),),
        in_specs=[spec(512), nk_spec, spec(VO_ALL), spec(1024), spec(1024), spec(VO_ALL)],
        out_specs=[spec(512), spec(512)],
        out_shape=[jax.ShapeDtypeStruct((n, 512), BF16)] * 2,
        compiler_params=_cparams(("arbitrary",)),
        name="attn_ctx",
    )(nq, nk, nvo, qm, km, vmo)


def _attn_na_kernel(nq_ref, nk_ref, nvo_ref, ck_ref, cvo_ref, bias_ref, out_ref, *, n_blocks):
    i = pl.program_id(1)
    ks = jnp.clip(NA_BLOCK_ROWS * i - NA_WIN_ROWS // 2, 0, n_blocks * NA_BLOCK_ROWS - NA_SLAB_ROWS)
    slab = pl.ds(pl.multiple_of(ks * GRID_W, GRID_W), NA_SLAB_ROWS * GRID_W)

    def segs_of(h):
        return [(_slot(nk_ref, h // 2, slab), _vo(nvo_ref, h, slab), bias_ref[h].astype(F32)),
                (_slot(ck_ref, h // 2), _vo(cvo_ref, h), None)]

    _attn_heads(out_ref, _pair_q(nq_ref), segs_of)


def _attn_na(nq, nk, nvo, cache_k, cvo, bias_tab, l, *, batch, seq):
    tq = NA_BLOCK_ROWS * GRID_W
    nb = seq // tq
    past = cache_k.shape[2]

    def cls(i):
        return jnp.where(i == 0, 0, jnp.where(i == nb - 1, 2, 1))

    return pl.pallas_call(
        functools.partial(_attn_na_kernel, n_blocks=nb),
        grid=(batch, nb),
        in_specs=[
            pl.BlockSpec((tq, 512), lambda b, i: (b * nb + i, 0)),
            pl.BlockSpec((seq, 512), lambda b, i: (b, 0)),
            pl.BlockSpec((seq, VO_ALL), lambda b, i: (b, 0)),
            pl.BlockSpec((None, None, past, 512), lambda b, i: (b, l, 0, 0)),
            pl.BlockSpec((None, past, VO_ALL), lambda b, i: (b, 0, 0)),
            pl.BlockSpec((None, None, NA_HEADS, tq, NA_SLAB_ROWS * GRID_W), lambda b, i: (l, cls(i), 0, 0, 0)),
        ],
        out_specs=pl.BlockSpec((tq, 512), lambda b, i: (b * nb + i, 0)),
        out_shape=jax.ShapeDtypeStruct((batch * seq, 512), BF16),
        compiler_params=_cparams(("arbitrary", "arbitrary")),
        name="attn_na_lat",
    )(nq, nk, nvo, cache_k, cvo, bias_tab)


def _cache_prep_kernel(ckv_ref, kpe_ref, nv_ref, wukv_ref, place_ref, km_ref, vmo_ref, nvo_ref):
    kv = _dot(ckv_ref[...].astype(BF16), wukv_ref[...])
    kslots = kv[:, :MLA_HEADS * SLOT] + _dot(kpe_ref[...].astype(BF16), place_ref[...])
    km_ref[...] = kslots.astype(BF16)
    _store_vo(vmo_ref, kv[:, MLA_HEADS * SLOT:])
    _store_vo(nvo_ref, nv_ref[...])


def _cache_prep(cache_ckv, cache_kpe, cache_nv, w_ukv, place, l):
    batch, _, past, _ = cache_ckv.shape
    tout = pl.BlockSpec((None, past, VO_ALL), lambda b: (b, 0, 0))
    tshape = jax.ShapeDtypeStruct((batch, past, VO_ALL), BF16)
    return pl.pallas_call(
        _cache_prep_kernel,
        grid=(batch,),
        in_specs=[
            pl.BlockSpec((None, None, past, MLA_KV_LORA), lambda b: (b, l, 0, 0)),
            pl.BlockSpec((None, None, past, MLA_ROPE_DIM), lambda b: (b, l, 0, 0)),
            pl.BlockSpec((None, None, past, 512), lambda b: (b, l, 0, 0)),
            pl.BlockSpec((None, MLA_KV_LORA, MLA_HEADS * SLOT + 512), lambda b: (l, 0, 0)),
            pl.BlockSpec((MLA_ROPE_DIM, MLA_HEADS * SLOT), lambda b: (0, 0)),
        ],
        out_specs=[pl.BlockSpec((None, past, MLA_HEADS * SLOT), lambda b: (b, 0, 0)), tout, tout],
        out_shape=[jax.ShapeDtypeStruct((batch, past, MLA_HEADS * SLOT), BF16), tshape, tshape],
        compiler_params=_cparams(("arbitrary",)),
        name="cache_prep",
    )(cache_ckv, cache_kpe, cache_nv, w_ukv, place)


def _attn_mla_kernel(qm_ref, km_ref, vmo_ref, ckm_ref, cvmo_ref, out_ref):
    _attn_heads(out_ref, lambda h: _slot(qm_ref, h),
                lambda h: [(_slot(km_ref, h), _vo(vmo_ref, h), None), (_slot(ckm_ref, h), _vo(cvmo_ref, h), None)])


def _attn_mla(qm, km, vmo, ckm, cvmo, *, batch, seq, tq):
    nb = seq // tq
    past = ckm.shape[1]
    return pl.pallas_call(
        _attn_mla_kernel,
        grid=(batch, nb),
        in_specs=[
            pl.BlockSpec((tq, MLA_HEADS * SLOT), lambda b, i: (b * nb + i, 0)),
            pl.BlockSpec((seq, MLA_HEADS * SLOT), lambda b, i: (b, 0)),
            pl.BlockSpec((seq, VO_ALL), lambda b, i: (b, 0)),
            pl.BlockSpec((None, past, MLA_HEADS * SLOT), lambda b, i: (b, 0, 0)),
            pl.BlockSpec((None, past, VO_ALL), lambda b, i: (b, 0, 0)),
        ],
        out_specs=pl.BlockSpec((tq, 512), lambda b, i: (b * nb + i, 0)),
        out_shape=jax.ShapeDtypeStruct((batch * seq, 512), BF16),
        compiler_params=_cparams(("arbitrary", "arbitrary")),
        name="attn_mla_lat",
    )(qm, km, vmo, ckm, cvmo)


POOL_HALO = 16


GROUP_LANE = N_EXPERTS


ROUTE_ROWS = 32


def _route(logits):
    lt = logits.T[:ROUTE_ROWS]
    idx = lax.broadcasted_iota(jnp.int32, lt.shape, 0).astype(F32)
    big = jnp.float32(1 << 20)
    is_g = idx < N_GROUPS
    gl = jnp.where(is_g, lt, NEG_INF)
    gmax = gl.max(axis=0, keepdims=True)
    ge = jnp.where(is_g, jnp.exp(gl - gmax), 0.0)
    gp = ge / ge.sum(axis=0, keepdims=True)
    g_w = jnp.where(is_g, gp, -1.0).max(axis=0, keepdims=True)
    g_idx = jnp.where(is_g & (gp == g_w), idx, big).min(axis=0, keepdims=True)
    e_idx = idx - N_GROUPS
    in_grp = (e_idx >= g_idx * EXPERTS_PER_GROUP) & (e_idx < (g_idx + 1) * EXPERTS_PER_GROUP)
    el = jnp.where(in_grp, lt, NEG_INF)
    emax = el.max(axis=0, keepdims=True)
    ee = jnp.where(in_grp, jnp.exp(el - emax), 0.0)
    ep = ee / ee.sum(axis=0, keepdims=True)
    p1 = jnp.where(in_grp, ep, -1.0).max(axis=0, keepdims=True)
    i1 = jnp.where(in_grp & (ep == p1), idx, big).min(axis=0, keepdims=True)
    rest = in_grp & (idx != i1)
    p2 = jnp.where(rest, ep, -1.0).max(axis=0, keepdims=True)
    i2 = jnp.where(rest & (ep == p2), idx, big).min(axis=0, keepdims=True)
    tot = p1 + p2
    out_t = (jnp.where(idx == i1 - N_GROUPS, g_w * (p1 / tot), 0.0) + jnp.where(idx == i2 - N_GROUPS, g_w * (p2 / tot), 0.0)
             + jnp.where(idx == GROUP_LANE, g_idx, 0.0))
    return jnp.concatenate([out_t, jnp.zeros((LANES - ROUTE_ROWS, lt.shape[1]), F32)], axis=0).T


def _merge_kernel(x_ref, u_ref, yna_ref, ymla_ref, gate_ref, gt1_ref, sh2_ref, sc2_ref, gpost_ref, gpre_ref,
                  wpool_ref, pscale_ref, wbr_ref, wout_ref, wr2_ref, wr1_ref,
                  xo_ref, h2_ref, rg_ref, *, seq, tm, steps_per_seq):
    def pool_diff(uwin, utile, r0, s0):
        rows, nwin = utile.shape[0], uwin.shape[0]
        t = r0 + lax.broadcasted_iota(jnp.int32, (rows, nwin), 0)
        s = s0 + lax.broadcasted_iota(jnp.int32, (rows, nwin), 1)
        t1 = r0 + lax.broadcasted_iota(jnp.int32, (rows, 1), 0)
        ds = []
        for g, w in enumerate(POOL_WINDOWS):
            lo = jnp.maximum(t - w // 2, 0)
            hi = jnp.minimum(t + (w - w // 2), seq)
            band = ((s >= lo) & (s < hi)).astype(BF16)
            cnt = (jnp.minimum(t1 + (w - w // 2), seq) - jnp.maximum(t1 - w // 2, 0)).astype(F32)
            sl = slice(g * POOL_GC, (g + 1) * POOL_GC)
            ds.append(_dot(band, uwin[:, sl]) / cnt - utile[:, sl].astype(F32))
        return jnp.concatenate(ds, axis=-1).astype(BF16)

    if steps_per_seq == 0:
        d = jnp.concatenate([pool_diff(u_ref[si], u_ref[si], 0, 0) for si in range(tm // seq)], axis=0)
    else:
        nwin = min(tm + 2 * POOL_HALO, seq)
        r0 = (pl.program_id(0) % steps_per_seq) * tm
        s0 = pl.multiple_of(jnp.clip(r0 - POOL_HALO, 0, seq - nwin), POOL_HALO)
        d = pool_diff(u_ref[0, pl.ds(s0, nwin), :], u_ref[0, pl.ds(pl.multiple_of(r0, POOL_HALO), tm), :], r0, s0)
    y_pool = (_dot(d, wpool_ref[...]) * pscale_ref[...]).astype(BF16)

    g = gate_ref[...]
    merged = (g[:, 0:D_MODEL].astype(F32) * _dot(y_pool, wbr_ref[0])
              + g[:, D_MODEL:2 * D_MODEL].astype(F32) * _dot(yna_ref[...], wbr_ref[1])
              + g[:, 2 * D_MODEL:].astype(F32) * _dot(ymla_ref[...], wbr_ref[2]))
    y = _dot(merged.astype(BF16), wout_ref[...])
    xn = x_ref[...] + gt1_ref[...] * _rms(y, gpost_ref[...])
    xo_ref[...] = xn

    h2 = _rms(xn, gpre_ref[...]) * (1.0 + sc2_ref[...]) + sh2_ref[...]
    h_hi = h2.astype(BF16)
    h2_ref[...] = h_hi
    h_lo = (h2 - h_hi.astype(F32)).astype(BF16)
    two = _dot(h_hi, wr2_ref[...])
    logits = two[:, :LANES] + two[:, LANES:] + _dot(h_lo, wr1_ref[...])
    rg_ref[...] = _route(logits)


def _merge(x, u, yna, ymla, gate, mod5, l, w, *, seq, ctx, tm):
    n = x.shape[0]
    nseq = n // seq
    if tm >= seq:
        assert ctx and tm % seq == 0
        steps_per_seq = 0
        u_spec = pl.BlockSpec((tm // seq, seq, 512), lambda i: (i, 0, 0))
    else:
        steps_per_seq = seq // tm
        u_spec = pl.BlockSpec((1, seq, 512), lambda i: (i // steps_per_seq, 0, 0))

    def mrow(i):
        return 0 if ctx else 1 + i // steps_per_seq

    const = lambda i: (l, 0, 0)
    modspec = lambda k: pl.BlockSpec((None, None, None, 1, D_MODEL), lambda i: (l, mrow(i), k, 0, 0))
    row = lambda wd: pl.BlockSpec((tm, wd), lambda i: (i, 0))
    return pl.pallas_call(
        functools.partial(_merge_kernel, seq=seq, tm=tm, steps_per_seq=steps_per_seq),
        grid=(n // tm,),
        in_specs=[
            row(D_MODEL),
            u_spec,
            row(512), row(512), row(3 * D_MODEL),
            modspec(2), modspec(3), modspec(4),
            pl.BlockSpec((None, 1, D_MODEL), const),
            pl.BlockSpec((None, 1, D_MODEL), const),
            pl.BlockSpec((None, 512, 512), const),
            pl.BlockSpec((None, 1, 512), const),
            pl.BlockSpec((None, 3, 512, D_MODEL), lambda i: (l, 0, 0, 0)),
            pl.BlockSpec((None, D_MODEL, D_MODEL), const),
            pl.BlockSpec((None, D_MODEL, 2 * LANES), const),
            pl.BlockSpec((None, D_MODEL, LANES), const),
        ],
        out_specs=[row(D_MODEL), row(D_MODEL), row(LANES)],
        out_shape=[jax.ShapeDtypeStruct((n, D_MODEL), F32), jax.ShapeDtypeStruct((n, D_MODEL), BF16),
                   jax.ShapeDtypeStruct((n, LANES), F32)],
        compiler_params=_cparams(("arbitrary",)),
        name="merge_ctx" if ctx else "merge_lat",
    )(x, u.reshape(nseq, seq, 512), yna, ymla, gate, mod5, mod5, mod5, w["g_post_mix"], w["g_pre_ffn"],
      w["w_pool_bd"], w["pool_scale"], w["w_branch"], w["w_out"], w["w_r2"], w["w_r1"])


MOE_TM = 1024
MOE_SUB = 512
MOE_ROWS = 160
MOE_ROWS_PAD = 256


def _moe_kernel(h_ref, rg_ref, x_ref, gt2_ref, gpost_ref, w1_ref, w3_ref, w2_ref, o_ref):
    tm = h_ref.shape[0]
    subs = [slice(s * MOE_SUB, (s + 1) * MOE_SUB) for s in range(tm // MOE_SUB)]
    acc_ref = o_ref
    rg = rg_ref[...]
    lane = lax.broadcasted_iota(jnp.int32, rg.shape, 1)
    grp = jnp.where(lane == GROUP_LANE, rg, 0.0).sum(axis=-1, keepdims=True)
    member = ((lane.astype(F32) == grp) & (lane < N_GROUPS)).astype(F32)
    t_row = lax.broadcasted_iota(jnp.int32, (MOE_SUB, MOE_SUB), 0)
    t_col = lax.broadcasted_iota(jnp.int32, (MOE_SUB, MOE_SUB), 1)
    earlier, later = (t_col < t_row).astype(BF16), (t_row < t_col).astype(BF16)
    member_t = member.T[:8]
    rank_col = jnp.concatenate([_dot(earlier, member[sl].astype(BF16)) for sl in subs], axis=0)
    rank_row = jnp.concatenate([_dot(member_t[:, sl].astype(BF16), later) for sl in subs], axis=1)
    count = functools.reduce(jnp.maximum, [member[sl].sum(axis=0, keepdims=True) for sl in subs])
    rg_hi = rg.astype(BF16)
    rg_lo = (rg - rg_hi.astype(F32)).astype(BF16)
    row_id = lax.broadcasted_iota(jnp.int32, (MOE_ROWS, 1), 0).astype(F32)
    col_id = lax.broadcasted_iota(jnp.int32, (1, MOE_ROWS_PAD), 1).astype(F32)
    lane1 = lax.broadcasted_iota(jnp.int32, (1, LANES), 1)

    def experts(g, hs, gs):
        hid = []
        for e in range(EXPERTS_PER_GROUP):
            k = g * EXPERTS_PER_GROUP + e
            a = _dot(hs, w1_ref[k])
            b = _dot(hs, w3_ref[k])
            hid.append(((a * jax.nn.sigmoid(a)) * b * gs[:, k:k + 1]).astype(BF16))
        return _dot(jnp.concatenate(hid, axis=-1), w2_ref[g]).astype(BF16)

    n_slots = N_GROUPS * MOE_ROWS
    slots_pad = -(-n_slots // LANES) * LANES
    slot_row = lax.broadcasted_iota(jnp.int32, (n_slots, 1), 0).astype(F32)
    slot_col = lax.broadcasted_iota(jnp.int32, (1, slots_pad), 1).astype(F32)
    dest_row = sum(jnp.where((member_t[g:g + 1] > 0.0) & (rank_row[g:g + 1] < MOE_ROWS),
                             rank_row[g:g + 1] + (g * MOE_ROWS + 1), 0.0) for g in range(N_GROUPS)) - 1.0
    dest_col = sum(jnp.where((member[:, g:g + 1] > 0.0) & (rank_col[:, g:g + 1] < MOE_ROWS),
                             rank_col[:, g:g + 1] + (g * MOE_ROWS + 1), 0.0) for g in range(N_GROUPS)) - 1.0
    hs_sub, gs_sub = [], []
    for sl in subs:
        take = (dest_row[:, sl] == slot_row).astype(BF16)
        hs_sub.append(_dot(take, h_ref[sl, :]).astype(BF16))
        gs_sub.append(_dot(take, rg_hi[sl]) + _dot(take, rg_lo[sl]))
    parts = []
    for g in range(N_GROUPS):
        rows = slice(g * MOE_ROWS, (g + 1) * MOE_ROWS)
        parts.append(experts(g, jnp.concatenate([h[rows] for h in hs_sub], axis=0),
                             jnp.concatenate([q[rows] for q in gs_sub], axis=0)))
    pad = [jnp.zeros((slots_pad - n_slots, D_MODEL), BF16)] if slots_pad > n_slots else []
    for s, sl in enumerate(subs):
        give = (dest_col[sl] == slot_col).astype(BF16)
        rows = jnp.concatenate([p[s * MOE_ROWS:(s + 1) * MOE_ROWS] for p in parts] + pad, axis=0)
        acc_ref[sl, :] = _dot(give, rows)

    pad = jnp.zeros((MOE_ROWS_PAD - MOE_ROWS, D_MODEL), BF16)
    for g in range(N_GROUPS):
        rr, mr = rank_row[g:g + 1, :], member_t[g:g + 1, :]
        rc, mc = rank_col[:, g:g + 1], member[:, g:g + 1]
        n_g = jnp.max(jnp.where(lane1 == g, count, 0.0)).astype(jnp.int32)

        def chunk(c, carry, g=g, rr=rr, mr=mr, rc=rc, mc=mc):
            base = (c * MOE_ROWS).astype(F32)
            hs, gs = [], []
            for sl in subs:
                take = ((rr[:, sl] - base == row_id) & (mr[:, sl] > 0.0)).astype(BF16)
                hs.append(_dot(take, h_ref[sl, :]).astype(BF16))
                gs.append(_dot(take, rg_hi[sl]) + _dot(take, rg_lo[sl]))
            part = experts(g, jnp.concatenate(hs, axis=0), jnp.concatenate(gs, axis=0))
            for s, sl in enumerate(subs):
                rel = rc[sl] - base
                give = ((rel == col_id) & (rel < MOE_ROWS) & (mc[sl] > 0.0)).astype(BF16)
                rows = jnp.concatenate([part[s * MOE_ROWS:(s + 1) * MOE_ROWS], pad], axis=0)
                acc_ref[sl, :] += _dot(give, rows)
            return carry

        lax.fori_loop(1, (n_g + MOE_ROWS - 1) // MOE_ROWS, chunk, 0)

    o_ref[...] = x_ref[...] + gt2_ref[...] * _rms(acc_ref[...], gpost_ref[...])


def _moe(h2, rg, x, mod5, l, w, *, seq, ctx):
    n = x.shape[0]
    tm = MOE_TM
    steps_per_seq = seq // tm if not ctx else 0

    def mrow(i):
        return 0 if ctx else 1 + i // steps_per_seq

    row = lambda wd: pl.BlockSpec((tm, wd), lambda i: (i, 0))
    once = pl.Buffered(1)
    wspec = lambda a, b: pl.BlockSpec((None, N_EXPERTS, a, b), lambda i: (l, 0, 0, 0), pipeline_mode=once)
    return pl.pallas_call(
        _moe_kernel,
        grid=(n // tm,),
        in_specs=[
            row(D_MODEL), row(LANES), row(D_MODEL),
            pl.BlockSpec((None, None, None, 1, D_MODEL), lambda i: (l, mrow(i), 5, 0, 0)),
            pl.BlockSpec((None, 1, D_MODEL), lambda i: (l, 0, 0)),
            wspec(D_MODEL, EXPERT_FF), wspec(D_MODEL, EXPERT_FF),
            pl.BlockSpec((None, N_GROUPS, EXPERTS_PER_GROUP * EXPERT_FF, D_MODEL), lambda i: (l, 0, 0, 0),
                         pipeline_mode=once),
        ],
        out_specs=row(D_MODEL),
        out_shape=jax.ShapeDtypeStruct((n, D_MODEL), F32),
        compiler_params=_cparams(("arbitrary",)),
        name="moe_ctx" if ctx else "moe_lat",
    )(h2, rg, x, mod5, w["g_post_ffn"], w["w1"], w["w3"], w["w2"])


def _rope_table(seq):
    t = np.arange(seq)
    n_freq = MLA_ROPE_DIM // 4
    inv = jnp.asarray(ROPE_BASE, F32) ** (-jnp.arange(n_freq, dtype=F32) / n_freq)
    ang_r = jnp.asarray(t // GRID_W, F32)[:, None] * inv
    ang_c = jnp.asarray(t % GRID_W, F32)[:, None] * inv
    cos = jnp.cos(jnp.stack([ang_r, ang_c], axis=1))
    sin = jnp.sin(jnp.stack([ang_r, ang_c], axis=1))
    zeros = jnp.zeros_like(sin)
    c32 = jnp.stack([cos, cos], axis=2).reshape(seq, MLA_ROPE_DIM)
    up32 = jnp.stack([-sin, zeros], axis=2).reshape(seq, MLA_ROPE_DIM)
    dn32 = jnp.stack([zeros, sin], axis=2).reshape(seq, MLA_ROPE_DIM)

    def slot(v, fill):
        return jnp.concatenate([jnp.full((seq, KPE_LANE), fill, F32), v,
                                jnp.full((seq, SLOT - KPE_LANE - MLA_ROPE_DIM), fill, F32)], axis=1)

    return jnp.concatenate([slot(c32, 1.0), slot(up32, 0.0), slot(dn32, 0.0)], axis=1)


def _na_row_classes(n_rows):
    i = np.arange(NA_BLOCK_ROWS)[:, None]
    j = np.arange(NA_SLAB_ROWS)[None, :]
    last_ks = n_rows - NA_SLAB_ROWS
    last_q0 = n_rows - NA_BLOCK_ROWS
    specs = [
        (j < NA_WIN_ROWS, j - i),
        ((j >= i) & (j < i + NA_WIN_ROWS), j - NA_WIN_ROWS // 2 - i),
        (last_ks + j >= n_rows - NA_WIN_ROWS, last_ks + j - last_q0 - i),
    ]
    ok = np.stack([np.broadcast_to(o, (NA_BLOCK_ROWS, NA_SLAB_ROWS)) for o, _ in specs])
    dr = np.stack([np.clip(d + NA_WIN_ROWS - 1, 0, 2 * NA_WIN_ROWS - 2) for _, d in specs])
    return ok, dr


def _bias_kernel(rpb_ref, out_ref, *, row_ok, row_dr):
    reach = NA_WIN_COLS - 1
    rows = rpb_ref[...] * LOG2E
    v_lo = pltpu.roll(rows, SLOT - reach, axis=1)
    v_hi = pltpu.roll(rows, GRID_W - reach, axis=1)
    lane1 = lax.broadcasted_iota(jnp.int32, (1, SLOT), 1)
    mid = (lane1 >= GRID_W - reach) & (lane1 <= GRID_W + reach)
    c = lax.broadcasted_iota(jnp.int32, (GRID_W, SLOT), 0)
    lane = lax.broadcasted_iota(jnp.int32, (GRID_W, SLOT), 1)
    kc = lane & (GRID_W - 1)
    c0 = jnp.clip(c - NA_WIN_COLS // 2, 0, GRID_W - NA_WIN_COLS)
    in_col = (kc >= c0) & (kc < c0 + NA_WIN_COLS)
    first = lane < GRID_W
    for k in range(3):
        for i in range(NA_BLOCK_ROWS):
            for jp in range(NA_SLAB_ROWS // 2):
                ok_a, ok_b = bool(row_ok[k, i, 2 * jp]), bool(row_ok[k, i, 2 * jp + 1])
                a, b = int(row_dr[k, i, 2 * jp]), int(row_dr[k, i, 2 * jp + 1])
                rs = slice(i * GRID_W, (i + 1) * GRID_W)
                cs = slice(jp * SLOT, (jp + 1) * SLOT)
                if not (ok_a or ok_b):
                    out_ref[k, rs, cs] = jnp.full((GRID_W, SLOT), NEG_INF, out_ref.dtype)
                    continue
                v = jnp.where(mid, v_hi[b:b + 1], v_lo[a:a + 1])
                tile = pltpu.roll(jnp.broadcast_to(v, (GRID_W, SLOT)), 0, axis=1, stride=1, stride_axis=0)
                ok = in_col
                if not ok_b:
                    ok = ok & first
                if not ok_a:
                    ok = ok & jnp.logical_not(first)
                out_ref[k, rs, cs] = jnp.where(ok, tile, NEG_INF).astype(out_ref.dtype)


def _na_bias_table(rpb, n_rows):
    n_dr, n_dc = rpb.shape[-2:]
    row_ok, row_dr = _na_row_classes(n_rows)
    rpb_p = jnp.pad(rpb, ((0, 0), (0, 0), (0, 0), (0, SLOT - n_dc)))
    tq, tk = NA_BLOCK_ROWS * GRID_W, NA_SLAB_ROWS * GRID_W
    return pl.pallas_call(
        functools.partial(_bias_kernel, row_ok=row_ok, row_dr=row_dr),
        grid=(DEPTH, NA_HEADS),
        in_specs=[pl.BlockSpec((None, None, n_dr, SLOT), lambda l, h: (l, h, 0, 0))],
        out_specs=pl.BlockSpec((None, 3, None, tq, tk), lambda l, h: (l, 0, h, 0, 0)),
        out_shape=jax.ShapeDtypeStruct((DEPTH, 3, NA_HEADS, tq, tk), BF16),
        compiler_params=_cparams(("arbitrary", "arbitrary")),
        name="na_bias_table",
    )(rpb_p)


def _prep(p):
    L = DEPTH
    w_in = p["w_in"]
    w_in_p = jnp.pad(w_in.astype(BF16), ((0, 0), (0, 0), (0, C_END - w_in.shape[-1])))
    w_uq = jnp.pad(p["mla_w_uq"].reshape(L, MLA_Q_LORA, MLA_HEADS, MLA_QK_DIM),
                   ((0, 0), (0, 0), (0, 0), (0, SLOT - MLA_QK_DIM))).reshape(L, MLA_Q_LORA, MLA_HEADS * SLOT)
    ukv = p["mla_w_ukv"].reshape(L, MLA_KV_LORA, MLA_HEADS, MLA_NOPE_DIM + MLA_V_DIM)
    uk = jnp.pad(ukv[..., :MLA_NOPE_DIM], ((0, 0), (0, 0), (0, 0), (0, SLOT - MLA_NOPE_DIM)))
    w_ukv = jnp.concatenate([uk.reshape(L, MLA_KV_LORA, MLA_HEADS * SLOT),
                             ukv[..., MLA_NOPE_DIM:].reshape(L, MLA_KV_LORA, MLA_HEADS * MLA_V_DIM)], axis=-1)
    w_pool_bd = jnp.einsum("lgcd,gh->lgchd", p["w_pool"], jnp.eye(4, dtype=F32)).reshape(L, POOL_WIDTH, POOL_WIDTH)
    w_r = jnp.concatenate([p["w_group_router"], p["w_expert_router"]], axis=-1)
    r_hi = w_r.astype(BF16)
    r_lo = (w_r - r_hi.astype(F32)).astype(BF16)
    padl = lambda a: jnp.pad(a, ((0, 0), (0, 0), (0, LANES - a.shape[-1])))
    vec = lambda a: a.reshape(L, 1, a.shape[-1])
    return {
        "w_in": w_in_p,
        "w_uq": w_uq.astype(BF16),
        "w_ukv": w_ukv.astype(BF16),
        "w_pool_bd": w_pool_bd.astype(BF16),
        "w_branch": p["w_branch"].astype(BF16),
        "w_out": p["w_out"].astype(BF16),
        "w_r2": jnp.concatenate([padl(r_hi), padl(r_lo)], axis=-1),
        "w_r1": padl(r_hi),
        "w1": p["w_exp_gate"].astype(BF16),
        "w3": p["w_exp_up"].astype(BF16),
        "w2": p["w_exp_down"].astype(BF16).reshape(L, N_GROUPS, EXPERTS_PER_GROUP * EXPERT_FF, D_MODEL),
        "g_pre_mix": vec(p["g_pre_mix"]), "g_post_mix": vec(p["g_post_mix"]),
        "g_pre_ffn": vec(p["g_pre_ffn"]), "g_post_ffn": vec(p["g_post_ffn"]),
        "pool_scale": vec(p["pool_scale"]), "q_norm": vec(p["mla_q_norm"]), "kv_norm": vec(p["mla_kv_norm"]),
    }


def kernel(x_prompt, x_sample, cache_na_k, cache_na_v, cache_mla_ckv, cache_mla_kpe, c, c_ctx, w_mod, b_mod,
           g_pre_mix, g_post_mix, g_pre_ffn, g_post_ffn, w_in, w_pool, pool_scale, na_rpb, mla_q_norm, mla_w_uq,
           mla_kv_norm, mla_w_ukv, w_branch, w_out, w_group_router, w_expert_router, w_exp_gate, w_exp_up,
           w_exp_down):
    batch, seq, _ = x_prompt.shape
    dbatch, dseq, _ = x_sample.shape
    past = cache_na_k.shape[2]
    assert dbatch + 1 <= 8 and dseq % (NA_BLOCK_ROWS * GRID_W) == 0

    w = _prep(dict(w_in=w_in, w_pool=w_pool, pool_scale=pool_scale, mla_q_norm=mla_q_norm, mla_w_uq=mla_w_uq,
                   mla_kv_norm=mla_kv_norm, mla_w_ukv=mla_w_ukv, w_branch=w_branch, w_out=w_out,
                   w_group_router=w_group_router, w_expert_router=w_expert_router, w_exp_gate=w_exp_gate,
                   w_exp_up=w_exp_up, w_exp_down=w_exp_down, g_pre_mix=g_pre_mix, g_post_mix=g_post_mix,
                   g_pre_ffn=g_pre_ffn, g_post_ffn=g_post_ffn))
    rope_tab = _rope_table(dseq)
    bias_tab = _na_bias_table(na_rpb, dseq // GRID_W)
    place = jnp.tile(jnp.pad(jnp.eye(MLA_ROPE_DIM, dtype=F32), ((0, 0), (KPE_LANE, SLOT - KPE_LANE - MLA_ROPE_DIM))),
                     (1, MLA_HEADS)).astype(BF16)

    cvec = jnp.concatenate([c_ctx[None], c, jnp.zeros((8 - 1 - dbatch, D_MODEL), F32)], axis=0)
    mod5 = _modulation(cvec, w_mod, b_mod).reshape(DEPTH, 8, 6, 1, D_MODEL)

    ck = cache_na_k.reshape(dbatch, DEPTH, past, NA_WIDTH)
    cv = cache_na_v.reshape(dbatch, DEPTH, past, NA_WIDTH)

    xp = x_prompt.reshape(batch * seq, D_MODEL)
    assert DEPTH == 2
    caches = None
    for l in range(DEPTH):
        u, nq, nvo, qm, km, vmo, gate, *caches = _inproj(xp, mod5, l, w, seq=seq, rope_tab=None, ctx_out=True,
                                                         tm=seq, prev=caches)
        yna, ymla = _attn_ctx(nq, caches[0], nvo, qm, km, vmo, seq=seq, nk_layer=l if l > 0 else None)
        xp, h2, rg = _merge(xp, u, yna, ymla, gate, mod5, l, w, seq=seq, ctx=True, tm=2 * seq)
        xp = _moe(h2, rg, xp, mod5, l, w, seq=seq, ctx=True)
    new_nk, new_nv, new_ckv, new_kpe = caches

    xs = x_sample.reshape(dbatch * dseq, D_MODEL)
    for l in range(DEPTH):
        u, nq, nvo, qm, km, vmo, gate, nk = _inproj(xs, mod5, l, w, seq=dseq, rope_tab=rope_tab, ctx_out=False, tm=512)
        ckm, cvmo, cnvo = _cache_prep(cache_mla_ckv, cache_mla_kpe, cv, w["w_ukv"], place, l)
        yna = _attn_na(nq, nk, nvo, ck, cnvo, bias_tab, l, batch=dbatch, seq=dseq)
        ymla = _attn_mla(qm, km, vmo, ckm, cvmo, batch=dbatch, seq=dseq, tq=512)
        xs, h2, rg = _merge(xs, u, yna, ymla, gate, mod5, l, w, seq=dseq, ctx=False, tm=512)
        xs = _moe(h2, rg, xs, mod5, l, w, seq=dseq, ctx=False)

    heads = (batch, DEPTH, seq, NA_HEADS, NA_HEAD_DIM)
    return (xp.reshape(batch, seq, D_MODEL), xs.reshape(dbatch, dseq, D_MODEL),
            new_nk.reshape(heads), new_nv.reshape(heads), new_ckv, new_kpe)
```

```python
import functools

import numpy as np
import jax
import jax.numpy as jnp
from jax import lax
from jax.experimental import pallas as pl
from jax.experimental.pallas import tpu as pltpu

F32 = jnp.float32
BF16 = jnp.bfloat16

D_MODEL = 1024
DEPTH = 2
GRID_W = 64
POOL_WINDOWS = (2, 4, 8, 16)
POOL_GC = 128
POOL_WIDTH = 512
NA_HEADS = 8
NA_HEAD_DIM = 64
NA_WIDTH = 512
NA_WIN_ROWS = 8
NA_WIN_COLS = 16
MLA_HEADS = 8
MLA_NOPE_DIM = 64
MLA_ROPE_DIM = 32
MLA_V_DIM = 64
MLA_QK_DIM = 96
MLA_Q_LORA = 384
MLA_KV_LORA = 256
N_GROUPS = 4
EXPERTS_PER_GROUP = 4
N_EXPERTS = 16
EXPERT_FF = 256
ROPE_BASE = 10000.0
EPS = 1e-6
NEG_INF = -1e30

LANES = 128
SLOT = LANES
VMEM_LIMIT = 56 * 1024 * 1024

C_U, C_NQ, C_NK, C_NV, C_CQ, C_CKV, C_KPE, C_G, C_END = 0, 512, 1024, 1536, 2048, 2432, 2688, 2720, 5888
G_WIDTH = 3 * D_MODEL
KPE_LANE = MLA_NOPE_DIM

LOG2E = 1.4426950408889634
VO_W = 2 * SLOT
VO_ALL = (NA_HEADS // 2) * VO_W

NA_BLOCK_ROWS = 4
NA_SLAB_ROWS = 12


def _cparams(sem):
    return pltpu.CompilerParams(dimension_semantics=sem, vmem_limit_bytes=VMEM_LIMIT)


def _rms(x, g):
    return x * lax.rsqrt(jnp.mean(x * x, axis=-1, keepdims=True) + EPS) * g


def _dot(a, b):
    return jnp.dot(a, b, preferred_element_type=F32)


def _dot_nt(a, b):
    return lax.dot_general(a, b, (((1,), (1,)), ((), ())), preferred_element_type=F32)


def _mod_kernel(c_ref, w_ref, b_ref, o_ref):
    c = c_ref[...]
    s = (c * jax.nn.sigmoid(c)).astype(BF16)
    o_ref[...] = _dot(s, w_ref[...].astype(BF16)) + b_ref[...]


def _modulation(cvec, w_mod, b_mod):
    tn = 1536
    n = w_mod.shape[-1]
    return pl.pallas_call(
        _mod_kernel,
        grid=(DEPTH, n // tn),
        in_specs=[
            pl.BlockSpec((8, D_MODEL), lambda l, j: (0, 0)),
            pl.BlockSpec((None, D_MODEL, tn), lambda l, j: (l, 0, j)),
            pl.BlockSpec((None, 1, tn), lambda l, j: (l, 0, j)),
        ],
        out_specs=pl.BlockSpec((None, 8, tn), lambda l, j: (l, 0, j)),
        out_shape=jax.ShapeDtypeStruct((DEPTH, 8, n), F32),
        compiler_params=_cparams(("arbitrary", "arbitrary")),
        name="modulation",
    )(cvec, w_mod, b_mod.reshape(DEPTH, 1, n))


def _rope_slot(x, tab):
    c, s_up, s_dn = tab[:, 0:SLOT], tab[:, SLOT:2 * SLOT], tab[:, 2 * SLOT:3 * SLOT]
    half = MLA_ROPE_DIM // 4
    x_up = pltpu.roll(x, SLOT - half, axis=1)
    x_dn = pltpu.roll(x, half, axis=1)
    return x * c + x_up * s_up + x_dn * s_dn


def _store_vo(vo_ref, v):
    tm = v.shape[0]
    for p in range(NA_HEADS // 2):
        vo_ref[:, p * VO_W:p * VO_W + SLOT] = v[:, p * SLOT:(p + 1) * SLOT].astype(BF16)
        vo_ref[:, p * VO_W + SLOT:(p + 1) * VO_W] = jnp.ones((tm, SLOT), BF16)


def _realign_gate_columns(win_ref, wtail_ref, wg_ref, wqk_ref):
    off = C_G % LANES
    lane = lax.broadcasted_iota(jnp.int32, (1, LANES), 1)
    base = C_G - off
    nxt = win_ref[:, base:base + LANES].astype(F32)
    kpe = pltpu.roll(nxt, KPE_LANE - (C_KPE - base), axis=1)
    wqk_ref[:, :MLA_Q_LORA] = win_ref[:, C_CQ:C_CKV]
    wqk_ref[:, MLA_Q_LORA:] = jnp.where((lane >= KPE_LANE) & (lane < KPE_LANE + MLA_ROPE_DIM), kpe, 0.0).astype(BF16)
    for j in range(G_WIDTH // LANES):
        cur = nxt
        last = j == G_WIDTH // LANES - 1
        nxt = (wtail_ref[...] if last else win_ref[:, base + (j + 1) * LANES:base + (j + 2) * LANES]).astype(F32)
        tile = pltpu.roll(jnp.where(lane >= off, cur, nxt), LANES - off, axis=1)
        wg_ref[:, j * LANES:(j + 1) * LANES] = tile.astype(BF16)


def _inproj_kernel(*refs, rope, ctx_out, stacked):
    n_in = 10 + (1 if rope else 0) + (4 if stacked else 0)
    ins, outs, (wg_ref, wqk_ref) = refs[:n_in], refs[n_in:-2], refs[-2:]
    x_ref, sh_ref, sc_ref, g_ref, win_ref, qn_ref, wuq_ref, kvn_ref, wukv_ref, wtail_ref = ins[:10]
    tab_ref = ins[10] if rope else None
    prev = ins[n_in - 4:] if stacked else None
    if ctx_out:
        u_ref, nq_ref, nvo_ref, qm_ref, km_ref, vmo_ref, gate_ref, nk_ref, nv_ref, ckv_ref, kpe_ref = outs
    else:
        u_ref, nq_ref, nvo_ref, qm_ref, km_ref, vmo_ref, gate_ref, nk_ref = outs

    def put(ref, k, val):
        if stacked:
            ref[0] = prev[k][...]
            ref[1] = val
        else:
            ref[...] = val

    @pl.when(pl.program_id(0) == 0)
    def _():
        _realign_gate_columns(win_ref, wtail_ref, wg_ref, wqk_ref)

    x = x_ref[...]
    h = (_rms(x, g_ref[...]) * (1.0 + sc_ref[...]) + sh_ref[...]).astype(BF16)

    def seg(a, b):
        return _dot(h, win_ref[:, a:b])

    u_ref[...] = seg(C_U, C_NQ).astype(u_ref.dtype)
    nq_ref[...] = (seg(C_NQ, C_NK) * (NA_HEAD_DIM ** -0.5 * LOG2E)).astype(BF16)
    nk = seg(C_NK, C_NV)
    nv = seg(C_NV, C_CQ)
    if ctx_out:
        put(nk_ref, 0, nk)
        put(nv_ref, 1, nv)
    else:
        nk_ref[...] = nk.astype(BF16)
    _store_vo(nvo_ref, nv)
    gate_ref[...] = jax.nn.sigmoid(_dot(h, wg_ref[...])).astype(BF16)

    cqk = _dot(h, wqk_ref[...])
    cq = _rms(cqk[:, :MLA_Q_LORA], qn_ref[...]).astype(BF16)
    q = _dot(cq, wuq_ref[...])
    tab = tab_ref[...] if rope else None
    for hd in range(MLA_HEADS):
        qs = q[:, hd * SLOT:(hd + 1) * SLOT]
        if rope:
            qs = _rope_slot(qs, tab)
        qm_ref[:, hd * SLOT:(hd + 1) * SLOT] = (qs * (MLA_QK_DIM ** -0.5 * LOG2E)).astype(BF16)

    ckv = _rms(seg(C_CKV, C_KPE), kvn_ref[...])
    kpe_slot = cqk[:, MLA_Q_LORA:]
    if ctx_out:
        put(ckv_ref, 2, ckv)
        put(kpe_ref, 3, kpe_slot[:, KPE_LANE:KPE_LANE + MLA_ROPE_DIM])
    if rope:
        kpe_slot = _rope_slot(kpe_slot, tab)
    kv = _dot(ckv.astype(BF16), wukv_ref[...])
    for hd in range(MLA_HEADS):
        km_ref[:, hd * SLOT:(hd + 1) * SLOT] = (kv[:, hd * SLOT:(hd + 1) * SLOT] + kpe_slot).astype(BF16)
    _store_vo(vmo_ref, kv[:, MLA_HEADS * SLOT:])


def _inproj(x, mod5, l, w, *, seq, rope_tab, ctx_out, tm, prev=None):
    n = x.shape[0]
    steps_per_seq = seq // tm if not ctx_out else 0
    stacked = prev is not None
    assert not stacked or (ctx_out and tm == seq and l == 1)

    def mrow(i):
        return 0 if ctx_out else 1 + i // steps_per_seq

    const = lambda i: (l, 0, 0)
    in_specs = [
        pl.BlockSpec((tm, D_MODEL), lambda i: (i, 0)),
        pl.BlockSpec((None, None, None, 1, D_MODEL), lambda i: (l, mrow(i), 0, 0, 0)),
        pl.BlockSpec((None, None, None, 1, D_MODEL), lambda i: (l, mrow(i), 1, 0, 0)),
        pl.BlockSpec((None, 1, D_MODEL), const),
        pl.BlockSpec((None, D_MODEL, w["w_in"].shape[-1]), const, pipeline_mode=pl.Buffered(1)),
        pl.BlockSpec((None, 1, MLA_Q_LORA), const),
        pl.BlockSpec((None, MLA_Q_LORA, MLA_HEADS * SLOT), const, pipeline_mode=pl.Buffered(1)),
        pl.BlockSpec((None, 1, MLA_KV_LORA), const),
        pl.BlockSpec((None, MLA_KV_LORA, MLA_HEADS * SLOT + 512), const, pipeline_mode=pl.Buffered(1)),
        pl.BlockSpec((None, D_MODEL, LANES), const),
    ]
    args = [x, mod5, mod5, w["g_pre_mix"], w["w_in"], w["q_norm"], w["w_uq"], w["kv_norm"], w["w_ukv"], w["w_in_tail"]]
    rope = rope_tab is not None
    if rope:
        in_specs.append(pl.BlockSpec((tm, 3 * SLOT), lambda i: (i % steps_per_seq, 0)))
        args.append(rope_tab)
    row = lambda wd: pl.BlockSpec((tm, wd), lambda i: (i, 0))
    rows_of = lambda wd, dt: jax.ShapeDtypeStruct((n, wd), dt)
    widths = [512, 512, VO_ALL, MLA_HEADS * SLOT, MLA_HEADS * SLOT, VO_ALL, 3 * D_MODEL]
    out_specs = [row(wd) for wd in widths]
    out_shape = [rows_of(wd, BF16) for wd in widths]
    cache_w = [512, 512, MLA_KV_LORA, MLA_ROPE_DIM] if ctx_out else []
    if not ctx_out:
        out_specs.append(row(512))
        out_shape.append(rows_of(512, BF16))
    elif stacked:
        in_specs += [row(wd) for wd in cache_w]
        args += list(prev)
        out_specs += [pl.BlockSpec((None, DEPTH, seq, wd), lambda i: (i, 0, 0, 0)) for wd in cache_w]
        out_shape += [jax.ShapeDtypeStruct((n // seq, DEPTH, seq, wd), F32) for wd in cache_w]
    else:
        out_specs += [row(wd) for wd in cache_w]
        out_shape += [rows_of(wd, F32) for wd in cache_w]
    return pl.pallas_call(
        functools.partial(_inproj_kernel, rope=rope, ctx_out=ctx_out, stacked=stacked),
        grid=(n // tm,),
        in_specs=in_specs,
        out_specs=out_specs,
        out_shape=out_shape,
        scratch_shapes=[pltpu.VMEM((D_MODEL, G_WIDTH), BF16), pltpu.VMEM((D_MODEL, MLA_Q_LORA + SLOT), BF16)],
        compiler_params=_cparams(("arbitrary",)),
        name="inproj_ctx" if ctx_out else "inproj_lat",
    )(*args)


def _attend(q_h, segs):
    scores = []
    for k, _, bias in segs:
        s = _dot_nt(q_h, k)
        if bias is not None:
            s = s + bias
        scores.append(s)
    m = scores[0].max(axis=-1, keepdims=True)
    for s in scores[1:]:
        m = jnp.maximum(m, s.max(axis=-1, keepdims=True))
    acc = None
    for s, (_, vo, _) in zip(scores, segs):
        o = _dot(jnp.exp2(s - m).astype(BF16), vo)
        acc = o if acc is None else acc + o
    return acc[:, :SLOT] / acc[:, SLOT:]


def _attn_heads(out_ref, q_of, segs_of, rows=slice(None)):
    lo = lax.broadcasted_iota(jnp.int32, (1, SLOT), 1) < NA_HEAD_DIM
    for p in range(NA_HEADS // 2):
        o0, o1 = [_attend(q_of(2 * p + e), segs_of(2 * p + e)) for e in range(2)]
        out_ref[rows, p * SLOT:(p + 1) * SLOT] = jnp.where(lo, o0, o1).astype(out_ref.dtype)


def _pair_q(q_ref, rows=slice(None)):
    lo = lax.broadcasted_iota(jnp.int32, (1, SLOT), 1) < NA_HEAD_DIM

    def q_of(h):
        q_pair = q_ref[rows, (h // 2) * SLOT:(h // 2 + 1) * SLOT]
        keep = lo if h % 2 == 0 else jnp.logical_not(lo)
        return jnp.where(keep, q_pair, jnp.zeros_like(q_pair))
    return q_of


def _slot(ref, i, rows=slice(None)):
    return ref[rows, i * SLOT:(i + 1) * SLOT].astype(BF16)


def _vo(ref, h, rows=slice(None)):
    return ref[rows, (h // 2) * VO_W:(h // 2 + 1) * VO_W]


CTX_SEQS_PER_STEP = 4


def _attn_ctx_kernel(nq_ref, nk_ref, nvo_ref, qm_ref, km_ref, vmo_ref, yna_ref, ymla_ref, *, seq, nk_stacked):
    for bi in range(CTX_SEQS_PER_STEP):
        rows = slice(bi * seq, (bi + 1) * seq)
        if nk_stacked:
            nk_of = lambda p, bi=bi: nk_ref[bi, :, p * SLOT:(p + 1) * SLOT].astype(BF16)
        else:
            nk_of = lambda p, rows=rows: _slot(nk_ref, p, rows)
        _attn_heads(yna_ref, _pair_q(nq_ref, rows),
                    lambda h, rows=rows, nk_of=nk_of: [(nk_of(h // 2), _vo(nvo_ref, h, rows), None)], rows)
        _attn_heads(ymla_ref, lambda h, rows=rows: _slot(qm_ref, h, rows),
                    lambda h, rows=rows: [(_slot(km_ref, h, rows), _vo(vmo_ref, h, rows), None)], rows)


def _attn_ctx(nq, nk, nvo, qm, km, vmo, *, seq, nk_layer=None):
    n = nq.shape[0]
    nb = CTX_SEQS_PER_STEP
    spec = lambda wd: pl.BlockSpec((nb * seq, wd), lambda b: (b, 0))
    nk_spec = spec(512) if nk_layer is None else pl.BlockSpec((nb, None, seq, 512), lambda b: (b, nk_layer, 0, 0))
    return pl.pallas_call(
        functools.partial(_attn_ctx_kernel, seq=seq, nk_stacked=nk_layer is not None),
        grid=(n // (nb * seq),),
        in_specs=[spec(512), nk_spec, spec(VO_ALL), spec(1024), spec(1024), spec(VO_ALL)],
        out_specs=[spec(512), spec(512)],
        out_shape=[jax.ShapeDtypeStruct((n, 512), BF16)] * 2,
        compiler_params=_cparams(("arbitrary",)),
        name="attn_ctx",
    )(nq, nk, nvo, qm, km, vmo)


NA_BLOCKS_PER_STEP = 2


def _attn_na_kernel(nq_ref, nk_ref, nvo_ref, ck_ref, cvo_ref, *rest, n_blocks):
    bias_refs, out_ref = rest[:NA_BLOCKS_PER_STEP], rest[NA_BLOCKS_PER_STEP]
    tq = NA_BLOCK_ROWS * GRID_W
    for bi in range(NA_BLOCKS_PER_STEP):
        i = pl.program_id(1) * NA_BLOCKS_PER_STEP + bi
        ks = jnp.clip(NA_BLOCK_ROWS * i - NA_WIN_ROWS // 2, 0, n_blocks * NA_BLOCK_ROWS - NA_SLAB_ROWS)
        slab = pl.ds(pl.multiple_of(ks * GRID_W, GRID_W), NA_SLAB_ROWS * GRID_W)
        rows = slice(bi * tq, (bi + 1) * tq)

        def segs_of(h, slab=slab, bias_ref=bias_refs[bi]):
            return [(_slot(nk_ref, h // 2, slab), _vo(nvo_ref, h, slab), bias_ref[h].astype(F32)),
                    (_slot(ck_ref, h // 2), _vo(cvo_ref, h), None)]

        _attn_heads(out_ref, _pair_q(nq_ref, rows), segs_of, rows)


def _attn_na(nq, nk, nvo, cache_k, cvo, bias_tab, l, *, batch, seq):
    tq = NA_BLOCK_ROWS * GRID_W
    nb = seq // tq
    per = NA_BLOCKS_PER_STEP
    steps = nb // per
    past = cache_k.shape[2]

    def cls(i):
        return jnp.where(i == 0, 0, jnp.where(i == nb - 1, 2, 1))

    bias_specs = [pl.BlockSpec((None, None, NA_HEADS, tq, NA_SLAB_ROWS * GRID_W),
                               lambda b, j, bi=bi: (l, cls(j * per + bi), 0, 0, 0)) for bi in range(per)]
    return pl.pallas_call(
        functools.partial(_attn_na_kernel, n_blocks=nb),
        grid=(batch, steps),
        in_specs=[
            pl.BlockSpec((per * tq, 512), lambda b, j: (b * steps + j, 0)),
            pl.BlockSpec((seq, 512), lambda b, j: (b, 0)),
            pl.BlockSpec((seq, VO_ALL), lambda b, j: (b, 0)),
            pl.BlockSpec((None, None, past, 512), lambda b, j: (b, l, 0, 0)),
            pl.BlockSpec((None, past, VO_ALL), lambda b, j: (b, 0, 0)),
        ] + bias_specs,
        out_specs=pl.BlockSpec((per * tq, 512), lambda b, j: (b * steps + j, 0)),
        out_shape=jax.ShapeDtypeStruct((batch * seq, 512), BF16),
        compiler_params=_cparams(("arbitrary", "arbitrary")),
        name="attn_na_lat",
    )(nq, nk, nvo, cache_k, cvo, *([bias_tab] * per))


def _cache_prep_kernel(ckv_ref, kpe_ref, nv_ref, wukv_ref, place_ref, km_ref, vmo_ref, nvo_ref):
    kv = _dot(ckv_ref[...].astype(BF16), wukv_ref[...])
    kslots = kv[:, :MLA_HEADS * SLOT] + _dot(kpe_ref[...].astype(BF16), place_ref[...])
    km_ref[...] = kslots.astype(BF16)
    _store_vo(vmo_ref, kv[:, MLA_HEADS * SLOT:])
    _store_vo(nvo_ref, nv_ref[...])


def _cache_prep(cache_ckv, cache_kpe, cache_nv, w_ukv, place, l):
    batch, _, past, _ = cache_ckv.shape
    tout = pl.BlockSpec((None, past, VO_ALL), lambda b: (b, 0, 0))
    tshape = jax.ShapeDtypeStruct((batch, past, VO_ALL), BF16)
    return pl.pallas_call(
        _cache_prep_kernel,
        grid=(batch,),
        in_specs=[
            pl.BlockSpec((None, None, past, MLA_KV_LORA), lambda b: (b, l, 0, 0)),
            pl.BlockSpec((None, None, past, MLA_ROPE_DIM), lambda b: (b, l, 0, 0)),
            pl.BlockSpec((None, None, past, 512), lambda b: (b, l, 0, 0)),
            pl.BlockSpec((None, MLA_KV_LORA, MLA_HEADS * SLOT + 512), lambda b: (l, 0, 0)),
            pl.BlockSpec((MLA_ROPE_DIM, MLA_HEADS * SLOT), lambda b: (0, 0)),
        ],
        out_specs=[pl.BlockSpec((None, past, MLA_HEADS * SLOT), lambda b: (b, 0, 0)), tout, tout],
        out_shape=[jax.ShapeDtypeStruct((batch, past, MLA_HEADS * SLOT), BF16), tshape, tshape],
        compiler_params=_cparams(("arbitrary",)),
        name="cache_prep",
    )(cache_ckv, cache_kpe, cache_nv, w_ukv, place)


def _attn_mla_kernel(qm_ref, km_ref, vmo_ref, ckm_ref, cvmo_ref, out_ref):
    _attn_heads(out_ref, lambda h: _slot(qm_ref, h),
                lambda h: [(_slot(km_ref, h), _vo(vmo_ref, h), None), (_slot(ckm_ref, h), _vo(cvmo_ref, h), None)])


def _attn_mla(qm, km, vmo, ckm, cvmo, *, batch, seq, tq):
    nb = seq // tq
    past = ckm.shape[1]
    return pl.pallas_call(
        _attn_mla_kernel,
        grid=(batch, nb),
        in_specs=[
            pl.BlockSpec((tq, MLA_HEADS * SLOT), lambda b, i: (b * nb + i, 0)),
            pl.BlockSpec((seq, MLA_HEADS * SLOT), lambda b, i: (b, 0)),
            pl.BlockSpec((seq, VO_ALL), lambda b, i: (b, 0)),
            pl.BlockSpec((None, past, MLA_HEADS * SLOT), lambda b, i: (b, 0, 0)),
            pl.BlockSpec((None, past, VO_ALL), lambda b, i: (b, 0, 0)),
        ],
        out_specs=pl.BlockSpec((tq, 512), lambda b, i: (b * nb + i, 0)),
        out_shape=jax.ShapeDtypeStruct((batch * seq, 512), BF16),
        compiler_params=_cparams(("arbitrary", "arbitrary")),
        name="attn_mla_lat",
    )(qm, km, vmo, ckm, cvmo)


POOL_HALO = 16


GROUP_LANE = N_EXPERTS


ROUTE_ROWS = 32


def _route(logits):
    lt = logits.T[:ROUTE_ROWS]
    idx = lax.broadcasted_iota(jnp.int32, lt.shape, 0).astype(F32)
    big = jnp.float32(1 << 20)
    is_g = idx < N_GROUPS
    gl = jnp.where(is_g, lt, NEG_INF)
    gmax = gl.max(axis=0, keepdims=True)
    ge = jnp.where(is_g, jnp.exp(gl - gmax), 0.0)
    gp = ge / ge.sum(axis=0, keepdims=True)
    g_w = jnp.where(is_g, gp, -1.0).max(axis=0, keepdims=True)
    g_idx = jnp.where(is_g & (gp == g_w), idx, big).min(axis=0, keepdims=True)
    e_idx = idx - N_GROUPS
    in_grp = (e_idx >= g_idx * EXPERTS_PER_GROUP) & (e_idx < (g_idx + 1) * EXPERTS_PER_GROUP)
    el = jnp.where(in_grp, lt, NEG_INF)
    emax = el.max(axis=0, keepdims=True)
    ee = jnp.where(in_grp, jnp.exp(el - emax), 0.0)
    ep = ee / ee.sum(axis=0, keepdims=True)
    p1 = jnp.where(in_grp, ep, -1.0).max(axis=0, keepdims=True)
    i1 = jnp.where(in_grp & (ep == p1), idx, big).min(axis=0, keepdims=True)
    rest = in_grp & (idx != i1)
    p2 = jnp.where(rest, ep, -1.0).max(axis=0, keepdims=True)
    i2 = jnp.where(rest & (ep == p2), idx, big).min(axis=0, keepdims=True)
    tot = p1 + p2
    out_t = (jnp.where(idx == i1 - N_GROUPS, g_w * (p1 / tot), 0.0) + jnp.where(idx == i2 - N_GROUPS, g_w * (p2 / tot), 0.0)
             + jnp.where(idx == GROUP_LANE, g_idx, 0.0))
    return jnp.concatenate([out_t, jnp.zeros((LANES - ROUTE_ROWS, lt.shape[1]), F32)], axis=0).T


def _merge_kernel(x_ref, u_ref, yna_ref, ymla_ref, gate_ref, gt1_ref, sh2_ref, sc2_ref, gpost_ref, gpre_ref,
                  wpool_ref, pscale_ref, wbr_ref, wout_ref, wr2_ref, wr1_ref,
                  xo_ref, h2_ref, rg_ref, *, seq, tm, steps_per_seq):
    def pool_diff(uwin, utile, r0, s0):
        rows, nwin = utile.shape[0], uwin.shape[0]
        t = r0 + lax.broadcasted_iota(jnp.int32, (rows, nwin), 0)
        s = s0 + lax.broadcasted_iota(jnp.int32, (rows, nwin), 1)
        t1 = r0 + lax.broadcasted_iota(jnp.int32, (rows, 1), 0)
        ds = []
        for g, w in enumerate(POOL_WINDOWS):
            lo = jnp.maximum(t - w // 2, 0)
            hi = jnp.minimum(t + (w - w // 2), seq)
            band = ((s >= lo) & (s < hi)).astype(BF16)
            cnt = (jnp.minimum(t1 + (w - w // 2), seq) - jnp.maximum(t1 - w // 2, 0)).astype(F32)
            sl = slice(g * POOL_GC, (g + 1) * POOL_GC)
            ds.append(_dot(band, uwin[:, sl]) / cnt - utile[:, sl].astype(F32))
        return jnp.concatenate(ds, axis=-1).astype(BF16)

    if steps_per_seq == 0:
        d = jnp.concatenate([pool_diff(u_ref[si], u_ref[si], 0, 0) for si in range(tm // seq)], axis=0)
    else:
        nwin = min(tm + 2 * POOL_HALO, seq)
        r0 = (pl.program_id(0) % steps_per_seq) * tm
        s0 = pl.multiple_of(jnp.clip(r0 - POOL_HALO, 0, seq - nwin), POOL_HALO)
        d = pool_diff(u_ref[0, pl.ds(s0, nwin), :], u_ref[0, pl.ds(pl.multiple_of(r0, POOL_HALO), tm), :], r0, s0)
    y_pool = (_dot(d, wpool_ref[...]) * pscale_ref[...]).astype(BF16)

    g = gate_ref[...]
    merged = (g[:, 0:D_MODEL].astype(F32) * _dot(y_pool, wbr_ref[0])
              + g[:, D_MODEL:2 * D_MODEL].astype(F32) * _dot(yna_ref[...], wbr_ref[1])
              + g[:, 2 * D_MODEL:].astype(F32) * _dot(ymla_ref[...], wbr_ref[2]))
    y = _dot(merged.astype(BF16), wout_ref[...])
    xn = x_ref[...] + gt1_ref[...] * _rms(y, gpost_ref[...])
    xo_ref[...] = xn

    h2 = _rms(xn, gpre_ref[...]) * (1.0 + sc2_ref[...]) + sh2_ref[...]
    h_hi = h2.astype(BF16)
    h2_ref[...] = h_hi
    h_lo = (h2 - h_hi.astype(F32)).astype(BF16)
    two = _dot(h_hi, wr2_ref[...])
    logits = two[:, :LANES] + two[:, LANES:] + _dot(h_lo, wr1_ref[...])
    rg_ref[...] = _route(logits)


def _merge(x, u, yna, ymla, gate, mod5, l, w, *, seq, ctx, tm):
    n = x.shape[0]
    nseq = n // seq
    if tm >= seq:
        assert ctx and tm % seq == 0
        steps_per_seq = 0
        u_spec = pl.BlockSpec((tm // seq, seq, 512), lambda i: (i, 0, 0))
    else:
        steps_per_seq = seq // tm
        u_spec = pl.BlockSpec((1, seq, 512), lambda i: (i // steps_per_seq, 0, 0))

    def mrow(i):
        return 0 if ctx else 1 + i // steps_per_seq

    const = lambda i: (l, 0, 0)
    modspec = lambda k: pl.BlockSpec((None, None, None, 1, D_MODEL), lambda i: (l, mrow(i), k, 0, 0))
    row = lambda wd: pl.BlockSpec((tm, wd), lambda i: (i, 0))
    return pl.pallas_call(
        functools.partial(_merge_kernel, seq=seq, tm=tm, steps_per_seq=steps_per_seq),
        grid=(n // tm,),
        in_specs=[
            row(D_MODEL),
            u_spec,
            row(512), row(512), row(3 * D_MODEL),
            modspec(2), modspec(3), modspec(4),
            pl.BlockSpec((None, 1, D_MODEL), const),
            pl.BlockSpec((None, 1, D_MODEL), const),
            pl.BlockSpec((None, 512, 512), const),
            pl.BlockSpec((None, 1, 512), const),
            pl.BlockSpec((None, 3, 512, D_MODEL), lambda i: (l, 0, 0, 0)),
            pl.BlockSpec((None, D_MODEL, D_MODEL), const),
            pl.BlockSpec((None, D_MODEL, 2 * LANES), const),
            pl.BlockSpec((None, D_MODEL, LANES), const),
        ],
        out_specs=[row(D_MODEL), row(D_MODEL), row(LANES)],
        out_shape=[jax.ShapeDtypeStruct((n, D_MODEL), F32), jax.ShapeDtypeStruct((n, D_MODEL), BF16),
                   jax.ShapeDtypeStruct((n, LANES), F32)],
        compiler_params=_cparams(("arbitrary",)),
        name="merge_ctx" if ctx else "merge_lat",
    )(x, u.reshape(nseq, seq, 512), yna, ymla, gate, mod5, mod5, mod5, w["g_post_mix"], w["g_pre_ffn"],
      w["w_pool_bd"], w["pool_scale"], w["w_branch"], w["w_out"], w["w_r2"], w["w_r1"])


MOE_TM = 1024
MOE_SUB = 512
MOE_ROWS = 160
MOE_ROWS_PAD = 256


def _moe_kernel(h_ref, rg_ref, x_ref, gt2_ref, gpost_ref, w1_ref, w3_ref, w2_ref, o_ref):
    tm = h_ref.shape[0]
    subs = [slice(s * MOE_SUB, (s + 1) * MOE_SUB) for s in range(tm // MOE_SUB)]
    acc_ref = o_ref
    rg = rg_ref[...]
    lane = lax.broadcasted_iota(jnp.int32, rg.shape, 1)
    grp = jnp.where(lane == GROUP_LANE, rg, 0.0).sum(axis=-1, keepdims=True)
    member = ((lane.astype(F32) == grp) & (lane < N_GROUPS)).astype(F32)
    t_row = lax.broadcasted_iota(jnp.int32, (MOE_SUB, MOE_SUB), 0)
    t_col = lax.broadcasted_iota(jnp.int32, (MOE_SUB, MOE_SUB), 1)
    earlier, later = (t_col < t_row).astype(BF16), (t_row < t_col).astype(BF16)
    member_t = member.T[:8]
    rank_col = jnp.concatenate([_dot(earlier, member[sl].astype(BF16)) for sl in subs], axis=0)
    rank_row = jnp.concatenate([_dot(member_t[:, sl].astype(BF16), later) for sl in subs], axis=1)
    count = functools.reduce(jnp.maximum, [member[sl].sum(axis=0, keepdims=True) for sl in subs])
    rg_hi = rg.astype(BF16)
    rg_lo = (rg - rg_hi.astype(F32)).astype(BF16)
    row_id = lax.broadcasted_iota(jnp.int32, (MOE_ROWS, 1), 0).astype(F32)
    col_id = lax.broadcasted_iota(jnp.int32, (1, MOE_ROWS_PAD), 1).astype(F32)
    lane1 = lax.broadcasted_iota(jnp.int32, (1, LANES), 1)

    def experts(g, hs, gs):
        hid = []
        for e in range(EXPERTS_PER_GROUP):
            k = g * EXPERTS_PER_GROUP + e
            a = _dot(hs, w1_ref[k])
            b = _dot(hs, w3_ref[k])
            hid.append(((a * jax.nn.sigmoid(a)) * b * gs[:, k:k + 1]).astype(BF16))
        return _dot(jnp.concatenate(hid, axis=-1), w2_ref[g]).astype(BF16)

    n_slots = N_GROUPS * MOE_ROWS
    slots_pad = -(-n_slots // LANES) * LANES
    slot_row = lax.broadcasted_iota(jnp.int32, (n_slots, 1), 0).astype(F32)
    slot_col = lax.broadcasted_iota(jnp.int32, (1, slots_pad), 1).astype(F32)
    dest_row = sum(jnp.where((member_t[g:g + 1] > 0.0) & (rank_row[g:g + 1] < MOE_ROWS),
                             rank_row[g:g + 1] + (g * MOE_ROWS + 1), 0.0) for g in range(N_GROUPS)) - 1.0
    dest_col = sum(jnp.where((member[:, g:g + 1] > 0.0) & (rank_col[:, g:g + 1] < MOE_ROWS),
                             rank_col[:, g:g + 1] + (g * MOE_ROWS + 1), 0.0) for g in range(N_GROUPS)) - 1.0
    hs_sub, gs_sub = [], []
    for sl in subs:
        take = (dest_row[:, sl] == slot_row).astype(BF16)
        hs_sub.append(_dot(take, h_ref[sl, :]).astype(BF16))
        gs_sub.append(_dot(take, rg_hi[sl]) + _dot(take, rg_lo[sl]))
    parts = []
    for g in range(N_GROUPS):
        rows = slice(g * MOE_ROWS, (g + 1) * MOE_ROWS)
        parts.append(experts(g, jnp.concatenate([h[rows] for h in hs_sub], axis=0),
                             jnp.concatenate([q[rows] for q in gs_sub], axis=0)))
    pad = [jnp.zeros((slots_pad - n_slots, D_MODEL), BF16)] if slots_pad > n_slots else []
    for s, sl in enumerate(subs):
        give = (dest_col[sl] == slot_col).astype(BF16)
        rows = jnp.concatenate([p[s * MOE_ROWS:(s + 1) * MOE_ROWS] for p in parts] + pad, axis=0)
        acc_ref[sl, :] = _dot(give, rows)

    pad = jnp.zeros((MOE_ROWS_PAD - MOE_ROWS, D_MODEL), BF16)
    for g in range(N_GROUPS):
        rr, mr = rank_row[g:g + 1, :], member_t[g:g + 1, :]
        rc, mc = rank_col[:, g:g + 1], member[:, g:g + 1]
        n_g = jnp.max(jnp.where(lane1 == g, count, 0.0)).astype(jnp.int32)

        def chunk(c, carry, g=g, rr=rr, mr=mr, rc=rc, mc=mc):
            base = (c * MOE_ROWS).astype(F32)
            hs, gs = [], []
            for sl in subs:
                take = ((rr[:, sl] - base == row_id) & (mr[:, sl] > 0.0)).astype(BF16)
                hs.append(_dot(take, h_ref[sl, :]).astype(BF16))
                gs.append(_dot(take, rg_hi[sl]) + _dot(take, rg_lo[sl]))
            part = experts(g, jnp.concatenate(hs, axis=0), jnp.concatenate(gs, axis=0))
            for s, sl in enumerate(subs):
                rel = rc[sl] - base
                give = ((rel == col_id) & (rel < MOE_ROWS) & (mc[sl] > 0.0)).astype(BF16)
                rows = jnp.concatenate([part[s * MOE_ROWS:(s + 1) * MOE_ROWS], pad], axis=0)
                acc_ref[sl, :] += _dot(give, rows)
            return carry

        lax.fori_loop(1, (n_g + MOE_ROWS - 1) // MOE_ROWS, chunk, 0)

    o_ref[...] = x_ref[...] + gt2_ref[...] * _rms(acc_ref[...], gpost_ref[...])


def _moe(h2, rg, x, mod5, l, w, *, seq, ctx):
    n = x.shape[0]
    tm = MOE_TM
    steps_per_seq = seq // tm if not ctx else 0

    def mrow(i):
        return 0 if ctx else 1 + i // steps_per_seq

    row = lambda wd: pl.BlockSpec((tm, wd), lambda i: (i, 0))
    once = pl.Buffered(1)
    wspec = lambda a, b: pl.BlockSpec((None, N_EXPERTS, a, b), lambda i: (l, 0, 0, 0), pipeline_mode=once)
    return pl.pallas_call(
        _moe_kernel,
        grid=(n // tm,),
        in_specs=[
            row(D_MODEL), row(LANES), row(D_MODEL),
            pl.BlockSpec((None, None, None, 1, D_MODEL), lambda i: (l, mrow(i), 5, 0, 0)),
            pl.BlockSpec((None, 1, D_MODEL), lambda i: (l, 0, 0)),
            wspec(D_MODEL, EXPERT_FF), wspec(D_MODEL, EXPERT_FF),
            pl.BlockSpec((None, N_GROUPS, EXPERTS_PER_GROUP * EXPERT_FF, D_MODEL), lambda i: (l, 0, 0, 0),
                         pipeline_mode=once),
        ],
        out_specs=row(D_MODEL),
        out_shape=jax.ShapeDtypeStruct((n, D_MODEL), F32),
        compiler_params=_cparams(("arbitrary",)),
        name="moe_ctx" if ctx else "moe_lat",
    )(h2, rg, x, mod5, w["g_post_ffn"], w["w1"], w["w3"], w["w2"])


def _rope_table(seq):
    t = np.arange(seq)
    n_freq = MLA_ROPE_DIM // 4
    inv = jnp.asarray(ROPE_BASE, F32) ** (-jnp.arange(n_freq, dtype=F32) / n_freq)
    ang_r = jnp.asarray(t // GRID_W, F32)[:, None] * inv
    ang_c = jnp.asarray(t % GRID_W, F32)[:, None] * inv
    cos = jnp.cos(jnp.stack([ang_r, ang_c], axis=1))
    sin = jnp.sin(jnp.stack([ang_r, ang_c], axis=1))
    zeros = jnp.zeros_like(sin)
    c32 = jnp.stack([cos, cos], axis=2).reshape(seq, MLA_ROPE_DIM)
    up32 = jnp.stack([-sin, zeros], axis=2).reshape(seq, MLA_ROPE_DIM)
    dn32 = jnp.stack([zeros, sin], axis=2).reshape(seq, MLA_ROPE_DIM)

    def slot(v, fill):
        return jnp.concatenate([jnp.full((seq, KPE_LANE), fill, F32), v,
                                jnp.full((seq, SLOT - KPE_LANE - MLA_ROPE_DIM), fill, F32)], axis=1)

    return jnp.concatenate([slot(c32, 1.0), slot(up32, 0.0), slot(dn32, 0.0)], axis=1)


def _na_row_classes(n_rows):
    i = np.arange(NA_BLOCK_ROWS)[:, None]
    j = np.arange(NA_SLAB_ROWS)[None, :]
    last_ks = n_rows - NA_SLAB_ROWS
    last_q0 = n_rows - NA_BLOCK_ROWS
    specs = [
        (j < NA_WIN_ROWS, j - i),
        ((j >= i) & (j < i + NA_WIN_ROWS), j - NA_WIN_ROWS // 2 - i),
        (last_ks + j >= n_rows - NA_WIN_ROWS, last_ks + j - last_q0 - i),
    ]
    ok = np.stack([np.broadcast_to(o, (NA_BLOCK_ROWS, NA_SLAB_ROWS)) for o, _ in specs])
    dr = np.stack([np.clip(d + NA_WIN_ROWS - 1, 0, 2 * NA_WIN_ROWS - 2) for _, d in specs])
    return ok, dr


def _bias_kernel(rpb_ref, out_ref, *, row_ok, row_dr):
    reach = NA_WIN_COLS - 1
    rows = rpb_ref[...] * LOG2E
    v_lo = pltpu.roll(rows, SLOT - reach, axis=1)
    v_hi = pltpu.roll(rows, GRID_W - reach, axis=1)
    lane1 = lax.broadcasted_iota(jnp.int32, (1, SLOT), 1)
    mid = (lane1 >= GRID_W - reach) & (lane1 <= GRID_W + reach)
    c = lax.broadcasted_iota(jnp.int32, (GRID_W, SLOT), 0)
    lane = lax.broadcasted_iota(jnp.int32, (GRID_W, SLOT), 1)
    kc = lane & (GRID_W - 1)
    c0 = jnp.clip(c - NA_WIN_COLS // 2, 0, GRID_W - NA_WIN_COLS)
    in_col = (kc >= c0) & (kc < c0 + NA_WIN_COLS)
    first = lane < GRID_W
    for k in range(3):
        for i in range(NA_BLOCK_ROWS):
            for jp in range(NA_SLAB_ROWS // 2):
                ok_a, ok_b = bool(row_ok[k, i, 2 * jp]), bool(row_ok[k, i, 2 * jp + 1])
                a, b = int(row_dr[k, i, 2 * jp]), int(row_dr[k, i, 2 * jp + 1])
                rs = slice(i * GRID_W, (i + 1) * GRID_W)
                cs = slice(jp * SLOT, (jp + 1) * SLOT)
                if not (ok_a or ok_b):
                    out_ref[k, rs, cs] = jnp.full((GRID_W, SLOT), NEG_INF, out_ref.dtype)
                    continue
                v = jnp.where(mid, v_hi[b:b + 1], v_lo[a:a + 1])
                tile = pltpu.roll(jnp.broadcast_to(v, (GRID_W, SLOT)), 0, axis=1, stride=1, stride_axis=0)
                ok = in_col
                if not ok_b:
                    ok = ok & first
                if not ok_a:
                    ok = ok & jnp.logical_not(first)
                out_ref[k, rs, cs] = jnp.where(ok, tile, NEG_INF).astype(out_ref.dtype)


def _na_bias_table(rpb, n_rows):
    n_dr, n_dc = rpb.shape[-2:]
    row_ok, row_dr = _na_row_classes(n_rows)
    rpb_p = jnp.pad(rpb, ((0, 0), (0, 0), (0, 0), (0, SLOT - n_dc)))
    tq, tk = NA_BLOCK_ROWS * GRID_W, NA_SLAB_ROWS * GRID_W
    return pl.pallas_call(
        functools.partial(_bias_kernel, row_ok=row_ok, row_dr=row_dr),
        grid=(DEPTH, NA_HEADS),
        in_specs=[pl.BlockSpec((None, None, n_dr, SLOT), lambda l, h: (l, h, 0, 0))],
        out_specs=pl.BlockSpec((None, 3, None, tq, tk), lambda l, h: (l, 0, h, 0, 0)),
        out_shape=jax.ShapeDtypeStruct((DEPTH, 3, NA_HEADS, tq, tk), BF16),
        compiler_params=_cparams(("arbitrary", "arbitrary")),
        name="na_bias_table",
    )(rpb_p)


def _prep(p):
    L = DEPTH
    w_in = p["w_in"]
    tail = C_END - LANES
    w_in_tail = jnp.pad(w_in[..., tail:], ((0, 0), (0, 0), (0, C_END - w_in.shape[-1]))).astype(BF16)
    w_uq = jnp.pad(p["mla_w_uq"].reshape(L, MLA_Q_LORA, MLA_HEADS, MLA_QK_DIM),
                   ((0, 0), (0, 0), (0, 0), (0, SLOT - MLA_QK_DIM))).reshape(L, MLA_Q_LORA, MLA_HEADS * SLOT)
    ukv = p["mla_w_ukv"].reshape(L, MLA_KV_LORA, MLA_HEADS, MLA_NOPE_DIM + MLA_V_DIM)
    uk = jnp.pad(ukv[..., :MLA_NOPE_DIM], ((0, 0), (0, 0), (0, 0), (0, SLOT - MLA_NOPE_DIM)))
    w_ukv = jnp.concatenate([uk.reshape(L, MLA_KV_LORA, MLA_HEADS * SLOT),
                             ukv[..., MLA_NOPE_DIM:].reshape(L, MLA_KV_LORA, MLA_HEADS * MLA_V_DIM)], axis=-1)
    w_pool_bd = jnp.einsum("lgcd,gh->lgchd", p["w_pool"], jnp.eye(4, dtype=F32)).reshape(L, POOL_WIDTH, POOL_WIDTH)
    w_r = jnp.concatenate([p["w_group_router"], p["w_expert_router"]], axis=-1)
    r_hi = w_r.astype(BF16)
    r_lo = (w_r - r_hi.astype(F32)).astype(BF16)
    padl = lambda a: jnp.pad(a, ((0, 0), (0, 0), (0, LANES - a.shape[-1])))
    vec = lambda a: a.reshape(L, 1, a.shape[-1])
    return {
        "w_in": w_in.astype(BF16), "w_in_tail": w_in_tail,
        "w_uq": w_uq.astype(BF16),
        "w_ukv": w_ukv.astype(BF16),
        "w_pool_bd": w_pool_bd.astype(BF16),
        "w_branch": p["w_branch"].astype(BF16),
        "w_out": p["w_out"].astype(BF16),
        "w_r2": jnp.concatenate([padl(r_hi), padl(r_lo)], axis=-1),
        "w_r1": padl(r_hi),
        "w1": p["w_exp_gate"].astype(BF16),
        "w3": p["w_exp_up"].astype(BF16),
        "w2": p["w_exp_down"].astype(BF16).reshape(L, N_GROUPS, EXPERTS_PER_GROUP * EXPERT_FF, D_MODEL),
        "g_pre_mix": vec(p["g_pre_mix"]), "g_post_mix": vec(p["g_post_mix"]),
        "g_pre_ffn": vec(p["g_pre_ffn"]), "g_post_ffn": vec(p["g_post_ffn"]),
        "pool_scale": vec(p["pool_scale"]), "q_norm": vec(p["mla_q_norm"]), "kv_norm": vec(p["mla_kv_norm"]),
    }


def kernel(x_prompt, x_sample, cache_na_k, cache_na_v, cache_mla_ckv, cache_mla_kpe, c, c_ctx, w_mod, b_mod,
           g_pre_mix, g_post_mix, g_pre_ffn, g_post_ffn, w_in, w_pool, pool_scale, na_rpb, mla_q_norm, mla_w_uq,
           mla_kv_norm, mla_w_ukv, w_branch, w_out, w_group_router, w_expert_router, w_exp_gate, w_exp_up,
           w_exp_down):
    batch, seq, _ = x_prompt.shape
    dbatch, dseq, _ = x_sample.shape
    past = cache_na_k.shape[2]
    assert dbatch + 1 <= 8 and dseq % (NA_BLOCK_ROWS * GRID_W) == 0

    w = _prep(dict(w_in=w_in, w_pool=w_pool, pool_scale=pool_scale, mla_q_norm=mla_q_norm, mla_w_uq=mla_w_uq,
                   mla_kv_norm=mla_kv_norm, mla_w_ukv=mla_w_ukv, w_branch=w_branch, w_out=w_out,
                   w_group_router=w_group_router, w_expert_router=w_expert_router, w_exp_gate=w_exp_gate,
                   w_exp_up=w_exp_up, w_exp_down=w_exp_down, g_pre_mix=g_pre_mix, g_post_mix=g_post_mix,
                   g_pre_ffn=g_pre_ffn, g_post_ffn=g_post_ffn))
    rope_tab = _rope_table(dseq)
    bias_tab = _na_bias_table(na_rpb, dseq // GRID_W)
    place = jnp.tile(jnp.pad(jnp.eye(MLA_ROPE_DIM, dtype=F32), ((0, 0), (KPE_LANE, SLOT - KPE_LANE - MLA_ROPE_DIM))),
                     (1, MLA_HEADS)).astype(BF16)

    cvec = jnp.concatenate([c_ctx[None], c, jnp.zeros((8 - 1 - dbatch, D_MODEL), F32)], axis=0)
    mod5 = _modulation(cvec, w_mod, b_mod).reshape(DEPTH, 8, 6, 1, D_MODEL)

    ck = cache_na_k.reshape(dbatch, DEPTH, past, NA_WIDTH)
    cv = cache_na_v.reshape(dbatch, DEPTH, past, NA_WIDTH)

    xp = x_prompt.reshape(batch * seq, D_MODEL)
    assert DEPTH == 2
    caches = None
    for l in range(DEPTH):
        u, nq, nvo, qm, km, vmo, gate, *caches = _inproj(xp, mod5, l, w, seq=seq, rope_tab=None, ctx_out=True,
                                                         tm=seq, prev=caches)
        yna, ymla = _attn_ctx(nq, caches[0], nvo, qm, km, vmo, seq=seq, nk_layer=l if l > 0 else None)
        xp, h2, rg = _merge(xp, u, yna, ymla, gate, mod5, l, w, seq=seq, ctx=True, tm=2 * seq)
        xp = _moe(h2, rg, xp, mod5, l, w, seq=seq, ctx=True)
    new_nk, new_nv, new_ckv, new_kpe = caches

    xs = x_sample.reshape(dbatch * dseq, D_MODEL)
    for l in range(DEPTH):
        u, nq, nvo, qm, km, vmo, gate, nk = _inproj(xs, mod5, l, w, seq=dseq, rope_tab=rope_tab, ctx_out=False, tm=512)
        ckm, cvmo, cnvo = _cache_prep(cache_mla_ckv, cache_mla_kpe, cv, w["w_ukv"], place, l)
        yna = _attn_na(nq, nk, nvo, ck, cnvo, bias_tab, l, batch=dbatch, seq=dseq)
        ymla = _attn_mla(qm, km, vmo, ckm, cvmo, batch=dbatch, seq=dseq, tq=512)
        xs, h2, rg = _merge(xs, u, yna, ymla, gate, mod5, l, w, seq=dseq, ctx=False, tm=512)
        xs = _moe(h2, rg, xs, mod5, l, w, seq=dseq, ctx=False)

    heads = (batch, DEPTH, seq, NA_HEADS, NA_HEAD_DIM)
    return (xp.reshape(batch, seq, D_MODEL), xs.reshape(dbatch, dseq, D_MODEL),
            new_nk.reshape(heads), new_nv.reshape(heads), new_ckv, new_kpe)
```

```python
import functools

import numpy as np
import jax
import jax.numpy as jnp
from jax import lax
from jax.experimental import pallas as pl
from jax.experimental.pallas import tpu as pltpu

F32 = jnp.float32
BF16 = jnp.bfloat16

D_MODEL = 1024
DEPTH = 2
GRID_W = 64
POOL_WINDOWS = (2, 4, 8, 16)
POOL_GC = 128
POOL_WIDTH = 512
NA_HEADS = 8
NA_HEAD_DIM = 64
NA_WIDTH = 512
NA_WIN_ROWS = 8
NA_WIN_COLS = 16
MLA_HEADS = 8
MLA_NOPE_DIM = 64
MLA_ROPE_DIM = 32
MLA_V_DIM = 64
MLA_QK_DIM = 96
MLA_Q_LORA = 384
MLA_KV_LORA = 256
N_GROUPS = 4
EXPERTS_PER_GROUP = 4
N_EXPERTS = 16
EXPERT_FF = 256
ROPE_BASE = 10000.0
EPS = 1e-6
NEG_INF = -1e30

LANES = 128
SLOT = LANES
VMEM_LIMIT = 56 * 1024 * 1024

C_U, C_NQ, C_NK, C_NV, C_CQ, C_CKV, C_KPE, C_G, C_END = 0, 512, 1024, 1536, 2048, 2432, 2688, 2720, 5888
G_WIDTH = 3 * D_MODEL
KPE_LANE = MLA_NOPE_DIM

LOG2E = 1.4426950408889634
VO_W = 2 * SLOT
VO_ALL = (NA_HEADS // 2) * VO_W

NA_BLOCK_ROWS = 4
NA_SLAB_ROWS = 12


def _cparams(sem):
    return pltpu.CompilerParams(dimension_semantics=sem, vmem_limit_bytes=VMEM_LIMIT)


def _rms(x, g):
    return x * lax.rsqrt(jnp.mean(x * x, axis=-1, keepdims=True) + EPS) * g


def _dot(a, b):
    return jnp.dot(a, b, preferred_element_type=F32)


def _dot_nt(a, b):
    return lax.dot_general(a, b, (((1,), (1,)), ((), ())), preferred_element_type=F32)


def _mod_kernel(c_ref, w_ref, b_ref, o_ref):
    c = c_ref[...]
    s = (c * jax.nn.sigmoid(c)).astype(BF16)
    o_ref[...] = _dot(s, w_ref[...].astype(BF16)) + b_ref[...]


def _modulation(cvec, w_mod, b_mod):
    tn = 1536
    n = w_mod.shape[-1]
    return pl.pallas_call(
        _mod_kernel,
        grid=(DEPTH, n // tn),
        in_specs=[
            pl.BlockSpec((8, D_MODEL), lambda l, j: (0, 0)),
            pl.BlockSpec((None, D_MODEL, tn), lambda l, j: (l, 0, j)),
            pl.BlockSpec((None, 1, tn), lambda l, j: (l, 0, j)),
        ],
        out_specs=pl.BlockSpec((None, 8, tn), lambda l, j: (l, 0, j)),
        out_shape=jax.ShapeDtypeStruct((DEPTH, 8, n), F32),
        compiler_params=_cparams(("arbitrary", "arbitrary")),
        name="modulation",
    )(cvec, w_mod, b_mod.reshape(DEPTH, 1, n))


def _rope_slot(x, tab):
    c, s_up, s_dn = tab[:, 0:SLOT], tab[:, SLOT:2 * SLOT], tab[:, 2 * SLOT:3 * SLOT]
    half = MLA_ROPE_DIM // 4
    x_up = pltpu.roll(x, SLOT - half, axis=1)
    x_dn = pltpu.roll(x, half, axis=1)
    return x * c + x_up * s_up + x_dn * s_dn


def _store_vo(vo_ref, v):
    tm = v.shape[0]
    for p in range(NA_HEADS // 2):
        vo_ref[:, p * VO_W:p * VO_W + SLOT] = v[:, p * SLOT:(p + 1) * SLOT].astype(BF16)
        vo_ref[:, p * VO_W + SLOT:(p + 1) * VO_W] = jnp.ones((tm, SLOT), BF16)


def _realign_gate_columns(win_ref, wtail_ref, wg_ref, wqk_ref):
    off = C_G % LANES
    lane = lax.broadcasted_iota(jnp.int32, (1, LANES), 1)
    base = C_G - off
    nxt = win_ref[:, base:base + LANES].astype(F32)
    kpe = pltpu.roll(nxt, KPE_LANE - (C_KPE - base), axis=1)
    wqk_ref[:, :MLA_Q_LORA] = win_ref[:, C_CQ:C_CKV]
    wqk_ref[:, MLA_Q_LORA:] = jnp.where((lane >= KPE_LANE) & (lane < KPE_LANE + MLA_ROPE_DIM), kpe, 0.0).astype(BF16)
    for j in range(G_WIDTH // LANES):
        cur = nxt
        last = j == G_WIDTH // LANES - 1
        nxt = (wtail_ref[...] if last else win_ref[:, base + (j + 1) * LANES:base + (j + 2) * LANES]).astype(F32)
        tile = pltpu.roll(jnp.where(lane >= off, cur, nxt), LANES - off, axis=1)
        wg_ref[:, j * LANES:(j + 1) * LANES] = tile.astype(BF16)


def _inproj_kernel(*refs, rope, ctx_out, stacked):
    n_in = 10 + (1 if rope else 0) + (4 if stacked else 0)
    ins, outs, (wg_ref, wqk_ref) = refs[:n_in], refs[n_in:-2], refs[-2:]
    x_ref, sh_ref, sc_ref, g_ref, win_ref, qn_ref, wuq_ref, kvn_ref, wukv_ref, wtail_ref = ins[:10]
    tab_ref = ins[10] if rope else None
    prev = ins[n_in - 4:] if stacked else None
    if ctx_out:
        u_ref, nq_ref, nvo_ref, qm_ref, km_ref, vmo_ref, gate_ref, nk_ref, nv_ref, ckv_ref, kpe_ref = outs
    else:
        u_ref, nq_ref, nvo_ref, qm_ref, km_ref, vmo_ref, gate_ref, nk_ref = outs

    def put(ref, k, val):
        if stacked:
            ref[0] = prev[k][...]
            ref[1] = val
        else:
            ref[...] = val

    @pl.when(pl.program_id(0) == 0)
    def _():
        _realign_gate_columns(win_ref, wtail_ref, wg_ref, wqk_ref)

    x = x_ref[...]
    h = (_rms(x, g_ref[...]) * (1.0 + sc_ref[...]) + sh_ref[...]).astype(BF16)

    def seg(a, b):
        return _dot(h, win_ref[:, a:b])

    u_ref[...] = seg(C_U, C_NQ).astype(u_ref.dtype)
    nq_ref[...] = (seg(C_NQ, C_NK) * (NA_HEAD_DIM ** -0.5 * LOG2E)).astype(BF16)
    nk = seg(C_NK, C_NV)
    nv = seg(C_NV, C_CQ)
    if ctx_out:
        put(nk_ref, 0, nk)
        put(nv_ref, 1, nv)
    else:
        nk_ref[...] = nk.astype(BF16)
    _store_vo(nvo_ref, nv)
    gate_ref[...] = jax.nn.sigmoid(_dot(h, wg_ref[...])).astype(BF16)

    cqk = _dot(h, wqk_ref[...])
    cq = _rms(cqk[:, :MLA_Q_LORA], qn_ref[...]).astype(BF16)
    q = _dot(cq, wuq_ref[...])
    tab = tab_ref[...] if rope else None
    for hd in range(MLA_HEADS):
        qs = q[:, hd * SLOT:(hd + 1) * SLOT]
        if rope:
            qs = _rope_slot(qs, tab)
        qm_ref[:, hd * SLOT:(hd + 1) * SLOT] = (qs * (MLA_QK_DIM ** -0.5 * LOG2E)).astype(BF16)

    ckv = _rms(seg(C_CKV, C_KPE), kvn_ref[...])
    kpe_slot = cqk[:, MLA_Q_LORA:]
    if ctx_out:
        put(ckv_ref, 2, ckv)
        put(kpe_ref, 3, kpe_slot[:, KPE_LANE:KPE_LANE + MLA_ROPE_DIM])
    if rope:
        kpe_slot = _rope_slot(kpe_slot, tab)
    kv = _dot(ckv.astype(BF16), wukv_ref[...])
    for hd in range(MLA_HEADS):
        km_ref[:, hd * SLOT:(hd + 1) * SLOT] = (kv[:, hd * SLOT:(hd + 1) * SLOT] + kpe_slot).astype(BF16)
    _store_vo(vmo_ref, kv[:, MLA_HEADS * SLOT:])


def _inproj(x, mod5, l, w, *, seq, rope_tab, ctx_out, tm, prev=None):
    n = x.shape[0]
    steps_per_seq = seq // tm if not ctx_out else 0
    stacked = prev is not None
    assert not stacked or (ctx_out and tm == seq and l == 1)

    def mrow(i):
        return 0 if ctx_out else 1 + i // steps_per_seq

    const = lambda i: (l, 0, 0)
    in_specs = [
        pl.BlockSpec((tm, D_MODEL), lambda i: (i, 0)),
        pl.BlockSpec((None, None, None, 1, D_MODEL), lambda i: (l, mrow(i), 0, 0, 0)),
        pl.BlockSpec((None, None, None, 1, D_MODEL), lambda i: (l, mrow(i), 1, 0, 0)),
        pl.BlockSpec((None, 1, D_MODEL), const),
        pl.BlockSpec((None, D_MODEL, w["w_in"].shape[-1]), const, pipeline_mode=pl.Buffered(1)),
        pl.BlockSpec((None, 1, MLA_Q_LORA), const),
        pl.BlockSpec((None, MLA_Q_LORA, MLA_HEADS * SLOT), const, pipeline_mode=pl.Buffered(1)),
        pl.BlockSpec((None, 1, MLA_KV_LORA), const),
        pl.BlockSpec((None, MLA_KV_LORA, MLA_HEADS * SLOT + 512), const, pipeline_mode=pl.Buffered(1)),
        pl.BlockSpec((None, D_MODEL, LANES), const),
    ]
    args = [x, mod5, mod5, w["g_pre_mix"], w["w_in"], w["q_norm"], w["w_uq"], w["kv_norm"], w["w_ukv"], w["w_in_tail"]]
    rope = rope_tab is not None
    if rope:
        in_specs.append(pl.BlockSpec((tm, 3 * SLOT), lambda i: (i % steps_per_seq, 0)))
        args.append(rope_tab)
    row = lambda wd: pl.BlockSpec((tm, wd), lambda i: (i, 0))
    rows_of = lambda wd, dt: jax.ShapeDtypeStruct((n, wd), dt)
    widths = [512, 512, VO_ALL, MLA_HEADS * SLOT, MLA_HEADS * SLOT, VO_ALL, 3 * D_MODEL]
    out_specs = [row(wd) for wd in widths]
    out_shape = [rows_of(wd, BF16) for wd in widths]
    cache_w = [512, 512, MLA_KV_LORA, MLA_ROPE_DIM] if ctx_out else []
    if not ctx_out:
        out_specs.append(row(512))
        out_shape.append(rows_of(512, BF16))
    elif stacked:
        in_specs += [row(wd) for wd in cache_w]
        args += list(prev)
        out_specs += [pl.BlockSpec((None, DEPTH, seq, wd), lambda i: (i, 0, 0, 0)) for wd in cache_w]
        out_shape += [jax.ShapeDtypeStruct((n // seq, DEPTH, seq, wd), F32) for wd in cache_w]
    else:
        out_specs += [row(wd) for wd in cache_w]
        out_shape += [rows_of(wd, F32) for wd in cache_w]
    return pl.pallas_call(
        functools.partial(_inproj_kernel, rope=rope, ctx_out=ctx_out, stacked=stacked),
        grid=(n // tm,),
        in_specs=in_specs,
        out_specs=out_specs,
        out_shape=out_shape,
        scratch_shapes=[pltpu.VMEM((D_MODEL, G_WIDTH), BF16), pltpu.VMEM((D_MODEL, MLA_Q_LORA + SLOT), BF16)],
        compiler_params=_cparams(("arbitrary",)),
        name="inproj_ctx" if ctx_out else "inproj_lat",
    )(*args)


def _attend(q_h, segs):
    scores = []
    for k, _, bias in segs:
        s = _dot_nt(q_h, k)
        if bias is not None:
            s = s + bias
        scores.append(s)
    m = scores[0].max(axis=-1, keepdims=True)
    for s in scores[1:]:
        m = jnp.maximum(m, s.max(axis=-1, keepdims=True))
    acc = None
    for s, (_, vo, _) in zip(scores, segs):
        o = _dot(jnp.exp2(s - m).astype(BF16), vo)
        acc = o if acc is None else acc + o
    return acc[:, :SLOT] / acc[:, SLOT:]


def _attn_heads(out_ref, q_of, segs_of, rows=slice(None)):
    lo = lax.broadcasted_iota(jnp.int32, (1, SLOT), 1) < NA_HEAD_DIM
    for p in range(NA_HEADS // 2):
        o0, o1 = [_attend(q_of(2 * p + e), segs_of(2 * p + e)) for e in range(2)]
        out_ref[rows, p * SLOT:(p + 1) * SLOT] = jnp.where(lo, o0, o1).astype(out_ref.dtype)


def _pair_q(q_ref, rows=slice(None)):
    lo = lax.broadcasted_iota(jnp.int32, (1, SLOT), 1) < NA_HEAD_DIM

    def q_of(h):
        q_pair = q_ref[rows, (h // 2) * SLOT:(h // 2 + 1) * SLOT]
        keep = lo if h % 2 == 0 else jnp.logical_not(lo)
        return jnp.where(keep, q_pair, jnp.zeros_like(q_pair))
    return q_of


def _slot(ref, i, rows=slice(None)):
    return ref[rows, i * SLOT:(i + 1) * SLOT].astype(BF16)


def _vo(ref, h, rows=slice(None)):
    return ref[rows, (h // 2) * VO_W:(h // 2 + 1) * VO_W]


CTX_SEQS_PER_STEP = 4


def _attn_ctx_kernel(nq_ref, nk_ref, nvo_ref, qm_ref, km_ref, vmo_ref, yna_ref, ymla_ref, *, seq, nk_stacked):
    for bi in range(CTX_SEQS_PER_STEP):
        rows = slice(bi * seq, (bi + 1) * seq)
        if nk_stacked:
            nk_of = lambda p, bi=bi: nk_ref[bi, :, p * SLOT:(p + 1) * SLOT].astype(BF16)
        else:
            nk_of = lambda p, rows=rows: _slot(nk_ref, p, rows)
        _attn_heads(yna_ref, _pair_q(nq_ref, rows),
                    lambda h, rows=rows, nk_of=nk_of: [(nk_of(h // 2), _vo(nvo_ref, h, rows), None)], rows)
        _attn_heads(ymla_ref, lambda h, rows=rows: _slot(qm_ref, h, rows),
                    lambda h, rows=rows: [(_slot(km_ref, h, rows), _vo(vmo_ref, h, rows), None)], rows)


def _attn_ctx(nq, nk, nvo, qm, km, vmo, *, seq, nk_layer=None):
    n = nq.shape[0]
    nb = CTX_SEQS_PER_STEP
    spec = lambda wd: pl.BlockSpec((nb * seq, wd), lambda b: (b, 0))
    nk_spec = spec(512) if nk_layer is None else pl.BlockSpec((nb, None, seq, 512), lambda b: (b, nk_layer, 0, 0))
    return pl.pallas_call(
        functools.partial(_attn_ctx_kernel, seq=seq, nk_stacked=nk_layer is not None),
        grid=(n // (nb * seq),),
        in_specs=[spec(512), nk_spec, spec(VO_ALL), spec(1024), spec(1024), spec(VO_ALL)],
        out_specs=[spec(512), spec(512)],
        out_shape=[jax.ShapeDtypeStruct((n, 512), BF16)] * 2,
        compiler_params=_cparams(("arbitrary",)),
        name="attn_ctx",
    )(nq, nk, nvo, qm, km, vmo)


NA_BLOCKS_PER_STEP = 2


def _attn_na_kernel(nq_ref, nk_ref, nvo_ref, ck_ref, cvo_ref, *rest, n_blocks):
    bias_refs, out_ref = rest[:NA_BLOCKS_PER_STEP], rest[NA_BLOCKS_PER_STEP]
    tq = NA_BLOCK_ROWS * GRID_W
    for bi in range(NA_BLOCKS_PER_STEP):
        i = pl.program_id(1) * NA_BLOCKS_PER_STEP + bi
        ks = jnp.clip(NA_BLOCK_ROWS * i - NA_WIN_ROWS // 2, 0, n_blocks * NA_BLOCK_ROWS - NA_SLAB_ROWS)
        slab = pl.ds(pl.multiple_of(ks * GRID_W, GRID_W), NA_SLAB_ROWS * GRID_W)
        rows = slice(bi * tq, (bi + 1) * tq)

        def segs_of(h, slab=slab, bias_ref=bias_refs[bi]):
            return [(_slot(nk_ref, h // 2, slab), _vo(nvo_ref, h, slab), bias_ref[h].astype(F32)),
                    (_slot(ck_ref, h // 2), _vo(cvo_ref, h), None)]

        _attn_heads(out_ref, _pair_q(nq_ref, rows), segs_of, rows)


def _attn_na(nq, nk, nvo, cache_k, cvo, bias_tab, l, *, batch, seq):
    tq = NA_BLOCK_ROWS * GRID_W
    nb = seq // tq
    per = NA_BLOCKS_PER_STEP
    steps = nb // per
    past = cache_k.shape[2]

    def cls(i):
        return jnp.where(i == 0, 0, jnp.where(i == nb - 1, 2, 1))

    bias_specs = [pl.BlockSpec((None, None, NA_HEADS, tq, NA_SLAB_ROWS * GRID_W),
                               lambda b, j, bi=bi: (l, cls(j * per + bi), 0, 0, 0)) for bi in range(per)]
    return pl.pallas_call(
        functools.partial(_attn_na_kernel, n_blocks=nb),
        grid=(batch, steps),
        in_specs=[
            pl.BlockSpec((per * tq, 512), lambda b, j: (b * steps + j, 0)),
            pl.BlockSpec((seq, 512), lambda b, j: (b, 0)),
            pl.BlockSpec((seq, VO_ALL), lambda b, j: (b, 0)),
            pl.BlockSpec((None, None, past, 512), lambda b, j: (b, l, 0, 0)),
            pl.BlockSpec((None, past, VO_ALL), lambda b, j: (b, 0, 0)),
        ] + bias_specs,
        out_specs=pl.BlockSpec((per * tq, 512), lambda b, j: (b * steps + j, 0)),
        out_shape=jax.ShapeDtypeStruct((batch * seq, 512), BF16),
        compiler_params=_cparams(("arbitrary", "arbitrary")),
        name="attn_na_lat",
    )(nq, nk, nvo, cache_k, cvo, *([bias_tab] * per))


def _cache_prep_kernel(ckv_ref, kpe_ref, nv_ref, wukv_ref, place_ref, km_ref, vmo_ref, nvo_ref):
    kv = _dot(ckv_ref[...].astype(BF16), wukv_ref[...])
    kslots = kv[:, :MLA_HEADS * SLOT] + _dot(kpe_ref[...].astype(BF16), place_ref[...])
    km_ref[...] = kslots.astype(BF16)
    _store_vo(vmo_ref, kv[:, MLA_HEADS * SLOT:])
    _store_vo(nvo_ref, nv_ref[...])


def _cache_prep(cache_ckv, cache_kpe, cache_nv, w_ukv, place, l):
    batch, _, past, _ = cache_ckv.shape
    tout = pl.BlockSpec((None, past, VO_ALL), lambda b: (b, 0, 0))
    tshape = jax.ShapeDtypeStruct((batch, past, VO_ALL), BF16)
    return pl.pallas_call(
        _cache_prep_kernel,
        grid=(batch,),
        in_specs=[
            pl.BlockSpec((None, None, past, MLA_KV_LORA), lambda b: (b, l, 0, 0)),
            pl.BlockSpec((None, None, past, MLA_ROPE_DIM), lambda b: (b, l, 0, 0)),
            pl.BlockSpec((None, None, past, 512), lambda b: (b, l, 0, 0)),
            pl.BlockSpec((None, MLA_KV_LORA, MLA_HEADS * SLOT + 512), lambda b: (l, 0, 0)),
            pl.BlockSpec((MLA_ROPE_DIM, MLA_HEADS * SLOT), lambda b: (0, 0)),
        ],
        out_specs=[pl.BlockSpec((None, past, MLA_HEADS * SLOT), lambda b: (b, 0, 0)), tout, tout],
        out_shape=[jax.ShapeDtypeStruct((batch, past, MLA_HEADS * SLOT), BF16), tshape, tshape],
        compiler_params=_cparams(("arbitrary",)),
        name="cache_prep",
    )(cache_ckv, cache_kpe, cache_nv, w_ukv, place)


def _attn_mla_kernel(qm_ref, km_ref, vmo_ref, ckm_ref, cvmo_ref, out_ref):
    _attn_heads(out_ref, lambda h: _slot(qm_ref, h),
                lambda h: [(_slot(km_ref, h), _vo(vmo_ref, h), None), (_slot(ckm_ref, h), _vo(cvmo_ref, h), None)])


def _attn_mla(qm, km, vmo, ckm, cvmo, *, batch, seq, tq):
    nb = seq // tq
    past = ckm.shape[1]
    return pl.pallas_call(
        _attn_mla_kernel,
        grid=(batch, nb),
        in_specs=[
            pl.BlockSpec((tq, MLA_HEADS * SLOT), lambda b, i: (b * nb + i, 0)),
            pl.BlockSpec((seq, MLA_HEADS * SLOT), lambda b, i: (b, 0)),
            pl.BlockSpec((seq, VO_ALL), lambda b, i: (b, 0)),
            pl.BlockSpec((None, past, MLA_HEADS * SLOT), lambda b, i: (b, 0, 0)),
            pl.BlockSpec((None, past, VO_ALL), lambda b, i: (b, 0, 0)),
        ],
        out_specs=pl.BlockSpec((tq, 512), lambda b, i: (b * nb + i, 0)),
        out_shape=jax.ShapeDtypeStruct((batch * seq, 512), BF16),
        compiler_params=_cparams(("arbitrary", "arbitrary")),
        name="attn_mla_lat",
    )(qm, km, vmo, ckm, cvmo)


POOL_HALO = 16


GROUP_LANE = N_EXPERTS


ROUTE_ROWS = 32


def _route(logits):
    lt = logits.T[:ROUTE_ROWS]
    idx = lax.broadcasted_iota(jnp.int32, lt.shape, 0).astype(F32)
    big = jnp.float32(1 << 20)
    is_g = idx < N_GROUPS
    gl = jnp.where(is_g, lt, NEG_INF)
    gmax = gl.max(axis=0, keepdims=True)
    ge = jnp.where(is_g, jnp.exp(gl - gmax), 0.0)
    gp = ge / ge.sum(axis=0, keepdims=True)
    g_w = jnp.where(is_g, gp, -1.0).max(axis=0, keepdims=True)
    g_idx = jnp.where(is_g & (gp == g_w), idx, big).min(axis=0, keepdims=True)
    e_idx = idx - N_GROUPS
    in_grp = (e_idx >= g_idx * EXPERTS_PER_GROUP) & (e_idx < (g_idx + 1) * EXPERTS_PER_GROUP)
    el = jnp.where(in_grp, lt, NEG_INF)
    emax = el.max(axis=0, keepdims=True)
    ee = jnp.where(in_grp, jnp.exp(el - emax), 0.0)
    ep = ee / ee.sum(axis=0, keepdims=True)
    p1 = jnp.where(in_grp, ep, -1.0).max(axis=0, keepdims=True)
    i1 = jnp.where(in_grp & (ep == p1), idx, big).min(axis=0, keepdims=True)
    rest = in_grp & (idx != i1)
    p2 = jnp.where(rest, ep, -1.0).max(axis=0, keepdims=True)
    i2 = jnp.where(rest & (ep == p2), idx, big).min(axis=0, keepdims=True)
    tot = p1 + p2
    out_t = (jnp.where(idx == i1 - N_GROUPS, g_w * (p1 / tot), 0.0) + jnp.where(idx == i2 - N_GROUPS, g_w * (p2 / tot), 0.0)
             + jnp.where(idx == GROUP_LANE, g_idx, 0.0))
    return jnp.concatenate([out_t, jnp.zeros((LANES - ROUTE_ROWS, lt.shape[1]), F32)], axis=0).T


def _merge_kernel(x_ref, u_ref, yna_ref, ymla_ref, gate_ref, gt1_ref, sh2_ref, sc2_ref, gpost_ref, gpre_ref,
                  wpool_ref, pscale_ref, wbr_ref, wout_ref, wr2_ref, wr1_ref,
                  xo_ref, h2_ref, rg_ref, *, seq, tm, steps_per_seq):
    def pool_diff(uwin, utile, r0, s0):
        rows, nwin = utile.shape[0], uwin.shape[0]
        t = r0 + lax.broadcasted_iota(jnp.int32, (rows, nwin), 0)
        s = s0 + lax.broadcasted_iota(jnp.int32, (rows, nwin), 1)
        t1 = r0 + lax.broadcasted_iota(jnp.int32, (rows, 1), 0)
        ds = []
        for g, w in enumerate(POOL_WINDOWS):
            lo = jnp.maximum(t - w // 2, 0)
            hi = jnp.minimum(t + (w - w // 2), seq)
            band = ((s >= lo) & (s < hi)).astype(BF16)
            cnt = (jnp.minimum(t1 + (w - w // 2), seq) - jnp.maximum(t1 - w // 2, 0)).astype(F32)
            sl = slice(g * POOL_GC, (g + 1) * POOL_GC)
            ds.append(_dot(band, uwin[:, sl]) / cnt - utile[:, sl].astype(F32))
        return jnp.concatenate(ds, axis=-1).astype(BF16)

    if steps_per_seq == 0:
        d = jnp.concatenate([pool_diff(u_ref[si], u_ref[si], 0, 0) for si in range(tm // seq)], axis=0)
    else:
        nwin = min(tm + 2 * POOL_HALO, seq)
        r0 = (pl.program_id(0) % steps_per_seq) * tm
        s0 = pl.multiple_of(jnp.clip(r0 - POOL_HALO, 0, seq - nwin), POOL_HALO)
        d = pool_diff(u_ref[0, pl.ds(s0, nwin), :], u_ref[0, pl.ds(pl.multiple_of(r0, POOL_HALO), tm), :], r0, s0)
    y_pool = (_dot(d, wpool_ref[...]) * pscale_ref[...]).astype(BF16)

    g = gate_ref[...]
    merged = (g[:, 0:D_MODEL].astype(F32) * _dot(y_pool, wbr_ref[0])
              + g[:, D_MODEL:2 * D_MODEL].astype(F32) * _dot(yna_ref[...], wbr_ref[1])
              + g[:, 2 * D_MODEL:].astype(F32) * _dot(ymla_ref[...], wbr_ref[2]))
    y = _dot(merged.astype(BF16), wout_ref[...])
    xn = x_ref[...] + gt1_ref[...] * _rms(y, gpost_ref[...])
    xo_ref[...] = xn

    h2 = _rms(xn, gpre_ref[...]) * (1.0 + sc2_ref[...]) + sh2_ref[...]
    h_hi = h2.astype(BF16)
    h2_ref[...] = h_hi
    h_lo = (h2 - h_hi.astype(F32)).astype(BF16)
    two = _dot(h_hi, wr2_ref[...])
    logits = two[:, :LANES] + two[:, LANES:] + _dot(h_lo, wr1_ref[...])
    rg_ref[...] = _route(logits)


def _merge(x, u, yna, ymla, gate, mod5, l, w, *, seq, ctx, tm):
    n = x.shape[0]
    nseq = n // seq
    if tm >= seq:
        assert ctx and tm % seq == 0
        steps_per_seq = 0
        u_spec = pl.BlockSpec((tm // seq, seq, 512), lambda i: (i, 0, 0))
    else:
        steps_per_seq = seq // tm
        u_spec = pl.BlockSpec((1, seq, 512), lambda i: (i // steps_per_seq, 0, 0))

    def mrow(i):
        return 0 if ctx else 1 + i // steps_per_seq

    const = lambda i: (l, 0, 0)
    modspec = lambda k: pl.BlockSpec((None, None, None, 1, D_MODEL), lambda i: (l, mrow(i), k, 0, 0))
    row = lambda wd: pl.BlockSpec((tm, wd), lambda i: (i, 0))
    return pl.pallas_call(
        functools.partial(_merge_kernel, seq=seq, tm=tm, steps_per_seq=steps_per_seq),
        grid=(n // tm,),
        in_specs=[
            row(D_MODEL),
            u_spec,
            row(512), row(512), row(3 * D_MODEL),
            modspec(2), modspec(3), modspec(4),
            pl.BlockSpec((None, 1, D_MODEL), const),
            pl.BlockSpec((None, 1, D_MODEL), const),
            pl.BlockSpec((None, 512, 512), const),
            pl.BlockSpec((None, 1, 512), const),
            pl.BlockSpec((None, 3, 512, D_MODEL), lambda i: (l, 0, 0, 0)),
            pl.BlockSpec((None, D_MODEL, D_MODEL), const),
            pl.BlockSpec((None, D_MODEL, 2 * LANES), const),
            pl.BlockSpec((None, D_MODEL, LANES), const),
        ],
        out_specs=[row(D_MODEL), row(D_MODEL), row(LANES)],
        out_shape=[jax.ShapeDtypeStruct((n, D_MODEL), F32), jax.ShapeDtypeStruct((n, D_MODEL), BF16),
                   jax.ShapeDtypeStruct((n, LANES), F32)],
        compiler_params=_cparams(("arbitrary",)),
        name="merge_ctx" if ctx else "merge_lat",
    )(x, u.reshape(nseq, seq, 512), yna, ymla, gate, mod5, mod5, mod5, w["g_post_mix"], w["g_pre_ffn"],
      w["w_pool_bd"], w["pool_scale"], w["w_branch"], w["w_out"], w["w_r2"], w["w_r1"])


MOE_TM = 1024
MOE_SUB = 512
MOE_ROWS = 160
MOE_ROWS_PAD = 256


def _moe_kernel(h_ref, rg_ref, x_ref, gt2_ref, gpost_ref, w1_ref, w3_ref, w2_ref, o_ref):
    tm = h_ref.shape[0]
    subs = [slice(s * MOE_SUB, (s + 1) * MOE_SUB) for s in range(tm // MOE_SUB)]
    acc_ref = o_ref
    rg = rg_ref[...]
    lane = lax.broadcasted_iota(jnp.int32, rg.shape, 1)
    grp = jnp.where(lane == GROUP_LANE, rg, 0.0).sum(axis=-1, keepdims=True)
    member = ((lane.astype(F32) == grp) & (lane < N_GROUPS)).astype(F32)
    t_row = lax.broadcasted_iota(jnp.int32, (MOE_SUB, MOE_SUB), 0)
    t_col = lax.broadcasted_iota(jnp.int32, (MOE_SUB, MOE_SUB), 1)
    earlier, later = (t_col < t_row).astype(BF16), (t_row < t_col).astype(BF16)
    member_t = member.T[:8]
    rank_col = jnp.concatenate([_dot(earlier, member[sl].astype(BF16)) for sl in subs], axis=0)
    rank_row = jnp.concatenate([_dot(member_t[:, sl].astype(BF16), later) for sl in subs], axis=1)
    count = functools.reduce(jnp.maximum, [member[sl].sum(axis=0, keepdims=True) for sl in subs])
    rg_hi = rg.astype(BF16)
    rg_lo = (rg - rg_hi.astype(F32)).astype(BF16)
    row_id = lax.broadcasted_iota(jnp.int32, (MOE_ROWS, 1), 0).astype(F32)
    col_id = lax.broadcasted_iota(jnp.int32, (1, MOE_ROWS_PAD), 1).astype(F32)
    lane1 = lax.broadcasted_iota(jnp.int32, (1, LANES), 1)

    def experts(g, hs, gs):
        hid = []
        for e in range(EXPERTS_PER_GROUP):
            k = g * EXPERTS_PER_GROUP + e
            a = _dot(hs, w1_ref[k])
            b = _dot(hs, w3_ref[k])
            hid.append(((a * jax.nn.sigmoid(a)) * b * gs[:, k:k + 1]).astype(BF16))
        return _dot(jnp.concatenate(hid, axis=-1), w2_ref[g]).astype(BF16)

    n_slots = N_GROUPS * MOE_ROWS
    slots_pad = -(-n_slots // LANES) * LANES
    slot_row = lax.broadcasted_iota(jnp.int32, (n_slots, 1), 0).astype(F32)
    slot_col = lax.broadcasted_iota(jnp.int32, (1, slots_pad), 1).astype(F32)
    dest_row = sum(jnp.where((member_t[g:g + 1] > 0.0) & (rank_row[g:g + 1] < MOE_ROWS),
                             rank_row[g:g + 1] + (g * MOE_ROWS + 1), 0.0) for g in range(N_GROUPS)) - 1.0
    dest_col = sum(jnp.where((member[:, g:g + 1] > 0.0) & (rank_col[:, g:g + 1] < MOE_ROWS),
                             rank_col[:, g:g + 1] + (g * MOE_ROWS + 1), 0.0) for g in range(N_GROUPS)) - 1.0
    hs_sub, gs_sub = [], []
    for sl in subs:
        take = (dest_row[:, sl] == slot_row).astype(BF16)
        hs_sub.append(_dot(take, h_ref[sl, :]).astype(BF16))
        gs_sub.append(_dot(take, rg_hi[sl]) + _dot(take, rg_lo[sl]))
    parts = []
    for g in range(N_GROUPS):
        rows = slice(g * MOE_ROWS, (g + 1) * MOE_ROWS)
        parts.append(experts(g, jnp.concatenate([h[rows] for h in hs_sub], axis=0),
                             jnp.concatenate([q[rows] for q in gs_sub], axis=0)))
    pad = [jnp.zeros((slots_pad - n_slots, D_MODEL), BF16)] if slots_pad > n_slots else []
    for s, sl in enumerate(subs):
        give = (dest_col[sl] == slot_col).astype(BF16)
        rows = jnp.concatenate([p[s * MOE_ROWS:(s + 1) * MOE_ROWS] for p in parts] + pad, axis=0)
        acc_ref[sl, :] = _dot(give, rows)

    pad = jnp.zeros((MOE_ROWS_PAD - MOE_ROWS, D_MODEL), BF16)
    for g in range(N_GROUPS):
        rr, mr = rank_row[g:g + 1, :], member_t[g:g + 1, :]
        rc, mc = rank_col[:, g:g + 1], member[:, g:g + 1]
        n_g = jnp.max(jnp.where(lane1 == g, count, 0.0)).astype(jnp.int32)

        def chunk(c, carry, g=g, rr=rr, mr=mr, rc=rc, mc=mc):
            base = (c * MOE_ROWS).astype(F32)
            hs, gs = [], []
            for sl in subs:
                take = ((rr[:, sl] - base == row_id) & (mr[:, sl] > 0.0)).astype(BF16)
                hs.append(_dot(take, h_ref[sl, :]).astype(BF16))
                gs.append(_dot(take, rg_hi[sl]) + _dot(take, rg_lo[sl]))
            part = experts(g, jnp.concatenate(hs, axis=0), jnp.concatenate(gs, axis=0))
            for s, sl in enumerate(subs):
                rel = rc[sl] - base
                give = ((rel == col_id) & (rel < MOE_ROWS) & (mc[sl] > 0.0)).astype(BF16)
                rows = jnp.concatenate([part[s * MOE_ROWS:(s + 1) * MOE_ROWS], pad], axis=0)
                acc_ref[sl, :] += _dot(give, rows)
            return carry

        lax.fori_loop(1, (n_g + MOE_ROWS - 1) // MOE_ROWS, chunk, 0)

    o_ref[...] = x_ref[...] + gt2_ref[...] * _rms(acc_ref[...], gpost_ref[...])


def _moe(h2, rg, x, mod5, l, w, *, seq, ctx):
    n = x.shape[0]
    tm = MOE_TM
    steps_per_seq = seq // tm if not ctx else 0

    def mrow(i):
        return 0 if ctx else 1 + i // steps_per_seq

    row = lambda wd: pl.BlockSpec((tm, wd), lambda i: (i, 0))
    once = pl.Buffered(1)
    wspec = lambda a, b: pl.BlockSpec((None, N_EXPERTS, a, b), lambda i: (l, 0, 0, 0), pipeline_mode=once)
    return pl.pallas_call(
        _moe_kernel,
        grid=(n // tm,),
        in_specs=[
            row(D_MODEL), row(LANES), row(D_MODEL),
            pl.BlockSpec((None, None, None, 1, D_MODEL), lambda i: (l, mrow(i), 5, 0, 0)),
            pl.BlockSpec((None, 1, D_MODEL), lambda i: (l, 0, 0)),
            wspec(D_MODEL, EXPERT_FF), wspec(D_MODEL, EXPERT_FF),
            pl.BlockSpec((None, N_GROUPS, EXPERTS_PER_GROUP * EXPERT_FF, D_MODEL), lambda i: (l, 0, 0, 0),
                         pipeline_mode=once),
        ],
        out_specs=row(D_MODEL),
        out_shape=jax.ShapeDtypeStruct((n, D_MODEL), F32),
        compiler_params=_cparams(("arbitrary",)),
        name="moe_ctx" if ctx else "moe_lat",
    )(h2, rg, x, mod5, w["g_post_ffn"], w["w1"], w["w3"], w["w2"])


def _rope_table(seq):
    t = np.arange(seq)
    n_freq = MLA_ROPE_DIM // 4
    inv = jnp.asarray(ROPE_BASE, F32) ** (-jnp.arange(n_freq, dtype=F32) / n_freq)
    ang_r = jnp.asarray(t // GRID_W, F32)[:, None] * inv
    ang_c = jnp.asarray(t % GRID_W, F32)[:, None] * inv
    cos = jnp.cos(jnp.stack([ang_r, ang_c], axis=1))
    sin = jnp.sin(jnp.stack([ang_r, ang_c], axis=1))
    zeros = jnp.zeros_like(sin)
    c32 = jnp.stack([cos, cos], axis=2).reshape(seq, MLA_ROPE_DIM)
    up32 = jnp.stack([-sin, zeros], axis=2).reshape(seq, MLA_ROPE_DIM)
    dn32 = jnp.stack([zeros, sin], axis=2).reshape(seq, MLA_ROPE_DIM)

    def slot(v, fill):
        return jnp.concatenate([jnp.full((seq, KPE_LANE), fill, F32), v,
                                jnp.full((seq, SLOT - KPE_LANE - MLA_ROPE_DIM), fill, F32)], axis=1)

    return jnp.concatenate([slot(c32, 1.0), slot(up32, 0.0), slot(dn32, 0.0)], axis=1)


def _na_row_classes(n_rows):
    i = np.arange(NA_BLOCK_ROWS)[:, None]
    j = np.arange(NA_SLAB_ROWS)[None, :]
    last_ks = n_rows - NA_SLAB_ROWS
    last_q0 = n_rows - NA_BLOCK_ROWS
    specs = [
        (j < NA_WIN_ROWS, j - i),
        ((j >= i) & (j < i + NA_WIN_ROWS), j - NA_WIN_ROWS // 2 - i),
        (last_ks + j >= n_rows - NA_WIN_ROWS, last_ks + j - last_q0 - i),
    ]
    ok = np.stack([np.broadcast_to(o, (NA_BLOCK_ROWS, NA_SLAB_ROWS)) for o, _ in specs])
    dr = np.stack([np.clip(d + NA_WIN_ROWS - 1, 0, 2 * NA_WIN_ROWS - 2) for _, d in specs])
    return ok, dr


def _bias_kernel(rpb_ref, out_ref, *, row_ok, row_dr):
    reach = NA_WIN_COLS - 1
    rows = rpb_ref[...] * LOG2E
    v_lo = pltpu.roll(rows, SLOT - reach, axis=1)
    v_hi = pltpu.roll(rows, GRID_W - reach, axis=1)
    lane1 = lax.broadcasted_iota(jnp.int32, (1, SLOT), 1)
    mid = (lane1 >= GRID_W - reach) & (lane1 <= GRID_W + reach)
    c = lax.broadcasted_iota(jnp.int32, (GRID_W, SLOT), 0)
    lane = lax.broadcasted_iota(jnp.int32, (GRID_W, SLOT), 1)
    kc = lane & (GRID_W - 1)
    c0 = jnp.clip(c - NA_WIN_COLS // 2, 0, GRID_W - NA_WIN_COLS)
    in_col = (kc >= c0) & (kc < c0 + NA_WIN_COLS)
    first = lane < GRID_W
    for k in range(3):
        for i in range(NA_BLOCK_ROWS):
            for jp in range(NA_SLAB_ROWS // 2):
                ok_a, ok_b = bool(row_ok[k, i, 2 * jp]), bool(row_ok[k, i, 2 * jp + 1])
                a, b = int(row_dr[k, i, 2 * jp]), int(row_dr[k, i, 2 * jp + 1])
                rs = slice(i * GRID_W, (i + 1) * GRID_W)
                cs = slice(jp * SLOT, (jp + 1) * SLOT)
                if not (ok_a or ok_b):
                    out_ref[k, rs, cs] = jnp.full((GRID_W, SLOT), NEG_INF, out_ref.dtype)
                    continue
                v = jnp.where(mid, v_hi[b:b + 1], v_lo[a:a + 1])
                tile = pltpu.roll(jnp.broadcast_to(v, (GRID_W, SLOT)), 0, axis=1, stride=1, stride_axis=0)
                ok = in_col
                if not ok_b:
                    ok = ok & first
                if not ok_a:
                    ok = ok & jnp.logical_not(first)
                out_ref[k, rs, cs] = jnp.where(ok, tile, NEG_INF).astype(out_ref.dtype)


def _na_bias_table(rpb, n_rows):
    n_dr, n_dc = rpb.shape[-2:]
    row_ok, row_dr = _na_row_classes(n_rows)
    rpb_p = jnp.pad(rpb, ((0, 0), (0, 0), (0, 0), (0, SLOT - n_dc)))
    tq, tk = NA_BLOCK_ROWS * GRID_W, NA_SLAB_ROWS * GRID_W
    return pl.pallas_call(
        functools.partial(_bias_kernel, row_ok=row_ok, row_dr=row_dr),
        grid=(DEPTH, NA_HEADS),
        in_specs=[pl.BlockSpec((None, None, n_dr, SLOT), lambda l, h: (l, h, 0, 0))],
        out_specs=pl.BlockSpec((None, 3, None, tq, tk), lambda l, h: (l, 0, h, 0, 0)),
        out_shape=jax.ShapeDtypeStruct((DEPTH, 3, NA_HEADS, tq, tk), BF16),
        compiler_params=_cparams(("arbitrary", "arbitrary")),
        name="na_bias_table",
    )(rpb_p)


def _prep(p):
    L = DEPTH
    w_in = p["w_in"]
    tail = C_END - LANES
    w_in_tail = jnp.pad(w_in[..., tail:], ((0, 0), (0, 0), (0, C_END - w_in.shape[-1]))).astype(BF16)
    w_uq = jnp.pad(p["mla_w_uq"].reshape(L, MLA_Q_LORA, MLA_HEADS, MLA_QK_DIM),
                   ((0, 0), (0, 0), (0, 0), (0, SLOT - MLA_QK_DIM))).reshape(L, MLA_Q_LORA, MLA_HEADS * SLOT)
    ukv = p["mla_w_ukv"].reshape(L, MLA_KV_LORA, MLA_HEADS, MLA_NOPE_DIM + MLA_V_DIM)
    uk = jnp.pad(ukv[..., :MLA_NOPE_DIM], ((0, 0), (0, 0), (0, 0), (0, SLOT - MLA_NOPE_DIM)))
    w_ukv = jnp.concatenate([uk.reshape(L, MLA_KV_LORA, MLA_HEADS * SLOT),
                             ukv[..., MLA_NOPE_DIM:].reshape(L, MLA_KV_LORA, MLA_HEADS * MLA_V_DIM)], axis=-1)
    w_pool_bd = jnp.einsum("lgcd,gh->lgchd", p["w_pool"], jnp.eye(4, dtype=F32)).reshape(L, POOL_WIDTH, POOL_WIDTH)
    w_r = jnp.concatenate([p["w_group_router"], p["w_expert_router"]], axis=-1)
    r_hi = w_r.astype(BF16)
    r_lo = (w_r - r_hi.astype(F32)).astype(BF16)
    padl = lambda a: jnp.pad(a, ((0, 0), (0, 0), (0, LANES - a.shape[-1])))
    vec = lambda a: a.reshape(L, 1, a.shape[-1])
    return {
        "w_in": w_in.astype(BF16), "w_in_tail": w_in_tail,
        "w_uq": w_uq.astype(BF16),
        "w_ukv": w_ukv.astype(BF16),
        "w_pool_bd": w_pool_bd.astype(BF16),
        "w_branch": p["w_branch"].astype(BF16),
        "w_out": p["w_out"].astype(BF16),
        "w_r2": jnp.concatenate([padl(r_hi), padl(r_lo)], axis=-1),
        "w_r1": padl(r_hi),
        "w1": p["w_exp_gate"].astype(BF16),
        "w3": p["w_exp_up"].astype(BF16),
        "w2": p["w_exp_down"].astype(BF16).reshape(L, N_GROUPS, EXPERTS_PER_GROUP * EXPERT_FF, D_MODEL),
        "g_pre_mix": vec(p["g_pre_mix"]), "g_post_mix": vec(p["g_post_mix"]),
        "g_pre_ffn": vec(p["g_pre_ffn"]), "g_post_ffn": vec(p["g_post_ffn"]),
        "pool_scale": vec(p["pool_scale"]), "q_norm": vec(p["mla_q_norm"]), "kv_norm": vec(p["mla_kv_norm"]),
    }


def kernel(x_prompt, x_sample, cache_na_k, cache_na_v, cache_mla_ckv, cache_mla_kpe, c, c_ctx, w_mod, b_mod,
           g_pre_mix, g_post_mix, g_pre_ffn, g_post_ffn, w_in, w_pool, pool_scale, na_rpb, mla_q_norm, mla_w_uq,
           mla_kv_norm, mla_w_ukv, w_branch, w_out, w_group_router, w_expert_router, w_exp_gate, w_exp_up,
           w_exp_down):
    batch, seq, _ = x_prompt.shape
    dbatch, dseq, _ = x_sample.shape
    past = cache_na_k.shape[2]
    assert dbatch + 1 <= 8 and dseq % (NA_BLOCK_ROWS * GRID_W) == 0

    w = _prep(dict(w_in=w_in, w_pool=w_pool, pool_scale=pool_scale, mla_q_norm=mla_q_norm, mla_w_uq=mla_w_uq,
                   mla_kv_norm=mla_kv_norm, mla_w_ukv=mla_w_ukv, w_branch=w_branch, w_out=w_out,
                   w_group_router=w_group_router, w_expert_router=w_expert_router, w_exp_gate=w_exp_gate,
                   w_exp_up=w_exp_up, w_exp_down=w_exp_down, g_pre_mix=g_pre_mix, g_post_mix=g_post_mix,
                   g_pre_ffn=g_pre_ffn, g_post_ffn=g_post_ffn))
    rope_tab = _rope_table(dseq)
    bias_tab = _na_bias_table(na_rpb, dseq // GRID_W)
    place = jnp.tile(jnp.pad(jnp.eye(MLA_ROPE_DIM, dtype=F32), ((0, 0), (KPE_LANE, SLOT - KPE_LANE - MLA_ROPE_DIM))),
                     (1, MLA_HEADS)).astype(BF16)

    cvec = jnp.concatenate([c_ctx[None], c, jnp.zeros((8 - 1 - dbatch, D_MODEL), F32)], axis=0)
    mod5 = _modulation(cvec, w_mod, b_mod).reshape(DEPTH, 8, 6, 1, D_MODEL)

    ck = cache_na_k.reshape(dbatch, DEPTH, past, NA_WIDTH)
    cv = cache_na_v.reshape(dbatch, DEPTH, past, NA_WIDTH)

    xp = x_prompt.reshape(batch * seq, D_MODEL)
    assert DEPTH == 2
    caches = None
    for l in range(DEPTH):
        u, nq, nvo, qm, km, vmo, gate, *caches = _inproj(xp, mod5, l, w, seq=seq, rope_tab=None, ctx_out=True,
                                                         tm=seq, prev=caches)
        yna, ymla = _attn_ctx(nq, caches[0], nvo, qm, km, vmo, seq=seq, nk_layer=l if l > 0 else None)
        xp, h2, rg = _merge(xp, u, yna, ymla, gate, mod5, l, w, seq=seq, ctx=True, tm=2 * seq)
        xp = _moe(h2, rg, xp, mod5, l, w, seq=seq, ctx=True)
    new_nk, new_nv, new_ckv, new_kpe = caches

    xs = x_sample.reshape(dbatch * dseq, D_MODEL)
    for l in range(DEPTH):
        u, nq, nvo, qm, km, vmo, gate, nk = _inproj(xs, mod5, l, w, seq=dseq, rope_tab=rope_tab, ctx_out=False, tm=512)
        ckm, cvmo, cnvo = _cache_prep(cache_mla_ckv, cache_mla_kpe, cv, w["w_ukv"], place, l)
        yna = _attn_na(nq, nk, nvo, ck, cnvo, bias_tab, l, batch=dbatch, seq=dseq)
        ymla = _attn_mla(qm, km, vmo, ckm, cvmo, batch=dbatch, seq=dseq, tq=1024)
        xs, h2, rg = _merge(xs, u, yna, ymla, gate, mod5, l, w, seq=dseq, ctx=False, tm=512)
        xs = _moe(h2, rg, xs, mod5, l, w, seq=dseq, ctx=False)

    heads = (batch, DEPTH, seq, NA_HEADS, NA_HEAD_DIM)
    return (xp.reshape(batch, seq, D_MODEL), xs.reshape(dbatch, dseq, D_MODEL),
            new_nk.reshape(heads), new_nv.reshape(heads), new_ckv, new_kpe)
```
